```python
import jax, jax.numpy as jnp
from jax import lax
import numpy as np

D_MODEL = 1024
BATCH = 8
SEQ = 4096
DEPTH = 2

DN_ALPHA = (2 * DEPTH) ** 0.25
DN_BETA = (8 * DEPTH) ** -0.25
LN_EPS = 1e-5
RMS_EPS = 1e-6

CONV_CH = D_MODEL // 2
CONV_WIDTH = 31

MLA_HEADS = 8
QK_NOPE = 64
QK_ROPE = 32
V_DIM = 64
Q_LORA = D_MODEL // 4
KV_LORA = D_MODEL // 8
ROPE_THETA = 10000.0

L0_IN = 2 * CONV_CH + Q_LORA + KV_LORA + QK_ROPE
L0_MIX = CONV_CH + MLA_HEADS * V_DIM

FOX_HEADS = 16
FOX_HD = D_MODEL // FOX_HEADS
L1_IN = 3 * FOX_HEADS * FOX_HD + FOX_HEADS
L1_MIX = FOX_HEADS * FOX_HD

BLOCK_Q = 128

N_EXPERTS = 64
TOP_K = 6
N_GROUPS = 8
TOPK_GROUPS = 4
EXPERT_HID = D_MODEL // 4
SHARED_HID = EXPERT_HID
ROUTED_SCALE = 2.5

kernel_name = "hybrid_conv_mla_fox_moe_deepnorm"


def layer_norm(x, g, b):
    xf = x.astype(jnp.float32)
    mu = jnp.mean(xf, axis=-1, keepdims=True)
    var = jnp.mean(jnp.square(xf - mu), axis=-1, keepdims=True)
    return ((xf - mu) * lax.rsqrt(var + LN_EPS) * g + b).astype(x.dtype)


def rms_norm(x, g):
    xf = x.astype(jnp.float32)
    return (xf * lax.rsqrt(jnp.mean(jnp.square(xf), axis=-1, keepdims=True) + RMS_EPS) * g).astype(x.dtype)


def rope(x, cos, sin):
    x1, x2 = jnp.split(x, 2, axis=-1)
    return jnp.concatenate([x1 * cos - x2 * sin, x2 * cos + x1 * sin], axis=-1)


def causal_block_attention(q, k, v, scale, log_fcum=None):
    seq = q.shape[2]
    outs = []
    for i in range(seq // BLOCK_Q):
        q0, q1 = i * BLOCK_Q, (i + 1) * BLOCK_Q
        s = jnp.einsum('bhqd,bhkd->bhqk', q[:, :, q0:q1], k[:, :, :q1],
                       preferred_element_type=jnp.float32) * scale
        if log_fcum is not None:
            s = s + (log_fcum[:, :, q0:q1, None] - log_fcum[:, :, None, :q1])
        mask = (q0 + jnp.arange(BLOCK_Q))[:, None] >= jnp.arange(q1)[None, :]
        p = jax.nn.softmax(jnp.where(mask, s, -jnp.inf), axis=-1)
        outs.append(jnp.einsum('bhqk,bhkd->bhqd', p.astype(v.dtype), v[:, :, :q1]))
    return jnp.concatenate(outs, axis=2)


def conv_mla_mixer(x, cos, sin, w_in, b_in, conv_w, conv_b, conv_ln_g, conv_ln_b,
                   q_norm_g, w_uq, kv_norm_g, w_ukv, w_out, b_out):
    bsz, seq, _ = x.shape
    h = x @ w_in + b_in
    i1 = 2 * CONV_CH
    i2 = i1 + Q_LORA
    i3 = i2 + KV_LORA
    conv_in, q_lat, kv_lat, k_rope = h[..., :i1], h[..., i1:i2], h[..., i2:i3], h[..., i3:]

    a, g = jnp.split(conv_in, 2, axis=-1)
    u = a * jax.nn.sigmoid(g)
    u = lax.conv_general_dilated(
        u, conv_w[:, None, :].astype(u.dtype), window_strides=(1,),
        padding=[(CONV_WIDTH - 1, 0)], dimension_numbers=('NWC', 'WIO', 'NWC'),
        feature_group_count=CONV_CH) + conv_b
    u = jax.nn.silu(layer_norm(u, conv_ln_g, conv_ln_b))

    q = (rms_norm(q_lat, q_norm_g) @ w_uq).reshape(bsz, seq, MLA_HEADS, QK_NOPE + QK_ROPE)
    q = jnp.concatenate([q[..., :QK_NOPE], rope(q[..., QK_NOPE:], cos, sin)], axis=-1)
    kv = (rms_norm(kv_lat, kv_norm_g) @ w_ukv).reshape(bsz, seq, MLA_HEADS, QK_NOPE + V_DIM)
    k_pe = jnp.broadcast_to(rope(k_rope[:, :, None, :], cos, sin), (bsz, seq, MLA_HEADS, QK_ROPE))
    k = jnp.concatenate([kv[..., :QK_NOPE], k_pe], axis=-1)
    v = kv[..., QK_NOPE:]
    o = causal_block_attention(q.transpose(0, 2, 1, 3), k.transpose(0, 2, 1, 3),
                               v.transpose(0, 2, 1, 3), (QK_NOPE + QK_ROPE) ** -0.5)
    o = o.transpose(0, 2, 1, 3).reshape(bsz, seq, MLA_HEADS * V_DIM)

    return jnp.concatenate([u, o], axis=-1) @ w_out + b_out


def fox_mixer(x, w_in, b_in, w_out, b_out):
    bsz, seq, _ = x.shape
    h = x @ w_in + b_in
    qkv = h[..., :3 * L1_MIX].reshape(bsz, seq, 3, FOX_HEADS, FOX_HD).transpose(2, 0, 3, 1, 4)
    log_f = jax.nn.log_sigmoid(h[..., 3 * L1_MIX:].astype(jnp.float32))
    log_fcum = jnp.cumsum(log_f, axis=1).transpose(0, 2, 1)
    o = causal_block_attention(qkv[0], qkv[1], qkv[2], FOX_HD ** -0.5, log_fcum)
    return o.transpose(0, 2, 1, 3).reshape(bsz, seq, L1_MIX) @ w_out + b_out


def moe(x, router_w, router_bias, exp_w_gate, exp_w_up, exp_w_down, sh_w_gate, sh_w_up, sh_w_down):
    bsz, seq, d = x.shape
    t = x.reshape(-1, d)
    n_tok = t.shape[0]
    s = jax.nn.sigmoid((t @ router_w).astype(jnp.float32))
    choice = s + router_bias.astype(jnp.float32)
    grp = choice.reshape(n_tok, N_GROUPS, N_EXPERTS // N_GROUPS)
    gscore = jnp.sum(lax.top_k(grp, 2)[0], axis=-1)
    _, gidx = lax.top_k(gscore, TOPK_GROUPS)
    gmask = jnp.sum(jax.nn.one_hot(gidx, N_GROUPS, dtype=jnp.float32), axis=1) > 0
    emask = jnp.repeat(gmask, N_EXPERTS // N_GROUPS, axis=-1)
    _, eidx = lax.top_k(jnp.where(emask, choice, -jnp.inf), TOP_K)
    w = jnp.take_along_axis(s, eidx, axis=-1)
    w = w / jnp.sum(w, axis=-1, keepdims=True) * ROUTED_SCALE
    gates = jnp.sum(jax.nn.one_hot(eidx, N_EXPERTS, dtype=jnp.float32) * w[..., None], axis=1)

    def expert_step(acc, p):
        wg, wu, wd, g = p
        hid = jax.nn.silu(t @ wg) * (t @ wu)
        return acc + (hid @ wd) * g[:, None].astype(t.dtype), None

    routed, _ = lax.scan(expert_step, jnp.zeros_like(t), (exp_w_gate, exp_w_up, exp_w_down, gates.T))
    shared = (jax.nn.silu(t @ sh_w_gate) * (t @ sh_w_up)) @ sh_w_down
    return (routed + shared).reshape(bsz, seq, d)


def setup_inputs(seed: int = 0) -> dict:
    key = jax.random.key(seed)
    ks = iter(jax.random.split(key, 64))

    def nrm(shape, scale):
        return jax.random.normal(next(ks), shape, jnp.float32) * scale

    def gain(n):
        return 1.0 + nrm((n,), 0.02)

    def small(n):
        return nrm((n,), 0.02)

    def moe_params(prefix):
        return {
            prefix + "router_w": nrm((D_MODEL, N_EXPERTS), D_MODEL ** -0.5),
            prefix + "router_bias": nrm((N_EXPERTS,), 0.01),
            prefix + "exp_w_gate": nrm((N_EXPERTS, D_MODEL, EXPERT_HID), D_MODEL ** -0.5),
            prefix + "exp_w_up": nrm((N_EXPERTS, D_MODEL, EXPERT_HID), D_MODEL ** -0.5),
            prefix + "exp_w_down": nrm((N_EXPERTS, EXPERT_HID, D_MODEL), EXPERT_HID ** -0.5 * DN_BETA),
            prefix + "sh_w_gate": nrm((D_MODEL, SHARED_HID), D_MODEL ** -0.5),
            prefix + "sh_w_up": nrm((D_MODEL, SHARED_HID), D_MODEL ** -0.5),
            prefix + "sh_w_down": nrm((SHARED_HID, D_MODEL), SHARED_HID ** -0.5 * DN_BETA),
            prefix + "ln2_g": gain(D_MODEL),
            prefix + "ln2_b": small(D_MODEL),
        }

    inp = {}
    inp["x"] = nrm((BATCH, SEQ, D_MODEL), 1.0)
    offset = jax.random.randint(next(ks), (BATCH, 1), 0, 1024, dtype=jnp.int32)
    inp["positions"] = offset + jnp.arange(SEQ, dtype=jnp.int32)[None, :]
    inp["l0_w_in"] = nrm((D_MODEL, L0_IN), D_MODEL ** -0.5)
    inp["l0_b_in"] = small(L0_IN)
    inp["l0_conv_w"] = nrm((CONV_WIDTH, CONV_CH), CONV_WIDTH ** -0.5)
    inp["l0_conv_b"] = small(CONV_CH)
    inp["l0_conv_ln_g"] = gain(CONV_CH)
    inp["l0_conv_ln_b"] = small(CONV_CH)
    inp["l0_q_norm_g"] = gain(Q_LORA)
    inp["l0_w_uq"] = nrm((Q_LORA, MLA_HEADS * (QK_NOPE + QK_ROPE)), Q_LORA ** -0.5)
    inp["l0_kv_norm_g"] = gain(KV_LORA)
    inp["l0_w_ukv"] = nrm((KV_LORA, MLA_HEADS * (QK_NOPE + V_DIM)), KV_LORA ** -0.5)
    inp["l0_w_out"] = nrm((L0_MIX, D_MODEL), L0_MIX ** -0.5 * DN_BETA)
    inp["l0_b_out"] = small(D_MODEL)
    inp["l0_ln1_g"] = gain(D_MODEL)
    inp["l0_ln1_b"] = small(D_MODEL)
    inp.update(moe_params("l0_"))
    inp["l1_w_in"] = nrm((D_MODEL, L1_IN), D_MODEL ** -0.5)
    inp["l1_b_in"] = small(L1_IN)
    inp["l1_w_out"] = nrm((L1_MIX, D_MODEL), L1_MIX ** -0.5 * DN_BETA)
    inp["l1_b_out"] = small(D_MODEL)
    inp["l1_ln1_g"] = gain(D_MODEL)
    inp["l1_ln1_b"] = small(D_MODEL)
    inp.update(moe_params("l1_"))
    return inp


def reference(x, positions,
              l0_w_in, l0_b_in, l0_conv_w, l0_conv_b, l0_conv_ln_g, l0_conv_ln_b,
              l0_q_norm_g, l0_w_uq, l0_kv_norm_g, l0_w_ukv, l0_w_out, l0_b_out,
              l0_ln1_g, l0_ln1_b,
              l0_router_w, l0_router_bias, l0_exp_w_gate, l0_exp_w_up, l0_exp_w_down,
              l0_sh_w_gate, l0_sh_w_up, l0_sh_w_down, l0_ln2_g, l0_ln2_b,
              l1_w_in, l1_b_in, l1_w_out, l1_b_out, l1_ln1_g, l1_ln1_b,
              l1_router_w, l1_router_bias, l1_exp_w_gate, l1_exp_w_up, l1_exp_w_down,
              l1_sh_w_gate, l1_sh_w_up, l1_sh_w_down, l1_ln2_g, l1_ln2_b):
    inv_freq = 1.0 / (ROPE_THETA ** (jnp.arange(0, QK_ROPE, 2, dtype=jnp.float32) / QK_ROPE))
    ang = positions.astype(jnp.float32)[..., None] * inv_freq
    cos = jnp.cos(ang)[:, :, None, :].astype(x.dtype)
    sin = jnp.sin(ang)[:, :, None, :].astype(x.dtype)

    mixer_params = [
        (l0_w_in, l0_b_in, l0_conv_w, l0_conv_b, l0_conv_ln_g, l0_conv_ln_b,
         l0_q_norm_g, l0_w_uq, l0_kv_norm_g, l0_w_ukv, l0_w_out, l0_b_out),
        (l1_w_in, l1_b_in, l1_w_out, l1_b_out),
    ]
    ln1 = [(l0_ln1_g, l0_ln1_b), (l1_ln1_g, l1_ln1_b)]
    moe_p = [
        (l0_router_w, l0_router_bias, l0_exp_w_gate, l0_exp_w_up, l0_exp_w_down,
         l0_sh_w_gate, l0_sh_w_up, l0_sh_w_down),
        (l1_router_w, l1_router_bias, l1_exp_w_gate, l1_exp_w_up, l1_exp_w_down,
         l1_sh_w_gate, l1_sh_w_up, l1_sh_w_down),
    ]
    ln2 = [(l0_ln2_g, l0_ln2_b), (l1_ln2_g, l1_ln2_b)]

    for layer in range(DEPTH):
        if layer % 2 == 0:
            mix = conv_mla_mixer(x, cos, sin, *mixer_params[layer])
        else:
            mix = fox_mixer(x, *mixer_params[layer])
        x = layer_norm(DN_ALPHA * x + mix, *ln1[layer])
        x = layer_norm(DN_ALPHA * x + moe(x, *moe_p[layer]), *ln2[layer])
    return x
```

```python
import functools

import jax
import jax.numpy as jnp
from jax import lax
from jax.experimental import pallas as pl
from jax.experimental.pallas import tpu as pltpu

D_MODEL = 1024
DEPTH = 2
DN_ALPHA = (2 * DEPTH) ** 0.25
LN_EPS = 1e-5
RMS_EPS = 1e-6

CONV_CH = 512
CONV_WIDTH = 31
MLA_HEADS = 8
QK_NOPE = 64
QK_ROPE = 32
V_DIM = 64
Q_LORA = 256
KV_LORA = 128
ROPE_THETA = 10000.0
FOX_HEADS = 16
FOX_HD = 64
N_EXPERTS = 64
TOP_K = 6
N_GROUPS = 8
TOPK_GROUPS = 4
EXPERT_HID = 256
ROUTED_SCALE = 2.5

LANES = 128
SUBLANES = 8
VMEM_LIMIT_BYTES = 56 * 1024 * 1024

SEQ_TILE = 512
TOKEN_TILE = 512
ATTN_TILE = 512
GROUP_TILE = 512
CONV_ROWS = 32
CONV_HALO = 32
PAD_HEAD = 128

BF16 = jnp.bfloat16
F32 = jnp.float32
NEG_INF = float("-inf")


def _params(*semantics):
    return pltpu.CompilerParams(dimension_semantics=semantics,
                                vmem_limit_bytes=VMEM_LIMIT_BYTES)


def _resident(shape):
    nd = len(shape)
    return pl.BlockSpec(shape, lambda *_: (0,) * nd)


def _dot(a, b):
    return jnp.dot(a, b, preferred_element_type=F32)


def _dot_nt(a, b, precision=None):
    return lax.dot_general(a, b, (((1,), (1,)), ((), ())),
                           precision=precision, preferred_element_type=F32)


def _layer_norm(x, g, b):
    mu = jnp.mean(x, axis=-1, keepdims=True)
    xc = x - mu
    var = jnp.mean(xc * xc, axis=-1, keepdims=True)
    return xc * lax.rsqrt(var + LN_EPS) * g + b


def _rms_norm(x, g):
    return x * lax.rsqrt(jnp.mean(x * x, axis=-1, keepdims=True) + RMS_EPS) * g


L0_A = 0
L0_G = CONV_CH
L0_Q = 2 * CONV_CH
L0_KV = L0_Q + Q_LORA
L0_KR = L0_KV + KV_LORA
L0_KRR = L0_KR + PAD_HEAD
L0_COLS = L0_KRR + PAD_HEAD


def _l0_front_kernel(x_ref, pos_ref, w_in_ref, b_in_ref, conv_w_ref, conv_b_ref,
                     cln_g_ref, cln_b_ref, qn_g_ref, wq_ref, wqr_ref, kvn_g_ref,
                     wk_ref, wv_ref, invf_ref,
                     u_ref, q_ref, k_ref, v_ref, ubuf):
    ts = x_ref.shape[0]

    @pl.when(pl.program_id(1) == 0)
    def _():
        ubuf[0:CONV_HALO, :] = jnp.zeros((CONV_HALO, CONV_CH), F32)

    h = _dot(x_ref[...].astype(BF16), w_in_ref[...]) + b_in_ref[...]

    ubuf[CONV_HALO:CONV_HALO + ts, :] = (
        h[:, L0_A:L0_A + CONV_CH] * jax.nn.sigmoid(h[:, L0_G:L0_G + CONV_CH]))
    first_tap = CONV_HALO - (CONV_WIDTH - 1)

    def conv_chunk(c, carry):
        base = pl.multiple_of(c * CONV_ROWS, CONV_ROWS)
        win = ubuf[pl.ds(base, CONV_ROWS + CONV_HALO), :]
        acc = jnp.broadcast_to(conv_b_ref[...], (CONV_ROWS, CONV_CH))
        for j in range(CONV_WIDTH):
            acc = acc + win[first_tap + j:first_tap + j + CONV_ROWS, :] * conv_w_ref[j:j + 1, :]
        y = _layer_norm(acc, cln_g_ref[...], cln_b_ref[...])
        u_ref[pl.ds(base, CONV_ROWS), :] = (y * jax.nn.sigmoid(y)).astype(BF16)
        return carry

    lax.fori_loop(0, ts // CONV_ROWS, conv_chunk, 0)
    ubuf[0:CONV_HALO, :] = ubuf[ts:ts + CONV_HALO, :]

    ang = pos_ref[...].astype(F32) * invf_ref[...]
    cos = jnp.cos(ang)
    sin = jnp.sin(ang)
    scale = (QK_NOPE + QK_ROPE) ** -0.5

    qn = _rms_norm(h[:, L0_Q:L0_Q + Q_LORA], qn_g_ref[...]).astype(BF16)
    q = _dot(qn, wq_ref[...])
    q_rot = _dot(qn, wqr_ref[...])
    cos_s = cos * scale
    sin_s = sin * scale
    for hh in range(MLA_HEADS):
        blk = slice(hh * PAD_HEAD, (hh + 1) * PAD_HEAD)
        q_ref[:, blk] = (q[:, blk] * cos_s + q_rot[:, blk] * sin_s).astype(BF16)

    kvn = _rms_norm(h[:, L0_KV:L0_KV + KV_LORA], kvn_g_ref[...]).astype(BF16)
    k_nope = _dot(kvn, wk_ref[...])
    k_pe = h[:, L0_KR:L0_KR + PAD_HEAD] * cos + h[:, L0_KRR:L0_KRR + PAD_HEAD] * sin
    for hh in range(MLA_HEADS):
        blk = slice(hh * PAD_HEAD, (hh + 1) * PAD_HEAD)
        k_ref[:, blk] = (k_nope[:, blk] + k_pe).astype(BF16)
    v_ref[...] = _dot(kvn, wv_ref[...]).astype(BF16)


def _rope_rotate_cols(w):
    half = QK_ROPE // 2
    return jnp.concatenate([-w[..., half:], w[..., :half]], axis=-1)


def _l0_front(x, positions, w_in, b_in, conv_w, conv_b, cln_g, cln_b,
              qn_g, w_uq, kvn_g, w_ukv):
    bsz, seq, _ = x.shape
    pad_lo = jnp.zeros((D_MODEL, QK_NOPE), F32)
    pad_hi = jnp.zeros((D_MODEL, PAD_HEAD - QK_NOPE - QK_ROPE), F32)
    i3 = L0_KR
    w_kr = w_in[:, i3:i3 + QK_ROPE]
    w_in_p = jnp.concatenate(
        [w_in[:, :i3], pad_lo, w_kr, pad_hi, pad_lo, _rope_rotate_cols(w_kr), pad_hi],
        axis=1).astype(BF16)
    b_kr = b_in[i3:i3 + QK_ROPE]
    zlo = jnp.zeros((QK_NOPE,), F32)
    zhi = jnp.zeros((PAD_HEAD - QK_NOPE - QK_ROPE,), F32)
    b_in_p = jnp.concatenate(
        [b_in[:i3], zlo, b_kr, zhi, zlo, _rope_rotate_cols(b_kr), zhi])[None, :]

    dq = QK_NOPE + QK_ROPE
    wq3 = w_uq.reshape(Q_LORA, MLA_HEADS, dq)
    zq = jnp.zeros((Q_LORA, MLA_HEADS, PAD_HEAD - dq), F32)
    wq_p = jnp.concatenate([wq3, zq], axis=-1).reshape(Q_LORA, MLA_HEADS * PAD_HEAD)
    wqr_p = jnp.concatenate(
        [jnp.zeros((Q_LORA, MLA_HEADS, QK_NOPE), F32),
         _rope_rotate_cols(wq3[..., QK_NOPE:]), zq], axis=-1
    ).reshape(Q_LORA, MLA_HEADS * PAD_HEAD)
    wkv3 = w_ukv.reshape(KV_LORA, MLA_HEADS, QK_NOPE + V_DIM)
    wk_p = jnp.concatenate(
        [wkv3[..., :QK_NOPE], jnp.zeros((KV_LORA, MLA_HEADS, PAD_HEAD - QK_NOPE), F32)],
        axis=-1).reshape(KV_LORA, MLA_HEADS * PAD_HEAD)
    wv = wkv3[..., QK_NOPE:].reshape(KV_LORA, MLA_HEADS * V_DIM)

    inv_freq = 1.0 / (ROPE_THETA ** (jnp.arange(0, QK_ROPE, 2, dtype=F32) / QK_ROPE))
    invf = jnp.concatenate([jnp.zeros((QK_NOPE,), F32), inv_freq, inv_freq,
                            jnp.zeros((PAD_HEAD - QK_NOPE - QK_ROPE,), F32)])[None, :]

    ts = SEQ_TILE
    row = lambda w: pl.BlockSpec((None, ts, w), lambda b, s: (b, s, 0))
    qk_w = MLA_HEADS * PAD_HEAD
    v_w = MLA_HEADS * V_DIM
    return pl.pallas_call(
        _l0_front_kernel,
        grid=(bsz, seq // ts),
        in_specs=[row(D_MODEL), row(1),
                  _resident((D_MODEL, L0_COLS)), _resident((1, L0_COLS)),
                  _resident((CONV_WIDTH, CONV_CH)), _resident((1, CONV_CH)),
                  _resident((1, CONV_CH)), _resident((1, CONV_CH)),
                  _resident((1, Q_LORA)), _resident((Q_LORA, qk_w)),
                  _resident((Q_LORA, qk_w)), _resident((1, KV_LORA)),
                  _resident((KV_LORA, qk_w)), _resident((KV_LORA, v_w)),
                  _resident((1, PAD_HEAD))],
        out_specs=[row(CONV_CH), row(qk_w), row(qk_w), row(v_w)],
        out_shape=[jax.ShapeDtypeStruct((bsz, seq, CONV_CH), BF16),
                   jax.ShapeDtypeStruct((bsz, seq, qk_w), BF16),
                   jax.ShapeDtypeStruct((bsz, seq, qk_w), BF16),
                   jax.ShapeDtypeStruct((bsz, seq, v_w), BF16)],
        scratch_shapes=[pltpu.VMEM((ts + CONV_HALO, CONV_CH), F32)],
        compiler_params=_params("arbitrary", "arbitrary"),
        name="l0_front",
    )(x, positions[..., None], w_in_p, b_in_p, conv_w, conv_b[None, :],
      cln_g[None, :], cln_b[None, :], qn_g[None, :], wq_p.astype(BF16),
      wqr_p.astype(BF16), kvn_g[None, :], wk_p.astype(BF16), wv.astype(BF16), invf)


def _l1_front_kernel(x_ref, w_ref, b_ref, wf_ref, bf_ref, q_ref, k_ref, v_ref,
                     cneg_ref, carry):
    ts = x_ref.shape[0]

    @pl.when(pl.program_id(1) == 0)
    def _():
        carry[...] = jnp.zeros_like(carry)

    xb = x_ref[...].astype(BF16)
    mix = FOX_HEADS * FOX_HD
    scale = FOX_HD ** -0.5
    q_ref[...] = ((_dot(xb, w_ref[:, 0:mix]) + b_ref[:, 0:mix]) * scale).astype(BF16)
    k_ref[...] = (_dot(xb, w_ref[:, mix:2 * mix]) + b_ref[:, mix:2 * mix]).astype(BF16)
    v_ref[...] = (_dot(xb, w_ref[:, 2 * mix:3 * mix]) + b_ref[:, 2 * mix:3 * mix]).astype(BF16)

    log_f = jax.nn.log_sigmoid(_dot_nt(wf_ref[...], xb) + bf_ref[...])
    r = lax.broadcasted_iota(jnp.int32, (ts, ts), 0)
    c = lax.broadcasted_iota(jnp.int32, (ts, ts), 1)
    upper = (r <= c).astype(F32)
    csum = jnp.dot(log_f, upper, precision=lax.Precision.HIGHEST,
                   preferred_element_type=F32) + carry[...]
    cneg_ref[...] = -csum
    carry[...] = carry[...] + jnp.sum(log_f, axis=1, keepdims=True)


def _l1_front(x, w_in, b_in):
    bsz, seq, _ = x.shape
    mix = FOX_HEADS * FOX_HD
    ts = SEQ_TILE
    row = lambda w: pl.BlockSpec((None, ts, w), lambda b, s: (b, s, 0))
    w_qkv = w_in[:, :3 * mix].astype(BF16)
    b_qkv = b_in[None, :3 * mix]
    wf_t = w_in[:, 3 * mix:].T.astype(BF16)
    bf_t = b_in[3 * mix:][:, None]
    return pl.pallas_call(
        _l1_front_kernel,
        grid=(bsz, seq // ts),
        in_specs=[row(D_MODEL), _resident((D_MODEL, 3 * mix)), _resident((1, 3 * mix)),
                  _resident((FOX_HEADS, D_MODEL)), _resident((FOX_HEADS, 1))],
        out_specs=[row(mix), row(mix), row(mix),
                   pl.BlockSpec((None, FOX_HEADS, ts), lambda b, s: (b, 0, s))],
        out_shape=[jax.ShapeDtypeStruct((bsz, seq, mix), BF16)] * 3
        + [jax.ShapeDtypeStruct((bsz, FOX_HEADS, seq), F32)],
        scratch_shapes=[pltpu.VMEM((FOX_HEADS, 1), F32)],
        compiler_params=_params("arbitrary", "arbitrary"),
        name="l1_front",
    )(x, w_qkv, b_qkv, wf_t, bf_t)


def _attn_kernel(*refs, packed, has_bias):
    if has_bias:
        q_ref, k_ref, v_ref, cneg_ref, o_ref = refs
    else:
        q_ref, k_ref, v_ref, o_ref = refs
        cneg_ref = None
    tq = q_ref.shape[0]
    tk = tq
    qi = pl.program_id(2)
    q2 = q_ref[...]
    half = LANES // 2
    if packed:
        lane = lax.broadcasted_iota(jnp.int32, q2.shape, 1)
        zero = jnp.zeros_like(q2)
        qs = (jnp.where(lane < half, q2, zero), jnp.where(lane >= half, q2, zero))
    else:
        qs = (q2[:, :PAD_HEAD], q2[:, PAD_HEAD:])

    def k_block(kj, head):
        rows = pl.ds(pl.multiple_of(kj * tk, tk), tk)
        if packed:
            return k_ref[rows, :]
        return k_ref[rows, head * PAD_HEAD:(head + 1) * PAD_HEAD]

    def step(kj, carry, masked):
        rows = pl.ds(pl.multiple_of(kj * tk, tk), tk)
        vblk = v_ref[rows, :]
        out = []
        for head in range(2):
            m, l, acc = carry[head]
            s = _dot_nt(qs[head], k_block(kj, head))
            if has_bias:
                s = s + cneg_ref[head, :, rows]
            if masked:
                r = lax.broadcasted_iota(jnp.int32, s.shape, 0)
                c = lax.broadcasted_iota(jnp.int32, s.shape, 1)
                s = jnp.where(r >= c, s, NEG_INF)
            m_new = jnp.maximum(m, jnp.max(s, axis=-1, keepdims=True))
            alpha = jnp.exp(m - m_new)
            p = jnp.exp(s - m_new)
            l = alpha * l + jnp.sum(p, axis=-1, keepdims=True)
            acc = alpha * acc + _dot(p.astype(BF16), vblk)
            out.append((m_new, l, acc))
        return tuple(out)

    init = tuple((jnp.full((tq, 1), NEG_INF, F32), jnp.zeros((tq, 1), F32),
                  jnp.zeros((tq, LANES), F32)) for _ in range(2))
    carry = lax.fori_loop(0, qi, lambda kj, c: step(kj, c, False), init)
    (_, l_a, acc_a), (_, l_b, acc_b) = step(qi, carry, True)
    lane_o = lax.broadcasted_iota(jnp.int32, (tq, LANES), 1)
    o_ref[...] = jnp.where(lane_o < half, acc_a / l_a, acc_b / l_b).astype(o_ref.dtype)


def _attention(q, k, v, cneg, packed):
    bsz, seq, v_w = v.shape
    pairs = v_w // LANES
    qk_w = q.shape[-1] // pairs
    tq = ATTN_TILE
    in_specs = [pl.BlockSpec((None, tq, qk_w), lambda b, p, i: (b, i, p)),
                pl.BlockSpec((None, seq, qk_w), lambda b, p, i: (b, 0, p)),
                pl.BlockSpec((None, seq, LANES), lambda b, p, i: (b, 0, p))]
    args = [q, k, v]
    if cneg is not None:
        in_specs.append(pl.BlockSpec((None, 2, 1, seq), lambda b, p, i: (b, p, 0, 0)))
        args.append(cneg)
    return pl.pallas_call(
        functools.partial(_attn_kernel, packed=packed, has_bias=cneg is not None),
        grid=(bsz, pairs, seq // tq),
        in_specs=in_specs,
        out_specs=pl.BlockSpec((None, tq, LANES), lambda b, p, i: (b, i, p)),
        out_shape=jax.ShapeDtypeStruct((bsz, seq, v_w), BF16),
        compiler_params=_params("arbitrary", "arbitrary", "arbitrary"),
        name="attn_packed" if packed else "attn_padded",
    )(*args)


def _first_hit(hits, found):
    out = []
    for hcur in hits:
        take = jnp.logical_and(hcur, jnp.logical_not(found))
        found = jnp.logical_or(found, take)
        out.append(take)
    return out, found


def _mix_out_router_kernel(a_ref, b_ref, x_ref, wa_ref, wb_ref, bo_ref, g_ref, be_ref,
                           rw_ref, rb_ref,
                           x1_ref, x1b_ref, e_ref, gate_ref, rank_ref, cnt_ref, cnt):
    tm = x_ref.shape[0]

    @pl.when(pl.program_id(0) == 0)
    def _():
        cnt[...] = jnp.zeros_like(cnt)

    mix = _dot(a_ref[...], wa_ref[...]) + _dot(b_ref[...], wb_ref[...]) + bo_ref[...]
    x1 = _layer_norm(DN_ALPHA * x_ref[...] + mix, g_ref[...], be_ref[...])
    x1_ref[...] = x1
    x1b_ref[...] = x1.astype(BF16)

    logits = _dot_nt(rw_ref[...], x1, precision=lax.Precision.HIGHEST)
    aff = jax.nn.sigmoid(logits)
    choice = aff + rb_ref[...]
    per_group = N_EXPERTS // N_GROUPS
    sub = lax.broadcasted_iota(jnp.int32, (per_group, tm), 0)
    groups = [choice[g * per_group:(g + 1) * per_group, :] for g in range(N_GROUPS)]

    gscore = []
    for cg in groups:
        m1 = jnp.max(cg, axis=0, keepdims=True)
        i1 = jnp.min(jnp.where(cg == m1, sub, per_group), axis=0, keepdims=True)
        m2 = jnp.max(jnp.where(sub == i1, NEG_INF, cg), axis=0, keepdims=True)
        gscore.append(m1 + m2)

    gsel = [jnp.zeros((1, tm), jnp.bool_) for _ in range(N_GROUPS)]
    for _ in range(TOPK_GROUPS):
        best = functools.reduce(jnp.maximum, gscore)
        takes, _ = _first_hit([gs == best for gs in gscore], jnp.zeros((1, tm), jnp.bool_))
        gsel = [jnp.logical_or(a, t) for a, t in zip(gsel, takes)]
        gscore = [jnp.where(t, NEG_INF, gs) for gs, t in zip(gscore, takes)]

    masked = [jnp.where(gs, cg, NEG_INF) for gs, cg in zip(gsel, groups)]
    eid = [sub + g * per_group for g in range(N_GROUPS)]
    affs = [aff[g * per_group:(g + 1) * per_group, :] for g in range(N_GROUPS)]
    sel = [jnp.zeros((per_group, tm), jnp.bool_) for _ in range(N_GROUPS)]
    picked_e, picked_w = [], []
    for _ in range(TOP_K):
        best = jnp.max(functools.reduce(jnp.maximum, masked), axis=0, keepdims=True)
        cand = [jnp.where(mg == best, ig, N_EXPERTS) for mg, ig in zip(masked, eid)]
        idx = jnp.min(functools.reduce(jnp.minimum, cand), axis=0, keepdims=True)
        onehot = [ig == idx for ig in eid]
        w = functools.reduce(
            jnp.add, [jnp.sum(jnp.where(oh, ag, 0.0), axis=0, keepdims=True)
                      for oh, ag in zip(onehot, affs)])
        picked_e.append(idx)
        picked_w.append(w)
        sel = [jnp.logical_or(sg, oh) for sg, oh in zip(sel, onehot)]
        masked = [jnp.where(oh, NEG_INF, mg) for mg, oh in zip(masked, onehot)]

    wsum = functools.reduce(jnp.add, picked_w)

    sel_f = jnp.concatenate([sg.astype(F32) for sg in sel], axis=0)
    r = lax.broadcasted_iota(jnp.int32, (tm, tm), 0)
    c = lax.broadcasted_iota(jnp.int32, (tm, tm), 1)
    before = _dot(sel_f.astype(BF16), (r < c).astype(BF16)) + cnt[...]
    eall = lax.broadcasted_iota(jnp.int32, (N_EXPERTS, tm), 0)
    for kk in range(TOP_K):
        rank = jnp.sum(jnp.where(eall == picked_e[kk], before, 0.0), axis=0, keepdims=True)
        e_ref[kk:kk + 1, :] = picked_e[kk]
        gate_ref[kk:kk + 1, :] = picked_w[kk] / wsum * ROUTED_SCALE
        rank_ref[kk:kk + 1, :] = rank.astype(jnp.int32)
    pad = SUBLANES - TOP_K
    e_ref[TOP_K:, :] = jnp.zeros((pad, tm), jnp.int32)
    gate_ref[TOP_K:, :] = jnp.zeros((pad, tm), F32)
    rank_ref[TOP_K:, :] = jnp.zeros((pad, tm), jnp.int32)
    cnt[...] = cnt[...] + jnp.sum(sel_f, axis=1, keepdims=True)
    cnt_ref[...] = jnp.broadcast_to(cnt[...], cnt_ref.shape)


def _mix_out_router(a, b, x, w_out, b_out, ln_g, ln_b, router_w, router_bias):
    n_tok = x.shape[0]
    tm = TOKEN_TILE
    half = w_out.shape[0] // 2
    a_cols = a.shape[1] // half
    row = lambda w: pl.BlockSpec((tm, w), lambda i: (i, 0))
    b_spec = (pl.BlockSpec((tm, half), lambda i: (i, 1)) if b is a
              else pl.BlockSpec((tm, half), lambda i: (i, 0)))
    del a_cols
    lane_row = pl.BlockSpec((SUBLANES, tm), lambda i: (0, i))
    w_bf = w_out.astype(BF16)
    return pl.pallas_call(
        _mix_out_router_kernel,
        grid=(n_tok // tm,),
        in_specs=[row(half), b_spec, row(D_MODEL),
                  pl.BlockSpec((half, D_MODEL), lambda i: (0, 0)),
                  pl.BlockSpec((half, D_MODEL), lambda i: (1, 0)),
                  _resident((1, D_MODEL)), _resident((1, D_MODEL)), _resident((1, D_MODEL)),
                  _resident((N_EXPERTS, D_MODEL)), _resident((N_EXPERTS, 1))],
        out_specs=[row(D_MODEL), row(D_MODEL), lane_row, lane_row, lane_row,
                   _resident((N_EXPERTS, LANES))],
        out_shape=[jax.ShapeDtypeStruct((n_tok, D_MODEL), F32),
                   jax.ShapeDtypeStruct((n_tok, D_MODEL), BF16),
                   jax.ShapeDtypeStruct((SUBLANES, n_tok), jnp.int32),
                   jax.ShapeDtypeStruct((SUBLANES, n_tok), F32),
                   jax.ShapeDtypeStruct((SUBLANES, n_tok), jnp.int32),
                   jax.ShapeDtypeStruct((N_EXPERTS, LANES), F32)],
        scratch_shapes=[pltpu.VMEM((N_EXPERTS, 1), F32)],
        compiler_params=_params("arbitrary"),
        name="mix_out_router",
    )(a, b, x, w_bf, w_bf, b_out[None, :], ln_g[None, :], ln_b[None, :],
      router_w.T, router_bias[:, None])


def _expert_ffn_kernel(tile_e_ref, tile_on_ref, xs_ref, wg_ref, wu_ref, wd_ref, ys_ref):
    del tile_e_ref
    on = tile_on_ref[pl.program_id(0)] > 0

    @pl.when(on)
    def _():
        xt = xs_ref[...]
        hid = jax.nn.silu(_dot(xt, wg_ref[...])) * _dot(xt, wu_ref[...])
        ys_ref[...] = _dot(hid.astype(BF16), wd_ref[...])

    @pl.when(jnp.logical_not(on))
    def _():
        ys_ref[...] = jnp.zeros_like(ys_ref)


def _expert_ffn(xs, tile_e, tile_on, w_gate, w_up, w_down):
    rows = xs.shape[0]
    tg = GROUP_TILE
    grid_spec = pltpu.PrefetchScalarGridSpec(
        num_scalar_prefetch=2,
        grid=(rows // tg,),
        in_specs=[pl.BlockSpec((tg, D_MODEL), lambda i, te, on: (i, 0)),
                  pl.BlockSpec((None, D_MODEL, EXPERT_HID), lambda i, te, on: (te[i], 0, 0)),
                  pl.BlockSpec((None, D_MODEL, EXPERT_HID), lambda i, te, on: (te[i], 0, 0)),
                  pl.BlockSpec((None, EXPERT_HID, D_MODEL), lambda i, te, on: (te[i], 0, 0))],
        out_specs=pl.BlockSpec((tg, D_MODEL), lambda i, te, on: (i, 0)),
    )
    return pl.pallas_call(
        _expert_ffn_kernel,
        grid_spec=grid_spec,
        out_shape=jax.ShapeDtypeStruct((rows, D_MODEL), F32),
        compiler_params=_params("arbitrary"),
        name="expert_ffn",
    )(tile_e, tile_on, xs, w_gate, w_up, w_down)


def _moe_out_kernel(x1_ref, routed_ref, sg_ref, su_ref, sd_ref, g_ref, b_ref, o_ref):
    x1 = x1_ref[...]
    xb = x1.astype(BF16)
    hid = jax.nn.silu(_dot(xb, sg_ref[...])) * _dot(xb, su_ref[...])
    shared = _dot(hid.astype(BF16), sd_ref[...])
    o_ref[...] = _layer_norm(DN_ALPHA * x1 + (routed_ref[...] + shared),
                             g_ref[...], b_ref[...])


def _moe_out(x1, routed, sh_gate, sh_up, sh_down, ln_g, ln_b):
    n_tok = x1.shape[0]
    tm = TOKEN_TILE
    row = pl.BlockSpec((tm, D_MODEL), lambda i: (i, 0))
    return pl.pallas_call(
        _moe_out_kernel,
        grid=(n_tok // tm,),
        in_specs=[row, row, _resident((D_MODEL, EXPERT_HID)), _resident((D_MODEL, EXPERT_HID)),
                  _resident((EXPERT_HID, D_MODEL)), _resident((1, D_MODEL)),
                  _resident((1, D_MODEL))],
        out_specs=row,
        out_shape=jax.ShapeDtypeStruct((n_tok, D_MODEL), F32),
        compiler_params=_params("arbitrary"),
        name="moe_out",
    )(x1, routed, sh_gate.astype(BF16), sh_up.astype(BF16), sh_down.astype(BF16),
      ln_g[None, :], ln_b[None, :])


def _moe(x1, x1b, e6, gate6, rank6, counts, exp_w_gate, exp_w_up, exp_w_down,
         sh_w_gate, sh_w_up, sh_w_down, ln_g, ln_b):
    n_tok = x1.shape[0]
    tg = GROUP_TILE
    n_rows = n_tok * TOP_K + N_EXPERTS * tg
    n_tiles = n_rows // tg

    cnt = counts[:, 0].astype(jnp.int32)
    tiles_per_e = (cnt + tg - 1) // tg
    tile_end = jnp.cumsum(tiles_per_e)
    offset = (tile_end - tiles_per_e) * tg
    tile_ids = jnp.arange(n_tiles, dtype=jnp.int32)
    tile_on = (tile_ids < tile_end[-1]).astype(jnp.int32)
    tile_e = jnp.minimum(
        jnp.sum((tile_ids[:, None] >= tile_end[None, :]).astype(jnp.int32), axis=1),
        N_EXPERTS - 1)
    tile_e = jnp.where(tile_on > 0, tile_e, tile_e[jnp.maximum(tile_end[-1] - 1, 0)])

    e_k = e6[:TOP_K]
    pos = jnp.sum(jnp.where(e_k[None] == jnp.arange(N_EXPERTS)[:, None, None],
                            offset[:, None, None], 0), axis=0) + rank6[:TOP_K]

    tok = jnp.broadcast_to(jnp.arange(n_tok, dtype=jnp.int32)[None, :], pos.shape)
    src = jnp.zeros((n_rows,), jnp.int32).at[pos.reshape(-1)].set(tok.reshape(-1))
    xs = jnp.take(x1b, src, axis=0)
    ys = _expert_ffn(xs, tile_e, tile_on, exp_w_gate.astype(BF16), exp_w_up.astype(BF16),
                     exp_w_down.astype(BF16))
    yg = jnp.take(ys, pos.reshape(-1), axis=0).reshape(TOP_K, n_tok, D_MODEL)
    routed = jnp.sum(yg * gate6[:TOP_K, :, None], axis=0)
    return _moe_out(x1, routed, sh_w_gate, sh_w_up, sh_w_down, ln_g, ln_b)


def kernel(x, positions,
           l0_w_in, l0_b_in, l0_conv_w, l0_conv_b, l0_conv_ln_g, l0_conv_ln_b,
           l0_q_norm_g, l0_w_uq, l0_kv_norm_g, l0_w_ukv, l0_w_out, l0_b_out,
           l0_ln1_g, l0_ln1_b,
           l0_router_w, l0_router_bias, l0_exp_w_gate, l0_exp_w_up, l0_exp_w_down,
           l0_sh_w_gate, l0_sh_w_up, l0_sh_w_down, l0_ln2_g, l0_ln2_b,
           l1_w_in, l1_b_in, l1_w_out, l1_b_out, l1_ln1_g, l1_ln1_b,
           l1_router_w, l1_router_bias, l1_exp_w_gate, l1_exp_w_up, l1_exp_w_down,
           l1_sh_w_gate, l1_sh_w_up, l1_sh_w_down, l1_ln2_g, l1_ln2_b):
    bsz, seq, d = x.shape
    n_tok = bsz * seq

    u, q, k, v = _l0_front(x, positions, l0_w_in, l0_b_in, l0_conv_w, l0_conv_b,
                           l0_conv_ln_g, l0_conv_ln_b, l0_q_norm_g, l0_w_uq,
                           l0_kv_norm_g, l0_w_ukv)
    o = _attention(q, k, v, None, packed=False)
    x_flat = x.reshape(n_tok, d)
    routing = _mix_out_router(u.reshape(n_tok, -1), o.reshape(n_tok, -1), x_flat,
                              l0_w_out, l0_b_out, l0_ln1_g, l0_ln1_b,
                              l0_router_w, l0_router_bias)
    x2 = _moe(*routing, l0_exp_w_gate, l0_exp_w_up, l0_exp_w_down,
              l0_sh_w_gate, l0_sh_w_up, l0_sh_w_down, l0_ln2_g, l0_ln2_b)

    q, k, v, cneg = _l1_front(x2.reshape(bsz, seq, d), l1_w_in, l1_b_in)
    o = _attention(q, k, v, cneg[:, :, None, :], packed=True).reshape(n_tok, -1)
    routing = _mix_out_router(o, o, x2, l1_w_out, l1_b_out, l1_ln1_g, l1_ln1_b,
                              l1_router_w, l1_router_bias)
    x3 = _moe(*routing, l1_exp_w_gate, l1_exp_w_up, l1_exp_w_down,
              l1_sh_w_gate, l1_sh_w_up, l1_sh_w_down, l1_ln2_g, l1_ln2_b)
    return x3.reshape(bsz, seq, d)
```

```python
import functools

import jax
import jax.numpy as jnp
from jax import lax
from jax.experimental import pallas as pl
from jax.experimental.pallas import tpu as pltpu
from jax.experimental.pallas import tpu_sc as plsc

D_MODEL = 1024
DEPTH = 2
DN_ALPHA = (2 * DEPTH) ** 0.25
LN_EPS = 1e-5
RMS_EPS = 1e-6

CONV_CH = 512
CONV_WIDTH = 31
MLA_HEADS = 8
QK_NOPE = 64
QK_ROPE = 32
V_DIM = 64
Q_LORA = 256
KV_LORA = 128
ROPE_THETA = 10000.0
FOX_HEADS = 16
FOX_HD = 64
N_EXPERTS = 64
TOP_K = 6
N_GROUPS = 8
TOPK_GROUPS = 4
EXPERT_HID = 256
ROUTED_SCALE = 2.5

LANES = 128
SUBLANES = 8
VMEM_LIMIT_BYTES = 56 * 1024 * 1024

SEQ_TILE = 512
TOKEN_TILE = 512
ATTN_TILE = 512
GROUP_TILE = 512
CONV_ROWS = 32
CONV_HALO = 32
PAD_HEAD = 128

PACK_COLS = D_MODEL // 2
PACK_BLOCKS = PACK_COLS // LANES
SC_CORES = 2
SC_SUBCORES = 16
SC_WORKERS = SC_CORES * SC_SUBCORES
SC_WINDOW = 128

BF16 = jnp.bfloat16
F32 = jnp.float32
NEG_INF = float("-inf")


def _params(*semantics):
    return pltpu.CompilerParams(dimension_semantics=semantics,
                                vmem_limit_bytes=VMEM_LIMIT_BYTES)


def _resident(shape):
    nd = len(shape)
    return pl.BlockSpec(shape, lambda *_: (0,) * nd)


def _dot(a, b):
    return jnp.dot(a, b, preferred_element_type=F32)


def _dot_nt(a, b, precision=None):
    return lax.dot_general(a, b, (((1,), (1,)), ((), ())),
                           precision=precision, preferred_element_type=F32)


def _layer_norm(x, g, b):
    mu = jnp.mean(x, axis=-1, keepdims=True)
    xc = x - mu
    var = jnp.mean(xc * xc, axis=-1, keepdims=True)
    return xc * lax.rsqrt(var + LN_EPS) * g + b


def _rms_norm(x, g):
    return x * lax.rsqrt(jnp.mean(x * x, axis=-1, keepdims=True) + RMS_EPS) * g


def _pack_rows(x):
    lo = lax.bitcast_convert_type(x[:, :PACK_COLS].astype(BF16).astype(F32), jnp.uint32)
    hi = lax.bitcast_convert_type(x[:, PACK_COLS:].astype(BF16).astype(F32), jnp.uint32)
    word = (lo >> 16) | (hi & jnp.uint32(0xFFFF0000))
    return lax.bitcast_convert_type(word, jnp.int32)


def _unpack_rows(words):
    u = lax.bitcast_convert_type(words, jnp.uint32)
    lo = lax.bitcast_convert_type(u << 16, F32)
    hi = lax.bitcast_convert_type(u & jnp.uint32(0xFFFF0000), F32)
    return lo, hi


def _store_packed(ref, words):
    groups = words.shape[0] // SUBLANES
    for cb in range(PACK_BLOCKS):
        ref[:, cb, :, :] = words[:, cb * LANES:(cb + 1) * LANES].reshape(groups, SUBLANES, LANES)


def _load_packed(ref):
    rows = ref.shape[0] * SUBLANES
    return jnp.concatenate(
        [ref[:, cb, :, :].reshape(rows, LANES) for cb in range(PACK_BLOCKS)], axis=1)


L0_A = 0
L0_G = CONV_CH
L0_Q = 2 * CONV_CH
L0_KV = L0_Q + Q_LORA
L0_KR = L0_KV + KV_LORA
L0_KRR = L0_KR + PAD_HEAD
L0_COLS = L0_KRR + PAD_HEAD


def _l0_front_kernel(x_ref, pos_ref, w_in_ref, b_in_ref, conv_w_ref, conv_b_ref,
                     cln_g_ref, cln_b_ref, qn_g_ref, wq_ref, wqr_ref, kvn_g_ref,
                     wk_ref, wv_ref, invf_ref,
                     u_ref, q_ref, k_ref, v_ref, ubuf):
    ts = x_ref.shape[0]

    @pl.when(pl.program_id(1) == 0)
    def _():
        ubuf[0:CONV_HALO, :] = jnp.zeros((CONV_HALO, CONV_CH), F32)

    h = _dot(x_ref[...].astype(BF16), w_in_ref[...]) + b_in_ref[...]

    ubuf[CONV_HALO:CONV_HALO + ts, :] = (
        h[:, L0_A:L0_A + CONV_CH] * jax.nn.sigmoid(h[:, L0_G:L0_G + CONV_CH]))
    first_tap = CONV_HALO - (CONV_WIDTH - 1)

    def conv_chunk(c, carry):
        base = pl.multiple_of(c * CONV_ROWS, CONV_ROWS)
        win = ubuf[pl.ds(base, CONV_ROWS + CONV_HALO), :]
        acc = jnp.broadcast_to(conv_b_ref[...], (CONV_ROWS, CONV_CH))
        for j in range(CONV_WIDTH):
            acc = acc + win[first_tap + j:first_tap + j + CONV_ROWS, :] * conv_w_ref[j:j + 1, :]
        y = _layer_norm(acc, cln_g_ref[...], cln_b_ref[...])
        u_ref[pl.ds(base, CONV_ROWS), :] = (y * jax.nn.sigmoid(y)).astype(BF16)
        return carry

    lax.fori_loop(0, ts // CONV_ROWS, conv_chunk, 0)
    ubuf[0:CONV_HALO, :] = ubuf[ts:ts + CONV_HALO, :]

    ang = pos_ref[...].astype(F32) * invf_ref[...]
    cos = jnp.cos(ang)
    sin = jnp.sin(ang)
    scale = (QK_NOPE + QK_ROPE) ** -0.5

    qn = _rms_norm(h[:, L0_Q:L0_Q + Q_LORA], qn_g_ref[...]).astype(BF16)
    q = _dot(qn, wq_ref[...])
    q_rot = _dot(qn, wqr_ref[...])
    cos_s = cos * scale
    sin_s = sin * scale
    for hh in range(MLA_HEADS):
        blk = slice(hh * PAD_HEAD, (hh + 1) * PAD_HEAD)
        q_ref[:, blk] = (q[:, blk] * cos_s + q_rot[:, blk] * sin_s).astype(BF16)

    kvn = _rms_norm(h[:, L0_KV:L0_KV + KV_LORA], kvn_g_ref[...]).astype(BF16)
    k_nope = _dot(kvn, wk_ref[...])
    k_pe = h[:, L0_KR:L0_KR + PAD_HEAD] * cos + h[:, L0_KRR:L0_KRR + PAD_HEAD] * sin
    for hh in range(MLA_HEADS):
        blk = slice(hh * PAD_HEAD, (hh + 1) * PAD_HEAD)
        k_ref[:, blk] = (k_nope[:, blk] + k_pe).astype(BF16)
    v_ref[...] = _dot(kvn, wv_ref[...]).astype(BF16)


def _rope_rotate_cols(w):
    half = QK_ROPE // 2
    return jnp.concatenate([-w[..., half:], w[..., :half]], axis=-1)


def _l0_front(x, positions, w_in, b_in, conv_w, conv_b, cln_g, cln_b,
              qn_g, w_uq, kvn_g, w_ukv):
    bsz, seq, _ = x.shape
    pad_lo = jnp.zeros((D_MODEL, QK_NOPE), F32)
    pad_hi = jnp.zeros((D_MODEL, PAD_HEAD - QK_NOPE - QK_ROPE), F32)
    i3 = L0_KR
    w_kr = w_in[:, i3:i3 + QK_ROPE]
    w_in_p = jnp.concatenate(
        [w_in[:, :i3], pad_lo, w_kr, pad_hi, pad_lo, _rope_rotate_cols(w_kr), pad_hi],
        axis=1).astype(BF16)
    b_kr = b_in[i3:i3 + QK_ROPE]
    zlo = jnp.zeros((QK_NOPE,), F32)
    zhi = jnp.zeros((PAD_HEAD - QK_NOPE - QK_ROPE,), F32)
    b_in_p = jnp.concatenate(
        [b_in[:i3], zlo, b_kr, zhi, zlo, _rope_rotate_cols(b_kr), zhi])[None, :]

    dq = QK_NOPE + QK_ROPE
    wq3 = w_uq.reshape(Q_LORA, MLA_HEADS, dq)
    zq = jnp.zeros((Q_LORA, MLA_HEADS, PAD_HEAD - dq), F32)
    wq_p = jnp.concatenate([wq3, zq], axis=-1).reshape(Q_LORA, MLA_HEADS * PAD_HEAD)
    wqr_p = jnp.concatenate(
        [jnp.zeros((Q_LORA, MLA_HEADS, QK_NOPE), F32),
         _rope_rotate_cols(wq3[..., QK_NOPE:]), zq], axis=-1
    ).reshape(Q_LORA, MLA_HEADS * PAD_HEAD)
    wkv3 = w_ukv.reshape(KV_LORA, MLA_HEADS, QK_NOPE + V_DIM)
    wk_p = jnp.concatenate(
        [wkv3[..., :QK_NOPE], jnp.zeros((KV_LORA, MLA_HEADS, PAD_HEAD - QK_NOPE), F32)],
        axis=-1).reshape(KV_LORA, MLA_HEADS * PAD_HEAD)
    wv = wkv3[..., QK_NOPE:].reshape(KV_LORA, MLA_HEADS * V_DIM)

    inv_freq = 1.0 / (ROPE_THETA ** (jnp.arange(0, QK_ROPE, 2, dtype=F32) / QK_ROPE))
    invf = jnp.concatenate([jnp.zeros((QK_NOPE,), F32), inv_freq, inv_freq,
                            jnp.zeros((PAD_HEAD - QK_NOPE - QK_ROPE,), F32)])[None, :]

    ts = SEQ_TILE
    row = lambda w: pl.BlockSpec((None, ts, w), lambda b, s: (b, s, 0))
    qk_w = MLA_HEADS * PAD_HEAD
    v_w = MLA_HEADS * V_DIM
    return pl.pallas_call(
        _l0_front_kernel,
        grid=(bsz, seq // ts),
        in_specs=[row(D_MODEL), row(1),
                  _resident((D_MODEL, L0_COLS)), _resident((1, L0_COLS)),
                  _resident((CONV_WIDTH, CONV_CH)), _resident((1, CONV_CH)),
                  _resident((1, CONV_CH)), _resident((1, CONV_CH)),
                  _resident((1, Q_LORA)), _resident((Q_LORA, qk_w)),
                  _resident((Q_LORA, qk_w)), _resident((1, KV_LORA)),
                  _resident((KV_LORA, qk_w)), _resident((KV_LORA, v_w)),
                  _resident((1, PAD_HEAD))],
        out_specs=[row(CONV_CH), row(qk_w), row(qk_w), row(v_w)],
        out_shape=[jax.ShapeDtypeStruct((bsz, seq, CONV_CH), BF16),
                   jax.ShapeDtypeStruct((bsz, seq, qk_w), BF16),
                   jax.ShapeDtypeStruct((bsz, seq, qk_w), BF16),
                   jax.ShapeDtypeStruct((bsz, seq, v_w), BF16)],
        scratch_shapes=[pltpu.VMEM((ts + CONV_HALO, CONV_CH), F32)],
        compiler_params=_params("arbitrary", "arbitrary"),
        name="l0_front",
    )(x, positions[..., None], w_in_p, b_in_p, conv_w, conv_b[None, :],
      cln_g[None, :], cln_b[None, :], qn_g[None, :], wq_p.astype(BF16),
      wqr_p.astype(BF16), kvn_g[None, :], wk_p.astype(BF16), wv.astype(BF16), invf)


def _l1_front_kernel(x_ref, w_ref, b_ref, wf_ref, bf_ref, q_ref, k_ref, v_ref,
                     cneg_ref, carry):
    ts = x_ref.shape[0]

    @pl.when(pl.program_id(1) == 0)
    def _():
        carry[...] = jnp.zeros_like(carry)

    xb = x_ref[...].astype(BF16)
    mix = FOX_HEADS * FOX_HD
    scale = FOX_HD ** -0.5
    q_ref[...] = ((_dot(xb, w_ref[:, 0:mix]) + b_ref[:, 0:mix]) * scale).astype(BF16)
    k_ref[...] = (_dot(xb, w_ref[:, mix:2 * mix]) + b_ref[:, mix:2 * mix]).astype(BF16)
    v_ref[...] = (_dot(xb, w_ref[:, 2 * mix:3 * mix]) + b_ref[:, 2 * mix:3 * mix]).astype(BF16)

    log_f = jax.nn.log_sigmoid(_dot_nt(wf_ref[...], xb) + bf_ref[...])
    r = lax.broadcasted_iota(jnp.int32, (ts, ts), 0)
    c = lax.broadcasted_iota(jnp.int32, (ts, ts), 1)
    upper = (r <= c).astype(F32)
    csum = jnp.dot(log_f, upper, precision=lax.Precision.HIGHEST,
                   preferred_element_type=F32) + carry[...]
    cneg_ref[...] = -csum
    carry[...] = carry[...] + jnp.sum(log_f, axis=1, keepdims=True)


def _l1_front(x, w_in, b_in):
    bsz, seq, _ = x.shape
    mix = FOX_HEADS * FOX_HD
    ts = SEQ_TILE
    row = lambda w: pl.BlockSpec((None, ts, w), lambda b, s: (b, s, 0))
    w_qkv = w_in[:, :3 * mix].astype(BF16)
    b_qkv = b_in[None, :3 * mix]
    wf_t = w_in[:, 3 * mix:].T.astype(BF16)
    bf_t = b_in[3 * mix:][:, None]
    return pl.pallas_call(
        _l1_front_kernel,
        grid=(bsz, seq // ts),
        in_specs=[row(D_MODEL), _resident((D_MODEL, 3 * mix)), _resident((1, 3 * mix)),
                  _resident((FOX_HEADS, D_MODEL)), _resident((FOX_HEADS, 1))],
        out_specs=[row(mix), row(mix), row(mix),
                   pl.BlockSpec((None, FOX_HEADS, ts), lambda b, s: (b, 0, s))],
        out_shape=[jax.ShapeDtypeStruct((bsz, seq, mix), BF16)] * 3
        + [jax.ShapeDtypeStruct((bsz, FOX_HEADS, seq), F32)],
        scratch_shapes=[pltpu.VMEM((FOX_HEADS, 1), F32)],
        compiler_params=_params("arbitrary", "arbitrary"),
        name="l1_front",
    )(x, w_qkv, b_qkv, wf_t, bf_t)


def _attn_kernel(*refs, packed, has_bias):
    if has_bias:
        q_ref, k_ref, v_ref, cneg_ref, o_ref = refs
    else:
        q_ref, k_ref, v_ref, o_ref = refs
        cneg_ref = None
    tq = q_ref.shape[0]
    tk = tq
    qi = pl.program_id(2)
    q2 = q_ref[...]
    half = LANES // 2
    if packed:
        lane = lax.broadcasted_iota(jnp.int32, q2.shape, 1)
        zero = jnp.zeros_like(q2)
        qs = (jnp.where(lane < half, q2, zero), jnp.where(lane >= half, q2, zero))
    else:
        qs = (q2[:, :PAD_HEAD], q2[:, PAD_HEAD:])

    def k_block(kj, head):
        rows = pl.ds(pl.multiple_of(kj * tk, tk), tk)
        if packed:
            return k_ref[rows, :]
        return k_ref[rows, head * PAD_HEAD:(head + 1) * PAD_HEAD]

    def step(kj, carry, masked):
        rows = pl.ds(pl.multiple_of(kj * tk, tk), tk)
        vblk = v_ref[rows, :]
        out = []
        for head in range(2):
            m, l, acc = carry[head]
            s = _dot_nt(qs[head], k_block(kj, head))
            if has_bias:
                s = s + cneg_ref[head, :, rows]
            if masked:
                r = lax.broadcasted_iota(jnp.int32, s.shape, 0)
                c = lax.broadcasted_iota(jnp.int32, s.shape, 1)
                s = jnp.where(r >= c, s, NEG_INF)
            m_new = jnp.maximum(m, jnp.max(s, axis=-1, keepdims=True))
            alpha = jnp.exp(m - m_new)
            p = jnp.exp(s - m_new)
            l = alpha * l + jnp.sum(p, axis=-1, keepdims=True)
            acc = alpha * acc + _dot(p.astype(BF16), vblk)
            out.append((m_new, l, acc))
        return tuple(out)

    init = tuple((jnp.full((tq, 1), NEG_INF, F32), jnp.zeros((tq, 1), F32),
                  jnp.zeros((tq, LANES), F32)) for _ in range(2))
    carry = lax.fori_loop(0, qi, lambda kj, c: step(kj, c, False), init)
    (_, l_a, acc_a), (_, l_b, acc_b) = step(qi, carry, True)
    lane_o = lax.broadcasted_iota(jnp.int32, (tq, LANES), 1)
    o_ref[...] = jnp.where(lane_o < half, acc_a / l_a, acc_b / l_b).astype(o_ref.dtype)


def _attention(q, k, v, cneg, packed):
    bsz, seq, v_w = v.shape
    pairs = v_w // LANES
    qk_w = q.shape[-1] // pairs
    tq = ATTN_TILE
    in_specs = [pl.BlockSpec((None, tq, qk_w), lambda b, p, i: (b, i, p)),
                pl.BlockSpec((None, seq, qk_w), lambda b, p, i: (b, 0, p)),
                pl.BlockSpec((None, seq, LANES), lambda b, p, i: (b, 0, p))]
    args = [q, k, v]
    if cneg is not None:
        in_specs.append(pl.BlockSpec((None, 2, 1, seq), lambda b, p, i: (b, p, 0, 0)))
        args.append(cneg)
    return pl.pallas_call(
        functools.partial(_attn_kernel, packed=packed, has_bias=cneg is not None),
        grid=(bsz, pairs, seq // tq),
        in_specs=in_specs,
        out_specs=pl.BlockSpec((None, tq, LANES), lambda b, p, i: (b, i, p)),
        out_shape=jax.ShapeDtypeStruct((bsz, seq, v_w), BF16),
        compiler_params=_params("arbitrary", "arbitrary", "arbitrary"),
        name="attn_packed" if packed else "attn_padded",
    )(*args)


def _first_hit(hits, found):
    out = []
    for hcur in hits:
        take = jnp.logical_and(hcur, jnp.logical_not(found))
        found = jnp.logical_or(found, take)
        out.append(take)
    return out, found


def _mix_out_router_kernel(a_ref, b_ref, x_ref, wa_ref, wb_ref, bo_ref, g_ref, be_ref,
                           rw_ref, rb_ref,
                           x1_ref, x1p_ref, e_ref, gate_ref, rank_ref, cnt_ref, cnt, gate_t):
    tm = x_ref.shape[0]

    @pl.when(pl.program_id(0) == 0)
    def _():
        cnt[...] = jnp.zeros_like(cnt)
        gate_t[...] = jnp.zeros_like(gate_t)

    mix = _dot(a_ref[...], wa_ref[...]) + _dot(b_ref[...], wb_ref[...]) + bo_ref[...]
    x1 = _layer_norm(DN_ALPHA * x_ref[...] + mix, g_ref[...], be_ref[...])
    x1_ref[...] = x1
    _store_packed(x1p_ref, _pack_rows(x1))

    logits = _dot_nt(rw_ref[...], x1, precision=lax.Precision.HIGHEST)
    aff = jax.nn.sigmoid(logits)
    choice = aff + rb_ref[...]
    per_group = N_EXPERTS // N_GROUPS
    sub = lax.broadcasted_iota(jnp.int32, (per_group, tm), 0)
    groups = [choice[g * per_group:(g + 1) * per_group, :] for g in range(N_GROUPS)]

    gscore = []
    for cg in groups:
        m1 = jnp.max(cg, axis=0, keepdims=True)
        i1 = jnp.min(jnp.where(cg == m1, sub, per_group), axis=0, keepdims=True)
        m2 = jnp.max(jnp.where(sub == i1, NEG_INF, cg), axis=0, keepdims=True)
        gscore.append(m1 + m2)

    gsel = [jnp.zeros((1, tm), jnp.bool_) for _ in range(N_GROUPS)]
    for _ in range(TOPK_GROUPS):
        best = functools.reduce(jnp.maximum, gscore)
        takes, _ = _first_hit([gs == best for gs in gscore], jnp.zeros((1, tm), jnp.bool_))
        gsel = [jnp.logical_or(a, t) for a, t in zip(gsel, takes)]
        gscore = [jnp.where(t, NEG_INF, gs) for gs, t in zip(gscore, takes)]

    masked = [jnp.where(gs, cg, NEG_INF) for gs, cg in zip(gsel, groups)]
    eid = [sub + g * per_group for g in range(N_GROUPS)]
    affs = [aff[g * per_group:(g + 1) * per_group, :] for g in range(N_GROUPS)]
    sel = [jnp.zeros((per_group, tm), jnp.bool_) for _ in range(N_GROUPS)]
    picked_e, picked_w = [], []
    for _ in range(TOP_K):
        best = jnp.max(functools.reduce(jnp.maximum, masked), axis=0, keepdims=True)
        cand = [jnp.where(mg == best, ig, N_EXPERTS) for mg, ig in zip(masked, eid)]
        idx = jnp.min(functools.reduce(jnp.minimum, cand), axis=0, keepdims=True)
        onehot = [ig == idx for ig in eid]
        w = functools.reduce(
            jnp.add, [jnp.sum(jnp.where(oh, ag, 0.0), axis=0, keepdims=True)
                      for oh, ag in zip(onehot, affs)])
        picked_e.append(idx)
        picked_w.append(w)
        sel = [jnp.logical_or(sg, oh) for sg, oh in zip(sel, onehot)]
        masked = [jnp.where(oh, NEG_INF, mg) for mg, oh in zip(masked, onehot)]

    wsum = functools.reduce(jnp.add, picked_w)

    sel_f = jnp.concatenate([sg.astype(F32) for sg in sel], axis=0)
    r = lax.broadcasted_iota(jnp.int32, (tm, tm), 0)
    c = lax.broadcasted_iota(jnp.int32, (tm, tm), 1)
    before = _dot(sel_f.astype(BF16), (r < c).astype(BF16)) + cnt[...]
    eall = lax.broadcasted_iota(jnp.int32, (N_EXPERTS, tm), 0)
    for kk in range(TOP_K):
        rank = jnp.sum(jnp.where(eall == picked_e[kk], before, 0.0), axis=0, keepdims=True)
        e_ref[kk:kk + 1, :] = picked_e[kk]
        gate_t[kk:kk + 1, :] = picked_w[kk] / wsum * ROUTED_SCALE
        rank_ref[kk:kk + 1, :] = rank.astype(jnp.int32)
    pad = SUBLANES - TOP_K
    e_ref[TOP_K:, :] = jnp.zeros((pad, tm), jnp.int32)
    rank_ref[TOP_K:, :] = jnp.zeros((pad, tm), jnp.int32)
    gate_ref[...] = gate_t[...].T
    cnt[...] = cnt[...] + jnp.sum(sel_f, axis=1, keepdims=True)
    cnt_ref[...] = jnp.broadcast_to(cnt[...], cnt_ref.shape)


def _mix_out_router(a, b, x, w_out, b_out, ln_g, ln_b, router_w, router_bias):
    n_tok = x.shape[0]
    tm = TOKEN_TILE
    half = w_out.shape[0] // 2
    row = lambda w: pl.BlockSpec((tm, w), lambda i: (i, 0))
    b_spec = (pl.BlockSpec((tm, half), lambda i: (i, 1)) if b is a
              else pl.BlockSpec((tm, half), lambda i: (i, 0)))
    lane_row = pl.BlockSpec((SUBLANES, tm), lambda i: (0, i))
    packed = pl.BlockSpec((tm // SUBLANES, PACK_BLOCKS, SUBLANES, LANES),
                          lambda i: (i, 0, 0, 0))
    w_bf = w_out.astype(BF16)
    return pl.pallas_call(
        _mix_out_router_kernel,
        grid=(n_tok // tm,),
        in_specs=[row(half), b_spec, row(D_MODEL),
                  pl.BlockSpec((half, D_MODEL), lambda i: (0, 0)),
                  pl.BlockSpec((half, D_MODEL), lambda i: (1, 0)),
                  _resident((1, D_MODEL)), _resident((1, D_MODEL)), _resident((1, D_MODEL)),
                  _resident((N_EXPERTS, D_MODEL)), _resident((N_EXPERTS, 1))],
        out_specs=[row(D_MODEL), packed, lane_row, row(LANES), lane_row,
                   _resident((N_EXPERTS, LANES))],
        out_shape=[jax.ShapeDtypeStruct((n_tok, D_MODEL), F32),
                   jax.ShapeDtypeStruct((n_tok // SUBLANES, PACK_BLOCKS, SUBLANES, LANES),
                                        jnp.int32),
                   jax.ShapeDtypeStruct((SUBLANES, n_tok), jnp.int32),
                   jax.ShapeDtypeStruct((n_tok, LANES), F32),
                   jax.ShapeDtypeStruct((SUBLANES, n_tok), jnp.int32),
                   jax.ShapeDtypeStruct((N_EXPERTS, LANES), F32)],
        scratch_shapes=[pltpu.VMEM((N_EXPERTS, 1), F32), pltpu.VMEM((LANES, tm), F32)],
        compiler_params=_params("arbitrary"),
        name="mix_out_router",
    )(a, b, x, w_bf, w_bf, b_out[None, :], ln_g[None, :], ln_b[None, :],
      router_w.T, router_bias[:, None])


def _expert_ffn_kernel(tile_e_ref, tile_on_ref, xs_ref, wg_ref, wu_ref, wd_ref, ys_ref):
    del tile_e_ref
    on = tile_on_ref[pl.program_id(0)] > 0

    @pl.when(on)
    def _():
        lo, hi = _unpack_rows(_load_packed(xs_ref))
        lo = lo.astype(BF16)
        hi = hi.astype(BF16)
        wg = wg_ref[...]
        wu = wu_ref[...]
        gate = _dot(lo, wg[:PACK_COLS]) + _dot(hi, wg[PACK_COLS:])
        up = _dot(lo, wu[:PACK_COLS]) + _dot(hi, wu[PACK_COLS:])
        hid = (jax.nn.silu(gate) * up).astype(BF16)
        _store_packed(ys_ref, _pack_rows(_dot(hid, wd_ref[...])))

    @pl.when(jnp.logical_not(on))
    def _():
        ys_ref[...] = jnp.zeros_like(ys_ref)


def _expert_ffn(xs, tile_e, tile_on, w_gate, w_up, w_down):
    tg = GROUP_TILE
    packed = pl.BlockSpec((tg // SUBLANES, PACK_BLOCKS, SUBLANES, LANES),
                          lambda i, te, on: (i, 0, 0, 0))
    grid_spec = pltpu.PrefetchScalarGridSpec(
        num_scalar_prefetch=2,
        grid=(xs.shape[0] * SUBLANES // tg,),
        in_specs=[packed,
                  pl.BlockSpec((None, D_MODEL, EXPERT_HID), lambda i, te, on: (te[i], 0, 0)),
                  pl.BlockSpec((None, D_MODEL, EXPERT_HID), lambda i, te, on: (te[i], 0, 0)),
                  pl.BlockSpec((None, EXPERT_HID, D_MODEL), lambda i, te, on: (te[i], 0, 0))],
        out_specs=packed,
    )
    return pl.pallas_call(
        _expert_ffn_kernel,
        grid_spec=grid_spec,
        out_shape=jax.ShapeDtypeStruct(xs.shape, jnp.int32),
        compiler_params=_params("arbitrary"),
        name="expert_ffn",
    )(tile_e, tile_on, xs, w_gate, w_up, w_down)


def _moe_out_kernel(x1_ref, yg_ref, gate_ref, sg_ref, su_ref, sd_ref, g_ref, b_ref, o_ref):
    x1 = x1_ref[...]
    xb = x1.astype(BF16)
    hid = jax.nn.silu(_dot(xb, sg_ref[...])) * _dot(xb, su_ref[...])
    shared = _dot(hid.astype(BF16), sd_ref[...])
    gates = gate_ref[...]
    lo_acc = shared[:, :PACK_COLS]
    hi_acc = shared[:, PACK_COLS:]
    for kk in range(TOP_K):
        lo, hi = _unpack_rows(_load_packed(yg_ref.at[kk]))
        w = gates[:, kk:kk + 1]
        lo_acc = lo_acc + w * lo
        hi_acc = hi_acc + w * hi
    moe = jnp.concatenate([lo_acc, hi_acc], axis=1)
    o_ref[...] = _layer_norm(DN_ALPHA * x1 + moe, g_ref[...], b_ref[...])


def _moe_out(x1, yg, gates, sh_gate, sh_up, sh_down, ln_g, ln_b):
    n_tok = x1.shape[0]
    tm = TOKEN_TILE
    row = lambda w: pl.BlockSpec((tm, w), lambda i: (i, 0))
    return pl.pallas_call(
        _moe_out_kernel,
        grid=(n_tok // tm,),
        in_specs=[row(D_MODEL),
                  pl.BlockSpec((TOP_K, tm // SUBLANES, PACK_BLOCKS, SUBLANES, LANES),
                               lambda i: (0, i, 0, 0, 0)),
                  row(LANES),
                  _resident((D_MODEL, EXPERT_HID)), _resident((D_MODEL, EXPERT_HID)),
                  _resident((EXPERT_HID, D_MODEL)), _resident((1, D_MODEL)),
                  _resident((1, D_MODEL))],
        out_specs=row(D_MODEL),
        out_shape=jax.ShapeDtypeStruct((n_tok, D_MODEL), F32),
        compiler_params=_params("arbitrary"),
        name="moe_out",
    )(x1, yg, gates, sh_gate.astype(BF16), sh_up.astype(BF16), sh_down.astype(BF16),
      ln_g[None, :], ln_b[None, :])


def _slot_kernel(offset_ref, e_ref, rank_ref, pos_ref):
    e = e_ref[...]
    base = jnp.zeros(e.shape, jnp.int32)
    for ex in range(N_EXPERTS):
        base = jnp.where(e == ex, offset_ref[ex], base)
    pos_ref[...] = base + rank_ref[...]


def _slots(offset, e6, rank6):
    n_tok = e6.shape[1]
    tl = 4096
    lane_row = pl.BlockSpec((SUBLANES, tl), lambda i, off: (0, i))
    return pl.pallas_call(
        _slot_kernel,
        grid_spec=pltpu.PrefetchScalarGridSpec(
            num_scalar_prefetch=1, grid=(n_tok // tl,),
            in_specs=[lane_row, lane_row], out_specs=lane_row),
        out_shape=jax.ShapeDtypeStruct((SUBLANES, n_tok), jnp.int32),
        compiler_params=_params("arbitrary"),
        name="slots",
    )(offset, e6, rank6)


def _sc_mesh():
    return plsc.VectorSubcoreMesh(core_axis_name="c", subcore_axis_name="s")


def _sc_worker():
    return lax.axis_index("s") * SC_CORES + lax.axis_index("c")


def _sc_dispatch(x_sub, idx, n_out_sub):
    windows = x_sub.shape[0] // (SC_WORKERS * SC_WINDOW)

    @functools.partial(
        pl.kernel, mesh=_sc_mesh(),
        out_type=jax.ShapeDtypeStruct((n_out_sub, LANES), jnp.int32),
        scratch_types=[pltpu.VMEM((SC_WINDOW, LANES), jnp.int32),
                       pltpu.VMEM((TOP_K, SC_WINDOW), jnp.int32)],
        name="sc_dispatch")
    def run(x_hbm, idx_hbm, out_hbm, rows_v, idx_v):
        wid = _sc_worker()

        @pl.loop(0, windows)
        def _(c):
            base = (wid * windows + c) * SC_WINDOW
            pltpu.sync_copy(x_hbm.at[pl.ds(base, SC_WINDOW)], rows_v)
            pltpu.sync_copy(idx_hbm.at[wid, c], idx_v)
            for kk in range(TOP_K):
                pltpu.sync_copy(rows_v, out_hbm.at[idx_v.at[kk]])

    return run(x_sub, idx)


def _sc_combine(y_sub, idx, n_tok_sub):
    windows = n_tok_sub // (SC_WORKERS * SC_WINDOW)

    @functools.partial(
        pl.kernel, mesh=_sc_mesh(),
        out_type=jax.ShapeDtypeStruct((TOP_K, n_tok_sub, LANES), jnp.int32),
        scratch_types=[pltpu.VMEM((SC_WINDOW, LANES), jnp.int32),
                       pltpu.VMEM((TOP_K, SC_WINDOW), jnp.int32)],
        name="sc_combine")
    def run(y_hbm, idx_hbm, out_hbm, rows_v, idx_v):
        wid = _sc_worker()

        @pl.loop(0, windows)
        def _(c):
            base = (wid * windows + c) * SC_WINDOW
            pltpu.sync_copy(idx_hbm.at[wid, c], idx_v)
            for kk in range(TOP_K):
                pltpu.sync_copy(y_hbm.at[idx_v.at[kk]], rows_v)
                pltpu.sync_copy(rows_v, out_hbm.at[kk, pl.ds(base, SC_WINDOW)])

    return run(y_sub, idx)


def _moe(x1, x1p, e6, gates, rank6, counts, exp_w_gate, exp_w_up, exp_w_down,
         sh_w_gate, sh_w_up, sh_w_down, ln_g, ln_b):
    n_tok = x1.shape[0]
    tg = GROUP_TILE
    n_rows = n_tok * TOP_K + N_EXPERTS * tg
    n_tiles = n_rows // tg

    cnt = counts[:, 0].astype(jnp.int32)
    tiles_per_e = (cnt + tg - 1) // tg
    tile_end = jnp.cumsum(tiles_per_e)
    offset = (tile_end - tiles_per_e) * tg
    tile_ids = jnp.arange(n_tiles, dtype=jnp.int32)
    tile_on = (tile_ids < tile_end[-1]).astype(jnp.int32)
    tile_e = jnp.minimum(
        jnp.sum((tile_ids[:, None] >= tile_end[None, :]).astype(jnp.int32), axis=1),
        N_EXPERTS - 1)
    tile_e = jnp.where(tile_on > 0, tile_e, tile_e[jnp.maximum(tile_end[-1] - 1, 0)])

    pos = _slots(offset, e6, rank6)[:TOP_K]

    group = PACK_BLOCKS * SUBLANES
    sub0 = (pos // SUBLANES) * group + pos % SUBLANES
    idx = (sub0.reshape(TOP_K, n_tok // SUBLANES, 1, SUBLANES)
           + (jnp.arange(PACK_BLOCKS, dtype=jnp.int32) * SUBLANES)[None, None, :, None])
    n_sub = n_tok * PACK_BLOCKS
    windows = n_sub // (SC_WORKERS * SC_WINDOW)
    idx = idx.reshape(TOP_K, SC_WORKERS, windows, SC_WINDOW).transpose(1, 2, 0, 3)

    xs = _sc_dispatch(x1p.reshape(n_sub, LANES), idx, n_rows * PACK_BLOCKS)
    ys = _expert_ffn(xs.reshape(n_rows // SUBLANES, PACK_BLOCKS, SUBLANES, LANES),
                     tile_e, tile_on, exp_w_gate.astype(BF16), exp_w_up.astype(BF16),
                     exp_w_down.astype(BF16))
    yg = _sc_combine(ys.reshape(n_rows * PACK_BLOCKS, LANES), idx, n_sub)
    yg = yg.reshape(TOP_K, n_tok // SUBLANES, PACK_BLOCKS, SUBLANES, LANES)
    return _moe_out(x1, yg, gates, sh_w_gate, sh_w_up, sh_w_down, ln_g, ln_b)


def kernel(x, positions,
           l0_w_in, l0_b_in, l0_conv_w, l0_conv_b, l0_conv_ln_g, l0_conv_ln_b,
           l0_q_norm_g, l0_w_uq, l0_kv_norm_g, l0_w_ukv, l0_w_out, l0_b_out,
           l0_ln1_g, l0_ln1_b,
           l0_router_w, l0_router_bias, l0_exp_w_gate, l0_exp_w_up, l0_exp_w_down,
           l0_sh_w_gate, l0_sh_w_up, l0_sh_w_down, l0_ln2_g, l0_ln2_b,
           l1_w_in, l1_b_in, l1_w_out, l1_b_out, l1_ln1_g, l1_ln1_b,
           l1_router_w, l1_router_bias, l1_exp_w_gate, l1_exp_w_up, l1_exp_w_down,
           l1_sh_w_gate, l1_sh_w_up, l1_sh_w_down, l1_ln2_g, l1_ln2_b):
    bsz, seq, d = x.shape
    n_tok = bsz * seq

    u, q, k, v = _l0_front(x, positions, l0_w_in, l0_b_in, l0_conv_w, l0_conv_b,
                           l0_conv_ln_g, l0_conv_ln_b, l0_q_norm_g, l0_w_uq,
                           l0_kv_norm_g, l0_w_ukv)
    o = _attention(q, k, v, None, packed=False)
    x_flat = x.reshape(n_tok, d)
    routing = _mix_out_router(u.reshape(n_tok, -1), o.reshape(n_tok, -1), x_flat,
                              l0_w_out, l0_b_out, l0_ln1_g, l0_ln1_b,
                              l0_router_w, l0_router_bias)
    x2 = _moe(*routing, l0_exp_w_gate, l0_exp_w_up, l0_exp_w_down,
              l0_sh_w_gate, l0_sh_w_up, l0_sh_w_down, l0_ln2_g, l0_ln2_b)

    q, k, v, cneg = _l1_front(x2.reshape(bsz, seq, d), l1_w_in, l1_b_in)
    o = _attention(q, k, v, cneg[:, :, None, :], packed=True).reshape(n_tok, -1)
    routing = _mix_out_router(o, o, x2, l1_w_out, l1_b_out, l1_ln1_g, l1_ln1_b,
                              l1_router_w, l1_router_bias)
    x3 = _moe(*routing, l1_exp_w_gate, l1_exp_w_up, l1_exp_w_down,
              l1_sh_w_gate, l1_sh_w_up, l1_sh_w_down, l1_ln2_g, l1_ln2_b)
    return x3.reshape(bsz, seq, d)
```

```python
import functools

import jax
import jax.numpy as jnp
from jax import lax
from jax.experimental import pallas as pl
from jax.experimental.pallas import tpu as pltpu
from jax.experimental.pallas import tpu_sc as plsc

D_MODEL = 1024
DEPTH = 2
DN_ALPHA = (2 * DEPTH) ** 0.25
LN_EPS = 1e-5
RMS_EPS = 1e-6

CONV_CH = 512
CONV_WIDTH = 31
MLA_HEADS = 8
QK_NOPE = 64
QK_ROPE = 32
V_DIM = 64
Q_LORA = 256
KV_LORA = 128
ROPE_THETA = 10000.0
FOX_HEADS = 16
FOX_HD = 64
N_EXPERTS = 64
TOP_K = 6
N_GROUPS = 8
TOPK_GROUPS = 4
EXPERT_HID = 256
ROUTED_SCALE = 2.5

LANES = 128
SUBLANES = 8
VMEM_LIMIT_BYTES = 56 * 1024 * 1024

SEQ_TILE = 512
TOKEN_TILE = 512
ATTN_TILE = 512
ATTN_KEY_TILE = 512
GROUP_TILE = 512
CONV_ROWS = 32
CONV_HALO = 32
PAD_HEAD = 128

PACK_COLS = D_MODEL // 2
PACK_BLOCKS = PACK_COLS // LANES
SC_CORES = 2
SC_SUBCORES = 16
SC_WORKERS = SC_CORES * SC_SUBCORES
SC_WINDOW = 128

BF16 = jnp.bfloat16
F32 = jnp.float32
NEG_INF = float("-inf")
LOG2_E = 1.4426950408889634


def _params(*semantics):
    return pltpu.CompilerParams(dimension_semantics=semantics,
                                vmem_limit_bytes=VMEM_LIMIT_BYTES)


def _resident(shape):
    nd = len(shape)
    return pl.BlockSpec(shape, lambda *_: (0,) * nd)


def _dot(a, b):
    return jnp.dot(a, b, preferred_element_type=F32)


def _dot_nt(a, b, precision=None):
    return lax.dot_general(a, b, (((1,), (1,)), ((), ())),
                           precision=precision, preferred_element_type=F32)


def _layer_norm(x, g, b):
    mu = jnp.mean(x, axis=-1, keepdims=True)
    xc = x - mu
    var = jnp.mean(xc * xc, axis=-1, keepdims=True)
    return xc * lax.rsqrt(var + LN_EPS) * g + b


def _rms_norm(x, g):
    return x * lax.rsqrt(jnp.mean(x * x, axis=-1, keepdims=True) + RMS_EPS) * g


def _ones_upper_half(width):
    lane = lax.broadcasted_iota(jnp.int32, (1, width), 1)
    return jnp.where((lane & (PAD_HEAD - 1)) >= V_DIM, 1.0, 0.0).astype(F32)


def _pad_heads(w, heads):
    d = w.shape[-1] // heads
    w3 = w.reshape(w.shape[:-1] + (heads, d))
    pad = jnp.zeros(w.shape[:-1] + (heads, PAD_HEAD - d), w.dtype)
    return jnp.concatenate([w3, pad], axis=-1).reshape(w.shape[:-1] + (heads * PAD_HEAD,))


def _pack_rows(x):
    lo = lax.bitcast_convert_type(x[:, :PACK_COLS].astype(BF16).astype(F32), jnp.uint32)
    hi = lax.bitcast_convert_type(x[:, PACK_COLS:].astype(BF16).astype(F32), jnp.uint32)
    word = (lo >> 16) | (hi & jnp.uint32(0xFFFF0000))
    return lax.bitcast_convert_type(word, jnp.int32)


def _unpack_rows(words):
    u = lax.bitcast_convert_type(words, jnp.uint32)
    lo = lax.bitcast_convert_type(u << 16, F32)
    hi = lax.bitcast_convert_type(u & jnp.uint32(0xFFFF0000), F32)
    return lo, hi


def _store_packed(ref, words):
    groups = words.shape[0] // SUBLANES
    for cb in range(PACK_BLOCKS):
        ref[:, cb, :, :] = words[:, cb * LANES:(cb + 1) * LANES].reshape(groups, SUBLANES, LANES)


def _load_packed(ref):
    rows = ref.shape[0] * SUBLANES
    return jnp.concatenate(
        [ref[:, cb, :, :].reshape(rows, LANES) for cb in range(PACK_BLOCKS)], axis=1)


L0_A = 0
L0_G = CONV_CH
L0_Q = 2 * CONV_CH
L0_KV = L0_Q + Q_LORA
L0_KR = L0_KV + KV_LORA
L0_KRR = L0_KR + PAD_HEAD
L0_COLS = L0_KRR + PAD_HEAD


def _l0_front_kernel(x_ref, pos_ref, w_in_ref, b_in_ref, conv_w_ref, conv_b_ref,
                     cln_g_ref, cln_b_ref, qn_g_ref, wq_ref, wqr_ref, kvn_g_ref,
                     wk_ref, wv_ref, invf_ref,
                     u_ref, q_ref, k_ref, v_ref, ubuf):
    ts = x_ref.shape[0]

    @pl.when(pl.program_id(1) == 0)
    def _():
        ubuf[0:CONV_HALO, :] = jnp.zeros((CONV_HALO, CONV_CH), F32)

    h = _dot(x_ref[...].astype(BF16), w_in_ref[...]) + b_in_ref[...]

    ubuf[CONV_HALO:CONV_HALO + ts, :] = (
        h[:, L0_A:L0_A + CONV_CH] * jax.nn.sigmoid(h[:, L0_G:L0_G + CONV_CH]))
    first_tap = CONV_HALO - (CONV_WIDTH - 1)

    def conv_chunk(c, carry):
        base = pl.multiple_of(c * CONV_ROWS, CONV_ROWS)
        acc = jnp.broadcast_to(conv_b_ref[...], (CONV_ROWS, CONV_CH))
        for res in range(SUBLANES):
            rows = CONV_ROWS + (SUBLANES if res else 0)
            part = None
            for off in range(res, first_tap + CONV_WIDTH, SUBLANES):
                j = off - first_tap
                if j < 0:
                    continue
                term = ubuf[pl.ds(base + (off - res), rows), :] * conv_w_ref[j:j + 1, :]
                part = term if part is None else part + term
            acc = acc + part[res:res + CONV_ROWS, :]
        y = _layer_norm(acc, cln_g_ref[...], cln_b_ref[...])
        u_ref[pl.ds(base, CONV_ROWS), :] = (y * jax.nn.sigmoid(y)).astype(BF16)
        return carry

    lax.fori_loop(0, ts // CONV_ROWS, conv_chunk, 0)
    ubuf[0:CONV_HALO, :] = ubuf[ts:ts + CONV_HALO, :]

    ang = pos_ref[...].astype(F32) * invf_ref[...]
    cos = jnp.cos(ang)
    sin = jnp.sin(ang)
    scale = (QK_NOPE + QK_ROPE) ** -0.5 * LOG2_E

    qn = _rms_norm(h[:, L0_Q:L0_Q + Q_LORA], qn_g_ref[...]).astype(BF16)
    q = _dot(qn, wq_ref[...])
    q_rot = _dot(qn, wqr_ref[...])
    cos_s = cos * scale
    sin_s = sin * scale
    for hh in range(MLA_HEADS):
        blk = slice(hh * PAD_HEAD, (hh + 1) * PAD_HEAD)
        q_ref[:, blk] = (q[:, blk] * cos_s + q_rot[:, blk] * sin_s).astype(BF16)

    kvn = _rms_norm(h[:, L0_KV:L0_KV + KV_LORA], kvn_g_ref[...]).astype(BF16)
    k_nope = _dot(kvn, wk_ref[...])
    k_pe = h[:, L0_KR:L0_KR + PAD_HEAD] * cos + h[:, L0_KRR:L0_KRR + PAD_HEAD] * sin
    for hh in range(MLA_HEADS):
        blk = slice(hh * PAD_HEAD, (hh + 1) * PAD_HEAD)
        k_ref[:, blk] = (k_nope[:, blk] + k_pe).astype(BF16)
    v_ref[...] = (_dot(kvn, wv_ref[...]) + _ones_upper_half(v_ref.shape[1])).astype(BF16)


def _rope_rotate_cols(w):
    half = QK_ROPE // 2
    return jnp.concatenate([-w[..., half:], w[..., :half]], axis=-1)


def _l0_front(x, positions, w_in, b_in, conv_w, conv_b, cln_g, cln_b,
              qn_g, w_uq, kvn_g, w_ukv):
    bsz, seq, _ = x.shape
    pad_lo = jnp.zeros((D_MODEL, QK_NOPE), F32)
    pad_hi = jnp.zeros((D_MODEL, PAD_HEAD - QK_NOPE - QK_ROPE), F32)
    i3 = L0_KR
    w_kr = w_in[:, i3:i3 + QK_ROPE]
    w_in_p = jnp.concatenate(
        [w_in[:, :i3], pad_lo, w_kr, pad_hi, pad_lo, _rope_rotate_cols(w_kr), pad_hi],
        axis=1).astype(BF16)
    b_kr = b_in[i3:i3 + QK_ROPE]
    zlo = jnp.zeros((QK_NOPE,), F32)
    zhi = jnp.zeros((PAD_HEAD - QK_NOPE - QK_ROPE,), F32)
    b_in_p = jnp.concatenate(
        [b_in[:i3], zlo, b_kr, zhi, zlo, _rope_rotate_cols(b_kr), zhi])[None, :]

    dq = QK_NOPE + QK_ROPE
    wq3 = w_uq.reshape(Q_LORA, MLA_HEADS, dq)
    zq = jnp.zeros((Q_LORA, MLA_HEADS, PAD_HEAD - dq), F32)
    wq_p = jnp.concatenate([wq3, zq], axis=-1).reshape(Q_LORA, MLA_HEADS * PAD_HEAD)
    wqr_p = jnp.concatenate(
        [jnp.zeros((Q_LORA, MLA_HEADS, QK_NOPE), F32),
         _rope_rotate_cols(wq3[..., QK_NOPE:]), zq], axis=-1
    ).reshape(Q_LORA, MLA_HEADS * PAD_HEAD)
    wkv3 = w_ukv.reshape(KV_LORA, MLA_HEADS, QK_NOPE + V_DIM)
    wk_p = jnp.concatenate(
        [wkv3[..., :QK_NOPE], jnp.zeros((KV_LORA, MLA_HEADS, PAD_HEAD - QK_NOPE), F32)],
        axis=-1).reshape(KV_LORA, MLA_HEADS * PAD_HEAD)
    wv = _pad_heads(wkv3[..., QK_NOPE:].reshape(KV_LORA, MLA_HEADS * V_DIM), MLA_HEADS)

    inv_freq = 1.0 / (ROPE_THETA ** (jnp.arange(0, QK_ROPE, 2, dtype=F32) / QK_ROPE))
    invf = jnp.concatenate([jnp.zeros((QK_NOPE,), F32), inv_freq, inv_freq,
                            jnp.zeros((PAD_HEAD - QK_NOPE - QK_ROPE,), F32)])[None, :]

    ts = SEQ_TILE
    row = lambda w: pl.BlockSpec((None, ts, w), lambda b, s: (b, s, 0))
    qk_w = MLA_HEADS * PAD_HEAD
    v_w = MLA_HEADS * PAD_HEAD
    return pl.pallas_call(
        _l0_front_kernel,
        grid=(bsz, seq // ts),
        in_specs=[row(D_MODEL), row(1),
                  _resident((D_MODEL, L0_COLS)), _resident((1, L0_COLS)),
                  _resident((CONV_WIDTH, CONV_CH)), _resident((1, CONV_CH)),
                  _resident((1, CONV_CH)), _resident((1, CONV_CH)),
                  _resident((1, Q_LORA)), _resident((Q_LORA, qk_w)),
                  _resident((Q_LORA, qk_w)), _resident((1, KV_LORA)),
                  _resident((KV_LORA, qk_w)), _resident((KV_LORA, v_w)),
                  _resident((1, PAD_HEAD))],
        out_specs=[row(CONV_CH), row(qk_w), row(qk_w), row(v_w)],
        out_shape=[jax.ShapeDtypeStruct((bsz, seq, CONV_CH), BF16),
                   jax.ShapeDtypeStruct((bsz, seq, qk_w), BF16),
                   jax.ShapeDtypeStruct((bsz, seq, qk_w), BF16),
                   jax.ShapeDtypeStruct((bsz, seq, v_w), BF16)],
        scratch_shapes=[pltpu.VMEM((ts + CONV_HALO, CONV_CH), F32)],
        compiler_params=_params("arbitrary", "arbitrary"),
        name="l0_front",
    )(x, positions[..., None], w_in_p, b_in_p, conv_w, conv_b[None, :],
      cln_g[None, :], cln_b[None, :], qn_g[None, :], wq_p.astype(BF16),
      wqr_p.astype(BF16), kvn_g[None, :], wk_p.astype(BF16), wv.astype(BF16), invf)


def _l1_front_kernel(x_ref, w_ref, b_ref, wv_ref, bv_ref, wf_ref, bf_ref,
                     q_ref, k_ref, v_ref, cneg_ref, carry):
    ts = x_ref.shape[0]

    @pl.when(pl.program_id(1) == 0)
    def _():
        carry[...] = jnp.zeros_like(carry)

    xb = x_ref[...].astype(BF16)
    mix = FOX_HEADS * FOX_HD
    scale = FOX_HD ** -0.5 * LOG2_E
    q_ref[...] = ((_dot(xb, w_ref[:, 0:mix]) + b_ref[:, 0:mix]) * scale).astype(BF16)
    k_ref[...] = (_dot(xb, w_ref[:, mix:2 * mix]) + b_ref[:, mix:2 * mix]).astype(BF16)
    v_ref[...] = (_dot(xb, wv_ref[...]) + bv_ref[...]
                  + _ones_upper_half(v_ref.shape[1])).astype(BF16)

    log_f = jax.nn.log_sigmoid(_dot_nt(wf_ref[...], xb) + bf_ref[...])
    r = lax.broadcasted_iota(jnp.int32, (ts, ts), 0)
    c = lax.broadcasted_iota(jnp.int32, (ts, ts), 1)
    upper = (r <= c).astype(F32)
    csum = jnp.dot(log_f, upper, precision=lax.Precision.HIGHEST,
                   preferred_element_type=F32) + carry[...]
    cneg_ref[...] = csum * -LOG2_E
    carry[...] = carry[...] + jnp.sum(log_f, axis=1, keepdims=True)


def _l1_front(x, w_in, b_in):
    bsz, seq, _ = x.shape
    mix = FOX_HEADS * FOX_HD
    ts = SEQ_TILE
    row = lambda w: pl.BlockSpec((None, ts, w), lambda b, s: (b, s, 0))
    w_qk = w_in[:, :2 * mix].astype(BF16)
    b_qk = b_in[None, :2 * mix]
    w_v = _pad_heads(w_in[:, 2 * mix:3 * mix], FOX_HEADS).astype(BF16)
    b_v = _pad_heads(b_in[None, 2 * mix:3 * mix], FOX_HEADS)
    v_w = FOX_HEADS * PAD_HEAD
    wf_t = w_in[:, 3 * mix:].T.astype(BF16)
    bf_t = b_in[3 * mix:][:, None]
    return pl.pallas_call(
        _l1_front_kernel,
        grid=(bsz, seq // ts),
        in_specs=[row(D_MODEL), _resident((D_MODEL, 2 * mix)), _resident((1, 2 * mix)),
                  _resident((D_MODEL, v_w)), _resident((1, v_w)),
                  _resident((FOX_HEADS, D_MODEL)), _resident((FOX_HEADS, 1))],
        out_specs=[row(mix), row(mix), row(v_w),
                   pl.BlockSpec((None, FOX_HEADS, ts), lambda b, s: (b, 0, s))],
        out_shape=[jax.ShapeDtypeStruct((bsz, seq, mix), BF16)] * 2
        + [jax.ShapeDtypeStruct((bsz, seq, v_w), BF16),
           jax.ShapeDtypeStruct((bsz, FOX_HEADS, seq), F32)],
        scratch_shapes=[pltpu.VMEM((FOX_HEADS, 1), F32)],
        compiler_params=_params("arbitrary", "arbitrary"),
        name="l1_front",
    )(x, w_qk, b_qk, w_v, b_v, wf_t, bf_t)


def _attn_kernel(*refs, packed, has_bias):
    if has_bias:
        q_ref, k_ref, v_ref, cneg_ref, o_ref = refs
    else:
        q_ref, k_ref, v_ref, o_ref = refs
        cneg_ref = None
    tq = q_ref.shape[0]
    tk = ATTN_KEY_TILE
    qi = pl.program_id(2)
    q2 = q_ref[...]
    half = LANES // 2
    if packed:
        lane = lax.broadcasted_iota(jnp.int32, q2.shape, 1)
        zero = jnp.zeros_like(q2)
        qs = (jnp.where(lane < half, q2, zero), jnp.where(lane >= half, q2, zero))
    else:
        qs = (q2[:, :PAD_HEAD], q2[:, PAD_HEAD:])

    def k_block(kj, head):
        rows = pl.ds(pl.multiple_of(kj * tk, tk), tk)
        if packed:
            return k_ref[rows, :]
        return k_ref[rows, head * PAD_HEAD:(head + 1) * PAD_HEAD]

    def step(kj, carry, diag):
        rows = pl.ds(pl.multiple_of(kj * tk, tk), tk)
        out = []
        for head in range(2):
            m, acc = carry[head]
            s = _dot_nt(qs[head], k_block(kj, head))
            if has_bias:
                s = s + cneg_ref[head, :, rows]
            if diag is not None:
                r = lax.broadcasted_iota(jnp.int32, s.shape, 0)
                c = lax.broadcasted_iota(jnp.int32, s.shape, 1)
                s = jnp.where(r >= c + diag * tk, s, NEG_INF)
            m_new = jnp.maximum(m, jnp.max(s, axis=-1, keepdims=True))
            alpha = jnp.exp2(m - m_new)
            p = jnp.exp2(s - m_new).astype(BF16)
            vblk = v_ref[rows, head * PAD_HEAD:(head + 1) * PAD_HEAD]
            out.append((m_new, alpha * acc + _dot(p, vblk)))
        return tuple(out)

    init = tuple((jnp.full((tq, 1), NEG_INF, F32), jnp.zeros((tq, LANES), F32))
                 for _ in range(2))
    per_tile = tq // tk
    n_full = qi * per_tile
    carry = lax.fori_loop(
        0, n_full // 2,
        lambda j, c: step(2 * j + 1, step(2 * j, c, None), None), init)
    carry = lax.cond(n_full % 2 == 1, lambda c: step(n_full - 1, c, None),
                     lambda c: c, carry)
    for d in range(per_tile):
        carry = step(n_full + d, carry, d)
    (_, acc_a), (_, acc_b) = carry
    out_a = acc_a / pltpu.roll(acc_a, half, axis=1)
    out_b = acc_b / pltpu.roll(acc_b, half, axis=1)
    lane_o = lax.broadcasted_iota(jnp.int32, (tq, LANES), 1)
    o_ref[...] = jnp.where(lane_o < half, out_a,
                           pltpu.roll(out_b, half, axis=1)).astype(o_ref.dtype)


def _attention(q, k, v, cneg, packed):
    bsz, seq, v_w = v.shape
    pairs = v_w // (2 * PAD_HEAD)
    qk_w = q.shape[-1] // pairs
    tq = ATTN_TILE
    in_specs = [pl.BlockSpec((None, tq, qk_w), lambda b, p, i: (b, i, p)),
                pl.BlockSpec((None, seq, qk_w), lambda b, p, i: (b, 0, p)),
                pl.BlockSpec((None, seq, 2 * PAD_HEAD), lambda b, p, i: (b, 0, p))]
    args = [q, k, v]
    if cneg is not None:
        in_specs.append(pl.BlockSpec((None, 2, 1, seq), lambda b, p, i: (b, p, 0, 0)))
        args.append(cneg)
    return pl.pallas_call(
        functools.partial(_attn_kernel, packed=packed, has_bias=cneg is not None),
        grid=(bsz, pairs, seq // tq),
        in_specs=in_specs,
        out_specs=pl.BlockSpec((None, tq, LANES), lambda b, p, i: (b, i, p)),
        out_shape=jax.ShapeDtypeStruct((bsz, seq, pairs * LANES), BF16),
        compiler_params=_params("arbitrary", "arbitrary", "arbitrary"),
        name="attn_packed" if packed else "attn_padded",
    )(*args)


def _first_hit(hits, found):
    out = []
    for hcur in hits:
        take = jnp.logical_and(hcur, jnp.logical_not(found))
        found = jnp.logical_or(found, take)
        out.append(take)
    return out, found


def _mix_out_router_kernel(a_ref, b_ref, x_ref, wa_ref, wb_ref, bo_ref, g_ref, be_ref,
                           rw_ref, rb_ref,
                           x1_ref, x1p_ref, e_ref, gate_ref, rank_ref, cnt_ref, cnt, gate_t):
    tm = x_ref.shape[0]

    @pl.when(pl.program_id(0) == 0)
    def _():
        cnt[...] = jnp.zeros_like(cnt)
        gate_t[...] = jnp.zeros_like(gate_t)

    mix = _dot(a_ref[...], wa_ref[...]) + _dot(b_ref[...], wb_ref[...]) + bo_ref[...]
    x1 = _layer_norm(DN_ALPHA * x_ref[...] + mix, g_ref[...], be_ref[...])
    x1_ref[...] = x1
    _store_packed(x1p_ref, _pack_rows(x1))

    logits = _dot_nt(rw_ref[...], x1, precision=lax.Precision.HIGHEST)
    aff = jax.nn.sigmoid(logits)
    choice = aff + rb_ref[...]
    per_group = N_EXPERTS // N_GROUPS
    sub = lax.broadcasted_iota(jnp.int32, (per_group, tm), 0)
    groups = [choice[g * per_group:(g + 1) * per_group, :] for g in range(N_GROUPS)]

    gscore = []
    for cg in groups:
        m1 = jnp.max(cg, axis=0, keepdims=True)
        i1 = jnp.min(jnp.where(cg == m1, sub, per_group), axis=0, keepdims=True)
        m2 = jnp.max(jnp.where(sub == i1, NEG_INF, cg), axis=0, keepdims=True)
        gscore.append(m1 + m2)

    gsel = [jnp.zeros((1, tm), jnp.bool_) for _ in range(N_GROUPS)]
    for _ in range(TOPK_GROUPS):
        best = functools.reduce(jnp.maximum, gscore)
        takes, _ = _first_hit([gs == best for gs in gscore], jnp.zeros((1, tm), jnp.bool_))
        gsel = [jnp.logical_or(a, t) for a, t in zip(gsel, takes)]
        gscore = [jnp.where(t, NEG_INF, gs) for gs, t in zip(gscore, takes)]

    masked = [jnp.where(gs, cg, NEG_INF) for gs, cg in zip(gsel, groups)]
    eid = [sub + g * per_group for g in range(N_GROUPS)]
    affs = [aff[g * per_group:(g + 1) * per_group, :] for g in range(N_GROUPS)]
    sel = [jnp.zeros((per_group, tm), jnp.bool_) for _ in range(N_GROUPS)]
    picked_e, picked_w = [], []
    for _ in range(TOP_K):
        best = jnp.max(functools.reduce(jnp.maximum, masked), axis=0, keepdims=True)
        cand = [jnp.where(mg == best, ig, N_EXPERTS) for mg, ig in zip(masked, eid)]
        idx = jnp.min(functools.reduce(jnp.minimum, cand), axis=0, keepdims=True)
        onehot = [ig == idx for ig in eid]
        w = functools.reduce(
            jnp.add, [jnp.sum(jnp.where(oh, ag, 0.0), axis=0, keepdims=True)
                      for oh, ag in zip(onehot, affs)])
        picked_e.append(idx)
        picked_w.append(w)
        sel = [jnp.logical_or(sg, oh) for sg, oh in zip(sel, onehot)]
        masked = [jnp.where(oh, NEG_INF, mg) for mg, oh in zip(masked, onehot)]

    wsum = functools.reduce(jnp.add, picked_w)

    sel_f = jnp.concatenate([sg.astype(F32) for sg in sel], axis=0)
    r = lax.broadcasted_iota(jnp.int32, (tm, tm), 0)
    c = lax.broadcasted_iota(jnp.int32, (tm, tm), 1)
    before = _dot(sel_f.astype(BF16), (r < c).astype(BF16)) + cnt[...]
    eall = lax.broadcasted_iota(jnp.int32, (N_EXPERTS, tm), 0)
    for kk in range(TOP_K):
        rank = jnp.sum(jnp.where(eall == picked_e[kk], before, 0.0), axis=0, keepdims=True)
        e_ref[kk:kk + 1, :] = picked_e[kk]
        gate_t[kk:kk + 1, :] = picked_w[kk] / wsum * ROUTED_SCALE
        rank_ref[kk:kk + 1, :] = rank.astype(jnp.int32)
    pad = SUBLANES - TOP_K
    e_ref[TOP_K:, :] = jnp.zeros((pad, tm), jnp.int32)
    rank_ref[TOP_K:, :] = jnp.zeros((pad, tm), jnp.int32)
    gate_ref[...] = gate_t[...].T
    cnt[...] = cnt[...] + jnp.sum(sel_f, axis=1, keepdims=True)
    cnt_ref[...] = jnp.broadcast_to(cnt[...], cnt_ref.shape)


def _mix_out_router(a, b, x, w_out, b_out, ln_g, ln_b, router_w, router_bias):
    n_tok = x.shape[0]
    tm = TOKEN_TILE
    half = w_out.shape[0] // 2
    row = lambda w: pl.BlockSpec((tm, w), lambda i: (i, 0))
    b_spec = (pl.BlockSpec((tm, half), lambda i: (i, 1)) if b is a
              else pl.BlockSpec((tm, half), lambda i: (i, 0)))
    lane_row = pl.BlockSpec((SUBLANES, tm), lambda i: (0, i))
    packed = pl.BlockSpec((tm // SUBLANES, PACK_BLOCKS, SUBLANES, LANES),
                          lambda i: (i, 0, 0, 0))
    w_bf = w_out.astype(BF16)
    return pl.pallas_call(
        _mix_out_router_kernel,
        grid=(n_tok // tm,),
        in_specs=[row(half), b_spec, row(D_MODEL),
                  pl.BlockSpec((half, D_MODEL), lambda i: (0, 0)),
                  pl.BlockSpec((half, D_MODEL), lambda i: (1, 0)),
                  _resident((1, D_MODEL)), _resident((1, D_MODEL)), _resident((1, D_MODEL)),
                  _resident((N_EXPERTS, D_MODEL)), _resident((N_EXPERTS, 1))],
        out_specs=[row(D_MODEL), packed, lane_row, row(LANES), lane_row,
                   _resident((N_EXPERTS, LANES))],
        out_shape=[jax.ShapeDtypeStruct((n_tok, D_MODEL), F32),
                   jax.ShapeDtypeStruct((n_tok // SUBLANES, PACK_BLOCKS, SUBLANES, LANES),
                                        jnp.int32),
                   jax.ShapeDtypeStruct((SUBLANES, n_tok), jnp.int32),
                   jax.ShapeDtypeStruct((n_tok, LANES), F32),
                   jax.ShapeDtypeStruct((SUBLANES, n_tok), jnp.int32),
                   jax.ShapeDtypeStruct((N_EXPERTS, LANES), F32)],
        scratch_shapes=[pltpu.VMEM((N_EXPERTS, 1), F32), pltpu.VMEM((LANES, tm), F32)],
        compiler_params=_params("arbitrary"),
        name="mix_out_router",
    )(a, b, x, w_bf, w_bf, b_out[None, :], ln_g[None, :], ln_b[None, :],
      router_w.T, router_bias[:, None])


def _expert_ffn_kernel(tile_e_ref, tile_on_ref, xs_ref, wg_ref, wu_ref, wd_ref, ys_ref,
                       wg_bf, wu_bf, wd_bf):
    i = pl.program_id(0)
    on = tile_on_ref[i] > 0
    fresh = jnp.logical_or(i == 0, tile_e_ref[i] != tile_e_ref[jnp.maximum(i - 1, 0)])

    @pl.when(jnp.logical_and(on, fresh))
    def _():
        wg_bf[...] = wg_ref[...].astype(BF16)
        wu_bf[...] = wu_ref[...].astype(BF16)
        wd_bf[...] = wd_ref[...].astype(BF16)

    @pl.when(on)
    def _():
        lo, hi = _unpack_rows(_load_packed(xs_ref))
        lo = lo.astype(BF16)
        hi = hi.astype(BF16)
        gate = _dot(lo, wg_bf[:PACK_COLS, :]) + _dot(hi, wg_bf[PACK_COLS:, :])
        up = _dot(lo, wu_bf[:PACK_COLS, :]) + _dot(hi, wu_bf[PACK_COLS:, :])
        hid = (jax.nn.silu(gate) * up).astype(BF16)
        _store_packed(ys_ref, _pack_rows(_dot(hid, wd_bf[...])))

    @pl.when(jnp.logical_not(on))
    def _():
        ys_ref[...] = jnp.zeros_like(ys_ref)


def _expert_ffn(xs, tile_e, tile_on, w_gate, w_up, w_down):
    tg = GROUP_TILE
    packed = pl.BlockSpec((tg // SUBLANES, PACK_BLOCKS, SUBLANES, LANES),
                          lambda i, te, on: (i, 0, 0, 0))
    grid_spec = pltpu.PrefetchScalarGridSpec(
        num_scalar_prefetch=2,
        grid=(xs.shape[0] * SUBLANES // tg,),
        in_specs=[packed,
                  pl.BlockSpec((None, D_MODEL, EXPERT_HID), lambda i, te, on: (te[i], 0, 0)),
                  pl.BlockSpec((None, D_MODEL, EXPERT_HID), lambda i, te, on: (te[i], 0, 0)),
                  pl.BlockSpec((None, EXPERT_HID, D_MODEL), lambda i, te, on: (te[i], 0, 0))],
        out_specs=packed,
        scratch_shapes=[pltpu.VMEM((D_MODEL, EXPERT_HID), BF16),
                        pltpu.VMEM((D_MODEL, EXPERT_HID), BF16),
                        pltpu.VMEM((EXPERT_HID, D_MODEL), BF16)],
    )
    return pl.pallas_call(
        _expert_ffn_kernel,
        grid_spec=grid_spec,
        out_shape=jax.ShapeDtypeStruct(xs.shape, jnp.int32),
        compiler_params=_params("arbitrary"),
        name="expert_ffn",
    )(tile_e, tile_on, xs, w_gate, w_up, w_down)


def _moe_out_kernel(x1_ref, yg_ref, gate_ref, sg_ref, su_ref, sd_ref, g_ref, b_ref, o_ref):
    x1 = x1_ref[...]
    xb = x1.astype(BF16)
    hid = jax.nn.silu(_dot(xb, sg_ref[...])) * _dot(xb, su_ref[...])
    shared = _dot(hid.astype(BF16), sd_ref[...])
    gates = gate_ref[...]
    lo_acc = shared[:, :PACK_COLS]
    hi_acc = shared[:, PACK_COLS:]
    for kk in range(TOP_K):
        lo, hi = _unpack_rows(_load_packed(yg_ref.at[kk]))
        w = gates[:, kk:kk + 1]
        lo_acc = lo_acc + w * lo
        hi_acc = hi_acc + w * hi
    moe = jnp.concatenate([lo_acc, hi_acc], axis=1)
    o_ref[...] = _layer_norm(DN_ALPHA * x1 + moe, g_ref[...], b_ref[...])


def _moe_out(x1, yg, gates, sh_gate, sh_up, sh_down, ln_g, ln_b):
    n_tok = x1.shape[0]
    tm = TOKEN_TILE
    row = lambda w: pl.BlockSpec((tm, w), lambda i: (i, 0))
    return pl.pallas_call(
        _moe_out_kernel,
        grid=(n_tok // tm,),
        in_specs=[row(D_MODEL),
                  pl.BlockSpec((TOP_K, tm // SUBLANES, PACK_BLOCKS, SUBLANES, LANES),
                               lambda i: (0, i, 0, 0, 0)),
                  row(LANES),
                  _resident((D_MODEL, EXPERT_HID)), _resident((D_MODEL, EXPERT_HID)),
                  _resident((EXPERT_HID, D_MODEL)), _resident((1, D_MODEL)),
                  _resident((1, D_MODEL))],
        out_specs=row(D_MODEL),
        out_shape=jax.ShapeDtypeStruct((n_tok, D_MODEL), F32),
        compiler_params=_params("arbitrary"),
        name="moe_out",
    )(x1, yg, gates, sh_gate.astype(BF16), sh_up.astype(BF16), sh_down.astype(BF16),
      ln_g[None, :], ln_b[None, :])


def _slot_kernel(offset_ref, e_ref, rank_ref, pos_ref):
    e = e_ref[...]
    base = jnp.zeros(e.shape, jnp.int32)
    for ex in range(N_EXPERTS):
        base = jnp.where(e == ex, offset_ref[ex], base)
    pos_ref[...] = base + rank_ref[...]


def _slots(offset, e6, rank6):
    n_tok = e6.shape[1]
    tl = 4096
    lane_row = pl.BlockSpec((SUBLANES, tl), lambda i, off: (0, i))
    return pl.pallas_call(
        _slot_kernel,
        grid_spec=pltpu.PrefetchScalarGridSpec(
            num_scalar_prefetch=1, grid=(n_tok // tl,),
            in_specs=[lane_row, lane_row], out_specs=lane_row),
        out_shape=jax.ShapeDtypeStruct((SUBLANES, n_tok), jnp.int32),
        compiler_params=_params("arbitrary"),
        name="slots",
    )(offset, e6, rank6)


def _sc_mesh():
    return plsc.VectorSubcoreMesh(core_axis_name="c", subcore_axis_name="s")


def _sc_worker():
    return lax.axis_index("s") * SC_CORES + lax.axis_index("c")


def _sc_dispatch(x_sub, idx, n_out_sub):
    windows = x_sub.shape[0] // (SC_WORKERS * SC_WINDOW)

    @functools.partial(
        pl.kernel, mesh=_sc_mesh(),
        out_type=jax.ShapeDtypeStruct((n_out_sub, LANES), jnp.int32),
        scratch_types=[pltpu.VMEM((SC_WINDOW, LANES), jnp.int32),
                       pltpu.VMEM((TOP_K, SC_WINDOW), jnp.int32),
                       pltpu.SemaphoreType.DMA((TOP_K,))],
        name="sc_dispatch")
    def run(x_hbm, idx_hbm, out_hbm, rows_v, idx_v, sems):
        wid = _sc_worker()

        @pl.loop(0, windows)
        def _(c):
            base = (wid * windows + c) * SC_WINDOW
            pltpu.sync_copy(x_hbm.at[pl.ds(base, SC_WINDOW)], rows_v)
            pltpu.sync_copy(idx_hbm.at[wid, c], idx_v)
            copies = [pltpu.async_copy(rows_v, out_hbm.at[idx_v.at[kk]], sems.at[kk])
                      for kk in range(TOP_K)]
            for cp in copies:
                cp.wait()

    return run(x_sub, idx)


def _sc_combine(y_sub, idx, n_tok_sub):
    windows = n_tok_sub // (SC_WORKERS * SC_WINDOW)

    @functools.partial(
        pl.kernel, mesh=_sc_mesh(),
        out_type=jax.ShapeDtypeStruct((TOP_K, n_tok_sub, LANES), jnp.int32),
        scratch_types=[pltpu.VMEM((2, SC_WINDOW, LANES), jnp.int32),
                       pltpu.VMEM((TOP_K, SC_WINDOW), jnp.int32),
                       pltpu.SemaphoreType.DMA((2,)),
                       pltpu.SemaphoreType.DMA((2,))],
        name="sc_combine")
    def run(y_hbm, idx_hbm, out_hbm, rows_v, idx_v, gather_sem, write_sem):
        wid = _sc_worker()

        def gather(kk):
            buf = kk % 2
            return pltpu.async_copy(y_hbm.at[idx_v.at[kk]], rows_v.at[buf], gather_sem.at[buf])

        @pl.loop(0, windows)
        def _(c):
            base = (wid * windows + c) * SC_WINDOW
            pltpu.sync_copy(idx_hbm.at[wid, c], idx_v)
            writes = []
            pending = gather(0)
            for kk in range(TOP_K):
                pending.wait()
                if kk + 1 < TOP_K:
                    if kk >= 1:
                        writes[kk - 1].wait()
                    pending = gather(kk + 1)
                writes.append(pltpu.async_copy(
                    rows_v.at[kk % 2], out_hbm.at[kk, pl.ds(base, SC_WINDOW)],
                    write_sem.at[kk % 2]))
            writes[TOP_K - 2].wait()
            writes[TOP_K - 1].wait()

    return run(y_sub, idx)


def _moe(x1, x1p, e6, gates, rank6, counts, exp_w_gate, exp_w_up, exp_w_down,
         sh_w_gate, sh_w_up, sh_w_down, ln_g, ln_b):
    n_tok = x1.shape[0]
    tg = GROUP_TILE
    n_rows = n_tok * TOP_K + N_EXPERTS * tg
    n_tiles = n_rows // tg

    cnt = counts[:, 0].astype(jnp.int32)
    tiles_per_e = (cnt + tg - 1) // tg
    tile_end = jnp.cumsum(tiles_per_e)
    offset = (tile_end - tiles_per_e) * tg
    tile_ids = jnp.arange(n_tiles, dtype=jnp.int32)
    tile_on = (tile_ids < tile_end[-1]).astype(jnp.int32)
    tile_e = jnp.minimum(
        jnp.sum((tile_ids[:, None] >= tile_end[None, :]).astype(jnp.int32), axis=1),
        N_EXPERTS - 1)
    tile_e = jnp.where(tile_on > 0, tile_e, tile_e[jnp.maximum(tile_end[-1] - 1, 0)])

    pos = _slots(offset, e6, rank6)[:TOP_K]

    group = PACK_BLOCKS * SUBLANES
    sub0 = (pos // SUBLANES) * group + pos % SUBLANES
    idx = (sub0.reshape(TOP_K, n_tok // SUBLANES, 1, SUBLANES)
           + (jnp.arange(PACK_BLOCKS, dtype=jnp.int32) * SUBLANES)[None, None, :, None])
    n_sub = n_tok * PACK_BLOCKS
    windows = n_sub // (SC_WORKERS * SC_WINDOW)
    idx = idx.reshape(TOP_K, SC_WORKERS, windows, SC_WINDOW).transpose(1, 2, 0, 3)

    xs = _sc_dispatch(x1p.reshape(n_sub, LANES), idx, n_rows * PACK_BLOCKS)
    ys = _expert_ffn(xs.reshape(n_rows // SUBLANES, PACK_BLOCKS, SUBLANES, LANES),
                     tile_e, tile_on, exp_w_gate, exp_w_up, exp_w_down)
    yg = _sc_combine(ys.reshape(n_rows * PACK_BLOCKS, LANES), idx, n_sub)
    yg = yg.reshape(TOP_K, n_tok // SUBLANES, PACK_BLOCKS, SUBLANES, LANES)
    return _moe_out(x1, yg, gates, sh_w_gate, sh_w_up, sh_w_down, ln_g, ln_b)


def kernel(x, positions,
           l0_w_in, l0_b_in, l0_conv_w, l0_conv_b, l0_conv_ln_g, l0_conv_ln_b,
           l0_q_norm_g, l0_w_uq, l0_kv_norm_g, l0_w_ukv, l0_w_out, l0_b_out,
           l0_ln1_g, l0_ln1_b,
           l0_router_w, l0_router_bias, l0_exp_w_gate, l0_exp_w_up, l0_exp_w_down,
           l0_sh_w_gate, l0_sh_w_up, l0_sh_w_down, l0_ln2_g, l0_ln2_b,
           l1_w_in, l1_b_in, l1_w_out, l1_b_out, l1_ln1_g, l1_ln1_b,
           l1_router_w, l1_router_bias, l1_exp_w_gate, l1_exp_w_up, l1_exp_w_down,
           l1_sh_w_gate, l1_sh_w_up, l1_sh_w_down, l1_ln2_g, l1_ln2_b):
    bsz, seq, d = x.shape
    n_tok = bsz * seq

    u, q, k, v = _l0_front(x, positions, l0_w_in, l0_b_in, l0_conv_w, l0_conv_b,
                           l0_conv_ln_g, l0_conv_ln_b, l0_q_norm_g, l0_w_uq,
                           l0_kv_norm_g, l0_w_ukv)
    o = _attention(q, k, v, None, packed=False)
    x_flat = x.reshape(n_tok, d)
    routing = _mix_out_router(u.reshape(n_tok, -1), o.reshape(n_tok, -1), x_flat,
                              l0_w_out, l0_b_out, l0_ln1_g, l0_ln1_b,
                              l0_router_w, l0_router_bias)
    x2 = _moe(*routing, l0_exp_w_gate, l0_exp_w_up, l0_exp_w_down,
              l0_sh_w_gate, l0_sh_w_up, l0_sh_w_down, l0_ln2_g, l0_ln2_b)

    q, k, v, cneg = _l1_front(x2.reshape(bsz, seq, d), l1_w_in, l1_b_in)
    o = _attention(q, k, v, cneg[:, :, None, :], packed=True).reshape(n_tok, -1)
    routing = _mix_out_router(o, o, x2, l1_w_out, l1_b_out, l1_ln1_g, l1_ln1_b,
                              l1_router_w, l1_router_bias)
    x3 = _moe(*routing, l1_exp_w_gate, l1_exp_w_up, l1_exp_w_down,
              l1_sh_w_gate, l1_sh_w_up, l1_sh_w_down, l1_ln2_g, l1_ln2_b)
    return x3.reshape(bsz, seq, d)
```

```python
import functools

import jax
import jax.numpy as jnp
from jax import lax
from jax.experimental import pallas as pl
from jax.experimental.pallas import tpu as pltpu
from jax.experimental.pallas import tpu_sc as plsc

D_MODEL = 1024
DEPTH = 2
DN_ALPHA = (2 * DEPTH) ** 0.25
LN_EPS = 1e-5
RMS_EPS = 1e-6

CONV_CH = 512
CONV_WIDTH = 31
MLA_HEADS = 8
QK_NOPE = 64
QK_ROPE = 32
V_DIM = 64
Q_LORA = 256
KV_LORA = 128
ROPE_THETA = 10000.0
FOX_HEADS = 16
FOX_HD = 64
N_EXPERTS = 64
TOP_K = 6
N_GROUPS = 8
TOPK_GROUPS = 4
EXPERT_HID = 256
ROUTED_SCALE = 2.5

LANES = 128
SUBLANES = 8
VMEM_LIMIT_BYTES = 56 * 1024 * 1024

SEQ_TILE = 512
TOKEN_TILE = 512
ATTN_TILE = 512
ATTN_KEY_TILE = 512
GROUP_TILE = 512
CONV_ROWS = 32
CONV_HALO = 32
PAD_HEAD = 128

PACK_COLS = D_MODEL // 2
PACK_BLOCKS = PACK_COLS // LANES
SC_CORES = 2
SC_SUBCORES = 16
SC_WORKERS = SC_CORES * SC_SUBCORES
SC_WINDOW = 128
BATCH_CHUNKS = 2

BF16 = jnp.bfloat16
F32 = jnp.float32
NEG_INF = float("-inf")
LOG2_E = 1.4426950408889634


def _params(*semantics):
    return pltpu.CompilerParams(dimension_semantics=semantics,
                                vmem_limit_bytes=VMEM_LIMIT_BYTES)


def _resident(shape):
    nd = len(shape)
    return pl.BlockSpec(shape, lambda *_: (0,) * nd)


def _dot(a, b):
    return jnp.dot(a, b, preferred_element_type=F32)


def _dot_nt(a, b, precision=None):
    return lax.dot_general(a, b, (((1,), (1,)), ((), ())),
                           precision=precision, preferred_element_type=F32)


def _layer_norm(x, g, b):
    mu = jnp.mean(x, axis=-1, keepdims=True)
    xc = x - mu
    var = jnp.mean(xc * xc, axis=-1, keepdims=True)
    return xc * lax.rsqrt(var + LN_EPS) * g + b


def _rms_norm(x, g):
    return x * lax.rsqrt(jnp.mean(x * x, axis=-1, keepdims=True) + RMS_EPS) * g


def _ones_upper_half(width):
    lane = lax.broadcasted_iota(jnp.int32, (1, width), 1)
    return jnp.where((lane & (PAD_HEAD - 1)) >= V_DIM, 1.0, 0.0).astype(F32)


def _pad_heads(w, heads):
    d = w.shape[-1] // heads
    w3 = w.reshape(w.shape[:-1] + (heads, d))
    pad = jnp.zeros(w.shape[:-1] + (heads, PAD_HEAD - d), w.dtype)
    return jnp.concatenate([w3, pad], axis=-1).reshape(w.shape[:-1] + (heads * PAD_HEAD,))


def _pack_rows(x):
    lo = lax.bitcast_convert_type(x[:, :PACK_COLS].astype(BF16).astype(F32), jnp.uint32)
    hi = lax.bitcast_convert_type(x[:, PACK_COLS:].astype(BF16).astype(F32), jnp.uint32)
    word = (lo >> 16) | (hi & jnp.uint32(0xFFFF0000))
    return lax.bitcast_convert_type(word, jnp.int32)


def _unpack_rows(words):
    u = lax.bitcast_convert_type(words, jnp.uint32)
    lo = lax.bitcast_convert_type(u << 16, F32)
    hi = lax.bitcast_convert_type(u & jnp.uint32(0xFFFF0000), F32)
    return lo, hi


def _store_packed(ref, words):
    groups = words.shape[0] // SUBLANES
    for cb in range(PACK_BLOCKS):
        ref[:, cb, :, :] = words[:, cb * LANES:(cb + 1) * LANES].reshape(groups, SUBLANES, LANES)


def _load_packed(ref):
    rows = ref.shape[0] * SUBLANES
    return jnp.concatenate(
        [ref[:, cb, :, :].reshape(rows, LANES) for cb in range(PACK_BLOCKS)], axis=1)


L0_A = 0
L0_G = CONV_CH
L0_Q = 2 * CONV_CH
L0_KV = L0_Q + Q_LORA
L0_KR = L0_KV + KV_LORA
L0_KRR = L0_KR + PAD_HEAD
L0_COLS = L0_KRR + PAD_HEAD


def _l0_front_kernel(x_ref, pos_ref, w_in_ref, b_in_ref, conv_w_ref, conv_b_ref,
                     cln_g_ref, cln_b_ref, qn_g_ref, wq_ref, wqr_ref, kvn_g_ref,
                     wk_ref, wv_ref, invf_ref,
                     u_ref, q_ref, k_ref, v_ref, ubuf):
    ts = x_ref.shape[0]

    @pl.when(pl.program_id(1) == 0)
    def _():
        ubuf[0:CONV_HALO, :] = jnp.zeros((CONV_HALO, CONV_CH), F32)

    h = _dot(x_ref[...].astype(BF16), w_in_ref[...]) + b_in_ref[...]

    ubuf[CONV_HALO:CONV_HALO + ts, :] = (
        h[:, L0_A:L0_A + CONV_CH] * jax.nn.sigmoid(h[:, L0_G:L0_G + CONV_CH]))
    first_tap = CONV_HALO - (CONV_WIDTH - 1)

    def conv_chunk(c, carry):
        base = pl.multiple_of(c * CONV_ROWS, CONV_ROWS)
        acc = jnp.broadcast_to(conv_b_ref[...], (CONV_ROWS, CONV_CH))
        for res in range(SUBLANES):
            rows = CONV_ROWS + (SUBLANES if res else 0)
            part = None
            for off in range(res, first_tap + CONV_WIDTH, SUBLANES):
                j = off - first_tap
                if j < 0:
                    continue
                term = ubuf[pl.ds(base + (off - res), rows), :] * conv_w_ref[j:j + 1, :]
                part = term if part is None else part + term
            acc = acc + part[res:res + CONV_ROWS, :]
        y = _layer_norm(acc, cln_g_ref[...], cln_b_ref[...])
        u_ref[pl.ds(base, CONV_ROWS), :] = (y * jax.nn.sigmoid(y)).astype(BF16)
        return carry

    lax.fori_loop(0, ts // CONV_ROWS, conv_chunk, 0)
    ubuf[0:CONV_HALO, :] = ubuf[ts:ts + CONV_HALO, :]

    ang = pos_ref[...].astype(F32) * invf_ref[...]
    cos = jnp.cos(ang)
    sin = jnp.sin(ang)
    scale = (QK_NOPE + QK_ROPE) ** -0.5 * LOG2_E

    qn = _rms_norm(h[:, L0_Q:L0_Q + Q_LORA], qn_g_ref[...]).astype(BF16)
    q = _dot(qn, wq_ref[...])
    q_rot = _dot(qn, wqr_ref[...])
    cos_s = cos * scale
    sin_s = sin * scale
    for hh in range(MLA_HEADS):
        blk = slice(hh * PAD_HEAD, (hh + 1) * PAD_HEAD)
        q_ref[:, blk] = (q[:, blk] * cos_s + q_rot[:, blk] * sin_s).astype(BF16)

    kvn = _rms_norm(h[:, L0_KV:L0_KV + KV_LORA], kvn_g_ref[...]).astype(BF16)
    k_nope = _dot(kvn, wk_ref[...])
    k_pe = h[:, L0_KR:L0_KR + PAD_HEAD] * cos + h[:, L0_KRR:L0_KRR + PAD_HEAD] * sin
    for hh in range(MLA_HEADS):
        blk = slice(hh * PAD_HEAD, (hh + 1) * PAD_HEAD)
        k_ref[:, blk] = (k_nope[:, blk] + k_pe).astype(BF16)
    v_ref[...] = (_dot(kvn, wv_ref[...]) + _ones_upper_half(v_ref.shape[1])).astype(BF16)


def _rope_rotate_cols(w):
    half = QK_ROPE // 2
    return jnp.concatenate([-w[..., half:], w[..., :half]], axis=-1)


def _l0_front(x, positions, w_in, b_in, conv_w, conv_b, cln_g, cln_b,
              qn_g, w_uq, kvn_g, w_ukv):
    bsz, seq, _ = x.shape
    pad_lo = jnp.zeros((D_MODEL, QK_NOPE), F32)
    pad_hi = jnp.zeros((D_MODEL, PAD_HEAD - QK_NOPE - QK_ROPE), F32)
    i3 = L0_KR
    w_kr = w_in[:, i3:i3 + QK_ROPE]
    w_in_p = jnp.concatenate(
        [w_in[:, :i3], pad_lo, w_kr, pad_hi, pad_lo, _rope_rotate_cols(w_kr), pad_hi],
        axis=1).astype(BF16)
    b_kr = b_in[i3:i3 + QK_ROPE]
    zlo = jnp.zeros((QK_NOPE,), F32)
    zhi = jnp.zeros((PAD_HEAD - QK_NOPE - QK_ROPE,), F32)
    b_in_p = jnp.concatenate(
        [b_in[:i3], zlo, b_kr, zhi, zlo, _rope_rotate_cols(b_kr), zhi])[None, :]

    dq = QK_NOPE + QK_ROPE
    wq3 = w_uq.reshape(Q_LORA, MLA_HEADS, dq)
    zq = jnp.zeros((Q_LORA, MLA_HEADS, PAD_HEAD - dq), F32)
    wq_p = jnp.concatenate([wq3, zq], axis=-1).reshape(Q_LORA, MLA_HEADS * PAD_HEAD)
    wqr_p = jnp.concatenate(
        [jnp.zeros((Q_LORA, MLA_HEADS, QK_NOPE), F32),
         _rope_rotate_cols(wq3[..., QK_NOPE:]), zq], axis=-1
    ).reshape(Q_LORA, MLA_HEADS * PAD_HEAD)
    wkv3 = w_ukv.reshape(KV_LORA, MLA_HEADS, QK_NOPE + V_DIM)
    wk_p = jnp.concatenate(
        [wkv3[..., :QK_NOPE], jnp.zeros((KV_LORA, MLA_HEADS, PAD_HEAD - QK_NOPE), F32)],
        axis=-1).reshape(KV_LORA, MLA_HEADS * PAD_HEAD)
    wv = _pad_heads(wkv3[..., QK_NOPE:].reshape(KV_LORA, MLA_HEADS * V_DIM), MLA_HEADS)

    inv_freq = 1.0 / (ROPE_THETA ** (jnp.arange(0, QK_ROPE, 2, dtype=F32) / QK_ROPE))
    invf = jnp.concatenate([jnp.zeros((QK_NOPE,), F32), inv_freq, inv_freq,
                            jnp.zeros((PAD_HEAD - QK_NOPE - QK_ROPE,), F32)])[None, :]

    ts = SEQ_TILE
    row = lambda w: pl.BlockSpec((None, ts, w), lambda b, s: (b, s, 0))
    qk_w = MLA_HEADS * PAD_HEAD
    v_w = MLA_HEADS * PAD_HEAD
    return pl.pallas_call(
        _l0_front_kernel,
        grid=(bsz, seq // ts),
        in_specs=[row(D_MODEL), row(1),
                  _resident((D_MODEL, L0_COLS)), _resident((1, L0_COLS)),
                  _resident((CONV_WIDTH, CONV_CH)), _resident((1, CONV_CH)),
                  _resident((1, CONV_CH)), _resident((1, CONV_CH)),
                  _resident((1, Q_LORA)), _resident((Q_LORA, qk_w)),
                  _resident((Q_LORA, qk_w)), _resident((1, KV_LORA)),
                  _resident((KV_LORA, qk_w)), _resident((KV_LORA, v_w)),
                  _resident((1, PAD_HEAD))],
        out_specs=[row(CONV_CH), row(qk_w), row(qk_w), row(v_w)],
        out_shape=[jax.ShapeDtypeStruct((bsz, seq, CONV_CH), BF16),
                   jax.ShapeDtypeStruct((bsz, seq, qk_w), BF16),
                   jax.ShapeDtypeStruct((bsz, seq, qk_w), BF16),
                   jax.ShapeDtypeStruct((bsz, seq, v_w), BF16)],
        scratch_shapes=[pltpu.VMEM((ts + CONV_HALO, CONV_CH), F32)],
        compiler_params=_params("arbitrary", "arbitrary"),
        name="l0_front",
    )(x, positions[..., None], w_in_p, b_in_p, conv_w, conv_b[None, :],
      cln_g[None, :], cln_b[None, :], qn_g[None, :], wq_p.astype(BF16),
      wqr_p.astype(BF16), kvn_g[None, :], wk_p.astype(BF16), wv.astype(BF16), invf)


def _l1_front_kernel(x_ref, w_ref, b_ref, wv_ref, bv_ref, wf_ref, bf_ref,
                     q_ref, k_ref, v_ref, cneg_ref, carry):
    ts = x_ref.shape[0]

    @pl.when(pl.program_id(1) == 0)
    def _():
        carry[...] = jnp.zeros_like(carry)

    xb = x_ref[...].astype(BF16)
    mix = FOX_HEADS * FOX_HD
    scale = FOX_HD ** -0.5 * LOG2_E
    q_ref[...] = ((_dot(xb, w_ref[:, 0:mix]) + b_ref[:, 0:mix]) * scale).astype(BF16)
    k_ref[...] = (_dot(xb, w_ref[:, mix:2 * mix]) + b_ref[:, mix:2 * mix]).astype(BF16)
    v_ref[...] = (_dot(xb, wv_ref[...]) + bv_ref[...]
                  + _ones_upper_half(v_ref.shape[1])).astype(BF16)

    log_f = jax.nn.log_sigmoid(_dot_nt(wf_ref[...], xb) + bf_ref[...])
    r = lax.broadcasted_iota(jnp.int32, (ts, ts), 0)
    c = lax.broadcasted_iota(jnp.int32, (ts, ts), 1)
    upper = (r <= c).astype(F32)
    csum = jnp.dot(log_f, upper, precision=lax.Precision.HIGHEST,
                   preferred_element_type=F32) + carry[...]
    cneg_ref[...] = csum * -LOG2_E
    carry[...] = carry[...] + jnp.sum(log_f, axis=1, keepdims=True)


def _l1_front(x, w_in, b_in):
    bsz, seq, _ = x.shape
    mix = FOX_HEADS * FOX_HD
    ts = SEQ_TILE
    row = lambda w: pl.BlockSpec((None, ts, w), lambda b, s: (b, s, 0))
    w_qk = w_in[:, :2 * mix].astype(BF16)
    b_qk = b_in[None, :2 * mix]
    w_v = _pad_heads(w_in[:, 2 * mix:3 * mix], FOX_HEADS).astype(BF16)
    b_v = _pad_heads(b_in[None, 2 * mix:3 * mix], FOX_HEADS)
    v_w = FOX_HEADS * PAD_HEAD
    wf_t = w_in[:, 3 * mix:].T.astype(BF16)
    bf_t = b_in[3 * mix:][:, None]
    return pl.pallas_call(
        _l1_front_kernel,
        grid=(bsz, seq // ts),
        in_specs=[row(D_MODEL), _resident((D_MODEL, 2 * mix)), _resident((1, 2 * mix)),
                  _resident((D_MODEL, v_w)), _resident((1, v_w)),
                  _resident((FOX_HEADS, D_MODEL)), _resident((FOX_HEADS, 1))],
        out_specs=[row(mix), row(mix), row(v_w),
                   pl.BlockSpec((None, FOX_HEADS, ts), lambda b, s: (b, 0, s))],
        out_shape=[jax.ShapeDtypeStruct((bsz, seq, mix), BF16)] * 2
        + [jax.ShapeDtypeStruct((bsz, seq, v_w), BF16),
           jax.ShapeDtypeStruct((bsz, FOX_HEADS, seq), F32)],
        scratch_shapes=[pltpu.VMEM((FOX_HEADS, 1), F32)],
        compiler_params=_params("arbitrary", "arbitrary"),
        name="l1_front",
    )(x, w_qk, b_qk, w_v, b_v, wf_t, bf_t)


def _attn_kernel(*refs, packed, has_bias):
    if has_bias:
        q_ref, k_ref, v_ref, cneg_ref, o_ref = refs
    else:
        q_ref, k_ref, v_ref, o_ref = refs
        cneg_ref = None
    tq = q_ref.shape[0]
    tk = ATTN_KEY_TILE
    qi = pl.program_id(2)
    q2 = q_ref[...]
    half = LANES // 2
    if packed:
        lane = lax.broadcasted_iota(jnp.int32, q2.shape, 1)
        zero = jnp.zeros_like(q2)
        qs = (jnp.where(lane < half, q2, zero), jnp.where(lane >= half, q2, zero))
    else:
        qs = (q2[:, :PAD_HEAD], q2[:, PAD_HEAD:])

    def k_block(kj, head):
        rows = pl.ds(pl.multiple_of(kj * tk, tk), tk)
        if packed:
            return k_ref[rows, :]
        return k_ref[rows, head * PAD_HEAD:(head + 1) * PAD_HEAD]

    def step(kj, carry, diag):
        rows = pl.ds(pl.multiple_of(kj * tk, tk), tk)
        out = []
        for head in range(2):
            m, acc = carry[head]
            s = _dot_nt(qs[head], k_block(kj, head))
            if has_bias:
                s = s + cneg_ref[head, :, rows]
            if diag is not None:
                r = lax.broadcasted_iota(jnp.int32, s.shape, 0)
                c = lax.broadcasted_iota(jnp.int32, s.shape, 1)
                s = jnp.where(r >= c + diag * tk, s, NEG_INF)
            m_new = jnp.maximum(m, jnp.max(s, axis=-1, keepdims=True))
            alpha = jnp.exp2(m - m_new)
            p = jnp.exp2(s - m_new).astype(BF16)
            vblk = v_ref[rows, head * PAD_HEAD:(head + 1) * PAD_HEAD]
            out.append((m_new, alpha * acc + _dot(p, vblk)))
        return tuple(out)

    init = tuple((jnp.full((tq, 1), NEG_INF, F32), jnp.zeros((tq, LANES), F32))
                 for _ in range(2))
    per_tile = tq // tk
    n_full = qi * per_tile
    carry = lax.fori_loop(
        0, n_full // 2,
        lambda j, c: step(2 * j + 1, step(2 * j, c, None), None), init)
    carry = lax.cond(n_full % 2 == 1, lambda c: step(n_full - 1, c, None),
                     lambda c: c, carry)
    for d in range(per_tile):
        carry = step(n_full + d, carry, d)
    (_, acc_a), (_, acc_b) = carry
    out_a = acc_a / pltpu.roll(acc_a, half, axis=1)
    out_b = acc_b / pltpu.roll(acc_b, half, axis=1)
    lane_o = lax.broadcasted_iota(jnp.int32, (tq, LANES), 1)
    o_ref[...] = jnp.where(lane_o < half, out_a,
                           pltpu.roll(out_b, half, axis=1)).astype(o_ref.dtype)


def _attention(q, k, v, cneg, packed):
    bsz, seq, v_w = v.shape
    pairs = v_w // (2 * PAD_HEAD)
    qk_w = q.shape[-1] // pairs
    tq = ATTN_TILE
    in_specs = [pl.BlockSpec((None, tq, qk_w), lambda b, p, i: (b, i, p)),
                pl.BlockSpec((None, seq, qk_w), lambda b, p, i: (b, 0, p)),
                pl.BlockSpec((None, seq, 2 * PAD_HEAD), lambda b, p, i: (b, 0, p))]
    args = [q, k, v]
    if cneg is not None:
        in_specs.append(pl.BlockSpec((None, 2, 1, seq), lambda b, p, i: (b, p, 0, 0)))
        args.append(cneg)
    return pl.pallas_call(
        functools.partial(_attn_kernel, packed=packed, has_bias=cneg is not None),
        grid=(bsz, pairs, seq // tq),
        in_specs=in_specs,
        out_specs=pl.BlockSpec((None, tq, LANES), lambda b, p, i: (b, i, p)),
        out_shape=jax.ShapeDtypeStruct((bsz, seq, pairs * LANES), BF16),
        compiler_params=_params("arbitrary", "arbitrary", "arbitrary"),
        name="attn_packed" if packed else "attn_padded",
    )(*args)


def _first_hit(hits, found):
    out = []
    for hcur in hits:
        take = jnp.logical_and(hcur, jnp.logical_not(found))
        found = jnp.logical_or(found, take)
        out.append(take)
    return out, found


def _mix_out_router_kernel(a_ref, b_ref, x_ref, wa_ref, wb_ref, bo_ref, g_ref, be_ref,
                           rw_ref, rb_ref,
                           x1_ref, x1p_ref, e_ref, gate_ref, rank_ref, cnt_ref, cnt, gate_t):
    tm = x_ref.shape[0]

    @pl.when(pl.program_id(0) == 0)
    def _():
        cnt[...] = jnp.zeros_like(cnt)
        gate_t[...] = jnp.zeros_like(gate_t)

    mix = _dot(a_ref[...], wa_ref[...]) + _dot(b_ref[...], wb_ref[...]) + bo_ref[...]
    x1 = _layer_norm(DN_ALPHA * x_ref[...] + mix, g_ref[...], be_ref[...])
    x1_ref[...] = x1
    _store_packed(x1p_ref, _pack_rows(x1))

    logits = _dot_nt(rw_ref[...], x1, precision=lax.Precision.HIGHEST)
    aff = jax.nn.sigmoid(logits)
    choice = aff + rb_ref[...]
    per_group = N_EXPERTS // N_GROUPS
    sub = lax.broadcasted_iota(jnp.int32, (per_group, tm), 0)
    groups = [choice[g * per_group:(g + 1) * per_group, :] for g in range(N_GROUPS)]

    gscore = []
    for cg in groups:
        m1 = jnp.max(cg, axis=0, keepdims=True)
        i1 = jnp.min(jnp.where(cg == m1, sub, per_group), axis=0, keepdims=True)
        m2 = jnp.max(jnp.where(sub == i1, NEG_INF, cg), axis=0, keepdims=True)
        gscore.append(m1 + m2)

    gsel = [jnp.zeros((1, tm), jnp.bool_) for _ in range(N_GROUPS)]
    for _ in range(TOPK_GROUPS):
        best = functools.reduce(jnp.maximum, gscore)
        takes, _ = _first_hit([gs == best for gs in gscore], jnp.zeros((1, tm), jnp.bool_))
        gsel = [jnp.logical_or(a, t) for a, t in zip(gsel, takes)]
        gscore = [jnp.where(t, NEG_INF, gs) for gs, t in zip(gscore, takes)]

    masked = [jnp.where(gs, cg, NEG_INF) for gs, cg in zip(gsel, groups)]
    eid = [sub + g * per_group for g in range(N_GROUPS)]
    affs = [aff[g * per_group:(g + 1) * per_group, :] for g in range(N_GROUPS)]
    sel = [jnp.zeros((per_group, tm), jnp.bool_) for _ in range(N_GROUPS)]
    picked_e, picked_w = [], []
    for _ in range(TOP_K):
        best = jnp.max(functools.reduce(jnp.maximum, masked), axis=0, keepdims=True)
        cand = [jnp.where(mg == best, ig, N_EXPERTS) for mg, ig in zip(masked, eid)]
        idx = jnp.min(functools.reduce(jnp.minimum, cand), axis=0, keepdims=True)
        onehot = [ig == idx for ig in eid]
        w = functools.reduce(
            jnp.add, [jnp.sum(jnp.where(oh, ag, 0.0), axis=0, keepdims=True)
                      for oh, ag in zip(onehot, affs)])
        picked_e.append(idx)
        picked_w.append(w)
        sel = [jnp.logical_or(sg, oh) for sg, oh in zip(sel, onehot)]
        masked = [jnp.where(oh, NEG_INF, mg) for mg, oh in zip(masked, onehot)]

    wsum = functools.reduce(jnp.add, picked_w)

    sel_f = jnp.concatenate([sg.astype(F32) for sg in sel], axis=0)
    r = lax.broadcasted_iota(jnp.int32, (tm, tm), 0)
    c = lax.broadcasted_iota(jnp.int32, (tm, tm), 1)
    before = _dot(sel_f.astype(BF16), (r < c).astype(BF16)) + cnt[...]
    eall = lax.broadcasted_iota(jnp.int32, (N_EXPERTS, tm), 0)
    for kk in range(TOP_K):
        rank = jnp.sum(jnp.where(eall == picked_e[kk], before, 0.0), axis=0, keepdims=True)
        e_ref[kk:kk + 1, :] = picked_e[kk]
        gate_t[kk:kk + 1, :] = picked_w[kk] / wsum * ROUTED_SCALE
        rank_ref[kk:kk + 1, :] = rank.astype(jnp.int32)
    pad = SUBLANES - TOP_K
    e_ref[TOP_K:, :] = jnp.zeros((pad, tm), jnp.int32)
    rank_ref[TOP_K:, :] = jnp.zeros((pad, tm), jnp.int32)
    gate_ref[...] = gate_t[...].T
    cnt[...] = cnt[...] + jnp.sum(sel_f, axis=1, keepdims=True)
    cnt_ref[...] = jnp.broadcast_to(cnt[...], cnt_ref.shape)


def _mix_out_router(a, b, x, w_out, b_out, ln_g, ln_b, router_w, router_bias):
    n_tok = x.shape[0]
    tm = TOKEN_TILE
    half = w_out.shape[0] // 2
    row = lambda w: pl.BlockSpec((tm, w), lambda i: (i, 0))
    b_spec = (pl.BlockSpec((tm, half), lambda i: (i, 1)) if b is a
              else pl.BlockSpec((tm, half), lambda i: (i, 0)))
    lane_row = pl.BlockSpec((SUBLANES, tm), lambda i: (0, i))
    packed = pl.BlockSpec((tm // SUBLANES, PACK_BLOCKS, SUBLANES, LANES),
                          lambda i: (i, 0, 0, 0))
    w_bf = w_out.astype(BF16)
    return pl.pallas_call(
        _mix_out_router_kernel,
        grid=(n_tok // tm,),
        in_specs=[row(half), b_spec, row(D_MODEL),
                  pl.BlockSpec((half, D_MODEL), lambda i: (0, 0)),
                  pl.BlockSpec((half, D_MODEL), lambda i: (1, 0)),
                  _resident((1, D_MODEL)), _resident((1, D_MODEL)), _resident((1, D_MODEL)),
                  _resident((N_EXPERTS, D_MODEL)), _resident((N_EXPERTS, 1))],
        out_specs=[row(D_MODEL), packed, lane_row, row(LANES), lane_row,
                   _resident((N_EXPERTS, LANES))],
        out_shape=[jax.ShapeDtypeStruct((n_tok, D_MODEL), F32),
                   jax.ShapeDtypeStruct((n_tok // SUBLANES, PACK_BLOCKS, SUBLANES, LANES),
                                        jnp.int32),
                   jax.ShapeDtypeStruct((SUBLANES, n_tok), jnp.int32),
                   jax.ShapeDtypeStruct((n_tok, LANES), F32),
                   jax.ShapeDtypeStruct((SUBLANES, n_tok), jnp.int32),
                   jax.ShapeDtypeStruct((N_EXPERTS, LANES), F32)],
        scratch_shapes=[pltpu.VMEM((N_EXPERTS, 1), F32), pltpu.VMEM((LANES, tm), F32)],
        compiler_params=_params("arbitrary"),
        name="mix_out_router",
    )(a, b, x, w_bf, w_bf, b_out[None, :], ln_g[None, :], ln_b[None, :],
      router_w.T, router_bias[:, None])


def _expert_ffn_kernel(tile_e_ref, tile_on_ref, xs_ref, wg_ref, wu_ref, wd_ref, ys_ref,
                       wg_bf, wu_bf, wd_bf):
    i = pl.program_id(0)
    on = tile_on_ref[i] > 0
    fresh = jnp.logical_or(i == 0, tile_e_ref[i] != tile_e_ref[jnp.maximum(i - 1, 0)])

    @pl.when(jnp.logical_and(on, fresh))
    def _():
        wg_bf[...] = wg_ref[...].astype(BF16)
        wu_bf[...] = wu_ref[...].astype(BF16)
        wd_bf[...] = wd_ref[...].astype(BF16)

    @pl.when(on)
    def _():
        lo, hi = _unpack_rows(_load_packed(xs_ref))
        lo = lo.astype(BF16)
        hi = hi.astype(BF16)
        gate = _dot(lo, wg_bf[:PACK_COLS, :]) + _dot(hi, wg_bf[PACK_COLS:, :])
        up = _dot(lo, wu_bf[:PACK_COLS, :]) + _dot(hi, wu_bf[PACK_COLS:, :])
        hid = (jax.nn.silu(gate) * up).astype(BF16)
        _store_packed(ys_ref, _pack_rows(_dot(hid, wd_bf[...])))

    @pl.when(jnp.logical_not(on))
    def _():
        ys_ref[...] = jnp.zeros_like(ys_ref)


def _expert_ffn(xs, tile_e, tile_on, w_gate, w_up, w_down):
    tg = GROUP_TILE
    packed = pl.BlockSpec((tg // SUBLANES, PACK_BLOCKS, SUBLANES, LANES),
                          lambda i, te, on: (i, 0, 0, 0))
    grid_spec = pltpu.PrefetchScalarGridSpec(
        num_scalar_prefetch=2,
        grid=(xs.shape[0] * SUBLANES // tg,),
        in_specs=[packed,
                  pl.BlockSpec((None, D_MODEL, EXPERT_HID), lambda i, te, on: (te[i], 0, 0)),
                  pl.BlockSpec((None, D_MODEL, EXPERT_HID), lambda i, te, on: (te[i], 0, 0)),
                  pl.BlockSpec((None, EXPERT_HID, D_MODEL), lambda i, te, on: (te[i], 0, 0))],
        out_specs=packed,
        scratch_shapes=[pltpu.VMEM((D_MODEL, EXPERT_HID), BF16),
                        pltpu.VMEM((D_MODEL, EXPERT_HID), BF16),
                        pltpu.VMEM((EXPERT_HID, D_MODEL), BF16)],
    )
    return pl.pallas_call(
        _expert_ffn_kernel,
        grid_spec=grid_spec,
        out_shape=jax.ShapeDtypeStruct(xs.shape, jnp.int32),
        compiler_params=_params("arbitrary"),
        name="expert_ffn",
    )(tile_e, tile_on, xs, w_gate, w_up, w_down)


def _moe_out_kernel(x1_ref, yg_ref, gate_ref, sg_ref, su_ref, sd_ref, g_ref, b_ref, o_ref):
    x1 = x1_ref[...]
    xb = x1.astype(BF16)
    hid = jax.nn.silu(_dot(xb, sg_ref[...])) * _dot(xb, su_ref[...])
    shared = _dot(hid.astype(BF16), sd_ref[...])
    gates = gate_ref[...]
    lo_acc = shared[:, :PACK_COLS]
    hi_acc = shared[:, PACK_COLS:]
    for kk in range(TOP_K):
        lo, hi = _unpack_rows(_load_packed(yg_ref.at[kk]))
        w = gates[:, kk:kk + 1]
        lo_acc = lo_acc + w * lo
        hi_acc = hi_acc + w * hi
    moe = jnp.concatenate([lo_acc, hi_acc], axis=1)
    o_ref[...] = _layer_norm(DN_ALPHA * x1 + moe, g_ref[...], b_ref[...])


def _moe_out(x1, yg, gates, sh_gate, sh_up, sh_down, ln_g, ln_b):
    n_tok = x1.shape[0]
    tm = TOKEN_TILE
    row = lambda w: pl.BlockSpec((tm, w), lambda i: (i, 0))
    return pl.pallas_call(
        _moe_out_kernel,
        grid=(n_tok // tm,),
        in_specs=[row(D_MODEL),
                  pl.BlockSpec((TOP_K, tm // SUBLANES, PACK_BLOCKS, SUBLANES, LANES),
                               lambda i: (0, i, 0, 0, 0)),
                  row(LANES),
                  _resident((D_MODEL, EXPERT_HID)), _resident((D_MODEL, EXPERT_HID)),
                  _resident((EXPERT_HID, D_MODEL)), _resident((1, D_MODEL)),
                  _resident((1, D_MODEL))],
        out_specs=row(D_MODEL),
        out_shape=jax.ShapeDtypeStruct((n_tok, D_MODEL), F32),
        compiler_params=_params("arbitrary"),
        name="moe_out",
    )(x1, yg, gates, sh_gate.astype(BF16), sh_up.astype(BF16), sh_down.astype(BF16),
      ln_g[None, :], ln_b[None, :])


def _slot_kernel(offset_ref, e_ref, rank_ref, pos_ref):
    e = e_ref[...]
    base = jnp.zeros(e.shape, jnp.int32)
    for ex in range(N_EXPERTS):
        base = jnp.where(e == ex, offset_ref[ex], base)
    pos_ref[...] = base + rank_ref[...]


def _slots(offset, e6, rank6):
    n_tok = e6.shape[1]
    tl = 4096
    lane_row = pl.BlockSpec((SUBLANES, tl), lambda i, off: (0, i))
    return pl.pallas_call(
        _slot_kernel,
        grid_spec=pltpu.PrefetchScalarGridSpec(
            num_scalar_prefetch=1, grid=(n_tok // tl,),
            in_specs=[lane_row, lane_row], out_specs=lane_row),
        out_shape=jax.ShapeDtypeStruct((SUBLANES, n_tok), jnp.int32),
        compiler_params=_params("arbitrary"),
        name="slots",
    )(offset, e6, rank6)


def _sc_mesh():
    return plsc.VectorSubcoreMesh(core_axis_name="c", subcore_axis_name="s")


def _sc_worker():
    return lax.axis_index("s") * SC_CORES + lax.axis_index("c")


def _sc_dispatch(x_sub, idx, n_out_sub):
    windows = x_sub.shape[0] // (SC_WORKERS * SC_WINDOW)

    @functools.partial(
        pl.kernel, mesh=_sc_mesh(),
        out_type=jax.ShapeDtypeStruct((n_out_sub, LANES), jnp.int32),
        scratch_types=[pltpu.VMEM((SC_WINDOW, LANES), jnp.int32),
                       pltpu.VMEM((TOP_K, SC_WINDOW), jnp.int32)],
        name="sc_dispatch")
    def run(x_hbm, idx_hbm, out_hbm, rows_v, idx_v):
        wid = _sc_worker()

        @pl.loop(0, windows)
        def _(c):
            base = (wid * windows + c) * SC_WINDOW
            pltpu.sync_copy(x_hbm.at[pl.ds(base, SC_WINDOW)], rows_v)
            pltpu.sync_copy(idx_hbm.at[wid, c], idx_v)
            for kk in range(TOP_K):
                pltpu.sync_copy(rows_v, out_hbm.at[idx_v.at[kk]])

    return run(x_sub, idx)


def _sc_combine(y_sub, idx, n_tok_sub):
    windows = n_tok_sub // (SC_WORKERS * SC_WINDOW)

    @functools.partial(
        pl.kernel, mesh=_sc_mesh(),
        out_type=jax.ShapeDtypeStruct((TOP_K, n_tok_sub, LANES), jnp.int32),
        scratch_types=[pltpu.VMEM((2, SC_WINDOW, LANES), jnp.int32),
                       pltpu.VMEM((TOP_K, SC_WINDOW), jnp.int32),
                       pltpu.SemaphoreType.DMA((2,)),
                       pltpu.SemaphoreType.DMA((2,))],
        name="sc_combine")
    def run(y_hbm, idx_hbm, out_hbm, rows_v, idx_v, gather_sem, write_sem):
        wid = _sc_worker()

        def gather(kk):
            buf = kk % 2
            return pltpu.async_copy(y_hbm.at[idx_v.at[kk]], rows_v.at[buf], gather_sem.at[buf])

        @pl.loop(0, windows)
        def _(c):
            base = (wid * windows + c) * SC_WINDOW
            pltpu.sync_copy(idx_hbm.at[wid, c], idx_v)
            writes = []
            pending = gather(0)
            for kk in range(TOP_K):
                pending.wait()
                if kk + 1 < TOP_K:
                    if kk >= 1:
                        writes[kk - 1].wait()
                    pending = gather(kk + 1)
                writes.append(pltpu.async_copy(
                    rows_v.at[kk % 2], out_hbm.at[kk, pl.ds(base, SC_WINDOW)],
                    write_sem.at[kk % 2]))
            writes[TOP_K - 2].wait()
            writes[TOP_K - 1].wait()

    return run(y_sub, idx)


def _moe(x1, x1p, e6, gates, rank6, counts, exp_w_gate, exp_w_up, exp_w_down,
         sh_w_gate, sh_w_up, sh_w_down, ln_g, ln_b):
    n_tok = x1.shape[0]
    tg = GROUP_TILE
    n_rows = n_tok * TOP_K + N_EXPERTS * tg
    n_tiles = n_rows // tg

    cnt = counts[:, 0].astype(jnp.int32)
    tiles_per_e = (cnt + tg - 1) // tg
    tile_end = jnp.cumsum(tiles_per_e)
    offset = (tile_end - tiles_per_e) * tg
    tile_ids = jnp.arange(n_tiles, dtype=jnp.int32)
    tile_on = (tile_ids < tile_end[-1]).astype(jnp.int32)
    tile_e = jnp.minimum(
        jnp.sum((tile_ids[:, None] >= tile_end[None, :]).astype(jnp.int32), axis=1),
        N_EXPERTS - 1)
    tile_e = jnp.where(tile_on > 0, tile_e, tile_e[jnp.maximum(tile_end[-1] - 1, 0)])

    pos = _slots(offset, e6, rank6)[:TOP_K]

    group = PACK_BLOCKS * SUBLANES
    sub0 = (pos // SUBLANES) * group + pos % SUBLANES
    idx = (sub0.reshape(TOP_K, n_tok // SUBLANES, 1, SUBLANES)
           + (jnp.arange(PACK_BLOCKS, dtype=jnp.int32) * SUBLANES)[None, None, :, None])
    n_sub = n_tok * PACK_BLOCKS
    windows = n_sub // (SC_WORKERS * SC_WINDOW)
    idx = idx.reshape(TOP_K, SC_WORKERS, windows, SC_WINDOW).transpose(1, 2, 0, 3)

    xs = _sc_dispatch(x1p.reshape(n_sub, LANES), idx, n_rows * PACK_BLOCKS)
    ys = _expert_ffn(xs.reshape(n_rows // SUBLANES, PACK_BLOCKS, SUBLANES, LANES),
                     tile_e, tile_on, exp_w_gate, exp_w_up, exp_w_down)
    yg = _sc_combine(ys.reshape(n_rows * PACK_BLOCKS, LANES), idx, n_sub)
    yg = yg.reshape(TOP_K, n_tok // SUBLANES, PACK_BLOCKS, SUBLANES, LANES)
    return _moe_out(x1, yg, gates, sh_w_gate, sh_w_up, sh_w_down, ln_g, ln_b)


def kernel(x, positions,
           l0_w_in, l0_b_in, l0_conv_w, l0_conv_b, l0_conv_ln_g, l0_conv_ln_b,
           l0_q_norm_g, l0_w_uq, l0_kv_norm_g, l0_w_ukv, l0_w_out, l0_b_out,
           l0_ln1_g, l0_ln1_b,
           l0_router_w, l0_router_bias, l0_exp_w_gate, l0_exp_w_up, l0_exp_w_down,
           l0_sh_w_gate, l0_sh_w_up, l0_sh_w_down, l0_ln2_g, l0_ln2_b,
           l1_w_in, l1_b_in, l1_w_out, l1_b_out, l1_ln1_g, l1_ln1_b,
           l1_router_w, l1_router_bias, l1_exp_w_gate, l1_exp_w_up, l1_exp_w_down,
           l1_sh_w_gate, l1_sh_w_up, l1_sh_w_down, l1_ln2_g, l1_ln2_b):
    weights = (l0_w_in, l0_b_in, l0_conv_w, l0_conv_b, l0_conv_ln_g, l0_conv_ln_b,
               l0_q_norm_g, l0_w_uq, l0_kv_norm_g, l0_w_ukv, l0_w_out, l0_b_out,
               l0_ln1_g, l0_ln1_b,
               l0_router_w, l0_router_bias, l0_exp_w_gate, l0_exp_w_up, l0_exp_w_down,
               l0_sh_w_gate, l0_sh_w_up, l0_sh_w_down, l0_ln2_g, l0_ln2_b,
               l1_w_in, l1_b_in, l1_w_out, l1_b_out, l1_ln1_g, l1_ln1_b,
               l1_router_w, l1_router_bias, l1_exp_w_gate, l1_exp_w_up, l1_exp_w_down,
               l1_sh_w_gate, l1_sh_w_up, l1_sh_w_down, l1_ln2_g, l1_ln2_b)
    step = x.shape[0] // BATCH_CHUNKS
    outs = [_trunk(x[i * step:(i + 1) * step], positions[i * step:(i + 1) * step], *weights)
            for i in range(BATCH_CHUNKS)]
    return jnp.concatenate(outs, axis=0)


def _trunk(x, positions,
           l0_w_in, l0_b_in, l0_conv_w, l0_conv_b, l0_conv_ln_g, l0_conv_ln_b,
           l0_q_norm_g, l0_w_uq, l0_kv_norm_g, l0_w_ukv, l0_w_out, l0_b_out,
           l0_ln1_g, l0_ln1_b,
           l0_router_w, l0_router_bias, l0_exp_w_gate, l0_exp_w_up, l0_exp_w_down,
           l0_sh_w_gate, l0_sh_w_up, l0_sh_w_down, l0_ln2_g, l0_ln2_b,
           l1_w_in, l1_b_in, l1_w_out, l1_b_out, l1_ln1_g, l1_ln1_b,
           l1_router_w, l1_router_bias, l1_exp_w_gate, l1_exp_w_up, l1_exp_w_down,
           l1_sh_w_gate, l1_sh_w_up, l1_sh_w_down, l1_ln2_g, l1_ln2_b):
    bsz, seq, d = x.shape
    n_tok = bsz * seq

    u, q, k, v = _l0_front(x, positions, l0_w_in, l0_b_in, l0_conv_w, l0_conv_b,
                           l0_conv_ln_g, l0_conv_ln_b, l0_q_norm_g, l0_w_uq,
                           l0_kv_norm_g, l0_w_ukv)
    o = _attention(q, k, v, None, packed=False)
    x_flat = x.reshape(n_tok, d)
    routing = _mix_out_router(u.reshape(n_tok, -1), o.reshape(n_tok, -1), x_flat,
                              l0_w_out, l0_b_out, l0_ln1_g, l0_ln1_b,
                              l0_router_w, l0_router_bias)
    x2 = _moe(*routing, l0_exp_w_gate, l0_exp_w_up, l0_exp_w_down,
              l0_sh_w_gate, l0_sh_w_up, l0_sh_w_down, l0_ln2_g, l0_ln2_b)

    q, k, v, cneg = _l1_front(x2.reshape(bsz, seq, d), l1_w_in, l1_b_in)
    o = _attention(q, k, v, cneg[:, :, None, :], packed=True).reshape(n_tok, -1)
    routing = _mix_out_router(o, o, x2, l1_w_out, l1_b_out, l1_ln1_g, l1_ln1_b,
                              l1_router_w, l1_router_bias)
    x3 = _moe(*routing, l1_exp_w_gate, l1_exp_w_up, l1_exp_w_down,
              l1_sh_w_gate, l1_sh_w_up, l1_sh_w_down, l1_ln2_g, l1_ln2_b)
    return x3.reshape(bsz, seq, d)
```

```python
import functools

import jax
import jax.numpy as jnp
from jax import lax
from jax.experimental import pallas as pl
from jax.experimental.pallas import tpu as pltpu
from jax.experimental.pallas import tpu_sc as plsc

D_MODEL = 1024
DEPTH = 2
DN_ALPHA = (2 * DEPTH) ** 0.25
LN_EPS = 1e-5
RMS_EPS = 1e-6

CONV_CH = 512
CONV_WIDTH = 31
MLA_HEADS = 8
QK_NOPE = 64
QK_ROPE = 32
V_DIM = 64
Q_LORA = 256
KV_LORA = 128
ROPE_THETA = 10000.0
FOX_HEADS = 16
FOX_HD = 64
N_EXPERTS = 64
TOP_K = 6
N_GROUPS = 8
TOPK_GROUPS = 4
EXPERT_HID = 256
ROUTED_SCALE = 2.5

LANES = 128
SUBLANES = 8
VMEM_LIMIT_BYTES = 56 * 1024 * 1024

SEQ_TILE = 512
TOKEN_TILE = 512
ATTN_TILE = 512
ATTN_KEY_TILE = 512
ATTN_UNROLL = 4
GROUP_TILE = 512
CONV_ROWS = 32
CONV_HALO = 32
PAD_HEAD = 128

PACK_COLS = D_MODEL // 2
PACK_BLOCKS = PACK_COLS // LANES
SC_CORES = 2
SC_SUBCORES = 16
SC_WORKERS = SC_CORES * SC_SUBCORES
SC_WINDOW = 128
BF16 = jnp.bfloat16
F32 = jnp.float32
NEG_INF = float("-inf")
LOG2_E = 1.4426950408889634


def _params(*semantics):
    return pltpu.CompilerParams(dimension_semantics=semantics,
                                vmem_limit_bytes=VMEM_LIMIT_BYTES)


def _resident(shape):
    nd = len(shape)
    return pl.BlockSpec(shape, lambda *_: (0,) * nd)


def _dot(a, b):
    return jnp.dot(a, b, preferred_element_type=F32)


def _dot_nt(a, b, precision=None):
    return lax.dot_general(a, b, (((1,), (1,)), ((), ())),
                           precision=precision, preferred_element_type=F32)


def _layer_norm(x, g, b):
    mu = jnp.mean(x, axis=-1, keepdims=True)
    xc = x - mu
    var = jnp.mean(xc * xc, axis=-1, keepdims=True)
    return xc * lax.rsqrt(var + LN_EPS) * g + b


def _rms_norm(x, g):
    return x * lax.rsqrt(jnp.mean(x * x, axis=-1, keepdims=True) + RMS_EPS) * g


def _ones_upper_half(width):
    lane = lax.broadcasted_iota(jnp.int32, (1, width), 1)
    return jnp.where((lane & (PAD_HEAD - 1)) >= V_DIM, 1.0, 0.0).astype(F32)


def _pad_heads(w, heads):
    d = w.shape[-1] // heads
    w3 = w.reshape(w.shape[:-1] + (heads, d))
    pad = jnp.zeros(w.shape[:-1] + (heads, PAD_HEAD - d), w.dtype)
    return jnp.concatenate([w3, pad], axis=-1).reshape(w.shape[:-1] + (heads * PAD_HEAD,))


def _pack_rows(x):
    lo = lax.bitcast_convert_type(x[:, :PACK_COLS].astype(BF16).astype(F32), jnp.uint32)
    hi = lax.bitcast_convert_type(x[:, PACK_COLS:].astype(BF16).astype(F32), jnp.uint32)
    word = (lo >> 16) | (hi & jnp.uint32(0xFFFF0000))
    return lax.bitcast_convert_type(word, jnp.int32)


def _unpack_rows(words):
    u = lax.bitcast_convert_type(words, jnp.uint32)
    lo = lax.bitcast_convert_type(u << 16, F32)
    hi = lax.bitcast_convert_type(u & jnp.uint32(0xFFFF0000), F32)
    return lo, hi


def _store_packed(ref, words):
    groups = words.shape[0] // SUBLANES
    for cb in range(PACK_BLOCKS):
        ref[:, cb, :, :] = words[:, cb * LANES:(cb + 1) * LANES].reshape(groups, SUBLANES, LANES)


def _load_packed(ref):
    rows = ref.shape[0] * SUBLANES
    return jnp.concatenate(
        [ref[:, cb, :, :].reshape(rows, LANES) for cb in range(PACK_BLOCKS)], axis=1)


L0_A = 0
L0_G = CONV_CH
L0_Q = 2 * CONV_CH
L0_KV = L0_Q + Q_LORA
L0_KR = L0_KV + KV_LORA
L0_KRR = L0_KR + PAD_HEAD
L0_COLS = L0_KRR + PAD_HEAD


def _l0_front_kernel(x_ref, pos_ref, w_in_ref, b_in_ref, conv_w_ref, conv_b_ref,
                     cln_g_ref, cln_b_ref, qn_g_ref, wq_ref, wqr_ref, kvn_g_ref,
                     wk_ref, wv_ref, invf_ref,
                     u_ref, q_ref, k_ref, v_ref, ubuf):
    ts = x_ref.shape[0]

    @pl.when(pl.program_id(1) == 0)
    def _():
        ubuf[0:CONV_HALO, :] = jnp.zeros((CONV_HALO, CONV_CH), F32)

    h = _dot(x_ref[...].astype(BF16), w_in_ref[...]) + b_in_ref[...]

    ubuf[CONV_HALO:CONV_HALO + ts, :] = (
        h[:, L0_A:L0_A + CONV_CH] * jax.nn.sigmoid(h[:, L0_G:L0_G + CONV_CH]))
    first_tap = CONV_HALO - (CONV_WIDTH - 1)

    def conv_chunk(c, carry):
        base = pl.multiple_of(c * CONV_ROWS, CONV_ROWS)
        acc = jnp.broadcast_to(conv_b_ref[...], (CONV_ROWS, CONV_CH))
        for res in range(SUBLANES):
            rows = CONV_ROWS + (SUBLANES if res else 0)
            part = None
            for off in range(res, first_tap + CONV_WIDTH, SUBLANES):
                j = off - first_tap
                if j < 0:
                    continue
                term = ubuf[pl.ds(base + (off - res), rows), :] * conv_w_ref[j:j + 1, :]
                part = term if part is None else part + term
            acc = acc + part[res:res + CONV_ROWS, :]
        y = _layer_norm(acc, cln_g_ref[...], cln_b_ref[...])
        u_ref[pl.ds(base, CONV_ROWS), :] = (y * jax.nn.sigmoid(y)).astype(BF16)
        return carry

    lax.fori_loop(0, ts // CONV_ROWS, conv_chunk, 0)
    ubuf[0:CONV_HALO, :] = ubuf[ts:ts + CONV_HALO, :]

    ang = pos_ref[...].astype(F32) * invf_ref[...]
    cos = jnp.cos(ang)
    sin = jnp.sin(ang)
    scale = (QK_NOPE + QK_ROPE) ** -0.5 * LOG2_E

    qn = _rms_norm(h[:, L0_Q:L0_Q + Q_LORA], qn_g_ref[...]).astype(BF16)
    q = _dot(qn, wq_ref[...])
    q_rot = _dot(qn, wqr_ref[...])
    cos_s = cos * scale
    sin_s = sin * scale
    for hh in range(MLA_HEADS):
        blk = slice(hh * PAD_HEAD, (hh + 1) * PAD_HEAD)
        q_ref[:, blk] = (q[:, blk] * cos_s + q_rot[:, blk] * sin_s).astype(BF16)

    kvn = _rms_norm(h[:, L0_KV:L0_KV + KV_LORA], kvn_g_ref[...]).astype(BF16)
    k_nope = _dot(kvn, wk_ref[...])
    k_pe = h[:, L0_KR:L0_KR + PAD_HEAD] * cos + h[:, L0_KRR:L0_KRR + PAD_HEAD] * sin
    for hh in range(MLA_HEADS):
        blk = slice(hh * PAD_HEAD, (hh + 1) * PAD_HEAD)
        k_ref[:, blk] = (k_nope[:, blk] + k_pe).astype(BF16)
    v_ref[...] = (_dot(kvn, wv_ref[...]) + _ones_upper_half(v_ref.shape[1])).astype(BF16)


def _rope_rotate_cols(w):
    half = QK_ROPE // 2
    return jnp.concatenate([-w[..., half:], w[..., :half]], axis=-1)


def _l0_front(x, positions, w_in, b_in, conv_w, conv_b, cln_g, cln_b,
              qn_g, w_uq, kvn_g, w_ukv):
    bsz, seq, _ = x.shape
    pad_lo = jnp.zeros((D_MODEL, QK_NOPE), F32)
    pad_hi = jnp.zeros((D_MODEL, PAD_HEAD - QK_NOPE - QK_ROPE), F32)
    i3 = L0_KR
    w_kr = w_in[:, i3:i3 + QK_ROPE]
    w_in_p = jnp.concatenate(
        [w_in[:, :i3], pad_lo, w_kr, pad_hi, pad_lo, _rope_rotate_cols(w_kr), pad_hi],
        axis=1).astype(BF16)
    b_kr = b_in[i3:i3 + QK_ROPE]
    zlo = jnp.zeros((QK_NOPE,), F32)
    zhi = jnp.zeros((PAD_HEAD - QK_NOPE - QK_ROPE,), F32)
    b_in_p = jnp.concatenate(
        [b_in[:i3], zlo, b_kr, zhi, zlo, _rope_rotate_cols(b_kr), zhi])[None, :]

    dq = QK_NOPE + QK_ROPE
    wq3 = w_uq.reshape(Q_LORA, MLA_HEADS, dq)
    zq = jnp.zeros((Q_LORA, MLA_HEADS, PAD_HEAD - dq), F32)
    wq_p = jnp.concatenate([wq3, zq], axis=-1).reshape(Q_LORA, MLA_HEADS * PAD_HEAD)
    wqr_p = jnp.concatenate(
        [jnp.zeros((Q_LORA, MLA_HEADS, QK_NOPE), F32),
         _rope_rotate_cols(wq3[..., QK_NOPE:]), zq], axis=-1
    ).reshape(Q_LORA, MLA_HEADS * PAD_HEAD)
    wkv3 = w_ukv.reshape(KV_LORA, MLA_HEADS, QK_NOPE + V_DIM)
    wk_p = jnp.concatenate(
        [wkv3[..., :QK_NOPE], jnp.zeros((KV_LORA, MLA_HEADS, PAD_HEAD - QK_NOPE), F32)],
        axis=-1).reshape(KV_LORA, MLA_HEADS * PAD_HEAD)
    wv = _pad_heads(wkv3[..., QK_NOPE:].reshape(KV_LORA, MLA_HEADS * V_DIM), MLA_HEADS)

    inv_freq = 1.0 / (ROPE_THETA ** (jnp.arange(0, QK_ROPE, 2, dtype=F32) / QK_ROPE))
    invf = jnp.concatenate([jnp.zeros((QK_NOPE,), F32), inv_freq, inv_freq,
                            jnp.zeros((PAD_HEAD - QK_NOPE - QK_ROPE,), F32)])[None, :]

    ts = SEQ_TILE
    row = lambda w: pl.BlockSpec((None, ts, w), lambda b, s: (b, s, 0))
    qk_w = MLA_HEADS * PAD_HEAD
    v_w = MLA_HEADS * PAD_HEAD
    return pl.pallas_call(
        _l0_front_kernel,
        grid=(bsz, seq // ts),
        in_specs=[row(D_MODEL), row(1),
                  _resident((D_MODEL, L0_COLS)), _resident((1, L0_COLS)),
                  _resident((CONV_WIDTH, CONV_CH)), _resident((1, CONV_CH)),
                  _resident((1, CONV_CH)), _resident((1, CONV_CH)),
                  _resident((1, Q_LORA)), _resident((Q_LORA, qk_w)),
                  _resident((Q_LORA, qk_w)), _resident((1, KV_LORA)),
                  _resident((KV_LORA, qk_w)), _resident((KV_LORA, v_w)),
                  _resident((1, PAD_HEAD))],
        out_specs=[row(CONV_CH), row(qk_w), row(qk_w), row(v_w)],
        out_shape=[jax.ShapeDtypeStruct((bsz, seq, CONV_CH), BF16),
                   jax.ShapeDtypeStruct((bsz, seq, qk_w), BF16),
                   jax.ShapeDtypeStruct((bsz, seq, qk_w), BF16),
                   jax.ShapeDtypeStruct((bsz, seq, v_w), BF16)],
        scratch_shapes=[pltpu.VMEM((ts + CONV_HALO, CONV_CH), F32)],
        compiler_params=_params("arbitrary", "arbitrary"),
        name="l0_front",
    )(x, positions[..., None], w_in_p, b_in_p, conv_w, conv_b[None, :],
      cln_g[None, :], cln_b[None, :], qn_g[None, :], wq_p.astype(BF16),
      wqr_p.astype(BF16), kvn_g[None, :], wk_p.astype(BF16), wv.astype(BF16), invf)


def _l1_front_kernel(x_ref, w_ref, b_ref, wv_ref, bv_ref, wf_ref, bf_ref,
                     q_ref, k_ref, v_ref, cneg_ref, carry):
    ts = x_ref.shape[0]

    @pl.when(pl.program_id(1) == 0)
    def _():
        carry[...] = jnp.zeros_like(carry)

    xb = x_ref[...].astype(BF16)
    mix = FOX_HEADS * FOX_HD
    scale = FOX_HD ** -0.5 * LOG2_E
    q_ref[...] = ((_dot(xb, w_ref[:, 0:mix]) + b_ref[:, 0:mix]) * scale).astype(BF16)
    k_ref[...] = (_dot(xb, w_ref[:, mix:2 * mix]) + b_ref[:, mix:2 * mix]).astype(BF16)
    v_ref[...] = (_dot(xb, wv_ref[...]) + bv_ref[...]
                  + _ones_upper_half(v_ref.shape[1])).astype(BF16)

    log_f = jax.nn.log_sigmoid(_dot_nt(wf_ref[...], xb) + bf_ref[...])
    r = lax.broadcasted_iota(jnp.int32, (ts, ts), 0)
    c = lax.broadcasted_iota(jnp.int32, (ts, ts), 1)
    upper = (r <= c).astype(F32)
    csum = jnp.dot(log_f, upper, precision=lax.Precision.HIGHEST,
                   preferred_element_type=F32) + carry[...]
    cneg_ref[...] = csum * -LOG2_E
    carry[...] = carry[...] + jnp.sum(log_f, axis=1, keepdims=True)


def _l1_front(x, w_in, b_in):
    bsz, seq, _ = x.shape
    mix = FOX_HEADS * FOX_HD
    ts = SEQ_TILE
    row = lambda w: pl.BlockSpec((None, ts, w), lambda b, s: (b, s, 0))
    w_qk = w_in[:, :2 * mix].astype(BF16)
    b_qk = b_in[None, :2 * mix]
    w_v = _pad_heads(w_in[:, 2 * mix:3 * mix], FOX_HEADS).astype(BF16)
    b_v = _pad_heads(b_in[None, 2 * mix:3 * mix], FOX_HEADS)
    v_w = FOX_HEADS * PAD_HEAD
    wf_t = w_in[:, 3 * mix:].T.astype(BF16)
    bf_t = b_in[3 * mix:][:, None]
    return pl.pallas_call(
        _l1_front_kernel,
        grid=(bsz, seq // ts),
        in_specs=[row(D_MODEL), _resident((D_MODEL, 2 * mix)), _resident((1, 2 * mix)),
                  _resident((D_MODEL, v_w)), _resident((1, v_w)),
                  _resident((FOX_HEADS, D_MODEL)), _resident((FOX_HEADS, 1))],
        out_specs=[row(mix), row(mix), row(v_w),
                   pl.BlockSpec((None, FOX_HEADS, ts), lambda b, s: (b, 0, s))],
        out_shape=[jax.ShapeDtypeStruct((bsz, seq, mix), BF16)] * 2
        + [jax.ShapeDtypeStruct((bsz, seq, v_w), BF16),
           jax.ShapeDtypeStruct((bsz, FOX_HEADS, seq), F32)],
        scratch_shapes=[pltpu.VMEM((FOX_HEADS, 1), F32)],
        compiler_params=_params("arbitrary", "arbitrary"),
        name="l1_front",
    )(x, w_qk, b_qk, w_v, b_v, wf_t, bf_t)


def _attn_kernel(*refs, packed, has_bias):
    if has_bias:
        q_ref, k_ref, v_ref, cneg_ref, o_ref = refs
    else:
        q_ref, k_ref, v_ref, o_ref = refs
        cneg_ref = None
    tq = q_ref.shape[0]
    tk = ATTN_KEY_TILE
    qi = pl.program_id(2)
    q2 = q_ref[...]
    half = LANES // 2
    if packed:
        lane = lax.broadcasted_iota(jnp.int32, q2.shape, 1)
        zero = jnp.zeros_like(q2)
        qs = (jnp.where(lane < half, q2, zero), jnp.where(lane >= half, q2, zero))
    else:
        qs = (q2[:, :PAD_HEAD], q2[:, PAD_HEAD:])

    def k_block(kj, head):
        rows = pl.ds(pl.multiple_of(kj * tk, tk), tk)
        if packed:
            return k_ref[rows, :]
        return k_ref[rows, head * PAD_HEAD:(head + 1) * PAD_HEAD]

    def step(kj, carry, diag):
        rows = pl.ds(pl.multiple_of(kj * tk, tk), tk)
        out = []
        for head in range(2):
            m, acc = carry[head]
            s = _dot_nt(qs[head], k_block(kj, head))
            if has_bias:
                s = s + cneg_ref[head, :, rows]
            if diag is not None:
                r = lax.broadcasted_iota(jnp.int32, s.shape, 0)
                c = lax.broadcasted_iota(jnp.int32, s.shape, 1)
                s = jnp.where(r >= c + diag * tk, s, NEG_INF)
            m_new = jnp.maximum(m, jnp.max(s, axis=-1, keepdims=True))
            alpha = jnp.exp2(m - m_new)
            p = jnp.exp2(s - m_new).astype(BF16)
            vblk = v_ref[rows, head * PAD_HEAD:(head + 1) * PAD_HEAD]
            out.append((m_new, alpha * acc + _dot(p, vblk)))
        return tuple(out)

    init = tuple((jnp.full((tq, 1), NEG_INF, F32), jnp.zeros((tq, LANES), F32))
                 for _ in range(2))
    per_tile = tq // tk
    n_full = qi * per_tile
    def group(first, c, size):
        for t in range(size):
            c = step(first + t, c, None)
        return c

    carry = lax.fori_loop(0, n_full // ATTN_UNROLL,
                          lambda j, c: group(ATTN_UNROLL * j, c, ATTN_UNROLL), init)
    done = (n_full // ATTN_UNROLL) * ATTN_UNROLL
    size = ATTN_UNROLL // 2
    while size >= 1:
        take = (n_full - done) >= size
        carry = lax.cond(take, lambda c, f=done, n=size: group(f, c, n), lambda c: c, carry)
        done = done + jnp.where(take, size, 0)
        size //= 2
    for d in range(per_tile):
        carry = step(n_full + d, carry, d)
    (_, acc_a), (_, acc_b) = carry
    out_a = acc_a / pltpu.roll(acc_a, half, axis=1)
    out_b = acc_b / pltpu.roll(acc_b, half, axis=1)
    lane_o = lax.broadcasted_iota(jnp.int32, (tq, LANES), 1)
    o_ref[...] = jnp.where(lane_o < half, out_a,
                           pltpu.roll(out_b, half, axis=1)).astype(o_ref.dtype)


def _attention(q, k, v, cneg, packed):
    bsz, seq, v_w = v.shape
    pairs = v_w // (2 * PAD_HEAD)
    qk_w = q.shape[-1] // pairs
    tq = ATTN_TILE
    in_specs = [pl.BlockSpec((None, tq, qk_w), lambda b, p, i: (b, i, p)),
                pl.BlockSpec((None, seq, qk_w), lambda b, p, i: (b, 0, p)),
                pl.BlockSpec((None, seq, 2 * PAD_HEAD), lambda b, p, i: (b, 0, p))]
    args = [q, k, v]
    if cneg is not None:
        in_specs.append(pl.BlockSpec((None, 2, 1, seq), lambda b, p, i: (b, p, 0, 0)))
        args.append(cneg)
    return pl.pallas_call(
        functools.partial(_attn_kernel, packed=packed, has_bias=cneg is not None),
        grid=(bsz, pairs, seq // tq),
        in_specs=in_specs,
        out_specs=pl.BlockSpec((None, tq, LANES), lambda b, p, i: (b, i, p)),
        out_shape=jax.ShapeDtypeStruct((bsz, seq, pairs * LANES), BF16),
        compiler_params=_params("arbitrary", "arbitrary", "arbitrary"),
        name="attn_packed" if packed else "attn_padded",
    )(*args)


def _first_hit(hits, found):
    out = []
    for hcur in hits:
        take = jnp.logical_and(hcur, jnp.logical_not(found))
        found = jnp.logical_or(found, take)
        out.append(take)
    return out, found


def _mix_out_router_kernel(a_ref, b_ref, x_ref, wa_ref, wb_ref, bo_ref, g_ref, be_ref,
                           rw_ref, rwl_ref, rb_ref,
                           x1_ref, x1p_ref, e_ref, gate_ref, rank_ref, cnt_ref, cnt, gate_t):
    tm = x_ref.shape[0]

    @pl.when(pl.program_id(0) == 0)
    def _():
        cnt[...] = jnp.zeros_like(cnt)
        gate_t[...] = jnp.zeros_like(gate_t)

    mix = _dot(a_ref[...], wa_ref[...]) + _dot(b_ref[...], wb_ref[...]) + bo_ref[...]
    x1 = _layer_norm(DN_ALPHA * x_ref[...] + mix, g_ref[...], be_ref[...])
    x1_ref[...] = x1
    _store_packed(x1p_ref, _pack_rows(x1))

    x_hi = x1.astype(BF16)
    x_lo = (x1 - x_hi.astype(F32)).astype(BF16)
    logits = (_dot_nt(rw_ref[...], x_hi) + _dot_nt(rw_ref[...], x_lo)
              + _dot_nt(rwl_ref[...], x_hi))
    aff = jax.nn.sigmoid(logits)
    choice = aff + rb_ref[...]
    per_group = N_EXPERTS // N_GROUPS
    sub = lax.broadcasted_iota(jnp.int32, (per_group, tm), 0)
    groups = [choice[g * per_group:(g + 1) * per_group, :] for g in range(N_GROUPS)]

    gscore = []
    for cg in groups:
        m1 = jnp.max(cg, axis=0, keepdims=True)
        i1 = jnp.min(jnp.where(cg == m1, sub, per_group), axis=0, keepdims=True)
        m2 = jnp.max(jnp.where(sub == i1, NEG_INF, cg), axis=0, keepdims=True)
        gscore.append(m1 + m2)

    gsel = [jnp.zeros((1, tm), jnp.bool_) for _ in range(N_GROUPS)]
    for _ in range(TOPK_GROUPS):
        best = functools.reduce(jnp.maximum, gscore)
        takes, _ = _first_hit([gs == best for gs in gscore], jnp.zeros((1, tm), jnp.bool_))
        gsel = [jnp.logical_or(a, t) for a, t in zip(gsel, takes)]
        gscore = [jnp.where(t, NEG_INF, gs) for gs, t in zip(gscore, takes)]

    masked = [jnp.where(gs, cg, NEG_INF) for gs, cg in zip(gsel, groups)]
    eid = [sub + g * per_group for g in range(N_GROUPS)]
    affs = [aff[g * per_group:(g + 1) * per_group, :] for g in range(N_GROUPS)]
    sel = [jnp.zeros((per_group, tm), jnp.bool_) for _ in range(N_GROUPS)]
    picked_e, picked_w = [], []
    for _ in range(TOP_K):
        best = jnp.max(functools.reduce(jnp.maximum, masked), axis=0, keepdims=True)
        cand = [jnp.where(mg == best, ig, N_EXPERTS) for mg, ig in zip(masked, eid)]
        idx = jnp.min(functools.reduce(jnp.minimum, cand), axis=0, keepdims=True)
        onehot = [ig == idx for ig in eid]
        w = functools.reduce(
            jnp.add, [jnp.sum(jnp.where(oh, ag, 0.0), axis=0, keepdims=True)
                      for oh, ag in zip(onehot, affs)])
        picked_e.append(idx)
        picked_w.append(w)
        sel = [jnp.logical_or(sg, oh) for sg, oh in zip(sel, onehot)]
        masked = [jnp.where(oh, NEG_INF, mg) for mg, oh in zip(masked, onehot)]

    wsum = functools.reduce(jnp.add, picked_w)

    sel_f = jnp.concatenate([sg.astype(F32) for sg in sel], axis=0)
    r = lax.broadcasted_iota(jnp.int32, (tm, tm), 0)
    c = lax.broadcasted_iota(jnp.int32, (tm, tm), 1)
    before = _dot(sel_f.astype(BF16), (r < c).astype(BF16)) + cnt[...]
    eall = lax.broadcasted_iota(jnp.int32, (N_EXPERTS, tm), 0)
    for kk in range(TOP_K):
        rank = jnp.sum(jnp.where(eall == picked_e[kk], before, 0.0), axis=0, keepdims=True)
        e_ref[kk:kk + 1, :] = picked_e[kk]
        gate_t[kk:kk + 1, :] = picked_w[kk] / wsum * ROUTED_SCALE
        rank_ref[kk:kk + 1, :] = rank.astype(jnp.int32)
    pad = SUBLANES - TOP_K
    e_ref[TOP_K:, :] = jnp.zeros((pad, tm), jnp.int32)
    rank_ref[TOP_K:, :] = jnp.zeros((pad, tm), jnp.int32)
    gate_ref[...] = gate_t[...].T
    cnt[...] = cnt[...] + jnp.sum(sel_f, axis=1, keepdims=True)
    cnt_ref[...] = jnp.broadcast_to(cnt[...], cnt_ref.shape)


def _mix_out_router(a, b, x, w_out, b_out, ln_g, ln_b, router_w, router_bias):
    n_tok = x.shape[0]
    tm = TOKEN_TILE
    half = w_out.shape[0] // 2
    row = lambda w: pl.BlockSpec((tm, w), lambda i: (i, 0))
    b_spec = (pl.BlockSpec((tm, half), lambda i: (i, 1)) if b is a
              else pl.BlockSpec((tm, half), lambda i: (i, 0)))
    lane_row = pl.BlockSpec((SUBLANES, tm), lambda i: (0, i))
    packed = pl.BlockSpec((tm // SUBLANES, PACK_BLOCKS, SUBLANES, LANES),
                          lambda i: (i, 0, 0, 0))
    w_bf = w_out.astype(BF16)
    rw_t = router_w.T
    rw_hi = rw_t.astype(BF16)
    rw_lo = (rw_t - rw_hi.astype(F32)).astype(BF16)
    return pl.pallas_call(
        _mix_out_router_kernel,
        grid=(n_tok // tm,),
        in_specs=[row(half), b_spec, row(D_MODEL),
                  pl.BlockSpec((half, D_MODEL), lambda i: (0, 0)),
                  pl.BlockSpec((half, D_MODEL), lambda i: (1, 0)),
                  _resident((1, D_MODEL)), _resident((1, D_MODEL)), _resident((1, D_MODEL)),
                  _resident((N_EXPERTS, D_MODEL)), _resident((N_EXPERTS, D_MODEL)),
                  _resident((N_EXPERTS, 1))],
        out_specs=[row(D_MODEL), packed, lane_row, row(LANES), lane_row,
                   _resident((N_EXPERTS, LANES))],
        out_shape=[jax.ShapeDtypeStruct((n_tok, D_MODEL), F32),
                   jax.ShapeDtypeStruct((n_tok // SUBLANES, PACK_BLOCKS, SUBLANES, LANES),
                                        jnp.int32),
                   jax.ShapeDtypeStruct((SUBLANES, n_tok), jnp.int32),
                   jax.ShapeDtypeStruct((n_tok, LANES), F32),
                   jax.ShapeDtypeStruct((SUBLANES, n_tok), jnp.int32),
                   jax.ShapeDtypeStruct((N_EXPERTS, LANES), F32)],
        scratch_shapes=[pltpu.VMEM((N_EXPERTS, 1), F32), pltpu.VMEM((LANES, tm), F32)],
        compiler_params=_params("arbitrary"),
        name="mix_out_router",
    )(a, b, x, w_bf, w_bf, b_out[None, :], ln_g[None, :], ln_b[None, :],
      rw_hi, rw_lo, router_bias[:, None])


def _expert_ffn_kernel(tile_e_ref, tile_on_ref, xs_ref, wg_ref, wu_ref, wd_ref, ys_ref,
                       wg_bf, wu_bf, wd_bf):
    i = pl.program_id(0)
    on = tile_on_ref[i] > 0
    fresh = jnp.logical_or(i == 0, tile_e_ref[i] != tile_e_ref[jnp.maximum(i - 1, 0)])

    @pl.when(jnp.logical_and(on, fresh))
    def _():
        wg_bf[...] = wg_ref[...].astype(BF16)
        wu_bf[...] = wu_ref[...].astype(BF16)
        wd_bf[...] = wd_ref[...].astype(BF16)

    @pl.when(on)
    def _():
        lo, hi = _unpack_rows(_load_packed(xs_ref))
        lo = lo.astype(BF16)
        hi = hi.astype(BF16)
        gate = _dot(lo, wg_bf[:PACK_COLS, :]) + _dot(hi, wg_bf[PACK_COLS:, :])
        up = _dot(lo, wu_bf[:PACK_COLS, :]) + _dot(hi, wu_bf[PACK_COLS:, :])
        hid = (jax.nn.silu(gate) * up).astype(BF16)
        _store_packed(ys_ref, _pack_rows(_dot(hid, wd_bf[...])))

    @pl.when(jnp.logical_not(on))
    def _():
        ys_ref[...] = jnp.zeros_like(ys_ref)


def _expert_ffn(xs, tile_e, tile_on, w_gate, w_up, w_down):
    tg = GROUP_TILE
    packed = pl.BlockSpec((tg // SUBLANES, PACK_BLOCKS, SUBLANES, LANES),
                          lambda i, te, on: (i, 0, 0, 0))
    grid_spec = pltpu.PrefetchScalarGridSpec(
        num_scalar_prefetch=2,
        grid=(xs.shape[0] * SUBLANES // tg,),
        in_specs=[packed,
                  pl.BlockSpec((None, D_MODEL, EXPERT_HID), lambda i, te, on: (te[i], 0, 0)),
                  pl.BlockSpec((None, D_MODEL, EXPERT_HID), lambda i, te, on: (te[i], 0, 0)),
                  pl.BlockSpec((None, EXPERT_HID, D_MODEL), lambda i, te, on: (te[i], 0, 0))],
        out_specs=packed,
        scratch_shapes=[pltpu.VMEM((D_MODEL, EXPERT_HID), BF16),
                        pltpu.VMEM((D_MODEL, EXPERT_HID), BF16),
                        pltpu.VMEM((EXPERT_HID, D_MODEL), BF16)],
    )
    return pl.pallas_call(
        _expert_ffn_kernel,
        grid_spec=grid_spec,
        out_shape=jax.ShapeDtypeStruct(xs.shape, jnp.int32),
        compiler_params=_params("arbitrary"),
        name="expert_ffn",
    )(tile_e, tile_on, xs, w_gate, w_up, w_down)


def _moe_out_kernel(x1_ref, yg_ref, gate_ref, sg_ref, su_ref, sd_ref, g_ref, b_ref, o_ref):
    x1 = x1_ref[...]
    xb = x1.astype(BF16)
    hid = jax.nn.silu(_dot(xb, sg_ref[...])) * _dot(xb, su_ref[...])
    shared = _dot(hid.astype(BF16), sd_ref[...])
    gates = gate_ref[...]
    lo_acc = shared[:, :PACK_COLS]
    hi_acc = shared[:, PACK_COLS:]
    for kk in range(TOP_K):
        lo, hi = _unpack_rows(_load_packed(yg_ref.at[kk]))
        w = gates[:, kk:kk + 1]
        lo_acc = lo_acc + w * lo
        hi_acc = hi_acc + w * hi
    moe = jnp.concatenate([lo_acc, hi_acc], axis=1)
    o_ref[...] = _layer_norm(DN_ALPHA * x1 + moe, g_ref[...], b_ref[...])


def _moe_out(x1, yg, gates, sh_gate, sh_up, sh_down, ln_g, ln_b):
    n_tok = x1.shape[0]
    tm = TOKEN_TILE
    row = lambda w: pl.BlockSpec((tm, w), lambda i: (i, 0))
    return pl.pallas_call(
        _moe_out_kernel,
        grid=(n_tok // tm,),
        in_specs=[row(D_MODEL),
                  pl.BlockSpec((TOP_K, tm // SUBLANES, PACK_BLOCKS, SUBLANES, LANES),
                               lambda i: (0, i, 0, 0, 0)),
                  row(LANES),
                  _resident((D_MODEL, EXPERT_HID)), _resident((D_MODEL, EXPERT_HID)),
                  _resident((EXPERT_HID, D_MODEL)), _resident((1, D_MODEL)),
                  _resident((1, D_MODEL))],
        out_specs=row(D_MODEL),
        out_shape=jax.ShapeDtypeStruct((n_tok, D_MODEL), F32),
        compiler_params=_params("arbitrary"),
        name="moe_out",
    )(x1, yg, gates, sh_gate.astype(BF16), sh_up.astype(BF16), sh_down.astype(BF16),
      ln_g[None, :], ln_b[None, :])


def _slot_kernel(offset_ref, e_ref, rank_ref, pos_ref):
    e = e_ref[...]
    base = jnp.zeros(e.shape, jnp.int32)
    for ex in range(N_EXPERTS):
        base = jnp.where(e == ex, offset_ref[ex], base)
    pos_ref[...] = base + rank_ref[...]


def _slots(offset, e6, rank6):
    n_tok = e6.shape[1]
    tl = 4096
    lane_row = pl.BlockSpec((SUBLANES, tl), lambda i, off: (0, i))
    return pl.pallas_call(
        _slot_kernel,
        grid_spec=pltpu.PrefetchScalarGridSpec(
            num_scalar_prefetch=1, grid=(n_tok // tl,),
            in_specs=[lane_row, lane_row], out_specs=lane_row),
        out_shape=jax.ShapeDtypeStruct((SUBLANES, n_tok), jnp.int32),
        compiler_params=_params("arbitrary"),
        name="slots",
    )(offset, e6, rank6)


def _sc_mesh():
    return plsc.VectorSubcoreMesh(core_axis_name="c", subcore_axis_name="s")


def _sc_worker():
    return lax.axis_index("s") * SC_CORES + lax.axis_index("c")


def _sc_dispatch(x_sub, idx, n_out_sub):
    windows = x_sub.shape[0] // (SC_WORKERS * SC_WINDOW)

    @functools.partial(
        pl.kernel, mesh=_sc_mesh(),
        out_type=jax.ShapeDtypeStruct((n_out_sub, LANES), jnp.int32),
        scratch_types=[pltpu.VMEM((SC_WINDOW, LANES), jnp.int32),
                       pltpu.VMEM((TOP_K, SC_WINDOW), jnp.int32)],
        name="sc_dispatch")
    def run(x_hbm, idx_hbm, out_hbm, rows_v, idx_v):
        wid = _sc_worker()

        @pl.loop(0, windows)
        def _(c):
            base = (wid * windows + c) * SC_WINDOW
            pltpu.sync_copy(x_hbm.at[pl.ds(base, SC_WINDOW)], rows_v)
            pltpu.sync_copy(idx_hbm.at[wid, c], idx_v)
            for kk in range(TOP_K):
                pltpu.sync_copy(rows_v, out_hbm.at[idx_v.at[kk]])

    return run(x_sub, idx)


def _sc_combine(y_sub, idx, n_tok_sub):
    windows = n_tok_sub // (SC_WORKERS * SC_WINDOW)

    @functools.partial(
        pl.kernel, mesh=_sc_mesh(),
        out_type=jax.ShapeDtypeStruct((TOP_K, n_tok_sub, LANES), jnp.int32),
        scratch_types=[pltpu.VMEM((2, SC_WINDOW, LANES), jnp.int32),
                       pltpu.VMEM((TOP_K, SC_WINDOW), jnp.int32),
                       pltpu.SemaphoreType.DMA((2,)),
                       pltpu.SemaphoreType.DMA((2,))],
        name="sc_combine")
    def run(y_hbm, idx_hbm, out_hbm, rows_v, idx_v, gather_sem, write_sem):
        wid = _sc_worker()

        def gather(kk):
            buf = kk % 2
            return pltpu.async_copy(y_hbm.at[idx_v.at[kk]], rows_v.at[buf], gather_sem.at[buf])

        @pl.loop(0, windows)
        def _(c):
            base = (wid * windows + c) * SC_WINDOW
            pltpu.sync_copy(idx_hbm.at[wid, c], idx_v)
            writes = []
            pending = gather(0)
            for kk in range(TOP_K):
                pending.wait()
                if kk + 1 < TOP_K:
                    if kk >= 1:
                        writes[kk - 1].wait()
                    pending = gather(kk + 1)
                writes.append(pltpu.async_copy(
                    rows_v.at[kk % 2], out_hbm.at[kk, pl.ds(base, SC_WINDOW)],
                    write_sem.at[kk % 2]))
            writes[TOP_K - 2].wait()
            writes[TOP_K - 1].wait()

    return run(y_sub, idx)


def _moe(x1, x1p, e6, gates, rank6, counts, exp_w_gate, exp_w_up, exp_w_down,
         sh_w_gate, sh_w_up, sh_w_down, ln_g, ln_b):
    n_tok = x1.shape[0]
    tg = GROUP_TILE
    n_rows = n_tok * TOP_K + N_EXPERTS * tg
    n_tiles = n_rows // tg

    cnt = counts[:, 0].astype(jnp.int32)
    tiles_per_e = (cnt + tg - 1) // tg
    tile_end = jnp.cumsum(tiles_per_e)
    offset = (tile_end - tiles_per_e) * tg
    tile_ids = jnp.arange(n_tiles, dtype=jnp.int32)
    tile_on = (tile_ids < tile_end[-1]).astype(jnp.int32)
    tile_e = jnp.minimum(
        jnp.sum((tile_ids[:, None] >= tile_end[None, :]).astype(jnp.int32), axis=1),
        N_EXPERTS - 1)
    tile_e = jnp.where(tile_on > 0, tile_e, tile_e[jnp.maximum(tile_end[-1] - 1, 0)])

    pos = _slots(offset, e6, rank6)[:TOP_K]

    group = PACK_BLOCKS * SUBLANES
    sub0 = (pos // SUBLANES) * group + pos % SUBLANES
    idx = (sub0.reshape(TOP_K, n_tok // SUBLANES, 1, SUBLANES)
           + (jnp.arange(PACK_BLOCKS, dtype=jnp.int32) * SUBLANES)[None, None, :, None])
    n_sub = n_tok * PACK_BLOCKS
    windows = n_sub // (SC_WORKERS * SC_WINDOW)
    idx = idx.reshape(TOP_K, SC_WORKERS, windows, SC_WINDOW).transpose(1, 2, 0, 3)

    xs = _sc_dispatch(x1p.reshape(n_sub, LANES), idx, n_rows * PACK_BLOCKS)
    ys = _expert_ffn(xs.reshape(n_rows // SUBLANES, PACK_BLOCKS, SUBLANES, LANES),
                     tile_e, tile_on, exp_w_gate, exp_w_up, exp_w_down)
    yg = _sc_combine(ys.reshape(n_rows * PACK_BLOCKS, LANES), idx, n_sub)
    yg = yg.reshape(TOP_K, n_tok // SUBLANES, PACK_BLOCKS, SUBLANES, LANES)
    return _moe_out(x1, yg, gates, sh_w_gate, sh_w_up, sh_w_down, ln_g, ln_b)


def kernel(x, positions,
           l0_w_in, l0_b_in, l0_conv_w, l0_conv_b, l0_conv_ln_g, l0_conv_ln_b,
           l0_q_norm_g, l0_w_uq, l0_kv_norm_g, l0_w_ukv, l0_w_out, l0_b_out,
           l0_ln1_g, l0_ln1_b,
           l0_router_w, l0_router_bias, l0_exp_w_gate, l0_exp_w_up, l0_exp_w_down,
           l0_sh_w_gate, l0_sh_w_up, l0_sh_w_down, l0_ln2_g, l0_ln2_b,
           l1_w_in, l1_b_in, l1_w_out, l1_b_out, l1_ln1_g, l1_ln1_b,
           l1_router_w, l1_router_bias, l1_exp_w_gate, l1_exp_w_up, l1_exp_w_down,
           l1_sh_w_gate, l1_sh_w_up, l1_sh_w_down, l1_ln2_g, l1_ln2_b):
    bsz, seq, d = x.shape
    n_tok = bsz * seq

    u, q, k, v = _l0_front(x, positions, l0_w_in, l0_b_in, l0_conv_w, l0_conv_b,
                           l0_conv_ln_g, l0_conv_ln_b, l0_q_norm_g, l0_w_uq,
                           l0_kv_norm_g, l0_w_ukv)
    o = _attention(q, k, v, None, packed=False)
    x_flat = x.reshape(n_tok, d)
    routing = _mix_out_router(u.reshape(n_tok, -1), o.reshape(n_tok, -1), x_flat,
                              l0_w_out, l0_b_out, l0_ln1_g, l0_ln1_b,
                              l0_router_w, l0_router_bias)
    x2 = _moe(*routing, l0_exp_w_gate, l0_exp_w_up, l0_exp_w_down,
              l0_sh_w_gate, l0_sh_w_up, l0_sh_w_down, l0_ln2_g, l0_ln2_b)

    q, k, v, cneg = _l1_front(x2.reshape(bsz, seq, d), l1_w_in, l1_b_in)
    o = _attention(q, k, v, cneg[:, :, None, :], packed=True).reshape(n_tok, -1)
    routing = _mix_out_router(o, o, x2, l1_w_out, l1_b_out, l1_ln1_g, l1_ln1_b,
                              l1_router_w, l1_router_bias)
    x3 = _moe(*routing, l1_exp_w_gate, l1_exp_w_up, l1_exp_w_down,
              l1_sh_w_gate, l1_sh_w_up, l1_sh_w_down, l1_ln2_g, l1_ln2_b)
    return x3.reshape(bsz, seq, d)
```

```python
import functools

import jax
import jax.numpy as jnp
from jax import lax
from jax.experimental import pallas as pl
from jax.experimental.pallas import tpu as pltpu
from jax.experimental.pallas import tpu_sc as plsc

D_MODEL = 1024
DEPTH = 2
DN_ALPHA = (2 * DEPTH) ** 0.25
LN_EPS = 1e-5
RMS_EPS = 1e-6

CONV_CH = 512
CONV_WIDTH = 31
MLA_HEADS = 8
QK_NOPE = 64
QK_ROPE = 32
V_DIM = 64
Q_LORA = 256
KV_LORA = 128
ROPE_THETA = 10000.0
FOX_HEADS = 16
FOX_HD = 64
N_EXPERTS = 64
TOP_K = 6
N_GROUPS = 8
TOPK_GROUPS = 4
EXPERT_HID = 256
ROUTED_SCALE = 2.5

LANES = 128
SUBLANES = 8
VMEM_LIMIT_BYTES = 56 * 1024 * 1024

SEQ_TILE = 512
TOKEN_TILE = 512
ATTN_TILE = 512
ATTN_KEY_TILE = 512
ATTN_UNROLL = 4
GROUP_TILE = 512
CONV_ROWS = 32
CONV_HALO = 32
PAD_HEAD = 128

PACK_COLS = D_MODEL // 2
PACK_BLOCKS = PACK_COLS // LANES
SC_CORES = 2
SC_SUBCORES = 16
SC_WORKERS = SC_CORES * SC_SUBCORES
SC_WINDOW = 128
MOE_PARTS = 2
BF16 = jnp.bfloat16
F32 = jnp.float32
NEG_INF = float("-inf")
LOG2_E = 1.4426950408889634


def _params(*semantics):
    return pltpu.CompilerParams(dimension_semantics=semantics,
                                vmem_limit_bytes=VMEM_LIMIT_BYTES)


def _resident(shape):
    nd = len(shape)
    return pl.BlockSpec(shape, lambda *_: (0,) * nd)


def _dot(a, b):
    return jnp.dot(a, b, preferred_element_type=F32)


def _dot_nt(a, b, precision=None):
    return lax.dot_general(a, b, (((1,), (1,)), ((), ())),
                           precision=precision, preferred_element_type=F32)


def _layer_norm(x, g, b):
    mu = jnp.mean(x, axis=-1, keepdims=True)
    xc = x - mu
    var = jnp.mean(xc * xc, axis=-1, keepdims=True)
    return xc * lax.rsqrt(var + LN_EPS) * g + b


def _rms_norm(x, g):
    return x * lax.rsqrt(jnp.mean(x * x, axis=-1, keepdims=True) + RMS_EPS) * g


def _ones_upper_half(width):
    lane = lax.broadcasted_iota(jnp.int32, (1, width), 1)
    return jnp.where((lane & (PAD_HEAD - 1)) >= V_DIM, 1.0, 0.0).astype(F32)


def _pad_heads(w, heads):
    d = w.shape[-1] // heads
    w3 = w.reshape(w.shape[:-1] + (heads, d))
    pad = jnp.zeros(w.shape[:-1] + (heads, PAD_HEAD - d), w.dtype)
    return jnp.concatenate([w3, pad], axis=-1).reshape(w.shape[:-1] + (heads * PAD_HEAD,))


def _pack_rows(x):
    lo = lax.bitcast_convert_type(x[:, :PACK_COLS].astype(BF16).astype(F32), jnp.uint32)
    hi = lax.bitcast_convert_type(x[:, PACK_COLS:].astype(BF16).astype(F32), jnp.uint32)
    word = (lo >> 16) | (hi & jnp.uint32(0xFFFF0000))
    return lax.bitcast_convert_type(word, jnp.int32)


def _unpack_rows(words):
    u = lax.bitcast_convert_type(words, jnp.uint32)
    lo = lax.bitcast_convert_type(u << 16, F32)
    hi = lax.bitcast_convert_type(u & jnp.uint32(0xFFFF0000), F32)
    return lo, hi


def _store_packed(ref, words):
    groups = words.shape[0] // SUBLANES
    for cb in range(PACK_BLOCKS):
        ref[:, cb, :, :] = words[:, cb * LANES:(cb + 1) * LANES].reshape(groups, SUBLANES, LANES)


def _load_packed(ref):
    rows = ref.shape[0] * SUBLANES
    return jnp.concatenate(
        [ref[:, cb, :, :].reshape(rows, LANES) for cb in range(PACK_BLOCKS)], axis=1)


L0_A = 0
L0_G = CONV_CH
L0_Q = 2 * CONV_CH
L0_KV = L0_Q + Q_LORA
L0_KR = L0_KV + KV_LORA
L0_KRR = L0_KR + PAD_HEAD
L0_COLS = L0_KRR + PAD_HEAD


def _l0_front_kernel(x_ref, pos_ref, w_in_ref, b_in_ref, conv_w_ref, conv_b_ref,
                     cln_g_ref, cln_b_ref, qn_g_ref, wq_ref, wqr_ref, kvn_g_ref,
                     wk_ref, wv_ref, invf_ref,
                     u_ref, q_ref, k_ref, v_ref, ubuf):
    ts = x_ref.shape[0]

    @pl.when(pl.program_id(1) == 0)
    def _():
        ubuf[0:CONV_HALO, :] = jnp.zeros((CONV_HALO, CONV_CH), F32)

    h = _dot(x_ref[...].astype(BF16), w_in_ref[...]) + b_in_ref[...]

    ubuf[CONV_HALO:CONV_HALO + ts, :] = (
        h[:, L0_A:L0_A + CONV_CH] * jax.nn.sigmoid(h[:, L0_G:L0_G + CONV_CH]))
    first_tap = CONV_HALO - (CONV_WIDTH - 1)

    def conv_chunk(c, carry):
        base = pl.multiple_of(c * CONV_ROWS, CONV_ROWS)
        acc = jnp.broadcast_to(conv_b_ref[...], (CONV_ROWS, CONV_CH))
        for res in range(SUBLANES):
            rows = CONV_ROWS + (SUBLANES if res else 0)
            part = None
            for off in range(res, first_tap + CONV_WIDTH, SUBLANES):
                j = off - first_tap
                if j < 0:
                    continue
                term = ubuf[pl.ds(base + (off - res), rows), :] * conv_w_ref[j:j + 1, :]
                part = term if part is None else part + term
            acc = acc + part[res:res + CONV_ROWS, :]
        y = _layer_norm(acc, cln_g_ref[...], cln_b_ref[...])
        u_ref[pl.ds(base, CONV_ROWS), :] = (y * jax.nn.sigmoid(y)).astype(BF16)
        return carry

    lax.fori_loop(0, ts // CONV_ROWS, conv_chunk, 0)
    ubuf[0:CONV_HALO, :] = ubuf[ts:ts + CONV_HALO, :]

    ang = pos_ref[...].astype(F32) * invf_ref[...]
    cos = jnp.cos(ang)
    sin = jnp.sin(ang)
    scale = (QK_NOPE + QK_ROPE) ** -0.5 * LOG2_E

    qn = _rms_norm(h[:, L0_Q:L0_Q + Q_LORA], qn_g_ref[...]).astype(BF16)
    q = _dot(qn, wq_ref[...])
    q_rot = _dot(qn, wqr_ref[...])
    cos_s = cos * scale
    sin_s = sin * scale
    for hh in range(MLA_HEADS):
        blk = slice(hh * PAD_HEAD, (hh + 1) * PAD_HEAD)
        q_ref[:, blk] = (q[:, blk] * cos_s + q_rot[:, blk] * sin_s).astype(BF16)

    kvn = _rms_norm(h[:, L0_KV:L0_KV + KV_LORA], kvn_g_ref[...]).astype(BF16)
    k_nope = _dot(kvn, wk_ref[...])
    k_pe = h[:, L0_KR:L0_KR + PAD_HEAD] * cos + h[:, L0_KRR:L0_KRR + PAD_HEAD] * sin
    for hh in range(MLA_HEADS):
        blk = slice(hh * PAD_HEAD, (hh + 1) * PAD_HEAD)
        k_ref[:, blk] = (k_nope[:, blk] + k_pe).astype(BF16)
    v_ref[...] = (_dot(kvn, wv_ref[...]) + _ones_upper_half(v_ref.shape[1])).astype(BF16)


def _rope_rotate_cols(w):
    half = QK_ROPE // 2
    return jnp.concatenate([-w[..., half:], w[..., :half]], axis=-1)


def _l0_front(x, positions, w_in, b_in, conv_w, conv_b, cln_g, cln_b,
              qn_g, w_uq, kvn_g, w_ukv):
    bsz, seq, _ = x.shape
    pad_lo = jnp.zeros((D_MODEL, QK_NOPE), F32)
    pad_hi = jnp.zeros((D_MODEL, PAD_HEAD - QK_NOPE - QK_ROPE), F32)
    i3 = L0_KR
    w_kr = w_in[:, i3:i3 + QK_ROPE]
    w_in_p = jnp.concatenate(
        [w_in[:, :i3], pad_lo, w_kr, pad_hi, pad_lo, _rope_rotate_cols(w_kr), pad_hi],
        axis=1).astype(BF16)
    b_kr = b_in[i3:i3 + QK_ROPE]
    zlo = jnp.zeros((QK_NOPE,), F32)
    zhi = jnp.zeros((PAD_HEAD - QK_NOPE - QK_ROPE,), F32)
    b_in_p = jnp.concatenate(
        [b_in[:i3], zlo, b_kr, zhi, zlo, _rope_rotate_cols(b_kr), zhi])[None, :]

    dq = QK_NOPE + QK_ROPE
    wq3 = w_uq.reshape(Q_LORA, MLA_HEADS, dq)
    zq = jnp.zeros((Q_LORA, MLA_HEADS, PAD_HEAD - dq), F32)
    wq_p = jnp.concatenate([wq3, zq], axis=-1).reshape(Q_LORA, MLA_HEADS * PAD_HEAD)
    wqr_p = jnp.concatenate(
        [jnp.zeros((Q_LORA, MLA_HEADS, QK_NOPE), F32),
         _rope_rotate_cols(wq3[..., QK_NOPE:]), zq], axis=-1
    ).reshape(Q_LORA, MLA_HEADS * PAD_HEAD)
    wkv3 = w_ukv.reshape(KV_LORA, MLA_HEADS, QK_NOPE + V_DIM)
    wk_p = jnp.concatenate(
        [wkv3[..., :QK_NOPE], jnp.zeros((KV_LORA, MLA_HEADS, PAD_HEAD - QK_NOPE), F32)],
        axis=-1).reshape(KV_LORA, MLA_HEADS * PAD_HEAD)
    wv = _pad_heads(wkv3[..., QK_NOPE:].reshape(KV_LORA, MLA_HEADS * V_DIM), MLA_HEADS)

    inv_freq = 1.0 / (ROPE_THETA ** (jnp.arange(0, QK_ROPE, 2, dtype=F32) / QK_ROPE))
    invf = jnp.concatenate([jnp.zeros((QK_NOPE,), F32), inv_freq, inv_freq,
                            jnp.zeros((PAD_HEAD - QK_NOPE - QK_ROPE,), F32)])[None, :]

    ts = SEQ_TILE
    row = lambda w: pl.BlockSpec((None, ts, w), lambda b, s: (b, s, 0))
    qk_w = MLA_HEADS * PAD_HEAD
    v_w = MLA_HEADS * PAD_HEAD
    return pl.pallas_call(
        _l0_front_kernel,
        grid=(bsz, seq // ts),
        in_specs=[row(D_MODEL), row(1),
                  _resident((D_MODEL, L0_COLS)), _resident((1, L0_COLS)),
                  _resident((CONV_WIDTH, CONV_CH)), _resident((1, CONV_CH)),
                  _resident((1, CONV_CH)), _resident((1, CONV_CH)),
                  _resident((1, Q_LORA)), _resident((Q_LORA, qk_w)),
                  _resident((Q_LORA, qk_w)), _resident((1, KV_LORA)),
                  _resident((KV_LORA, qk_w)), _resident((KV_LORA, v_w)),
                  _resident((1, PAD_HEAD))],
        out_specs=[row(CONV_CH), row(qk_w), row(qk_w), row(v_w)],
        out_shape=[jax.ShapeDtypeStruct((bsz, seq, CONV_CH), BF16),
                   jax.ShapeDtypeStruct((bsz, seq, qk_w), BF16),
                   jax.ShapeDtypeStruct((bsz, seq, qk_w), BF16),
                   jax.ShapeDtypeStruct((bsz, seq, v_w), BF16)],
        scratch_shapes=[pltpu.VMEM((ts + CONV_HALO, CONV_CH), F32)],
        compiler_params=_params("arbitrary", "arbitrary"),
        name="l0_front",
    )(x, positions[..., None], w_in_p, b_in_p, conv_w, conv_b[None, :],
      cln_g[None, :], cln_b[None, :], qn_g[None, :], wq_p.astype(BF16),
      wqr_p.astype(BF16), kvn_g[None, :], wk_p.astype(BF16), wv.astype(BF16), invf)


def _l1_front_kernel(x_ref, w_ref, b_ref, wv_ref, bv_ref, wf_ref, bf_ref,
                     q_ref, k_ref, v_ref, cneg_ref, carry):
    ts = x_ref.shape[0]

    @pl.when(pl.program_id(1) == 0)
    def _():
        carry[...] = jnp.zeros_like(carry)

    xb = x_ref[...].astype(BF16)
    mix = FOX_HEADS * FOX_HD
    scale = FOX_HD ** -0.5 * LOG2_E
    q_ref[...] = ((_dot(xb, w_ref[:, 0:mix]) + b_ref[:, 0:mix]) * scale).astype(BF16)
    k_ref[...] = (_dot(xb, w_ref[:, mix:2 * mix]) + b_ref[:, mix:2 * mix]).astype(BF16)
    v_ref[...] = (_dot(xb, wv_ref[...]) + bv_ref[...]
                  + _ones_upper_half(v_ref.shape[1])).astype(BF16)

    log_f = jax.nn.log_sigmoid(_dot_nt(wf_ref[...], xb) + bf_ref[...])
    r = lax.broadcasted_iota(jnp.int32, (ts, ts), 0)
    c = lax.broadcasted_iota(jnp.int32, (ts, ts), 1)
    upper = (r <= c).astype(F32)
    csum = jnp.dot(log_f, upper, precision=lax.Precision.HIGHEST,
                   preferred_element_type=F32) + carry[...]
    cneg_ref[...] = csum * -LOG2_E
    carry[...] = carry[...] + jnp.sum(log_f, axis=1, keepdims=True)


def _l1_front(x, w_in, b_in):
    bsz, seq, _ = x.shape
    mix = FOX_HEADS * FOX_HD
    ts = SEQ_TILE
    row = lambda w: pl.BlockSpec((None, ts, w), lambda b, s: (b, s, 0))
    w_qk = w_in[:, :2 * mix].astype(BF16)
    b_qk = b_in[None, :2 * mix]
    w_v = _pad_heads(w_in[:, 2 * mix:3 * mix], FOX_HEADS).astype(BF16)
    b_v = _pad_heads(b_in[None, 2 * mix:3 * mix], FOX_HEADS)
    v_w = FOX_HEADS * PAD_HEAD
    wf_t = w_in[:, 3 * mix:].T.astype(BF16)
    bf_t = b_in[3 * mix:][:, None]
    return pl.pallas_call(
        _l1_front_kernel,
        grid=(bsz, seq // ts),
        in_specs=[row(D_MODEL), _resident((D_MODEL, 2 * mix)), _resident((1, 2 * mix)),
                  _resident((D_MODEL, v_w)), _resident((1, v_w)),
                  _resident((FOX_HEADS, D_MODEL)), _resident((FOX_HEADS, 1))],
        out_specs=[row(mix), row(mix), row(v_w),
                   pl.BlockSpec((None, FOX_HEADS, ts), lambda b, s: (b, 0, s))],
        out_shape=[jax.ShapeDtypeStruct((bsz, seq, mix), BF16)] * 2
        + [jax.ShapeDtypeStruct((bsz, seq, v_w), BF16),
           jax.ShapeDtypeStruct((bsz, FOX_HEADS, seq), F32)],
        scratch_shapes=[pltpu.VMEM((FOX_HEADS, 1), F32)],
        compiler_params=_params("arbitrary", "arbitrary"),
        name="l1_front",
    )(x, w_qk, b_qk, w_v, b_v, wf_t, bf_t)


def _attn_kernel(*refs, packed, has_bias):
    if has_bias:
        q_ref, k_ref, v_ref, cneg_ref, o_ref = refs
    else:
        q_ref, k_ref, v_ref, o_ref = refs
        cneg_ref = None
    tq = q_ref.shape[0]
    tk = ATTN_KEY_TILE
    qi = pl.program_id(2)
    q2 = q_ref[...]
    half = LANES // 2
    if packed:
        lane = lax.broadcasted_iota(jnp.int32, q2.shape, 1)
        zero = jnp.zeros_like(q2)
        qs = (jnp.where(lane < half, q2, zero), jnp.where(lane >= half, q2, zero))
    else:
        qs = (q2[:, :PAD_HEAD], q2[:, PAD_HEAD:])

    def k_block(kj, head):
        rows = pl.ds(pl.multiple_of(kj * tk, tk), tk)
        if packed:
            return k_ref[rows, :]
        return k_ref[rows, head * PAD_HEAD:(head + 1) * PAD_HEAD]

    def step(kj, carry, diag):
        rows = pl.ds(pl.multiple_of(kj * tk, tk), tk)
        out = []
        for head in range(2):
            m, acc = carry[head]
            s = _dot_nt(qs[head], k_block(kj, head))
            if has_bias:
                s = s + cneg_ref[head, :, rows]
            if diag is not None:
                r = lax.broadcasted_iota(jnp.int32, s.shape, 0)
                c = lax.broadcasted_iota(jnp.int32, s.shape, 1)
                s = jnp.where(r >= c + diag * tk, s, NEG_INF)
            m_new = jnp.maximum(m, jnp.max(s, axis=-1, keepdims=True))
            alpha = jnp.exp2(m - m_new)
            p = jnp.exp2(s - m_new).astype(BF16)
            vblk = v_ref[rows, head * PAD_HEAD:(head + 1) * PAD_HEAD]
            out.append((m_new, alpha * acc + _dot(p, vblk)))
        return tuple(out)

    init = tuple((jnp.full((tq, 1), NEG_INF, F32), jnp.zeros((tq, LANES), F32))
                 for _ in range(2))
    per_tile = tq // tk
    n_full = qi * per_tile
    def group(first, c, size):
        for t in range(size):
            c = step(first + t, c, None)
        return c

    carry = lax.fori_loop(0, n_full // ATTN_UNROLL,
                          lambda j, c: group(ATTN_UNROLL * j, c, ATTN_UNROLL), init)
    done = (n_full // ATTN_UNROLL) * ATTN_UNROLL
    size = ATTN_UNROLL // 2
    while size >= 1:
        take = (n_full - done) >= size
        carry = lax.cond(take, lambda c, f=done, n=size: group(f, c, n), lambda c: c, carry)
        done = done + jnp.where(take, size, 0)
        size //= 2
    for d in range(per_tile):
        carry = step(n_full + d, carry, d)
    (_, acc_a), (_, acc_b) = carry
    out_a = acc_a / pltpu.roll(acc_a, half, axis=1)
    out_b = acc_b / pltpu.roll(acc_b, half, axis=1)
    lane_o = lax.broadcasted_iota(jnp.int32, (tq, LANES), 1)
    o_ref[...] = jnp.where(lane_o < half, out_a,
                           pltpu.roll(out_b, half, axis=1)).astype(o_ref.dtype)


def _attention(q, k, v, cneg, packed):
    bsz, seq, v_w = v.shape
    pairs = v_w // (2 * PAD_HEAD)
    qk_w = q.shape[-1] // pairs
    tq = ATTN_TILE
    in_specs = [pl.BlockSpec((None, tq, qk_w), lambda b, p, i: (b, i, p)),
                pl.BlockSpec((None, seq, qk_w), lambda b, p, i: (b, 0, p)),
                pl.BlockSpec((None, seq, 2 * PAD_HEAD), lambda b, p, i: (b, 0, p))]
    args = [q, k, v]
    if cneg is not None:
        in_specs.append(pl.BlockSpec((None, 2, 1, seq), lambda b, p, i: (b, p, 0, 0)))
        args.append(cneg)
    return pl.pallas_call(
        functools.partial(_attn_kernel, packed=packed, has_bias=cneg is not None),
        grid=(bsz, pairs, seq // tq),
        in_specs=in_specs,
        out_specs=pl.BlockSpec((None, tq, LANES), lambda b, p, i: (b, i, p)),
        out_shape=jax.ShapeDtypeStruct((bsz, seq, pairs * LANES), BF16),
        compiler_params=_params("arbitrary", "arbitrary", "arbitrary"),
        name="attn_packed" if packed else "attn_padded",
    )(*args)


def _first_hit(hits, found):
    out = []
    for hcur in hits:
        take = jnp.logical_and(hcur, jnp.logical_not(found))
        found = jnp.logical_or(found, take)
        out.append(take)
    return out, found


def _mix_out_router_kernel(a_ref, b_ref, x_ref, wa_ref, wb_ref, bo_ref, g_ref, be_ref,
                           rw_ref, rwl_ref, rb_ref,
                           x1_ref, x1p_ref, e_ref, gate_ref, rank_ref, cnt_ref, cnt, gate_t):
    tm = x_ref.shape[0]

    @pl.when(pl.program_id(0) == 0)
    def _():
        cnt[...] = jnp.zeros_like(cnt)
        gate_t[...] = jnp.zeros_like(gate_t)

    mix = _dot(a_ref[...], wa_ref[...]) + _dot(b_ref[...], wb_ref[...]) + bo_ref[...]
    x1 = _layer_norm(DN_ALPHA * x_ref[...] + mix, g_ref[...], be_ref[...])
    x1_ref[...] = x1
    _store_packed(x1p_ref, _pack_rows(x1))

    x_hi = x1.astype(BF16)
    x_lo = (x1 - x_hi.astype(F32)).astype(BF16)
    logits = (_dot_nt(rw_ref[...], x_hi) + _dot_nt(rw_ref[...], x_lo)
              + _dot_nt(rwl_ref[...], x_hi))
    aff = jax.nn.sigmoid(logits)
    choice = aff + rb_ref[...]
    per_group = N_EXPERTS // N_GROUPS
    sub = lax.broadcasted_iota(jnp.int32, (per_group, tm), 0)
    groups = [choice[g * per_group:(g + 1) * per_group, :] for g in range(N_GROUPS)]

    gscore = []
    for cg in groups:
        m1 = jnp.max(cg, axis=0, keepdims=True)
        i1 = jnp.min(jnp.where(cg == m1, sub, per_group), axis=0, keepdims=True)
        m2 = jnp.max(jnp.where(sub == i1, NEG_INF, cg), axis=0, keepdims=True)
        gscore.append(m1 + m2)

    gsel = [jnp.zeros((1, tm), jnp.bool_) for _ in range(N_GROUPS)]
    for _ in range(TOPK_GROUPS):
        best = functools.reduce(jnp.maximum, gscore)
        takes, _ = _first_hit([gs == best for gs in gscore], jnp.zeros((1, tm), jnp.bool_))
        gsel = [jnp.logical_or(a, t) for a, t in zip(gsel, takes)]
        gscore = [jnp.where(t, NEG_INF, gs) for gs, t in zip(gscore, takes)]

    masked = [jnp.where(gs, cg, NEG_INF) for gs, cg in zip(gsel, groups)]
    eid = [sub + g * per_group for g in range(N_GROUPS)]
    affs = [aff[g * per_group:(g + 1) * per_group, :] for g in range(N_GROUPS)]
    sel = [jnp.zeros((per_group, tm), jnp.bool_) for _ in range(N_GROUPS)]
    picked_e, picked_w = [], []
    for _ in range(TOP_K):
        best = jnp.max(functools.reduce(jnp.maximum, masked), axis=0, keepdims=True)
        cand = [jnp.where(mg == best, ig, N_EXPERTS) for mg, ig in zip(masked, eid)]
        idx = jnp.min(functools.reduce(jnp.minimum, cand), axis=0, keepdims=True)
        onehot = [ig == idx for ig in eid]
        w = functools.reduce(
            jnp.add, [jnp.sum(jnp.where(oh, ag, 0.0), axis=0, keepdims=True)
                      for oh, ag in zip(onehot, affs)])
        picked_e.append(idx)
        picked_w.append(w)
        sel = [jnp.logical_or(sg, oh) for sg, oh in zip(sel, onehot)]
        masked = [jnp.where(oh, NEG_INF, mg) for mg, oh in zip(masked, onehot)]

    wsum = functools.reduce(jnp.add, picked_w)

    sel_f = jnp.concatenate([sg.astype(F32) for sg in sel], axis=0)
    r = lax.broadcasted_iota(jnp.int32, (tm, tm), 0)
    c = lax.broadcasted_iota(jnp.int32, (tm, tm), 1)
    before = _dot(sel_f.astype(BF16), (r < c).astype(BF16)) + cnt[...]
    eall = lax.broadcasted_iota(jnp.int32, (N_EXPERTS, tm), 0)
    for kk in range(TOP_K):
        rank = jnp.sum(jnp.where(eall == picked_e[kk], before, 0.0), axis=0, keepdims=True)
        e_ref[kk:kk + 1, :] = picked_e[kk]
        gate_t[kk:kk + 1, :] = picked_w[kk] / wsum * ROUTED_SCALE
        rank_ref[kk:kk + 1, :] = rank.astype(jnp.int32)
    pad = SUBLANES - TOP_K
    e_ref[TOP_K:, :] = jnp.zeros((pad, tm), jnp.int32)
    rank_ref[TOP_K:, :] = jnp.zeros((pad, tm), jnp.int32)
    gate_ref[...] = gate_t[...].T
    cnt[...] = cnt[...] + jnp.sum(sel_f, axis=1, keepdims=True)
    cnt_ref[...] = jnp.broadcast_to(cnt[...], cnt_ref.shape)


def _mix_out_router(a, b, x, w_out, b_out, ln_g, ln_b, router_w, router_bias):
    n_tok = x.shape[0]
    tm = TOKEN_TILE
    half = w_out.shape[0] // 2
    row = lambda w: pl.BlockSpec((tm, w), lambda i: (i, 0))
    b_spec = (pl.BlockSpec((tm, half), lambda i: (i, 1)) if b is a
              else pl.BlockSpec((tm, half), lambda i: (i, 0)))
    lane_row = pl.BlockSpec((SUBLANES, tm), lambda i: (0, i))
    packed = pl.BlockSpec((tm // SUBLANES, PACK_BLOCKS, SUBLANES, LANES),
                          lambda i: (i, 0, 0, 0))
    w_bf = w_out.astype(BF16)
    rw_t = router_w.T
    rw_hi = rw_t.astype(BF16)
    rw_lo = (rw_t - rw_hi.astype(F32)).astype(BF16)
    return pl.pallas_call(
        _mix_out_router_kernel,
        grid=(n_tok // tm,),
        in_specs=[row(half), b_spec, row(D_MODEL),
                  pl.BlockSpec((half, D_MODEL), lambda i: (0, 0)),
                  pl.BlockSpec((half, D_MODEL), lambda i: (1, 0)),
                  _resident((1, D_MODEL)), _resident((1, D_MODEL)), _resident((1, D_MODEL)),
                  _resident((N_EXPERTS, D_MODEL)), _resident((N_EXPERTS, D_MODEL)),
                  _resident((N_EXPERTS, 1))],
        out_specs=[row(D_MODEL), packed, lane_row, row(LANES), lane_row,
                   _resident((N_EXPERTS, LANES))],
        out_shape=[jax.ShapeDtypeStruct((n_tok, D_MODEL), F32),
                   jax.ShapeDtypeStruct((n_tok // SUBLANES, PACK_BLOCKS, SUBLANES, LANES),
                                        jnp.int32),
                   jax.ShapeDtypeStruct((SUBLANES, n_tok), jnp.int32),
                   jax.ShapeDtypeStruct((n_tok, LANES), F32),
                   jax.ShapeDtypeStruct((SUBLANES, n_tok), jnp.int32),
                   jax.ShapeDtypeStruct((N_EXPERTS, LANES), F32)],
        scratch_shapes=[pltpu.VMEM((N_EXPERTS, 1), F32), pltpu.VMEM((LANES, tm), F32)],
        compiler_params=_params("arbitrary"),
        name="mix_out_router",
    )(a, b, x, w_bf, w_bf, b_out[None, :], ln_g[None, :], ln_b[None, :],
      rw_hi, rw_lo, router_bias[:, None])


def _expert_ffn_kernel(tile_e_ref, tile_on_ref, xs_ref, wg_ref, wu_ref, wd_ref, ys_ref,
                       wg_bf, wu_bf, wd_bf):
    i = pl.program_id(0)
    on = tile_on_ref[i] > 0
    fresh = jnp.logical_or(i == 0, tile_e_ref[i] != tile_e_ref[jnp.maximum(i - 1, 0)])

    @pl.when(jnp.logical_and(on, fresh))
    def _():
        wg_bf[...] = wg_ref[...].astype(BF16)
        wu_bf[...] = wu_ref[...].astype(BF16)
        wd_bf[...] = wd_ref[...].astype(BF16)

    @pl.when(on)
    def _():
        lo, hi = _unpack_rows(_load_packed(xs_ref))
        lo = lo.astype(BF16)
        hi = hi.astype(BF16)
        gate = _dot(lo, wg_bf[:PACK_COLS, :]) + _dot(hi, wg_bf[PACK_COLS:, :])
        up = _dot(lo, wu_bf[:PACK_COLS, :]) + _dot(hi, wu_bf[PACK_COLS:, :])
        hid = (jax.nn.silu(gate) * up).astype(BF16)
        _store_packed(ys_ref, _pack_rows(_dot(hid, wd_bf[...])))

    @pl.when(jnp.logical_not(on))
    def _():
        ys_ref[...] = jnp.zeros_like(ys_ref)


def _expert_ffn(xs, tile_e, tile_on, w_gate, w_up, w_down):
    tg = GROUP_TILE
    packed = pl.BlockSpec((tg // SUBLANES, PACK_BLOCKS, SUBLANES, LANES),
                          lambda i, te, on: (i, 0, 0, 0))
    grid_spec = pltpu.PrefetchScalarGridSpec(
        num_scalar_prefetch=2,
        grid=(xs.shape[0] * SUBLANES // tg,),
        in_specs=[packed,
                  pl.BlockSpec((None, D_MODEL, EXPERT_HID), lambda i, te, on: (te[i], 0, 0)),
                  pl.BlockSpec((None, D_MODEL, EXPERT_HID), lambda i, te, on: (te[i], 0, 0)),
                  pl.BlockSpec((None, EXPERT_HID, D_MODEL), lambda i, te, on: (te[i], 0, 0))],
        out_specs=packed,
        scratch_shapes=[pltpu.VMEM((D_MODEL, EXPERT_HID), BF16),
                        pltpu.VMEM((D_MODEL, EXPERT_HID), BF16),
                        pltpu.VMEM((EXPERT_HID, D_MODEL), BF16)],
    )
    return pl.pallas_call(
        _expert_ffn_kernel,
        grid_spec=grid_spec,
        out_shape=jax.ShapeDtypeStruct(xs.shape, jnp.int32),
        compiler_params=_params("arbitrary"),
        name="expert_ffn",
    )(tile_e, tile_on, xs, w_gate, w_up, w_down)


def _moe_out_kernel(x1_ref, yg_ref, gate_ref, sg_ref, su_ref, sd_ref, g_ref, b_ref, *rest):
    o_ref = rest[-1]
    x1 = x1_ref[...]
    xb = x1.astype(BF16)
    hid = jax.nn.silu(_dot(xb, sg_ref[...])) * _dot(xb, su_ref[...])
    shared = _dot(hid.astype(BF16), sd_ref[...])
    gates = gate_ref[...]
    lo_acc = shared[:, :PACK_COLS]
    hi_acc = shared[:, PACK_COLS:]
    for kk in range(TOP_K):
        lo, hi = _unpack_rows(_load_packed(yg_ref.at[kk]))
        w = gates[:, kk:kk + 1]
        lo_acc = lo_acc + w * lo
        hi_acc = hi_acc + w * hi
    moe = jnp.concatenate([lo_acc, hi_acc], axis=1)
    o_ref[...] = _layer_norm(DN_ALPHA * x1 + moe, g_ref[...], b_ref[...])


def _moe_out(x1, yg, gates, sh_gate, sh_up, sh_down, ln_g, ln_b, part, earlier):
    n_tok = x1.shape[0]
    tm = TOKEN_TILE
    steps = n_tok // tm // MOE_PARTS
    first = part * steps
    row = lambda w: pl.BlockSpec((tm, w), lambda i: (i + first, 0))
    in_specs = [row(D_MODEL),
                pl.BlockSpec((TOP_K, tm // SUBLANES, PACK_BLOCKS, SUBLANES, LANES),
                             lambda i: (0, i, 0, 0, 0)),
                row(LANES),
                _resident((D_MODEL, EXPERT_HID)), _resident((D_MODEL, EXPERT_HID)),
                _resident((EXPERT_HID, D_MODEL)), _resident((1, D_MODEL)),
                _resident((1, D_MODEL))]
    args = [x1, yg, gates, sh_gate.astype(BF16), sh_up.astype(BF16), sh_down.astype(BF16),
            ln_g[None, :], ln_b[None, :]]
    aliases = {}
    if earlier is not None:
        in_specs.append(pl.BlockSpec(memory_space=pl.ANY))
        args.append(earlier)
        aliases = {len(args) - 1: 0}
    return pl.pallas_call(
        _moe_out_kernel,
        grid=(steps,),
        in_specs=in_specs,
        out_specs=row(D_MODEL),
        out_shape=jax.ShapeDtypeStruct((n_tok, D_MODEL), F32),
        input_output_aliases=aliases,
        compiler_params=_params("arbitrary"),
        name="moe_out",
    )(*args)


def _slot_kernel(offset_ref, e_ref, rank_ref, pos_ref):
    e = e_ref[...]
    base = jnp.zeros(e.shape, jnp.int32)
    for ex in range(N_EXPERTS):
        base = jnp.where(e == ex, offset_ref[ex], base)
    pos_ref[...] = base + rank_ref[...]


def _slots(offset, e6, rank6):
    n_tok = e6.shape[1]
    tl = 4096
    lane_row = pl.BlockSpec((SUBLANES, tl), lambda i, off: (0, i))
    return pl.pallas_call(
        _slot_kernel,
        grid_spec=pltpu.PrefetchScalarGridSpec(
            num_scalar_prefetch=1, grid=(n_tok // tl,),
            in_specs=[lane_row, lane_row], out_specs=lane_row),
        out_shape=jax.ShapeDtypeStruct((SUBLANES, n_tok), jnp.int32),
        compiler_params=_params("arbitrary"),
        name="slots",
    )(offset, e6, rank6)


def _sc_mesh():
    return plsc.VectorSubcoreMesh(core_axis_name="c", subcore_axis_name="s")


def _sc_worker():
    return lax.axis_index("s") * SC_CORES + lax.axis_index("c")


def _sc_dispatch(x_sub, idx, n_out_sub):
    windows = x_sub.shape[0] // (SC_WORKERS * SC_WINDOW)

    @functools.partial(
        pl.kernel, mesh=_sc_mesh(),
        out_type=jax.ShapeDtypeStruct((n_out_sub, LANES), jnp.int32),
        scratch_types=[pltpu.VMEM((SC_WINDOW, LANES), jnp.int32),
                       pltpu.VMEM((TOP_K, SC_WINDOW), jnp.int32)],
        name="sc_dispatch")
    def run(x_hbm, idx_hbm, out_hbm, rows_v, idx_v):
        wid = _sc_worker()

        @pl.loop(0, windows)
        def _(c):
            base = (wid * windows + c) * SC_WINDOW
            pltpu.sync_copy(x_hbm.at[pl.ds(base, SC_WINDOW)], rows_v)
            pltpu.sync_copy(idx_hbm.at[wid, c], idx_v)
            for kk in range(TOP_K):
                pltpu.sync_copy(rows_v, out_hbm.at[idx_v.at[kk]])

    return run(x_sub, idx)


def _sc_combine(y_sub, idx, n_tok_sub):
    windows = n_tok_sub // (SC_WORKERS * SC_WINDOW)

    @functools.partial(
        pl.kernel, mesh=_sc_mesh(),
        out_type=jax.ShapeDtypeStruct((TOP_K, n_tok_sub, LANES), jnp.int32),
        scratch_types=[pltpu.VMEM((2, SC_WINDOW, LANES), jnp.int32),
                       pltpu.VMEM((TOP_K, SC_WINDOW), jnp.int32),
                       pltpu.SemaphoreType.DMA((2,)),
                       pltpu.SemaphoreType.DMA((2,))],
        name="sc_combine")
    def run(y_hbm, idx_hbm, out_hbm, rows_v, idx_v, gather_sem, write_sem):
        wid = _sc_worker()

        def gather(kk):
            buf = kk % 2
            return pltpu.async_copy(y_hbm.at[idx_v.at[kk]], rows_v.at[buf], gather_sem.at[buf])

        @pl.loop(0, windows)
        def _(c):
            base = (wid * windows + c) * SC_WINDOW
            pltpu.sync_copy(idx_hbm.at[wid, c], idx_v)
            writes = []
            pending = gather(0)
            for kk in range(TOP_K):
                pending.wait()
                if kk + 1 < TOP_K:
                    if kk >= 1:
                        writes[kk - 1].wait()
                    pending = gather(kk + 1)
                writes.append(pltpu.async_copy(
                    rows_v.at[kk % 2], out_hbm.at[kk, pl.ds(base, SC_WINDOW)],
                    write_sem.at[kk % 2]))
            writes[TOP_K - 2].wait()
            writes[TOP_K - 1].wait()

    return run(y_sub, idx)


def _moe(x1, x1p, e6, gates, rank6, counts, exp_w_gate, exp_w_up, exp_w_down,
         sh_w_gate, sh_w_up, sh_w_down, ln_g, ln_b):
    n_tok = x1.shape[0]
    tg = GROUP_TILE
    n_rows = n_tok * TOP_K + N_EXPERTS * tg
    n_tiles = n_rows // tg

    cnt = counts[:, 0].astype(jnp.int32)
    tiles_per_e = (cnt + tg - 1) // tg
    tile_end = jnp.cumsum(tiles_per_e)
    offset = (tile_end - tiles_per_e) * tg
    tile_ids = jnp.arange(n_tiles, dtype=jnp.int32)
    tile_on = (tile_ids < tile_end[-1]).astype(jnp.int32)
    tile_e = jnp.minimum(
        jnp.sum((tile_ids[:, None] >= tile_end[None, :]).astype(jnp.int32), axis=1),
        N_EXPERTS - 1)
    tile_e = jnp.where(tile_on > 0, tile_e, tile_e[jnp.maximum(tile_end[-1] - 1, 0)])

    pos = _slots(offset, e6, rank6)[:TOP_K]

    group = PACK_BLOCKS * SUBLANES
    sub0 = (pos // SUBLANES) * group + pos % SUBLANES
    idx = (sub0.reshape(TOP_K, n_tok // SUBLANES, 1, SUBLANES)
           + (jnp.arange(PACK_BLOCKS, dtype=jnp.int32) * SUBLANES)[None, None, :, None])
    n_sub = n_tok * PACK_BLOCKS
    idx = idx.reshape(TOP_K, n_sub)

    def windowed(ix):
        windows = ix.shape[1] // (SC_WORKERS * SC_WINDOW)
        return ix.reshape(TOP_K, SC_WORKERS, windows, SC_WINDOW).transpose(1, 2, 0, 3)

    xs = _sc_dispatch(x1p.reshape(n_sub, LANES), windowed(idx), n_rows * PACK_BLOCKS)
    ys = _expert_ffn(xs.reshape(n_rows // SUBLANES, PACK_BLOCKS, SUBLANES, LANES),
                     tile_e, tile_on, exp_w_gate, exp_w_up, exp_w_down)
    ys = ys.reshape(n_rows * PACK_BLOCKS, LANES)
    part_sub = n_sub // MOE_PARTS
    out = None
    for part in range(MOE_PARTS):
        yg = _sc_combine(ys, windowed(idx[:, part * part_sub:(part + 1) * part_sub]), part_sub)
        yg = yg.reshape(TOP_K, part_sub // (PACK_BLOCKS * SUBLANES), PACK_BLOCKS,
                        SUBLANES, LANES)
        out = _moe_out(x1, yg, gates, sh_w_gate, sh_w_up, sh_w_down, ln_g, ln_b, part, out)
    return out


def kernel(x, positions,
           l0_w_in, l0_b_in, l0_conv_w, l0_conv_b, l0_conv_ln_g, l0_conv_ln_b,
           l0_q_norm_g, l0_w_uq, l0_kv_norm_g, l0_w_ukv, l0_w_out, l0_b_out,
           l0_ln1_g, l0_ln1_b,
           l0_router_w, l0_router_bias, l0_exp_w_gate, l0_exp_w_up, l0_exp_w_down,
           l0_sh_w_gate, l0_sh_w_up, l0_sh_w_down, l0_ln2_g, l0_ln2_b,
           l1_w_in, l1_b_in, l1_w_out, l1_b_out, l1_ln1_g, l1_ln1_b,
           l1_router_w, l1_router_bias, l1_exp_w_gate, l1_exp_w_up, l1_exp_w_down,
           l1_sh_w_gate, l1_sh_w_up, l1_sh_w_down, l1_ln2_g, l1_ln2_b):
    bsz, seq, d = x.shape
    n_tok = bsz * seq

    u, q, k, v = _l0_front(x, positions, l0_w_in, l0_b_in, l0_conv_w, l0_conv_b,
                           l0_conv_ln_g, l0_conv_ln_b, l0_q_norm_g, l0_w_uq,
                           l0_kv_norm_g, l0_w_ukv)
    o = _attention(q, k, v, None, packed=False)
    x_flat = x.reshape(n_tok, d)
    routing = _mix_out_router(u.reshape(n_tok, -1), o.reshape(n_tok, -1), x_flat,
                              l0_w_out, l0_b_out, l0_ln1_g, l0_ln1_b,
                              l0_router_w, l0_router_bias)
    x2 = _moe(*routing, l0_exp_w_gate, l0_exp_w_up, l0_exp_w_down,
              l0_sh_w_gate, l0_sh_w_up, l0_sh_w_down, l0_ln2_g, l0_ln2_b)

    q, k, v, cneg = _l1_front(x2.reshape(bsz, seq, d), l1_w_in, l1_b_in)
    o = _attention(q, k, v, cneg[:, :, None, :], packed=True).reshape(n_tok, -1)
    routing = _mix_out_router(o, o, x2, l1_w_out, l1_b_out, l1_ln1_g, l1_ln1_b,
                              l1_router_w, l1_router_bias)
    x3 = _moe(*routing, l1_exp_w_gate, l1_exp_w_up, l1_exp_w_down,
              l1_sh_w_gate, l1_sh_w_up, l1_sh_w_down, l1_ln2_g, l1_ln2_b)
    return x3.reshape(bsz, seq, d)
```

```python
import functools

import jax
import jax.numpy as jnp
from jax import lax
from jax.experimental import pallas as pl
from jax.experimental.pallas import tpu as pltpu
from jax.experimental.pallas import tpu_sc as plsc

D_MODEL = 1024
DEPTH = 2
DN_ALPHA = (2 * DEPTH) ** 0.25
LN_EPS = 1e-5
RMS_EPS = 1e-6

CONV_CH = 512
CONV_WIDTH = 31
MLA_HEADS = 8
QK_NOPE = 64
QK_ROPE = 32
V_DIM = 64
Q_LORA = 256
KV_LORA = 128
ROPE_THETA = 10000.0
FOX_HEADS = 16
FOX_HD = 64
N_EXPERTS = 64
TOP_K = 6
N_GROUPS = 8
TOPK_GROUPS = 4
EXPERT_HID = 256
ROUTED_SCALE = 2.5

LANES = 128
SUBLANES = 8
VMEM_LIMIT_BYTES = 56 * 1024 * 1024

SEQ_TILE = 512
TOKEN_TILE = 512
ATTN_TILE = 512
ATTN_KEY_TILE = 512
ATTN_UNROLL = 4
GROUP_TILE = 512
CONV_ROWS = 32
CONV_HALO = 32
PAD_HEAD = 128

PACK_COLS = D_MODEL // 2
PACK_BLOCKS = PACK_COLS // LANES
SC_CORES = 2
SC_SUBCORES = 16
SC_WORKERS = SC_CORES * SC_SUBCORES
SC_WINDOW = 128
MOE_PARTS = 2
BF16 = jnp.bfloat16
F32 = jnp.float32
NEG_INF = float("-inf")
LOG2_E = 1.4426950408889634


def _params(*semantics):
    return pltpu.CompilerParams(dimension_semantics=semantics,
                                vmem_limit_bytes=VMEM_LIMIT_BYTES)


def _resident(shape):
    nd = len(shape)
    return pl.BlockSpec(shape, lambda *_: (0,) * nd)


def _dot(a, b):
    return jnp.dot(a, b, preferred_element_type=F32)


def _dot_nt(a, b, precision=None):
    return lax.dot_general(a, b, (((1,), (1,)), ((), ())),
                           precision=precision, preferred_element_type=F32)


def _layer_norm(x, g, b):
    mu = jnp.mean(x, axis=-1, keepdims=True)
    xc = x - mu
    var = jnp.mean(xc * xc, axis=-1, keepdims=True)
    return xc * lax.rsqrt(var + LN_EPS) * g + b


def _rms_norm(x, g):
    return x * lax.rsqrt(jnp.mean(x * x, axis=-1, keepdims=True) + RMS_EPS) * g


def _ones_upper_half(width):
    lane = lax.broadcasted_iota(jnp.int32, (1, width), 1)
    return jnp.where((lane & (PAD_HEAD - 1)) >= V_DIM, 1.0, 0.0).astype(F32)


def _pad_heads(w, heads):
    d = w.shape[-1] // heads
    w3 = w.reshape(w.shape[:-1] + (heads, d))
    pad = jnp.zeros(w.shape[:-1] + (heads, PAD_HEAD - d), w.dtype)
    return jnp.concatenate([w3, pad], axis=-1).reshape(w.shape[:-1] + (heads * PAD_HEAD,))


def _pack_rows(x):
    lo = lax.bitcast_convert_type(x[:, :PACK_COLS].astype(BF16).astype(F32), jnp.uint32)
    hi = lax.bitcast_convert_type(x[:, PACK_COLS:].astype(BF16).astype(F32), jnp.uint32)
    word = (lo >> 16) | (hi & jnp.uint32(0xFFFF0000))
    return lax.bitcast_convert_type(word, jnp.int32)


def _unpack_rows(words):
    u = lax.bitcast_convert_type(words, jnp.uint32)
    lo = lax.bitcast_convert_type(u << 16, F32)
    hi = lax.bitcast_convert_type(u & jnp.uint32(0xFFFF0000), F32)
    return lo, hi


def _store_packed(ref, words):
    groups = words.shape[0] // SUBLANES
    for cb in range(PACK_BLOCKS):
        ref[:, cb, :, :] = words[:, cb * LANES:(cb + 1) * LANES].reshape(groups, SUBLANES, LANES)


def _load_packed(ref):
    rows = ref.shape[0] * SUBLANES
    return jnp.concatenate(
        [ref[:, cb, :, :].reshape(rows, LANES) for cb in range(PACK_BLOCKS)], axis=1)


L0_A = 0
L0_G = CONV_CH
L0_Q = 2 * CONV_CH
L0_KV = L0_Q + Q_LORA
L0_KR = L0_KV + KV_LORA
L0_KRR = L0_KR + PAD_HEAD
L0_COLS = L0_KRR + PAD_HEAD


def _l0_front_kernel(x_ref, pos_ref, w_in_ref, b_in_ref, conv_w_ref, conv_b_ref,
                     cln_g_ref, cln_b_ref, qn_g_ref, wq_ref, wqr_ref, kvn_g_ref,
                     wk_ref, wv_ref, invf_ref,
                     u_ref, q_ref, k_ref, v_ref, ubuf):
    ts = x_ref.shape[0]

    @pl.when(pl.program_id(1) == 0)
    def _():
        ubuf[0:CONV_HALO, :] = jnp.zeros((CONV_HALO, CONV_CH), F32)

    h = _dot(x_ref[...].astype(BF16), w_in_ref[...]) + b_in_ref[...]

    ubuf[CONV_HALO:CONV_HALO + ts, :] = (
        h[:, L0_A:L0_A + CONV_CH] * jax.nn.sigmoid(h[:, L0_G:L0_G + CONV_CH]))
    first_tap = CONV_HALO - (CONV_WIDTH - 1)

    def conv_chunk(c, carry):
        base = pl.multiple_of(c * CONV_ROWS, CONV_ROWS)
        acc = jnp.broadcast_to(conv_b_ref[...], (CONV_ROWS, CONV_CH))
        for res in range(SUBLANES):
            rows = CONV_ROWS + (SUBLANES if res else 0)
            part = None
            for off in range(res, first_tap + CONV_WIDTH, SUBLANES):
                j = off - first_tap
                if j < 0:
                    continue
                term = ubuf[pl.ds(base + (off - res), rows), :] * conv_w_ref[j:j + 1, :]
                part = term if part is None else part + term
            acc = acc + part[res:res + CONV_ROWS, :]
        y = _layer_norm(acc, cln_g_ref[...], cln_b_ref[...])
        u_ref[pl.ds(base, CONV_ROWS), :] = (y * jax.nn.sigmoid(y)).astype(BF16)
        return carry

    lax.fori_loop(0, ts // CONV_ROWS, conv_chunk, 0)
    ubuf[0:CONV_HALO, :] = ubuf[ts:ts + CONV_HALO, :]

    ang = pos_ref[...].astype(F32) * invf_ref[...]
    cos = jnp.cos(ang)
    sin = jnp.sin(ang)
    scale = (QK_NOPE + QK_ROPE) ** -0.5 * LOG2_E

    qn = _rms_norm(h[:, L0_Q:L0_Q + Q_LORA], qn_g_ref[...]).astype(BF16)
    q = _dot(qn, wq_ref[...])
    q_rot = _dot(qn, wqr_ref[...])
    cos_s = cos * scale
    sin_s = sin * scale
    for hh in range(MLA_HEADS):
        blk = slice(hh * PAD_HEAD, (hh + 1) * PAD_HEAD)
        q_ref[:, blk] = (q[:, blk] * cos_s + q_rot[:, blk] * sin_s).astype(BF16)

    kvn = _rms_norm(h[:, L0_KV:L0_KV + KV_LORA], kvn_g_ref[...]).astype(BF16)
    k_nope = _dot(kvn, wk_ref[...])
    k_pe = h[:, L0_KR:L0_KR + PAD_HEAD] * cos + h[:, L0_KRR:L0_KRR + PAD_HEAD] * sin
    for hh in range(MLA_HEADS):
        blk = slice(hh * PAD_HEAD, (hh + 1) * PAD_HEAD)
        k_ref[:, blk] = (k_nope[:, blk] + k_pe).astype(BF16)
    v_ref[...] = (_dot(kvn, wv_ref[...]) + _ones_upper_half(v_ref.shape[1])).astype(BF16)


def _rope_rotate_cols(w):
    half = QK_ROPE // 2
    return jnp.concatenate([-w[..., half:], w[..., :half]], axis=-1)


def _l0_front(x, positions, w_in, b_in, conv_w, conv_b, cln_g, cln_b,
              qn_g, w_uq, kvn_g, w_ukv):
    bsz, seq, _ = x.shape
    pad_lo = jnp.zeros((D_MODEL, QK_NOPE), F32)
    pad_hi = jnp.zeros((D_MODEL, PAD_HEAD - QK_NOPE - QK_ROPE), F32)
    i3 = L0_KR
    w_kr = w_in[:, i3:i3 + QK_ROPE]
    w_in_p = jnp.concatenate(
        [w_in[:, :i3], pad_lo, w_kr, pad_hi, pad_lo, _rope_rotate_cols(w_kr), pad_hi],
        axis=1).astype(BF16)
    b_kr = b_in[i3:i3 + QK_ROPE]
    zlo = jnp.zeros((QK_NOPE,), F32)
    zhi = jnp.zeros((PAD_HEAD - QK_NOPE - QK_ROPE,), F32)
    b_in_p = jnp.concatenate(
        [b_in[:i3], zlo, b_kr, zhi, zlo, _rope_rotate_cols(b_kr), zhi])[None, :]

    dq = QK_NOPE + QK_ROPE
    wq3 = w_uq.reshape(Q_LORA, MLA_HEADS, dq)
    zq = jnp.zeros((Q_LORA, MLA_HEADS, PAD_HEAD - dq), F32)
    wq_p = jnp.concatenate([wq3, zq], axis=-1).reshape(Q_LORA, MLA_HEADS * PAD_HEAD)
    wqr_p = jnp.concatenate(
        [jnp.zeros((Q_LORA, MLA_HEADS, QK_NOPE), F32),
         _rope_rotate_cols(wq3[..., QK_NOPE:]), zq], axis=-1
    ).reshape(Q_LORA, MLA_HEADS * PAD_HEAD)
    wkv3 = w_ukv.reshape(KV_LORA, MLA_HEADS, QK_NOPE + V_DIM)
    wk_p = jnp.concatenate(
        [wkv3[..., :QK_NOPE], jnp.zeros((KV_LORA, MLA_HEADS, PAD_HEAD - QK_NOPE), F32)],
        axis=-1).reshape(KV_LORA, MLA_HEADS * PAD_HEAD)
    wv = _pad_heads(wkv3[..., QK_NOPE:].reshape(KV_LORA, MLA_HEADS * V_DIM), MLA_HEADS)

    inv_freq = 1.0 / (ROPE_THETA ** (jnp.arange(0, QK_ROPE, 2, dtype=F32) / QK_ROPE))
    invf = jnp.concatenate([jnp.zeros((QK_NOPE,), F32), inv_freq, inv_freq,
                            jnp.zeros((PAD_HEAD - QK_NOPE - QK_ROPE,), F32)])[None, :]

    ts = SEQ_TILE
    row = lambda w: pl.BlockSpec((None, ts, w), lambda b, s: (b, s, 0))
    qk_w = MLA_HEADS * PAD_HEAD
    v_w = MLA_HEADS * PAD_HEAD
    return pl.pallas_call(
        _l0_front_kernel,
        grid=(bsz, seq // ts),
        in_specs=[row(D_MODEL), row(1),
                  _resident((D_MODEL, L0_COLS)), _resident((1, L0_COLS)),
                  _resident((CONV_WIDTH, CONV_CH)), _resident((1, CONV_CH)),
                  _resident((1, CONV_CH)), _resident((1, CONV_CH)),
                  _resident((1, Q_LORA)), _resident((Q_LORA, qk_w)),
                  _resident((Q_LORA, qk_w)), _resident((1, KV_LORA)),
                  _resident((KV_LORA, qk_w)), _resident((KV_LORA, v_w)),
                  _resident((1, PAD_HEAD))],
        out_specs=[row(CONV_CH), row(qk_w), row(qk_w), row(v_w)],
        out_shape=[jax.ShapeDtypeStruct((bsz, seq, CONV_CH), BF16),
                   jax.ShapeDtypeStruct((bsz, seq, qk_w), BF16),
                   jax.ShapeDtypeStruct((bsz, seq, qk_w), BF16),
                   jax.ShapeDtypeStruct((bsz, seq, v_w), BF16)],
        scratch_shapes=[pltpu.VMEM((ts + CONV_HALO, CONV_CH), F32)],
        compiler_params=_params("arbitrary", "arbitrary"),
        name="l0_front",
    )(x, positions[..., None], w_in_p, b_in_p, conv_w, conv_b[None, :],
      cln_g[None, :], cln_b[None, :], qn_g[None, :], wq_p.astype(BF16),
      wqr_p.astype(BF16), kvn_g[None, :], wk_p.astype(BF16), wv.astype(BF16), invf)


def _l1_front_kernel(x_ref, w_ref, b_ref, wv_ref, bv_ref, wf_ref, bf_ref,
                     q_ref, k_ref, v_ref, cneg_ref, carry):
    ts = x_ref.shape[0]

    @pl.when(pl.program_id(1) == 0)
    def _():
        carry[...] = jnp.zeros_like(carry)

    xb = x_ref[...].astype(BF16)
    mix = FOX_HEADS * FOX_HD
    scale = FOX_HD ** -0.5 * LOG2_E
    q_ref[...] = ((_dot(xb, w_ref[:, 0:mix]) + b_ref[:, 0:mix]) * scale).astype(BF16)
    k_ref[...] = (_dot(xb, w_ref[:, mix:2 * mix]) + b_ref[:, mix:2 * mix]).astype(BF16)
    v_ref[...] = (_dot(xb, wv_ref[...]) + bv_ref[...]
                  + _ones_upper_half(v_ref.shape[1])).astype(BF16)

    log_f = jax.nn.log_sigmoid(_dot_nt(wf_ref[...], xb) + bf_ref[...])
    r = lax.broadcasted_iota(jnp.int32, (ts, ts), 0)
    c = lax.broadcasted_iota(jnp.int32, (ts, ts), 1)
    upper = (r <= c).astype(F32)
    csum = jnp.dot(log_f, upper, precision=lax.Precision.HIGHEST,
                   preferred_element_type=F32) + carry[...]
    cneg_ref[...] = csum * -LOG2_E
    carry[...] = carry[...] + jnp.sum(log_f, axis=1, keepdims=True)


def _l1_front(x, w_in, b_in):
    bsz, seq, _ = x.shape
    mix = FOX_HEADS * FOX_HD
    ts = SEQ_TILE
    row = lambda w: pl.BlockSpec((None, ts, w), lambda b, s: (b, s, 0))
    w_qk = w_in[:, :2 * mix].astype(BF16)
    b_qk = b_in[None, :2 * mix]
    w_v = _pad_heads(w_in[:, 2 * mix:3 * mix], FOX_HEADS).astype(BF16)
    b_v = _pad_heads(b_in[None, 2 * mix:3 * mix], FOX_HEADS)
    v_w = FOX_HEADS * PAD_HEAD
    wf_t = w_in[:, 3 * mix:].T.astype(BF16)
    bf_t = b_in[3 * mix:][:, None]
    return pl.pallas_call(
        _l1_front_kernel,
        grid=(bsz, seq // ts),
        in_specs=[row(D_MODEL), _resident((D_MODEL, 2 * mix)), _resident((1, 2 * mix)),
                  _resident((D_MODEL, v_w)), _resident((1, v_w)),
                  _resident((FOX_HEADS, D_MODEL)), _resident((FOX_HEADS, 1))],
        out_specs=[row(mix), row(mix), row(v_w),
                   pl.BlockSpec((None, FOX_HEADS, ts), lambda b, s: (b, 0, s))],
        out_shape=[jax.ShapeDtypeStruct((bsz, seq, mix), BF16)] * 2
        + [jax.ShapeDtypeStruct((bsz, seq, v_w), BF16),
           jax.ShapeDtypeStruct((bsz, FOX_HEADS, seq), F32)],
        scratch_shapes=[pltpu.VMEM((FOX_HEADS, 1), F32)],
        compiler_params=_params("arbitrary", "arbitrary"),
        name="l1_front",
    )(x, w_qk, b_qk, w_v, b_v, wf_t, bf_t)


def _attn_kernel(*refs, packed, has_bias):
    if has_bias:
        q_ref, k_ref, v_ref, cneg_ref, o_ref = refs
    else:
        q_ref, k_ref, v_ref, o_ref = refs
        cneg_ref = None
    tq = q_ref.shape[0]
    tk = ATTN_KEY_TILE
    qi = pl.program_id(2)
    q2 = q_ref[...]
    half = LANES // 2
    if packed:
        lane = lax.broadcasted_iota(jnp.int32, q2.shape, 1)
        zero = jnp.zeros_like(q2)
        qs = (jnp.where(lane < half, q2, zero), jnp.where(lane >= half, q2, zero))
    else:
        qs = (q2[:, :PAD_HEAD], q2[:, PAD_HEAD:])

    def k_block(kj, head):
        rows = pl.ds(pl.multiple_of(kj * tk, tk), tk)
        if packed:
            return k_ref[rows, :]
        return k_ref[rows, head * PAD_HEAD:(head + 1) * PAD_HEAD]

    def step(kj, carry, diag):
        rows = pl.ds(pl.multiple_of(kj * tk, tk), tk)
        out = []
        for head in range(2):
            m, acc = carry[head]
            s = _dot_nt(qs[head], k_block(kj, head))
            if has_bias:
                s = s + cneg_ref[head, :, rows]
            if diag is not None:
                r = lax.broadcasted_iota(jnp.int32, s.shape, 0)
                c = lax.broadcasted_iota(jnp.int32, s.shape, 1)
                s = jnp.where(r >= c + diag * tk, s, NEG_INF)
            m_new = jnp.maximum(m, jnp.max(s, axis=-1, keepdims=True))
            alpha = jnp.exp2(m - m_new)
            p = jnp.exp2(s - m_new).astype(BF16)
            vblk = v_ref[rows, head * PAD_HEAD:(head + 1) * PAD_HEAD]
            out.append((m_new, alpha * acc + _dot(p, vblk)))
        return tuple(out)

    init = tuple((jnp.full((tq, 1), NEG_INF, F32), jnp.zeros((tq, LANES), F32))
                 for _ in range(2))
    per_tile = tq // tk
    n_full = qi * per_tile
    def group(first, c, size):
        for t in range(size):
            c = step(first + t, c, None)
        return c

    carry = lax.fori_loop(0, n_full // ATTN_UNROLL,
                          lambda j, c: group(ATTN_UNROLL * j, c, ATTN_UNROLL), init)
    done = (n_full // ATTN_UNROLL) * ATTN_UNROLL
    size = ATTN_UNROLL // 2
    while size >= 1:
        take = (n_full - done) >= size
        carry = lax.cond(take, lambda c, f=done, n=size: group(f, c, n), lambda c: c, carry)
        done = done + jnp.where(take, size, 0)
        size //= 2
    for d in range(per_tile):
        carry = step(n_full + d, carry, d)
    (_, acc_a), (_, acc_b) = carry
    out_a = acc_a / pltpu.roll(acc_a, half, axis=1)
    out_b = acc_b / pltpu.roll(acc_b, half, axis=1)
    lane_o = lax.broadcasted_iota(jnp.int32, (tq, LANES), 1)
    o_ref[...] = jnp.where(lane_o < half, out_a,
                           pltpu.roll(out_b, half, axis=1)).astype(o_ref.dtype)


def _attention(q, k, v, cneg, packed):
    bsz, seq, v_w = v.shape
    pairs = v_w // (2 * PAD_HEAD)
    qk_w = q.shape[-1] // pairs
    tq = ATTN_TILE
    in_specs = [pl.BlockSpec((None, tq, qk_w), lambda b, p, i: (b, i, p)),
                pl.BlockSpec((None, seq, qk_w), lambda b, p, i: (b, 0, p)),
                pl.BlockSpec((None, seq, 2 * PAD_HEAD), lambda b, p, i: (b, 0, p))]
    args = [q, k, v]
    if cneg is not None:
        in_specs.append(pl.BlockSpec((None, 2, 1, seq), lambda b, p, i: (b, p, 0, 0)))
        args.append(cneg)
    return pl.pallas_call(
        functools.partial(_attn_kernel, packed=packed, has_bias=cneg is not None),
        grid=(bsz, pairs, seq // tq),
        in_specs=in_specs,
        out_specs=pl.BlockSpec((None, tq, LANES), lambda b, p, i: (b, i, p)),
        out_shape=jax.ShapeDtypeStruct((bsz, seq, pairs * LANES), BF16),
        compiler_params=_params("arbitrary", "arbitrary", "arbitrary"),
        name="attn_packed" if packed else "attn_padded",
    )(*args)


def _first_hit(hits, found):
    out = []
    for hcur in hits:
        take = jnp.logical_and(hcur, jnp.logical_not(found))
        found = jnp.logical_or(found, take)
        out.append(take)
    return out, found


def _mix_out_router_kernel(a_ref, b_ref, x_ref, wa_ref, wb_ref, bo_ref, g_ref, be_ref,
                           rw_ref, rwl_ref, rb_ref,
                           x1_ref, x1p_ref, e_ref, gate_ref, rank_ref, cnt_ref, cnt, gate_t):
    tm = x_ref.shape[0]

    @pl.when(pl.program_id(0) == 0)
    def _():
        cnt[...] = jnp.zeros_like(cnt)
        gate_t[...] = jnp.zeros_like(gate_t)

    mix = _dot(a_ref[...], wa_ref[...]) + _dot(b_ref[...], wb_ref[...]) + bo_ref[...]
    x1 = _layer_norm(DN_ALPHA * x_ref[...] + mix, g_ref[...], be_ref[...])
    x1_ref[...] = x1
    _store_packed(x1p_ref, _pack_rows(x1))

    x_hi = x1.astype(BF16)
    x_lo = (x1 - x_hi.astype(F32)).astype(BF16)
    logits = (_dot_nt(rw_ref[...], x_hi) + _dot_nt(rw_ref[...], x_lo)
              + _dot_nt(rwl_ref[...], x_hi))
    aff = jax.nn.sigmoid(logits)
    choice = aff + rb_ref[...]
    per_group = N_EXPERTS // N_GROUPS
    sub = lax.broadcasted_iota(jnp.int32, (per_group, tm), 0)
    groups = [choice[g * per_group:(g + 1) * per_group, :] for g in range(N_GROUPS)]

    gscore = []
    for cg in groups:
        m1 = jnp.max(cg, axis=0, keepdims=True)
        i1 = jnp.min(jnp.where(cg == m1, sub, per_group), axis=0, keepdims=True)
        m2 = jnp.max(jnp.where(sub == i1, NEG_INF, cg), axis=0, keepdims=True)
        gscore.append(m1 + m2)

    gsel = [jnp.zeros((1, tm), jnp.bool_) for _ in range(N_GROUPS)]
    for _ in range(TOPK_GROUPS):
        best = functools.reduce(jnp.maximum, gscore)
        takes, _ = _first_hit([gs == best for gs in gscore], jnp.zeros((1, tm), jnp.bool_))
        gsel = [jnp.logical_or(a, t) for a, t in zip(gsel, takes)]
        gscore = [jnp.where(t, NEG_INF, gs) for gs, t in zip(gscore, takes)]

    masked = [jnp.where(gs, cg, NEG_INF) for gs, cg in zip(gsel, groups)]
    eid = [sub + g * per_group for g in range(N_GROUPS)]
    affs = [aff[g * per_group:(g + 1) * per_group, :] for g in range(N_GROUPS)]
    sel = [jnp.zeros((per_group, tm), jnp.bool_) for _ in range(N_GROUPS)]
    picked_e, picked_w = [], []
    for _ in range(TOP_K):
        best = jnp.max(functools.reduce(jnp.maximum, masked), axis=0, keepdims=True)
        cand = [jnp.where(mg == best, ig, N_EXPERTS) for mg, ig in zip(masked, eid)]
        idx = jnp.min(functools.reduce(jnp.minimum, cand), axis=0, keepdims=True)
        onehot = [ig == idx for ig in eid]
        w = functools.reduce(
            jnp.add, [jnp.sum(jnp.where(oh, ag, 0.0), axis=0, keepdims=True)
                      for oh, ag in zip(onehot, affs)])
        picked_e.append(idx)
        picked_w.append(w)
        sel = [jnp.logical_or(sg, oh) for sg, oh in zip(sel, onehot)]
        masked = [jnp.where(oh, NEG_INF, mg) for mg, oh in zip(masked, onehot)]

    wsum = functools.reduce(jnp.add, picked_w)

    sel_f = jnp.concatenate([sg.astype(F32) for sg in sel], axis=0)
    r = lax.broadcasted_iota(jnp.int32, (tm, tm), 0)
    c = lax.broadcasted_iota(jnp.int32, (tm, tm), 1)
    before = _dot(sel_f.astype(BF16), (r < c).astype(BF16)) + cnt[...]
    eall = lax.broadcasted_iota(jnp.int32, (N_EXPERTS, tm), 0)
    for kk in range(TOP_K):
        rank = jnp.sum(jnp.where(eall == picked_e[kk], before, 0.0), axis=0, keepdims=True)
        e_ref[kk:kk + 1, :] = picked_e[kk]
        gate_t[kk:kk + 1, :] = picked_w[kk] / wsum * ROUTED_SCALE
        rank_ref[kk:kk + 1, :] = rank.astype(jnp.int32)
    pad = SUBLANES - TOP_K
    e_ref[TOP_K:, :] = jnp.zeros((pad, tm), jnp.int32)
    rank_ref[TOP_K:, :] = jnp.zeros((pad, tm), jnp.int32)
    gate_ref[...] = gate_t[...].T
    cnt[...] = cnt[...] + jnp.sum(sel_f, axis=1, keepdims=True)
    cnt_ref[...] = jnp.broadcast_to(cnt[...], cnt_ref.shape)


def _mix_out_router(a, b, x, w_out, b_out, ln_g, ln_b, router_w, router_bias):
    n_tok = x.shape[0]
    tm = TOKEN_TILE
    half = w_out.shape[0] // 2
    row = lambda w: pl.BlockSpec((tm, w), lambda i: (i, 0))
    b_spec = (pl.BlockSpec((tm, half), lambda i: (i, 1)) if b is a
              else pl.BlockSpec((tm, half), lambda i: (i, 0)))
    lane_row = pl.BlockSpec((SUBLANES, tm), lambda i: (0, i))
    packed = pl.BlockSpec((tm // SUBLANES, PACK_BLOCKS, SUBLANES, LANES),
                          lambda i: (i, 0, 0, 0))
    w_bf = w_out.astype(BF16)
    rw_t = router_w.T
    rw_hi = rw_t.astype(BF16)
    rw_lo = (rw_t - rw_hi.astype(F32)).astype(BF16)
    return pl.pallas_call(
        _mix_out_router_kernel,
        grid=(n_tok // tm,),
        in_specs=[row(half), b_spec, row(D_MODEL),
                  pl.BlockSpec((half, D_MODEL), lambda i: (0, 0)),
                  pl.BlockSpec((half, D_MODEL), lambda i: (1, 0)),
                  _resident((1, D_MODEL)), _resident((1, D_MODEL)), _resident((1, D_MODEL)),
                  _resident((N_EXPERTS, D_MODEL)), _resident((N_EXPERTS, D_MODEL)),
                  _resident((N_EXPERTS, 1))],
        out_specs=[row(D_MODEL), packed, lane_row, row(LANES), lane_row,
                   _resident((N_EXPERTS, LANES))],
        out_shape=[jax.ShapeDtypeStruct((n_tok, D_MODEL), F32),
                   jax.ShapeDtypeStruct((n_tok // SUBLANES, PACK_BLOCKS, SUBLANES, LANES),
                                        jnp.int32),
                   jax.ShapeDtypeStruct((SUBLANES, n_tok), jnp.int32),
                   jax.ShapeDtypeStruct((n_tok, LANES), F32),
                   jax.ShapeDtypeStruct((SUBLANES, n_tok), jnp.int32),
                   jax.ShapeDtypeStruct((N_EXPERTS, LANES), F32)],
        scratch_shapes=[pltpu.VMEM((N_EXPERTS, 1), F32), pltpu.VMEM((LANES, tm), F32)],
        compiler_params=_params("arbitrary"),
        name="mix_out_router",
    )(a, b, x, w_bf, w_bf, b_out[None, :], ln_g[None, :], ln_b[None, :],
      rw_hi, rw_lo, router_bias[:, None])


def _expert_ffn_kernel(tile_e_ref, tile_on_ref, xs_ref, wg_ref, wu_ref, wd_ref, ys_ref,
                       wg_bf, wu_bf, wd_bf):
    i = pl.program_id(0)
    on = tile_on_ref[i] > 0
    fresh = jnp.logical_or(i == 0, tile_e_ref[i] != tile_e_ref[jnp.maximum(i - 1, 0)])

    @pl.when(jnp.logical_and(on, fresh))
    def _():
        wg_bf[...] = wg_ref[...].astype(BF16)
        wu_bf[...] = wu_ref[...].astype(BF16)
        wd_bf[...] = wd_ref[...].astype(BF16)

    @pl.when(on)
    def _():
        lo, hi = _unpack_rows(_load_packed(xs_ref))
        lo = lo.astype(BF16)
        hi = hi.astype(BF16)
        gate = _dot(lo, wg_bf[:PACK_COLS, :]) + _dot(hi, wg_bf[PACK_COLS:, :])
        up = _dot(lo, wu_bf[:PACK_COLS, :]) + _dot(hi, wu_bf[PACK_COLS:, :])
        hid = (jax.nn.silu(gate) * up).astype(BF16)
        _store_packed(ys_ref, _pack_rows(_dot(hid, wd_bf[...])))

    @pl.when(jnp.logical_not(on))
    def _():
        ys_ref[...] = jnp.zeros_like(ys_ref)


def _expert_ffn(xs, tile_e, tile_on, w_gate, w_up, w_down):
    tg = GROUP_TILE
    packed = pl.BlockSpec((tg // SUBLANES, PACK_BLOCKS, SUBLANES, LANES),
                          lambda i, te, on: (i, 0, 0, 0))
    grid_spec = pltpu.PrefetchScalarGridSpec(
        num_scalar_prefetch=2,
        grid=(xs.shape[0] * SUBLANES // tg,),
        in_specs=[packed,
                  pl.BlockSpec((None, D_MODEL, EXPERT_HID), lambda i, te, on: (te[i], 0, 0)),
                  pl.BlockSpec((None, D_MODEL, EXPERT_HID), lambda i, te, on: (te[i], 0, 0)),
                  pl.BlockSpec((None, EXPERT_HID, D_MODEL), lambda i, te, on: (te[i], 0, 0))],
        out_specs=packed,
        scratch_shapes=[pltpu.VMEM((D_MODEL, EXPERT_HID), BF16),
                        pltpu.VMEM((D_MODEL, EXPERT_HID), BF16),
                        pltpu.VMEM((EXPERT_HID, D_MODEL), BF16)],
    )
    return pl.pallas_call(
        _expert_ffn_kernel,
        grid_spec=grid_spec,
        out_shape=jax.ShapeDtypeStruct(xs.shape, jnp.int32),
        compiler_params=_params("arbitrary"),
        name="expert_ffn",
    )(tile_e, tile_on, xs, w_gate, w_up, w_down)


def _moe_out_kernel(x1_ref, yg_ref, gate_ref, sg_ref, su_ref, sd_ref, g_ref, b_ref, *rest):
    o_ref = rest[-1]
    x1 = x1_ref[...]
    xb = x1.astype(BF16)
    hid = jax.nn.silu(_dot(xb, sg_ref[...])) * _dot(xb, su_ref[...])
    shared = _dot(hid.astype(BF16), sd_ref[...])
    gates = gate_ref[...]
    lo_acc = shared[:, :PACK_COLS]
    hi_acc = shared[:, PACK_COLS:]
    for kk in range(TOP_K):
        lo, hi = _unpack_rows(_load_packed(yg_ref.at[kk]))
        w = gates[:, kk:kk + 1]
        lo_acc = lo_acc + w * lo
        hi_acc = hi_acc + w * hi
    moe = jnp.concatenate([lo_acc, hi_acc], axis=1)
    o_ref[...] = _layer_norm(DN_ALPHA * x1 + moe, g_ref[...], b_ref[...])


def _moe_out(x1, yg, gates, sh_gate, sh_up, sh_down, ln_g, ln_b, part, earlier):
    n_tok = x1.shape[0]
    tm = TOKEN_TILE
    steps = n_tok // tm // MOE_PARTS
    first = part * steps
    row = lambda w: pl.BlockSpec((tm, w), lambda i: (i + first, 0))
    in_specs = [row(D_MODEL),
                pl.BlockSpec((TOP_K, tm // SUBLANES, PACK_BLOCKS, SUBLANES, LANES),
                             lambda i: (0, i, 0, 0, 0)),
                row(LANES),
                _resident((D_MODEL, EXPERT_HID)), _resident((D_MODEL, EXPERT_HID)),
                _resident((EXPERT_HID, D_MODEL)), _resident((1, D_MODEL)),
                _resident((1, D_MODEL))]
    args = [x1, yg, gates, sh_gate.astype(BF16), sh_up.astype(BF16), sh_down.astype(BF16),
            ln_g[None, :], ln_b[None, :]]
    aliases = {}
    if earlier is not None:
        in_specs.append(pl.BlockSpec(memory_space=pl.ANY))
        args.append(earlier)
        aliases = {len(args) - 1: 0}
    return pl.pallas_call(
        _moe_out_kernel,
        grid=(steps,),
        in_specs=in_specs,
        out_specs=row(D_MODEL),
        out_shape=jax.ShapeDtypeStruct((n_tok, D_MODEL), F32),
        input_output_aliases=aliases,
        compiler_params=_params("arbitrary"),
        name="moe_out",
    )(*args)


def _slot_kernel(offset_ref, e_ref, rank_ref, pos_ref):
    e = e_ref[...]
    base = jnp.zeros(e.shape, jnp.int32)
    for ex in range(N_EXPERTS):
        base = jnp.where(e == ex, offset_ref[ex], base)
    pos_ref[...] = base + rank_ref[...]


def _slots(offset, e6, rank6):
    n_tok = e6.shape[1]
    tl = 4096
    lane_row = pl.BlockSpec((SUBLANES, tl), lambda i, off: (0, i))
    return pl.pallas_call(
        _slot_kernel,
        grid_spec=pltpu.PrefetchScalarGridSpec(
            num_scalar_prefetch=1, grid=(n_tok // tl,),
            in_specs=[lane_row, lane_row], out_specs=lane_row),
        out_shape=jax.ShapeDtypeStruct((SUBLANES, n_tok), jnp.int32),
        compiler_params=_params("arbitrary"),
        name="slots",
    )(offset, e6, rank6)


def _sc_mesh():
    return plsc.VectorSubcoreMesh(core_axis_name="c", subcore_axis_name="s")


def _sc_worker():
    return lax.axis_index("s") * SC_CORES + lax.axis_index("c")


def _sc_dispatch(x_sub, idx, n_out_sub):
    windows = x_sub.shape[0] // (SC_WORKERS * SC_WINDOW)

    @functools.partial(
        pl.kernel, mesh=_sc_mesh(),
        out_type=jax.ShapeDtypeStruct((n_out_sub, LANES), jnp.int32),
        scratch_types=[pltpu.VMEM((SC_WINDOW, LANES), jnp.int32),
                       pltpu.VMEM((TOP_K, SC_WINDOW), jnp.int32)],
        name="sc_dispatch")
    def run(x_hbm, idx_hbm, out_hbm, rows_v, idx_v):
        wid = _sc_worker()

        @pl.loop(0, windows)
        def _(c):
            base = (wid * windows + c) * SC_WINDOW
            pltpu.sync_copy(x_hbm.at[pl.ds(base, SC_WINDOW)], rows_v)
            pltpu.sync_copy(idx_hbm.at[wid, c], idx_v)
            for kk in range(TOP_K):
                pltpu.sync_copy(rows_v, out_hbm.at[idx_v.at[kk]])

    return run(x_sub, idx)


def _sc_combine(y_sub, idx, n_tok_sub):
    windows = n_tok_sub // (SC_WORKERS * SC_WINDOW)

    @functools.partial(
        pl.kernel, mesh=_sc_mesh(),
        out_type=jax.ShapeDtypeStruct((TOP_K, n_tok_sub, LANES), jnp.int32),
        scratch_types=[pltpu.VMEM((2, SC_WINDOW, LANES), jnp.int32),
                       pltpu.VMEM((TOP_K, SC_WINDOW), jnp.int32),
                       pltpu.SemaphoreType.DMA((2,)),
                       pltpu.SemaphoreType.DMA((2,))],
        name="sc_combine")
    def run(y_hbm, idx_hbm, out_hbm, rows_v, idx_v, gather_sem, write_sem):
        wid = _sc_worker()

        def gather(kk):
            buf = kk % 2
            return pltpu.async_copy(y_hbm.at[idx_v.at[kk]], rows_v.at[buf], gather_sem.at[buf])

        @pl.loop(0, windows)
        def _(c):
            base = (wid * windows + c) * SC_WINDOW
            pltpu.sync_copy(idx_hbm.at[wid, c], idx_v)
            writes = []
            pending = gather(0)
            for kk in range(TOP_K):
                pending.wait()
                if kk + 1 < TOP_K:
                    if kk >= 1:
                        writes[kk - 1].wait()
                    pending = gather(kk + 1)
                writes.append(pltpu.async_copy(
                    rows_v.at[kk % 2], out_hbm.at[kk, pl.ds(base, SC_WINDOW)],
                    write_sem.at[kk % 2]))
            writes[TOP_K - 2].wait()
            writes[TOP_K - 1].wait()

    return run(y_sub, idx)


def _moe(x1, x1p, e6, gates, rank6, counts, exp_w_gate, exp_w_up, exp_w_down,
         sh_w_gate, sh_w_up, sh_w_down, ln_g, ln_b):
    n_tok = x1.shape[0]
    tg = GROUP_TILE
    n_rows = n_tok * TOP_K + N_EXPERTS * tg
    n_tiles = n_rows // tg

    cnt = counts[:, 0].astype(jnp.int32)
    tiles_per_e = (cnt + tg - 1) // tg
    tile_end = jnp.cumsum(tiles_per_e)
    offset = (tile_end - tiles_per_e) * tg
    tile_ids = jnp.arange(n_tiles, dtype=jnp.int32)
    tile_on = (tile_ids < tile_end[-1]).astype(jnp.int32)
    tile_e = jnp.minimum(
        jnp.sum((tile_ids[:, None] >= tile_end[None, :]).astype(jnp.int32), axis=1),
        N_EXPERTS - 1)
    tile_e = jnp.where(tile_on > 0, tile_e, tile_e[jnp.maximum(tile_end[-1] - 1, 0)])

    pos = _slots(offset, e6, rank6)[:TOP_K]

    group = PACK_BLOCKS * SUBLANES
    sub0 = (pos // SUBLANES) * group + pos % SUBLANES
    lane = jnp.arange(SC_WINDOW)
    spread = (jnp.arange(group)[:, None]
              == (lane // group) * SUBLANES + lane % SUBLANES).astype(F32)
    idx = jnp.einsum("kws,sd->kwd", sub0.reshape(TOP_K, n_tok // group, group).astype(F32),
                     spread, precision=lax.Precision.HIGHEST).astype(jnp.int32)
    idx = idx + ((lane % group) // SUBLANES * SUBLANES).astype(jnp.int32)
    n_sub = n_tok * PACK_BLOCKS
    idx = idx.reshape(TOP_K, n_sub)

    def windowed(ix):
        windows = ix.shape[1] // (SC_WORKERS * SC_WINDOW)
        return ix.reshape(TOP_K, SC_WORKERS, windows, SC_WINDOW).transpose(1, 2, 0, 3)

    xs = _sc_dispatch(x1p.reshape(n_sub, LANES), windowed(idx), n_rows * PACK_BLOCKS)
    ys = _expert_ffn(xs.reshape(n_rows // SUBLANES, PACK_BLOCKS, SUBLANES, LANES),
                     tile_e, tile_on, exp_w_gate, exp_w_up, exp_w_down)
    ys = ys.reshape(n_rows * PACK_BLOCKS, LANES)
    part_sub = n_sub // MOE_PARTS
    out = None
    for part in range(MOE_PARTS):
        yg = _sc_combine(ys, windowed(idx[:, part * part_sub:(part + 1) * part_sub]), part_sub)
        yg = yg.reshape(TOP_K, part_sub // (PACK_BLOCKS * SUBLANES), PACK_BLOCKS,
                        SUBLANES, LANES)
        out = _moe_out(x1, yg, gates, sh_w_gate, sh_w_up, sh_w_down, ln_g, ln_b, part, out)
    return out


def kernel(x, positions,
           l0_w_in, l0_b_in, l0_conv_w, l0_conv_b, l0_conv_ln_g, l0_conv_ln_b,
           l0_q_norm_g, l0_w_uq, l0_kv_norm_g, l0_w_ukv, l0_w_out, l0_b_out,
           l0_ln1_g, l0_ln1_b,
           l0_router_w, l0_router_bias, l0_exp_w_gate, l0_exp_w_up, l0_exp_w_down,
           l0_sh_w_gate, l0_sh_w_up, l0_sh_w_down, l0_ln2_g, l0_ln2_b,
           l1_w_in, l1_b_in, l1_w_out, l1_b_out, l1_ln1_g, l1_ln1_b,
           l1_router_w, l1_router_bias, l1_exp_w_gate, l1_exp_w_up, l1_exp_w_down,
           l1_sh_w_gate, l1_sh_w_up, l1_sh_w_down, l1_ln2_g, l1_ln2_b):
    bsz, seq, d = x.shape
    n_tok = bsz * seq

    u, q, k, v = _l0_front(x, positions, l0_w_in, l0_b_in, l0_conv_w, l0_conv_b,
                           l0_conv_ln_g, l0_conv_ln_b, l0_q_norm_g, l0_w_uq,
                           l0_kv_norm_g, l0_w_ukv)
    o = _attention(q, k, v, None, packed=False)
    x_flat = x.reshape(n_tok, d)
    routing = _mix_out_router(u.reshape(n_tok, -1), o.reshape(n_tok, -1), x_flat,
                              l0_w_out, l0_b_out, l0_ln1_g, l0_ln1_b,
                              l0_router_w, l0_router_bias)
    x2 = _moe(*routing, l0_exp_w_gate, l0_exp_w_up, l0_exp_w_down,
              l0_sh_w_gate, l0_sh_w_up, l0_sh_w_down, l0_ln2_g, l0_ln2_b)

    q, k, v, cneg = _l1_front(x2.reshape(bsz, seq, d), l1_w_in, l1_b_in)
    o = _attention(q, k, v, cneg[:, :, None, :], packed=True).reshape(n_tok, -1)
    routing = _mix_out_router(o, o, x2, l1_w_out, l1_b_out, l1_ln1_g, l1_ln1_b,
                              l1_router_w, l1_router_bias)
    x3 = _moe(*routing, l1_exp_w_gate, l1_exp_w_up, l1_exp_w_down,
              l1_sh_w_gate, l1_sh_w_up, l1_sh_w_down, l1_ln2_g, l1_ln2_b)
    return x3.reshape(bsz, seq, d)
```

```python
import functools

import jax
import jax.numpy as jnp
from jax import lax
from jax.experimental import pallas as pl
from jax.experimental.pallas import tpu as pltpu
from jax.experimental.pallas import tpu_sc as plsc

D_MODEL = 1024
DEPTH = 2
DN_ALPHA = (2 * DEPTH) ** 0.25
LN_EPS = 1e-5
RMS_EPS = 1e-6

CONV_CH = 512
CONV_WIDTH = 31
MLA_HEADS = 8
QK_NOPE = 64
QK_ROPE = 32
V_DIM = 64
Q_LORA = 256
KV_LORA = 128
ROPE_THETA = 10000.0
FOX_HEADS = 16
FOX_HD = 64
N_EXPERTS = 64
TOP_K = 6
N_GROUPS = 8
TOPK_GROUPS = 4
EXPERT_HID = 256
ROUTED_SCALE = 2.5

LANES = 128
SUBLANES = 8
VMEM_LIMIT_BYTES = 56 * 1024 * 1024

SEQ_TILE = 512
TOKEN_TILE = 512
ATTN_TILE = 512
ATTN_KEY_TILE = 512
ATTN_UNROLL = 4
GROUP_TILE = 512
FFN_SLOTS = 2
CONV_ROWS = 32
CONV_HALO = 32
PAD_HEAD = 128

PACK_COLS = D_MODEL // 2
PACK_BLOCKS = PACK_COLS // LANES
SC_CORES = 2
SC_SUBCORES = 16
SC_WORKERS = SC_CORES * SC_SUBCORES
SC_WINDOW = 128
MOE_PARTS = 2
BF16 = jnp.bfloat16
F32 = jnp.float32
NEG_INF = float("-inf")
LOG2_E = 1.4426950408889634


def _params(*semantics):
    return pltpu.CompilerParams(dimension_semantics=semantics,
                                vmem_limit_bytes=VMEM_LIMIT_BYTES)


def _resident(shape):
    nd = len(shape)
    return pl.BlockSpec(shape, lambda *_: (0,) * nd)


def _dot(a, b):
    return jnp.dot(a, b, preferred_element_type=F32)


def _dot_nt(a, b, precision=None):
    return lax.dot_general(a, b, (((1,), (1,)), ((), ())),
                           precision=precision, preferred_element_type=F32)


def _layer_norm(x, g, b):
    mu = jnp.mean(x, axis=-1, keepdims=True)
    xc = x - mu
    var = jnp.mean(xc * xc, axis=-1, keepdims=True)
    return xc * lax.rsqrt(var + LN_EPS) * g + b


def _rms_norm(x, g):
    return x * lax.rsqrt(jnp.mean(x * x, axis=-1, keepdims=True) + RMS_EPS) * g


def _ones_upper_half(width):
    lane = lax.broadcasted_iota(jnp.int32, (1, width), 1)
    return jnp.where((lane & (PAD_HEAD - 1)) >= V_DIM, 1.0, 0.0).astype(F32)


def _pad_heads(w, heads):
    d = w.shape[-1] // heads
    w3 = w.reshape(w.shape[:-1] + (heads, d))
    pad = jnp.zeros(w.shape[:-1] + (heads, PAD_HEAD - d), w.dtype)
    return jnp.concatenate([w3, pad], axis=-1).reshape(w.shape[:-1] + (heads * PAD_HEAD,))


def _pack_rows(x):
    lo = lax.bitcast_convert_type(x[:, :PACK_COLS].astype(BF16).astype(F32), jnp.uint32)
    hi = lax.bitcast_convert_type(x[:, PACK_COLS:].astype(BF16).astype(F32), jnp.uint32)
    word = (lo >> 16) | (hi & jnp.uint32(0xFFFF0000))
    return lax.bitcast_convert_type(word, jnp.int32)


def _unpack_rows(words):
    u = lax.bitcast_convert_type(words, jnp.uint32)
    lo = lax.bitcast_convert_type(u << 16, F32)
    hi = lax.bitcast_convert_type(u & jnp.uint32(0xFFFF0000), F32)
    return lo, hi


def _store_packed(ref, words):
    groups = words.shape[0] // SUBLANES
    for cb in range(PACK_BLOCKS):
        ref[:, cb, :, :] = words[:, cb * LANES:(cb + 1) * LANES].reshape(groups, SUBLANES, LANES)


def _load_packed(ref):
    rows = ref.shape[0] * SUBLANES
    return jnp.concatenate(
        [ref[:, cb, :, :].reshape(rows, LANES) for cb in range(PACK_BLOCKS)], axis=1)


L0_A = 0
L0_G = CONV_CH
L0_Q = 2 * CONV_CH
L0_KV = L0_Q + Q_LORA
L0_KR = L0_KV + KV_LORA
L0_KRR = L0_KR + PAD_HEAD
L0_COLS = L0_KRR + PAD_HEAD


def _l0_front_kernel(x_ref, pos_ref, w_in_ref, b_in_ref, conv_w_ref, conv_b_ref,
                     cln_g_ref, cln_b_ref, qn_g_ref, wq_ref, wqr_ref, kvn_g_ref,
                     wk_ref, wv_ref, invf_ref,
                     u_ref, q_ref, k_ref, v_ref, ubuf):
    ts = x_ref.shape[0]

    @pl.when(pl.program_id(1) == 0)
    def _():
        ubuf[0:CONV_HALO, :] = jnp.zeros((CONV_HALO, CONV_CH), F32)

    h = _dot(x_ref[...].astype(BF16), w_in_ref[...]) + b_in_ref[...]

    ubuf[CONV_HALO:CONV_HALO + ts, :] = (
        h[:, L0_A:L0_A + CONV_CH] * jax.nn.sigmoid(h[:, L0_G:L0_G + CONV_CH]))
    first_tap = CONV_HALO - (CONV_WIDTH - 1)

    def conv_chunk(c, carry):
        base = pl.multiple_of(c * CONV_ROWS, CONV_ROWS)
        acc = jnp.broadcast_to(conv_b_ref[...], (CONV_ROWS, CONV_CH))
        for res in range(SUBLANES):
            rows = CONV_ROWS + (SUBLANES if res else 0)
            part = None
            for off in range(res, first_tap + CONV_WIDTH, SUBLANES):
                j = off - first_tap
                if j < 0:
                    continue
                term = ubuf[pl.ds(base + (off - res), rows), :] * conv_w_ref[j:j + 1, :]
                part = term if part is None else part + term
            acc = acc + part[res:res + CONV_ROWS, :]
        y = _layer_norm(acc, cln_g_ref[...], cln_b_ref[...])
        u_ref[pl.ds(base, CONV_ROWS), :] = (y * jax.nn.sigmoid(y)).astype(BF16)
        return carry

    lax.fori_loop(0, ts // CONV_ROWS, conv_chunk, 0)
    ubuf[0:CONV_HALO, :] = ubuf[ts:ts + CONV_HALO, :]

    ang = pos_ref[...].astype(F32) * invf_ref[...]
    cos = jnp.cos(ang)
    sin = jnp.sin(ang)
    scale = (QK_NOPE + QK_ROPE) ** -0.5 * LOG2_E

    qn = _rms_norm(h[:, L0_Q:L0_Q + Q_LORA], qn_g_ref[...]).astype(BF16)
    q = _dot(qn, wq_ref[...])
    q_rot = _dot(qn, wqr_ref[...])
    cos_s = cos * scale
    sin_s = sin * scale
    for hh in range(MLA_HEADS):
        blk = slice(hh * PAD_HEAD, (hh + 1) * PAD_HEAD)
        q_ref[:, blk] = (q[:, blk] * cos_s + q_rot[:, blk] * sin_s).astype(BF16)

    kvn = _rms_norm(h[:, L0_KV:L0_KV + KV_LORA], kvn_g_ref[...]).astype(BF16)
    k_nope = _dot(kvn, wk_ref[...])
    k_pe = h[:, L0_KR:L0_KR + PAD_HEAD] * cos + h[:, L0_KRR:L0_KRR + PAD_HEAD] * sin
    for hh in range(MLA_HEADS):
        blk = slice(hh * PAD_HEAD, (hh + 1) * PAD_HEAD)
        k_ref[:, blk] = (k_nope[:, blk] + k_pe).astype(BF16)
    v_ref[...] = (_dot(kvn, wv_ref[...]) + _ones_upper_half(v_ref.shape[1])).astype(BF16)


def _rope_rotate_cols(w):
    half = QK_ROPE // 2
    return jnp.concatenate([-w[..., half:], w[..., :half]], axis=-1)


def _l0_front(x, positions, w_in, b_in, conv_w, conv_b, cln_g, cln_b,
              qn_g, w_uq, kvn_g, w_ukv):
    bsz, seq, _ = x.shape
    pad_lo = jnp.zeros((D_MODEL, QK_NOPE), F32)
    pad_hi = jnp.zeros((D_MODEL, PAD_HEAD - QK_NOPE - QK_ROPE), F32)
    i3 = L0_KR
    w_kr = w_in[:, i3:i3 + QK_ROPE]
    w_in_p = jnp.concatenate(
        [w_in[:, :i3], pad_lo, w_kr, pad_hi, pad_lo, _rope_rotate_cols(w_kr), pad_hi],
        axis=1).astype(BF16)
    b_kr = b_in[i3:i3 + QK_ROPE]
    zlo = jnp.zeros((QK_NOPE,), F32)
    zhi = jnp.zeros((PAD_HEAD - QK_NOPE - QK_ROPE,), F32)
    b_in_p = jnp.concatenate(
        [b_in[:i3], zlo, b_kr, zhi, zlo, _rope_rotate_cols(b_kr), zhi])[None, :]

    dq = QK_NOPE + QK_ROPE
    wq3 = w_uq.reshape(Q_LORA, MLA_HEADS, dq)
    zq = jnp.zeros((Q_LORA, MLA_HEADS, PAD_HEAD - dq), F32)
    wq_p = jnp.concatenate([wq3, zq], axis=-1).reshape(Q_LORA, MLA_HEADS * PAD_HEAD)
    wqr_p = jnp.concatenate(
        [jnp.zeros((Q_LORA, MLA_HEADS, QK_NOPE), F32),
         _rope_rotate_cols(wq3[..., QK_NOPE:]), zq], axis=-1
    ).reshape(Q_LORA, MLA_HEADS * PAD_HEAD)
    wkv3 = w_ukv.reshape(KV_LORA, MLA_HEADS, QK_NOPE + V_DIM)
    wk_p = jnp.concatenate(
        [wkv3[..., :QK_NOPE], jnp.zeros((KV_LORA, MLA_HEADS, PAD_HEAD - QK_NOPE), F32)],
        axis=-1).reshape(KV_LORA, MLA_HEADS * PAD_HEAD)
    wv = _pad_heads(wkv3[..., QK_NOPE:].reshape(KV_LORA, MLA_HEADS * V_DIM), MLA_HEADS)

    inv_freq = 1.0 / (ROPE_THETA ** (jnp.arange(0, QK_ROPE, 2, dtype=F32) / QK_ROPE))
    invf = jnp.concatenate([jnp.zeros((QK_NOPE,), F32), inv_freq, inv_freq,
                            jnp.zeros((PAD_HEAD - QK_NOPE - QK_ROPE,), F32)])[None, :]

    ts = SEQ_TILE
    row = lambda w: pl.BlockSpec((None, ts, w), lambda b, s: (b, s, 0))
    qk_w = MLA_HEADS * PAD_HEAD
    v_w = MLA_HEADS * PAD_HEAD
    return pl.pallas_call(
        _l0_front_kernel,
        grid=(bsz, seq // ts),
        in_specs=[row(D_MODEL), row(1),
                  _resident((D_MODEL, L0_COLS)), _resident((1, L0_COLS)),
                  _resident((CONV_WIDTH, CONV_CH)), _resident((1, CONV_CH)),
                  _resident((1, CONV_CH)), _resident((1, CONV_CH)),
                  _resident((1, Q_LORA)), _resident((Q_LORA, qk_w)),
                  _resident((Q_LORA, qk_w)), _resident((1, KV_LORA)),
                  _resident((KV_LORA, qk_w)), _resident((KV_LORA, v_w)),
                  _resident((1, PAD_HEAD))],
        out_specs=[row(CONV_CH), row(qk_w), row(qk_w), row(v_w)],
        out_shape=[jax.ShapeDtypeStruct((bsz, seq, CONV_CH), BF16),
                   jax.ShapeDtypeStruct((bsz, seq, qk_w), BF16),
                   jax.ShapeDtypeStruct((bsz, seq, qk_w), BF16),
                   jax.ShapeDtypeStruct((bsz, seq, v_w), BF16)],
        scratch_shapes=[pltpu.VMEM((ts + CONV_HALO, CONV_CH), F32)],
        compiler_params=_params("arbitrary", "arbitrary"),
        name="l0_front",
    )(x, positions[..., None], w_in_p, b_in_p, conv_w, conv_b[None, :],
      cln_g[None, :], cln_b[None, :], qn_g[None, :], wq_p.astype(BF16),
      wqr_p.astype(BF16), kvn_g[None, :], wk_p.astype(BF16), wv.astype(BF16), invf)


def _l1_front_kernel(x_ref, w_ref, b_ref, wv_ref, bv_ref, wf_ref, bf_ref,
                     q_ref, k_ref, v_ref, cneg_ref, carry):
    ts = x_ref.shape[0]

    @pl.when(pl.program_id(1) == 0)
    def _():
        carry[...] = jnp.zeros_like(carry)

    xb = x_ref[...].astype(BF16)
    mix = FOX_HEADS * FOX_HD
    scale = FOX_HD ** -0.5 * LOG2_E
    q_ref[...] = ((_dot(xb, w_ref[:, 0:mix]) + b_ref[:, 0:mix]) * scale).astype(BF16)
    k_ref[...] = (_dot(xb, w_ref[:, mix:2 * mix]) + b_ref[:, mix:2 * mix]).astype(BF16)
    v_ref[...] = (_dot(xb, wv_ref[...]) + bv_ref[...]
                  + _ones_upper_half(v_ref.shape[1])).astype(BF16)

    log_f = jax.nn.log_sigmoid(_dot_nt(wf_ref[...], xb) + bf_ref[...])
    r = lax.broadcasted_iota(jnp.int32, (ts, ts), 0)
    c = lax.broadcasted_iota(jnp.int32, (ts, ts), 1)
    upper = (r <= c).astype(F32)
    csum = jnp.dot(log_f, upper, precision=lax.Precision.HIGHEST,
                   preferred_element_type=F32) + carry[...]
    cneg_ref[...] = csum * -LOG2_E
    carry[...] = carry[...] + jnp.sum(log_f, axis=1, keepdims=True)


def _l1_front(x, w_in, b_in):
    bsz, seq, _ = x.shape
    mix = FOX_HEADS * FOX_HD
    ts = SEQ_TILE
    row = lambda w: pl.BlockSpec((None, ts, w), lambda b, s: (b, s, 0))
    w_qk = w_in[:, :2 * mix].astype(BF16)
    b_qk = b_in[None, :2 * mix]
    w_v = _pad_heads(w_in[:, 2 * mix:3 * mix], FOX_HEADS).astype(BF16)
    b_v = _pad_heads(b_in[None, 2 * mix:3 * mix], FOX_HEADS)
    v_w = FOX_HEADS * PAD_HEAD
    wf_t = w_in[:, 3 * mix:].T.astype(BF16)
    bf_t = b_in[3 * mix:][:, None]
    return pl.pallas_call(
        _l1_front_kernel,
        grid=(bsz, seq // ts),
        in_specs=[row(D_MODEL), _resident((D_MODEL, 2 * mix)), _resident((1, 2 * mix)),
                  _resident((D_MODEL, v_w)), _resident((1, v_w)),
                  _resident((FOX_HEADS, D_MODEL)), _resident((FOX_HEADS, 1))],
        out_specs=[row(mix), row(mix), row(v_w),
                   pl.BlockSpec((None, FOX_HEADS, ts), lambda b, s: (b, 0, s))],
        out_shape=[jax.ShapeDtypeStruct((bsz, seq, mix), BF16)] * 2
        + [jax.ShapeDtypeStruct((bsz, seq, v_w), BF16),
           jax.ShapeDtypeStruct((bsz, FOX_HEADS, seq), F32)],
        scratch_shapes=[pltpu.VMEM((FOX_HEADS, 1), F32)],
        compiler_params=_params("arbitrary", "arbitrary"),
        name="l1_front",
    )(x, w_qk, b_qk, w_v, b_v, wf_t, bf_t)


def _attn_kernel(*refs, packed, has_bias):
    if has_bias:
        q_ref, k_ref, v_ref, cneg_ref, o_ref = refs
    else:
        q_ref, k_ref, v_ref, o_ref = refs
        cneg_ref = None
    tq = q_ref.shape[0]
    tk = ATTN_KEY_TILE
    qi = pl.program_id(2)
    q2 = q_ref[...]
    half = LANES // 2
    if packed:
        lane = lax.broadcasted_iota(jnp.int32, q2.shape, 1)
        zero = jnp.zeros_like(q2)
        qs = (jnp.where(lane < half, q2, zero), jnp.where(lane >= half, q2, zero))
    else:
        qs = (q2[:, :PAD_HEAD], q2[:, PAD_HEAD:])

    def k_block(kj, head):
        rows = pl.ds(pl.multiple_of(kj * tk, tk), tk)
        if packed:
            return k_ref[rows, :]
        return k_ref[rows, head * PAD_HEAD:(head + 1) * PAD_HEAD]

    def step(kj, carry, diag):
        rows = pl.ds(pl.multiple_of(kj * tk, tk), tk)
        out = []
        for head in range(2):
            m, acc = carry[head]
            s = _dot_nt(qs[head], k_block(kj, head))
            if has_bias:
                s = s + cneg_ref[head, :, rows]
            if diag is not None:
                r = lax.broadcasted_iota(jnp.int32, s.shape, 0)
                c = lax.broadcasted_iota(jnp.int32, s.shape, 1)
                s = jnp.where(r >= c + diag * tk, s, NEG_INF)
            m_new = jnp.maximum(m, jnp.max(s, axis=-1, keepdims=True))
            alpha = jnp.exp2(m - m_new)
            p = jnp.exp2(s - m_new).astype(BF16)
            vblk = v_ref[rows, head * PAD_HEAD:(head + 1) * PAD_HEAD]
            out.append((m_new, alpha * acc + _dot(p, vblk)))
        return tuple(out)

    init = tuple((jnp.full((tq, 1), NEG_INF, F32), jnp.zeros((tq, LANES), F32))
                 for _ in range(2))
    per_tile = tq // tk
    n_full = qi * per_tile
    def group(first, c, size):
        for t in range(size):
            c = step(first + t, c, None)
        return c

    carry = lax.fori_loop(0, n_full // ATTN_UNROLL,
                          lambda j, c: group(ATTN_UNROLL * j, c, ATTN_UNROLL), init)
    done = (n_full // ATTN_UNROLL) * ATTN_UNROLL
    size = ATTN_UNROLL // 2
    while size >= 1:
        take = (n_full - done) >= size
        carry = lax.cond(take, lambda c, f=done, n=size: group(f, c, n), lambda c: c, carry)
        done = done + jnp.where(take, size, 0)
        size //= 2
    for d in range(per_tile):
        carry = step(n_full + d, carry, d)
    (_, acc_a), (_, acc_b) = carry
    out_a = acc_a / pltpu.roll(acc_a, half, axis=1)
    out_b = acc_b / pltpu.roll(acc_b, half, axis=1)
    lane_o = lax.broadcasted_iota(jnp.int32, (tq, LANES), 1)
    o_ref[...] = jnp.where(lane_o < half, out_a,
                           pltpu.roll(out_b, half, axis=1)).astype(o_ref.dtype)


def _attention(q, k, v, cneg, packed):
    bsz, seq, v_w = v.shape
    pairs = v_w // (2 * PAD_HEAD)
    qk_w = q.shape[-1] // pairs
    tq = ATTN_TILE
    in_specs = [pl.BlockSpec((None, tq, qk_w), lambda b, p, i: (b, i, p)),
                pl.BlockSpec((None, seq, qk_w), lambda b, p, i: (b, 0, p)),
                pl.BlockSpec((None, seq, 2 * PAD_HEAD), lambda b, p, i: (b, 0, p))]
    args = [q, k, v]
    if cneg is not None:
        in_specs.append(pl.BlockSpec((None, 2, 1, seq), lambda b, p, i: (b, p, 0, 0)))
        args.append(cneg)
    return pl.pallas_call(
        functools.partial(_attn_kernel, packed=packed, has_bias=cneg is not None),
        grid=(bsz, pairs, seq // tq),
        in_specs=in_specs,
        out_specs=pl.BlockSpec((None, tq, LANES), lambda b, p, i: (b, i, p)),
        out_shape=jax.ShapeDtypeStruct((bsz, seq, pairs * LANES), BF16),
        compiler_params=_params("arbitrary", "arbitrary", "arbitrary"),
        name="attn_packed" if packed else "attn_padded",
    )(*args)


def _first_hit(hits, found):
    out = []
    for hcur in hits:
        take = jnp.logical_and(hcur, jnp.logical_not(found))
        found = jnp.logical_or(found, take)
        out.append(take)
    return out, found


def _mix_out_router_kernel(a_ref, b_ref, x_ref, wa_ref, wb_ref, bo_ref, g_ref, be_ref,
                           rw_ref, rwl_ref, rb_ref,
                           x1_ref, x1p_ref, e_ref, gate_ref, rank_ref, cnt_ref, cnt, gate_t):
    tm = x_ref.shape[0]

    @pl.when(pl.program_id(0) == 0)
    def _():
        cnt[...] = jnp.zeros_like(cnt)
        gate_t[...] = jnp.zeros_like(gate_t)

    mix = _dot(a_ref[...], wa_ref[...]) + _dot(b_ref[...], wb_ref[...]) + bo_ref[...]
    x1 = _layer_norm(DN_ALPHA * x_ref[...] + mix, g_ref[...], be_ref[...])
    x1_ref[...] = x1
    _store_packed(x1p_ref, _pack_rows(x1))

    x_hi = x1.astype(BF16)
    x_lo = (x1 - x_hi.astype(F32)).astype(BF16)
    logits = (_dot_nt(rw_ref[...], x_hi) + _dot_nt(rw_ref[...], x_lo)
              + _dot_nt(rwl_ref[...], x_hi))
    aff = jax.nn.sigmoid(logits)
    choice = aff + rb_ref[...]
    per_group = N_EXPERTS // N_GROUPS
    sub = lax.broadcasted_iota(jnp.int32, (per_group, tm), 0)
    groups = [choice[g * per_group:(g + 1) * per_group, :] for g in range(N_GROUPS)]

    gscore = []
    for cg in groups:
        m1 = jnp.max(cg, axis=0, keepdims=True)
        i1 = jnp.min(jnp.where(cg == m1, sub, per_group), axis=0, keepdims=True)
        m2 = jnp.max(jnp.where(sub == i1, NEG_INF, cg), axis=0, keepdims=True)
        gscore.append(m1 + m2)

    gsel = [jnp.zeros((1, tm), jnp.bool_) for _ in range(N_GROUPS)]
    for _ in range(TOPK_GROUPS):
        best = functools.reduce(jnp.maximum, gscore)
        takes, _ = _first_hit([gs == best for gs in gscore], jnp.zeros((1, tm), jnp.bool_))
        gsel = [jnp.logical_or(a, t) for a, t in zip(gsel, takes)]
        gscore = [jnp.where(t, NEG_INF, gs) for gs, t in zip(gscore, takes)]

    masked = [jnp.where(gs, cg, NEG_INF) for gs, cg in zip(gsel, groups)]
    eid = [sub + g * per_group for g in range(N_GROUPS)]
    affs = [aff[g * per_group:(g + 1) * per_group, :] for g in range(N_GROUPS)]
    sel = [jnp.zeros((per_group, tm), jnp.bool_) for _ in range(N_GROUPS)]
    picked_e, picked_w = [], []
    for _ in range(TOP_K):
        best = jnp.max(functools.reduce(jnp.maximum, masked), axis=0, keepdims=True)
        cand = [jnp.where(mg == best, ig, N_EXPERTS) for mg, ig in zip(masked, eid)]
        idx = jnp.min(functools.reduce(jnp.minimum, cand), axis=0, keepdims=True)
        onehot = [ig == idx for ig in eid]
        w = functools.reduce(
            jnp.add, [jnp.sum(jnp.where(oh, ag, 0.0), axis=0, keepdims=True)
                      for oh, ag in zip(onehot, affs)])
        picked_e.append(idx)
        picked_w.append(w)
        sel = [jnp.logical_or(sg, oh) for sg, oh in zip(sel, onehot)]
        masked = [jnp.where(oh, NEG_INF, mg) for mg, oh in zip(masked, onehot)]

    wsum = functools.reduce(jnp.add, picked_w)

    sel_f = jnp.concatenate([sg.astype(F32) for sg in sel], axis=0)
    r = lax.broadcasted_iota(jnp.int32, (tm, tm), 0)
    c = lax.broadcasted_iota(jnp.int32, (tm, tm), 1)
    before = _dot(sel_f.astype(BF16), (r < c).astype(BF16)) + cnt[...]
    eall = lax.broadcasted_iota(jnp.int32, (N_EXPERTS, tm), 0)
    for kk in range(TOP_K):
        rank = jnp.sum(jnp.where(eall == picked_e[kk], before, 0.0), axis=0, keepdims=True)
        e_ref[kk:kk + 1, :] = picked_e[kk]
        gate_t[kk:kk + 1, :] = picked_w[kk] / wsum * ROUTED_SCALE
        rank_ref[kk:kk + 1, :] = rank.astype(jnp.int32)
    pad = SUBLANES - TOP_K
    e_ref[TOP_K:, :] = jnp.zeros((pad, tm), jnp.int32)
    rank_ref[TOP_K:, :] = jnp.zeros((pad, tm), jnp.int32)
    gate_ref[...] = gate_t[...].T
    cnt[...] = cnt[...] + jnp.sum(sel_f, axis=1, keepdims=True)
    cnt_ref[...] = jnp.broadcast_to(cnt[...], cnt_ref.shape)


def _mix_out_router(a, b, x, w_out, b_out, ln_g, ln_b, router_w, router_bias):
    n_tok = x.shape[0]
    tm = TOKEN_TILE
    half = w_out.shape[0] // 2
    row = lambda w: pl.BlockSpec((tm, w), lambda i: (i, 0))
    b_spec = (pl.BlockSpec((tm, half), lambda i: (i, 1)) if b is a
              else pl.BlockSpec((tm, half), lambda i: (i, 0)))
    lane_row = pl.BlockSpec((SUBLANES, tm), lambda i: (0, i))
    packed = pl.BlockSpec((tm // SUBLANES, PACK_BLOCKS, SUBLANES, LANES),
                          lambda i: (i, 0, 0, 0))
    w_bf = w_out.astype(BF16)
    rw_t = router_w.T
    rw_hi = rw_t.astype(BF16)
    rw_lo = (rw_t - rw_hi.astype(F32)).astype(BF16)
    return pl.pallas_call(
        _mix_out_router_kernel,
        grid=(n_tok // tm,),
        in_specs=[row(half), b_spec, row(D_MODEL),
                  pl.BlockSpec((half, D_MODEL), lambda i: (0, 0)),
                  pl.BlockSpec((half, D_MODEL), lambda i: (1, 0)),
                  _resident((1, D_MODEL)), _resident((1, D_MODEL)), _resident((1, D_MODEL)),
                  _resident((N_EXPERTS, D_MODEL)), _resident((N_EXPERTS, D_MODEL)),
                  _resident((N_EXPERTS, 1))],
        out_specs=[row(D_MODEL), packed, lane_row, row(LANES), lane_row,
                   _resident((N_EXPERTS, LANES))],
        out_shape=[jax.ShapeDtypeStruct((n_tok, D_MODEL), F32),
                   jax.ShapeDtypeStruct((n_tok // SUBLANES, PACK_BLOCKS, SUBLANES, LANES),
                                        jnp.int32),
                   jax.ShapeDtypeStruct((SUBLANES, n_tok), jnp.int32),
                   jax.ShapeDtypeStruct((n_tok, LANES), F32),
                   jax.ShapeDtypeStruct((SUBLANES, n_tok), jnp.int32),
                   jax.ShapeDtypeStruct((N_EXPERTS, LANES), F32)],
        scratch_shapes=[pltpu.VMEM((N_EXPERTS, 1), F32), pltpu.VMEM((LANES, tm), F32)],
        compiler_params=_params("arbitrary"),
        name="mix_out_router",
    )(a, b, x, w_bf, w_bf, b_out[None, :], ln_g[None, :], ln_b[None, :],
      rw_hi, rw_lo, router_bias[:, None])


def _expert_ffn_kernel(tile_e_ref, tile_on_ref, xs_ref, *refs):
    w_refs = refs[:3 * FFN_SLOTS]
    ys_ref = refs[3 * FFN_SLOTS]
    wg_bf, wu_bf, wd_bf = refs[3 * FFN_SLOTS + 1:]
    i = pl.program_id(0)
    sub = GROUP_TILE // SUBLANES

    for slot in range(FFN_SLOTS):
        t = FFN_SLOTS * i + slot
        fresh = jnp.logical_or(
            i == 0, tile_e_ref[t] != tile_e_ref[jnp.maximum(t - FFN_SLOTS, 0)])

        @pl.when(fresh)
        def _(slot=slot):
            wg_ref, wu_ref, wd_ref = w_refs[3 * slot:3 * slot + 3]
            wg_bf[slot] = wg_ref[...].astype(BF16)
            wu_bf[slot] = wu_ref[...].astype(BF16)
            wd_bf[slot] = wd_ref[...].astype(BF16)

    on = tile_on_ref[FFN_SLOTS * i] > 0

    @pl.when(on)
    def _():
        for slot in range(FFN_SLOTS):
            rows = pl.ds(slot * sub, sub)
            lo, hi = _unpack_rows(_load_packed(xs_ref.at[rows]))
            lo = lo.astype(BF16)
            hi = hi.astype(BF16)
            gate = (_dot(lo, wg_bf[slot, :PACK_COLS, :]) + _dot(hi, wg_bf[slot, PACK_COLS:, :]))
            up = (_dot(lo, wu_bf[slot, :PACK_COLS, :]) + _dot(hi, wu_bf[slot, PACK_COLS:, :]))
            hid = (jax.nn.silu(gate) * up).astype(BF16)
            _store_packed(ys_ref.at[rows], _pack_rows(_dot(hid, wd_bf[slot])))

    @pl.when(jnp.logical_not(on))
    def _():
        ys_ref[...] = jnp.zeros_like(ys_ref)


def _expert_ffn(xs, tile_e, tile_on, w_gate, w_up, w_down):
    rows_per_step = GROUP_TILE * FFN_SLOTS
    packed = pl.BlockSpec((rows_per_step // SUBLANES, PACK_BLOCKS, SUBLANES, LANES),
                          lambda i, te, on: (i, 0, 0, 0))
    w_specs = []
    for slot in range(FFN_SLOTS):
        pick = lambda i, te, on, slot=slot: (te[FFN_SLOTS * i + slot], 0, 0)
        w_specs += [pl.BlockSpec((None, D_MODEL, EXPERT_HID), pick),
                    pl.BlockSpec((None, D_MODEL, EXPERT_HID), pick),
                    pl.BlockSpec((None, EXPERT_HID, D_MODEL), pick)]
    grid_spec = pltpu.PrefetchScalarGridSpec(
        num_scalar_prefetch=2,
        grid=(xs.shape[0] * SUBLANES // rows_per_step,),
        in_specs=[packed] + w_specs,
        out_specs=packed,
        scratch_shapes=[pltpu.VMEM((FFN_SLOTS, D_MODEL, EXPERT_HID), BF16),
                        pltpu.VMEM((FFN_SLOTS, D_MODEL, EXPERT_HID), BF16),
                        pltpu.VMEM((FFN_SLOTS, EXPERT_HID, D_MODEL), BF16)],
    )
    return pl.pallas_call(
        _expert_ffn_kernel,
        grid_spec=grid_spec,
        out_shape=jax.ShapeDtypeStruct(xs.shape, jnp.int32),
        compiler_params=_params("arbitrary"),
        name="expert_ffn",
    )(tile_e, tile_on, xs, *([w_gate, w_up, w_down] * FFN_SLOTS))


def _moe_out_kernel(x1_ref, yg_ref, gate_ref, sg_ref, su_ref, sd_ref, g_ref, b_ref, *rest):
    o_ref = rest[-1]
    x1 = x1_ref[...]
    xb = x1.astype(BF16)
    hid = jax.nn.silu(_dot(xb, sg_ref[...])) * _dot(xb, su_ref[...])
    shared = _dot(hid.astype(BF16), sd_ref[...])
    gates = gate_ref[...]
    lo_acc = shared[:, :PACK_COLS]
    hi_acc = shared[:, PACK_COLS:]
    for kk in range(TOP_K):
        lo, hi = _unpack_rows(_load_packed(yg_ref.at[kk]))
        w = gates[:, kk:kk + 1]
        lo_acc = lo_acc + w * lo
        hi_acc = hi_acc + w * hi
    moe = jnp.concatenate([lo_acc, hi_acc], axis=1)
    o_ref[...] = _layer_norm(DN_ALPHA * x1 + moe, g_ref[...], b_ref[...])


def _moe_out(x1, yg, gates, sh_gate, sh_up, sh_down, ln_g, ln_b, part, earlier):
    n_tok = x1.shape[0]
    tm = TOKEN_TILE
    steps = n_tok // tm // MOE_PARTS
    first = part * steps
    row = lambda w: pl.BlockSpec((tm, w), lambda i: (i + first, 0))
    in_specs = [row(D_MODEL),
                pl.BlockSpec((TOP_K, tm // SUBLANES, PACK_BLOCKS, SUBLANES, LANES),
                             lambda i: (0, i, 0, 0, 0)),
                row(LANES),
                _resident((D_MODEL, EXPERT_HID)), _resident((D_MODEL, EXPERT_HID)),
                _resident((EXPERT_HID, D_MODEL)), _resident((1, D_MODEL)),
                _resident((1, D_MODEL))]
    args = [x1, yg, gates, sh_gate.astype(BF16), sh_up.astype(BF16), sh_down.astype(BF16),
            ln_g[None, :], ln_b[None, :]]
    aliases = {}
    if earlier is not None:
        in_specs.append(pl.BlockSpec(memory_space=pl.ANY))
        args.append(earlier)
        aliases = {len(args) - 1: 0}
    return pl.pallas_call(
        _moe_out_kernel,
        grid=(steps,),
        in_specs=in_specs,
        out_specs=row(D_MODEL),
        out_shape=jax.ShapeDtypeStruct((n_tok, D_MODEL), F32),
        input_output_aliases=aliases,
        compiler_params=_params("arbitrary"),
        name="moe_out",
    )(*args)


def _slot_kernel(offset_ref, e_ref, rank_ref, pos_ref):
    e = e_ref[...]
    base = jnp.zeros(e.shape, jnp.int32)
    for ex in range(N_EXPERTS):
        base = jnp.where(e == ex, offset_ref[ex], base)
    pos_ref[...] = base + rank_ref[...]


def _slots(offset, e6, rank6):
    n_tok = e6.shape[1]
    tl = 4096
    lane_row = pl.BlockSpec((SUBLANES, tl), lambda i, off: (0, i))
    return pl.pallas_call(
        _slot_kernel,
        grid_spec=pltpu.PrefetchScalarGridSpec(
            num_scalar_prefetch=1, grid=(n_tok // tl,),
            in_specs=[lane_row, lane_row], out_specs=lane_row),
        out_shape=jax.ShapeDtypeStruct((SUBLANES, n_tok), jnp.int32),
        compiler_params=_params("arbitrary"),
        name="slots",
    )(offset, e6, rank6)


def _sc_mesh():
    return plsc.VectorSubcoreMesh(core_axis_name="c", subcore_axis_name="s")


def _sc_worker():
    return lax.axis_index("s") * SC_CORES + lax.axis_index("c")


def _sc_dispatch(x_sub, idx, n_out_sub):
    windows = x_sub.shape[0] // (SC_WORKERS * SC_WINDOW)

    @functools.partial(
        pl.kernel, mesh=_sc_mesh(),
        out_type=jax.ShapeDtypeStruct((n_out_sub, LANES), jnp.int32),
        scratch_types=[pltpu.VMEM((SC_WINDOW, LANES), jnp.int32),
                       pltpu.VMEM((TOP_K, SC_WINDOW), jnp.int32)],
        name="sc_dispatch")
    def run(x_hbm, idx_hbm, out_hbm, rows_v, idx_v):
        wid = _sc_worker()

        @pl.loop(0, windows)
        def _(c):
            base = (wid * windows + c) * SC_WINDOW
            pltpu.sync_copy(x_hbm.at[pl.ds(base, SC_WINDOW)], rows_v)
            pltpu.sync_copy(idx_hbm.at[wid, c], idx_v)
            for kk in range(TOP_K):
                pltpu.sync_copy(rows_v, out_hbm.at[idx_v.at[kk]])

    return run(x_sub, idx)


def _sc_combine(y_sub, idx, n_tok_sub):
    windows = n_tok_sub // (SC_WORKERS * SC_WINDOW)

    @functools.partial(
        pl.kernel, mesh=_sc_mesh(),
        out_type=jax.ShapeDtypeStruct((TOP_K, n_tok_sub, LANES), jnp.int32),
        scratch_types=[pltpu.VMEM((2, SC_WINDOW, LANES), jnp.int32),
                       pltpu.VMEM((TOP_K, SC_WINDOW), jnp.int32),
                       pltpu.SemaphoreType.DMA((2,)),
                       pltpu.SemaphoreType.DMA((2,))],
        name="sc_combine")
    def run(y_hbm, idx_hbm, out_hbm, rows_v, idx_v, gather_sem, write_sem):
        wid = _sc_worker()

        def gather(kk):
            buf = kk % 2
            return pltpu.async_copy(y_hbm.at[idx_v.at[kk]], rows_v.at[buf], gather_sem.at[buf])

        @pl.loop(0, windows)
        def _(c):
            base = (wid * windows + c) * SC_WINDOW
            pltpu.sync_copy(idx_hbm.at[wid, c], idx_v)
            writes = []
            pending = gather(0)
            for kk in range(TOP_K):
                pending.wait()
                if kk + 1 < TOP_K:
                    if kk >= 1:
                        writes[kk - 1].wait()
                    pending = gather(kk + 1)
                writes.append(pltpu.async_copy(
                    rows_v.at[kk % 2], out_hbm.at[kk, pl.ds(base, SC_WINDOW)],
                    write_sem.at[kk % 2]))
            writes[TOP_K - 2].wait()
            writes[TOP_K - 1].wait()

    return run(y_sub, idx)


def _moe(x1, x1p, e6, gates, rank6, counts, exp_w_gate, exp_w_up, exp_w_down,
         sh_w_gate, sh_w_up, sh_w_down, ln_g, ln_b):
    n_tok = x1.shape[0]
    tg = GROUP_TILE
    n_rows = n_tok * TOP_K + N_EXPERTS * tg
    n_tiles = n_rows // tg

    cnt = counts[:, 0].astype(jnp.int32)
    tiles_per_e = (cnt + tg - 1) // tg
    tile_end = jnp.cumsum(tiles_per_e)
    offset = (tile_end - tiles_per_e) * tg
    tile_ids = jnp.arange(n_tiles, dtype=jnp.int32)
    tile_on = (tile_ids < tile_end[-1]).astype(jnp.int32)
    tile_e = jnp.minimum(
        jnp.sum((tile_ids[:, None] >= tile_end[None, :]).astype(jnp.int32), axis=1),
        N_EXPERTS - 1)
    tile_e = jnp.where(tile_on > 0, tile_e, tile_e[jnp.maximum(tile_end[-1] - 1, 0)])

    pos = _slots(offset, e6, rank6)[:TOP_K]

    group = PACK_BLOCKS * SUBLANES
    sub0 = (pos // SUBLANES) * group + pos % SUBLANES
    lane = jnp.arange(SC_WINDOW)
    spread = (jnp.arange(group)[:, None]
              == (lane // group) * SUBLANES + lane % SUBLANES).astype(F32)
    idx = jnp.einsum("kws,sd->kwd", sub0.reshape(TOP_K, n_tok // group, group).astype(F32),
                     spread, precision=lax.Precision.HIGHEST).astype(jnp.int32)
    idx = idx + ((lane % group) // SUBLANES * SUBLANES).astype(jnp.int32)
    n_sub = n_tok * PACK_BLOCKS
    idx = idx.reshape(TOP_K, n_sub)

    def windowed(ix):
        windows = ix.shape[1] // (SC_WORKERS * SC_WINDOW)
        return ix.reshape(TOP_K, SC_WORKERS, windows, SC_WINDOW).transpose(1, 2, 0, 3)

    xs = _sc_dispatch(x1p.reshape(n_sub, LANES), windowed(idx), n_rows * PACK_BLOCKS)
    ys = _expert_ffn(xs.reshape(n_rows // SUBLANES, PACK_BLOCKS, SUBLANES, LANES),
                     tile_e, tile_on, exp_w_gate, exp_w_up, exp_w_down)
    ys = ys.reshape(n_rows * PACK_BLOCKS, LANES)
    part_sub = n_sub // MOE_PARTS
    out = None
    for part in range(MOE_PARTS):
        yg = _sc_combine(ys, windowed(idx[:, part * part_sub:(part + 1) * part_sub]), part_sub)
        yg = yg.reshape(TOP_K, part_sub // (PACK_BLOCKS * SUBLANES), PACK_BLOCKS,
                        SUBLANES, LANES)
        out = _moe_out(x1, yg, gates, sh_w_gate, sh_w_up, sh_w_down, ln_g, ln_b, part, out)
    return out


def kernel(x, positions,
           l0_w_in, l0_b_in, l0_conv_w, l0_conv_b, l0_conv_ln_g, l0_conv_ln_b,
           l0_q_norm_g, l0_w_uq, l0_kv_norm_g, l0_w_ukv, l0_w_out, l0_b_out,
           l0_ln1_g, l0_ln1_b,
           l0_router_w, l0_router_bias, l0_exp_w_gate, l0_exp_w_up, l0_exp_w_down,
           l0_sh_w_gate, l0_sh_w_up, l0_sh_w_down, l0_ln2_g, l0_ln2_b,
           l1_w_in, l1_b_in, l1_w_out, l1_b_out, l1_ln1_g, l1_ln1_b,
           l1_router_w, l1_router_bias, l1_exp_w_gate, l1_exp_w_up, l1_exp_w_down,
           l1_sh_w_gate, l1_sh_w_up, l1_sh_w_down, l1_ln2_g, l1_ln2_b):
    bsz, seq, d = x.shape
    n_tok = bsz * seq

    u, q, k, v = _l0_front(x, positions, l0_w_in, l0_b_in, l0_conv_w, l0_conv_b,
                           l0_conv_ln_g, l0_conv_ln_b, l0_q_norm_g, l0_w_uq,
                           l0_kv_norm_g, l0_w_ukv)
    o = _attention(q, k, v, None, packed=False)
    x_flat = x.reshape(n_tok, d)
    routing = _mix_out_router(u.reshape(n_tok, -1), o.reshape(n_tok, -1), x_flat,
                              l0_w_out, l0_b_out, l0_ln1_g, l0_ln1_b,
                              l0_router_w, l0_router_bias)
    x2 = _moe(*routing, l0_exp_w_gate, l0_exp_w_up, l0_exp_w_down,
              l0_sh_w_gate, l0_sh_w_up, l0_sh_w_down, l0_ln2_g, l0_ln2_b)

    q, k, v, cneg = _l1_front(x2.reshape(bsz, seq, d), l1_w_in, l1_b_in)
    o = _attention(q, k, v, cneg[:, :, None, :], packed=True).reshape(n_tok, -1)
    routing = _mix_out_router(o, o, x2, l1_w_out, l1_b_out, l1_ln1_g, l1_ln1_b,
                              l1_router_w, l1_router_bias)
    x3 = _moe(*routing, l1_exp_w_gate, l1_exp_w_up, l1_exp_w_down,
              l1_sh_w_gate, l1_sh_w_up, l1_sh_w_down, l1_ln2_g, l1_ln2_b)
    return x3.reshape(bsz, seq, d)
```

```python
import functools

import jax
import jax.numpy as jnp
from jax import lax
from jax.experimental import pallas as pl
from jax.experimental.pallas import tpu as pltpu
from jax.experimental.pallas import tpu_sc as plsc

D_MODEL = 1024
DEPTH = 2
DN_ALPHA = (2 * DEPTH) ** 0.25
LN_EPS = 1e-5
RMS_EPS = 1e-6

CONV_CH = 512
CONV_WIDTH = 31
MLA_HEADS = 8
QK_NOPE = 64
QK_ROPE = 32
V_DIM = 64
Q_LORA = 256
KV_LORA = 128
ROPE_THETA = 10000.0
FOX_HEADS = 16
FOX_HD = 64
N_EXPERTS = 64
TOP_K = 6
N_GROUPS = 8
TOPK_GROUPS = 4
EXPERT_HID = 256
ROUTED_SCALE = 2.5

LANES = 128
SUBLANES = 8
VMEM_LIMIT_BYTES = 56 * 1024 * 1024

SEQ_TILE = 512
TOKEN_TILE = 512
ATTN_TILE = 512
ATTN_KEY_TILE = 512
ATTN_UNROLL = 4
GROUP_TILE = 512
FFN_SLOTS = 2
CONV_ROWS = 32
CONV_HALO = 32
PAD_HEAD = 128

PACK_COLS = D_MODEL // 2
PACK_BLOCKS = PACK_COLS // LANES
SC_CORES = 2
SC_SUBCORES = 16
SC_WORKERS = SC_CORES * SC_SUBCORES
SC_WINDOW = 128
MOE_PARTS = 2
BF16 = jnp.bfloat16
F32 = jnp.float32
NEG_INF = float("-inf")
LOG2_E = 1.4426950408889634


def _params(*semantics):
    return pltpu.CompilerParams(dimension_semantics=semantics,
                                vmem_limit_bytes=VMEM_LIMIT_BYTES)


def _resident(shape):
    nd = len(shape)
    return pl.BlockSpec(shape, lambda *_: (0,) * nd)


def _dot(a, b):
    return jnp.dot(a, b, preferred_element_type=F32)


def _dot_nt(a, b, precision=None):
    return lax.dot_general(a, b, (((1,), (1,)), ((), ())),
                           precision=precision, preferred_element_type=F32)


def _layer_norm(x, g, b):
    mu = jnp.mean(x, axis=-1, keepdims=True)
    xc = x - mu
    var = jnp.mean(xc * xc, axis=-1, keepdims=True)
    return xc * lax.rsqrt(var + LN_EPS) * g + b


def _rms_norm(x, g):
    return x * lax.rsqrt(jnp.mean(x * x, axis=-1, keepdims=True) + RMS_EPS) * g


def _ones_upper_half(width):
    lane = lax.broadcasted_iota(jnp.int32, (1, width), 1)
    return jnp.where((lane & (PAD_HEAD - 1)) >= V_DIM, 1.0, 0.0).astype(F32)


def _pad_heads(w, heads):
    d = w.shape[-1] // heads
    w3 = w.reshape(w.shape[:-1] + (heads, d))
    pad = jnp.zeros(w.shape[:-1] + (heads, PAD_HEAD - d), w.dtype)
    return jnp.concatenate([w3, pad], axis=-1).reshape(w.shape[:-1] + (heads * PAD_HEAD,))


def _pack_rows(x):
    lo = lax.bitcast_convert_type(x[:, :PACK_COLS].astype(BF16).astype(F32), jnp.uint32)
    hi = lax.bitcast_convert_type(x[:, PACK_COLS:].astype(BF16).astype(F32), jnp.uint32)
    word = (lo >> 16) | (hi & jnp.uint32(0xFFFF0000))
    return lax.bitcast_convert_type(word, jnp.int32)


def _unpack_rows(words):
    u = lax.bitcast_convert_type(words, jnp.uint32)
    lo = lax.bitcast_convert_type(u << 16, F32)
    hi = lax.bitcast_convert_type(u & jnp.uint32(0xFFFF0000), F32)
    return lo, hi


def _store_packed(ref, words):
    groups = words.shape[0] // SUBLANES
    for cb in range(PACK_BLOCKS):
        ref[:, cb, :, :] = words[:, cb * LANES:(cb + 1) * LANES].reshape(groups, SUBLANES, LANES)


def _load_packed(ref):
    rows = ref.shape[0] * SUBLANES
    return jnp.concatenate(
        [ref[:, cb, :, :].reshape(rows, LANES) for cb in range(PACK_BLOCKS)], axis=1)


L0_A = 0
L0_G = CONV_CH
L0_Q = 2 * CONV_CH
L0_KV = L0_Q + Q_LORA
L0_KR = L0_KV + KV_LORA
L0_KRR = L0_KR + PAD_HEAD
L0_COLS = L0_KRR + PAD_HEAD


def _l0_front_kernel(x_ref, pos_ref, w_in_ref, b_in_ref, conv_w_ref, conv_b_ref,
                     cln_g_ref, cln_b_ref, qn_g_ref, wq_ref, wqr_ref, kvn_g_ref,
                     wk_ref, wv_ref, invf_ref,
                     u_ref, q_ref, k_ref, v_ref, ubuf):
    ts = x_ref.shape[0]

    @pl.when(pl.program_id(1) == 0)
    def _():
        ubuf[0:CONV_HALO, :] = jnp.zeros((CONV_HALO, CONV_CH), F32)

    h = _dot(x_ref[...].astype(BF16), w_in_ref[...]) + b_in_ref[...]

    ubuf[CONV_HALO:CONV_HALO + ts, :] = (
        h[:, L0_A:L0_A + CONV_CH] * jax.nn.sigmoid(h[:, L0_G:L0_G + CONV_CH]))
    first_tap = CONV_HALO - (CONV_WIDTH - 1)

    def conv_chunk(c, carry):
        base = pl.multiple_of(c * CONV_ROWS, CONV_ROWS)
        acc = jnp.broadcast_to(conv_b_ref[...], (CONV_ROWS, CONV_CH))
        for res in range(SUBLANES):
            rows = CONV_ROWS + (SUBLANES if res else 0)
            part = None
            for off in range(res, first_tap + CONV_WIDTH, SUBLANES):
                j = off - first_tap
                if j < 0:
                    continue
                term = ubuf[pl.ds(base + (off - res), rows), :] * conv_w_ref[j:j + 1, :]
                part = term if part is None else part + term
            acc = acc + part[res:res + CONV_ROWS, :]
        y = _layer_norm(acc, cln_g_ref[...], cln_b_ref[...])
        u_ref[pl.ds(base, CONV_ROWS), :] = (y * jax.nn.sigmoid(y)).astype(BF16)
        return carry

    lax.fori_loop(0, ts // CONV_ROWS, conv_chunk, 0)
    ubuf[0:CONV_HALO, :] = ubuf[ts:ts + CONV_HALO, :]

    ang = pos_ref[...].astype(F32) * invf_ref[...]
    cos = jnp.cos(ang)
    sin = jnp.sin(ang)
    scale = (QK_NOPE + QK_ROPE) ** -0.5 * LOG2_E

    qn = _rms_norm(h[:, L0_Q:L0_Q + Q_LORA], qn_g_ref[...]).astype(BF16)
    q = _dot(qn, wq_ref[...])
    q_rot = _dot(qn, wqr_ref[...])
    cos_s = cos * scale
    sin_s = sin * scale
    for hh in range(MLA_HEADS):
        blk = slice(hh * PAD_HEAD, (hh + 1) * PAD_HEAD)
        q_ref[:, blk] = (q[:, blk] * cos_s + q_rot[:, blk] * sin_s).astype(BF16)

    kvn = _rms_norm(h[:, L0_KV:L0_KV + KV_LORA], kvn_g_ref[...]).astype(BF16)
    k_nope = _dot(kvn, wk_ref[...])
    k_pe = h[:, L0_KR:L0_KR + PAD_HEAD] * cos + h[:, L0_KRR:L0_KRR + PAD_HEAD] * sin
    for hh in range(MLA_HEADS):
        blk = slice(hh * PAD_HEAD, (hh + 1) * PAD_HEAD)
        k_ref[:, blk] = (k_nope[:, blk] + k_pe).astype(BF16)
    v_ref[...] = (_dot(kvn, wv_ref[...]) + _ones_upper_half(v_ref.shape[1])).astype(BF16)


def _rope_rotate_cols(w):
    half = QK_ROPE // 2
    return jnp.concatenate([-w[..., half:], w[..., :half]], axis=-1)


def _l0_front(x, positions, w_in, b_in, conv_w, conv_b, cln_g, cln_b,
              qn_g, w_uq, kvn_g, w_ukv):
    bsz, seq, _ = x.shape
    pad_lo = jnp.zeros((D_MODEL, QK_NOPE), F32)
    pad_hi = jnp.zeros((D_MODEL, PAD_HEAD - QK_NOPE - QK_ROPE), F32)
    i3 = L0_KR
    w_kr = w_in[:, i3:i3 + QK_ROPE]
    w_in_p = jnp.concatenate(
        [w_in[:, :i3], pad_lo, w_kr, pad_hi, pad_lo, _rope_rotate_cols(w_kr), pad_hi],
        axis=1).astype(BF16)
    b_kr = b_in[i3:i3 + QK_ROPE]
    zlo = jnp.zeros((QK_NOPE,), F32)
    zhi = jnp.zeros((PAD_HEAD - QK_NOPE - QK_ROPE,), F32)
    b_in_p = jnp.concatenate(
        [b_in[:i3], zlo, b_kr, zhi, zlo, _rope_rotate_cols(b_kr), zhi])[None, :]

    dq = QK_NOPE + QK_ROPE
    wq3 = w_uq.reshape(Q_LORA, MLA_HEADS, dq)
    zq = jnp.zeros((Q_LORA, MLA_HEADS, PAD_HEAD - dq), F32)
    wq_p = jnp.concatenate([wq3, zq], axis=-1).reshape(Q_LORA, MLA_HEADS * PAD_HEAD)
    wqr_p = jnp.concatenate(
        [jnp.zeros((Q_LORA, MLA_HEADS, QK_NOPE), F32),
         _rope_rotate_cols(wq3[..., QK_NOPE:]), zq], axis=-1
    ).reshape(Q_LORA, MLA_HEADS * PAD_HEAD)
    wkv3 = w_ukv.reshape(KV_LORA, MLA_HEADS, QK_NOPE + V_DIM)
    wk_p = jnp.concatenate(
        [wkv3[..., :QK_NOPE], jnp.zeros((KV_LORA, MLA_HEADS, PAD_HEAD - QK_NOPE), F32)],
        axis=-1).reshape(KV_LORA, MLA_HEADS * PAD_HEAD)
    wv = _pad_heads(wkv3[..., QK_NOPE:].reshape(KV_LORA, MLA_HEADS * V_DIM), MLA_HEADS)

    inv_freq = 1.0 / (ROPE_THETA ** (jnp.arange(0, QK_ROPE, 2, dtype=F32) / QK_ROPE))
    invf = jnp.concatenate([jnp.zeros((QK_NOPE,), F32), inv_freq, inv_freq,
                            jnp.zeros((PAD_HEAD - QK_NOPE - QK_ROPE,), F32)])[None, :]

    ts = SEQ_TILE
    row = lambda w: pl.BlockSpec((None, ts, w), lambda b, s: (b, s, 0))
    qk_w = MLA_HEADS * PAD_HEAD
    v_w = MLA_HEADS * PAD_HEAD
    return pl.pallas_call(
        _l0_front_kernel,
        grid=(bsz, seq // ts),
        in_specs=[row(D_MODEL), row(1),
                  _resident((D_MODEL, L0_COLS)), _resident((1, L0_COLS)),
                  _resident((CONV_WIDTH, CONV_CH)), _resident((1, CONV_CH)),
                  _resident((1, CONV_CH)), _resident((1, CONV_CH)),
                  _resident((1, Q_LORA)), _resident((Q_LORA, qk_w)),
                  _resident((Q_LORA, qk_w)), _resident((1, KV_LORA)),
                  _resident((KV_LORA, qk_w)), _resident((KV_LORA, v_w)),
                  _resident((1, PAD_HEAD))],
        out_specs=[row(CONV_CH), row(qk_w), row(qk_w), row(v_w)],
        out_shape=[jax.ShapeDtypeStruct((bsz, seq, CONV_CH), BF16),
                   jax.ShapeDtypeStruct((bsz, seq, qk_w), BF16),
                   jax.ShapeDtypeStruct((bsz, seq, qk_w), BF16),
                   jax.ShapeDtypeStruct((bsz, seq, v_w), BF16)],
        scratch_shapes=[pltpu.VMEM((ts + CONV_HALO, CONV_CH), F32)],
        compiler_params=_params("arbitrary", "arbitrary"),
        name="l0_front",
    )(x, positions[..., None], w_in_p, b_in_p, conv_w, conv_b[None, :],
      cln_g[None, :], cln_b[None, :], qn_g[None, :], wq_p.astype(BF16),
      wqr_p.astype(BF16), kvn_g[None, :], wk_p.astype(BF16), wv.astype(BF16), invf)


def _l1_front_kernel(x_ref, w_ref, b_ref, wv_ref, bv_ref, wf_ref, bf_ref,
                     q_ref, k_ref, v_ref, cneg_ref, carry):
    ts = x_ref.shape[0]

    @pl.when(pl.program_id(1) == 0)
    def _():
        carry[...] = jnp.zeros_like(carry)

    xb = x_ref[...].astype(BF16)
    mix = FOX_HEADS * FOX_HD
    scale = FOX_HD ** -0.5 * LOG2_E
    q_ref[...] = ((_dot(xb, w_ref[:, 0:mix]) + b_ref[:, 0:mix]) * scale).astype(BF16)
    k_ref[...] = (_dot(xb, w_ref[:, mix:2 * mix]) + b_ref[:, mix:2 * mix]).astype(BF16)
    v_ref[...] = (_dot(xb, wv_ref[...]) + bv_ref[...]
                  + _ones_upper_half(v_ref.shape[1])).astype(BF16)

    log_f = jax.nn.log_sigmoid(_dot_nt(wf_ref[...], xb) + bf_ref[...])
    r = lax.broadcasted_iota(jnp.int32, (ts, ts), 0)
    c = lax.broadcasted_iota(jnp.int32, (ts, ts), 1)
    upper = (r <= c).astype(F32)
    csum = jnp.dot(log_f, upper, precision=lax.Precision.HIGHEST,
                   preferred_element_type=F32) + carry[...]
    cneg_ref[...] = csum * -LOG2_E
    carry[...] = carry[...] + jnp.sum(log_f, axis=1, keepdims=True)


def _l1_front(x, w_in, b_in):
    bsz, seq, _ = x.shape
    mix = FOX_HEADS * FOX_HD
    ts = SEQ_TILE
    row = lambda w: pl.BlockSpec((None, ts, w), lambda b, s: (b, s, 0))
    w_qk = w_in[:, :2 * mix].astype(BF16)
    b_qk = b_in[None, :2 * mix]
    w_v = _pad_heads(w_in[:, 2 * mix:3 * mix], FOX_HEADS).astype(BF16)
    b_v = _pad_heads(b_in[None, 2 * mix:3 * mix], FOX_HEADS)
    v_w = FOX_HEADS * PAD_HEAD
    wf_t = w_in[:, 3 * mix:].T.astype(BF16)
    bf_t = b_in[3 * mix:][:, None]
    return pl.pallas_call(
        _l1_front_kernel,
        grid=(bsz, seq // ts),
        in_specs=[row(D_MODEL), _resident((D_MODEL, 2 * mix)), _resident((1, 2 * mix)),
                  _resident((D_MODEL, v_w)), _resident((1, v_w)),
                  _resident((FOX_HEADS, D_MODEL)), _resident((FOX_HEADS, 1))],
        out_specs=[row(mix), row(mix), row(v_w),
                   pl.BlockSpec((None, FOX_HEADS, ts), lambda b, s: (b, 0, s))],
        out_shape=[jax.ShapeDtypeStruct((bsz, seq, mix), BF16)] * 2
        + [jax.ShapeDtypeStruct((bsz, seq, v_w), BF16),
           jax.ShapeDtypeStruct((bsz, FOX_HEADS, seq), F32)],
        scratch_shapes=[pltpu.VMEM((FOX_HEADS, 1), F32)],
        compiler_params=_params("arbitrary", "arbitrary"),
        name="l1_front",
    )(x, w_qk, b_qk, w_v, b_v, wf_t, bf_t)


def _attn_kernel(*refs, packed, has_bias):
    if has_bias:
        q_ref, k_ref, v_ref, cneg_ref, o_ref = refs
    else:
        q_ref, k_ref, v_ref, o_ref = refs
        cneg_ref = None
    tq = ATTN_TILE
    tk = ATTN_KEY_TILE
    half = LANES // 2

    def query_tile(qi):
        q_rows = pl.ds(pl.multiple_of(qi * tq, tq), tq)
        q2 = q_ref[q_rows, :]
        if packed:
            lane = lax.broadcasted_iota(jnp.int32, q2.shape, 1)
            zero = jnp.zeros_like(q2)
            qs = (jnp.where(lane < half, q2, zero), jnp.where(lane >= half, q2, zero))
        else:
            qs = (q2[:, :PAD_HEAD], q2[:, PAD_HEAD:])

        def k_block(kj, head):
            rows = pl.ds(pl.multiple_of(kj * tk, tk), tk)
            if packed:
                return k_ref[rows, :]
            return k_ref[rows, head * PAD_HEAD:(head + 1) * PAD_HEAD]

        def step(kj, carry, diag):
            rows = pl.ds(pl.multiple_of(kj * tk, tk), tk)
            out = []
            for head in range(2):
                m, acc = carry[head]
                s = _dot_nt(qs[head], k_block(kj, head))
                if has_bias:
                    s = s + cneg_ref[head, :, rows]
                if diag is not None:
                    r = lax.broadcasted_iota(jnp.int32, s.shape, 0)
                    c = lax.broadcasted_iota(jnp.int32, s.shape, 1)
                    s = jnp.where(r >= c + diag * tk, s, NEG_INF)
                m_new = jnp.maximum(m, jnp.max(s, axis=-1, keepdims=True))
                alpha = jnp.exp2(m - m_new)
                p = jnp.exp2(s - m_new).astype(BF16)
                vblk = v_ref[rows, head * PAD_HEAD:(head + 1) * PAD_HEAD]
                out.append((m_new, alpha * acc + _dot(p, vblk)))
            return tuple(out)

        init = tuple((jnp.full((tq, 1), NEG_INF, F32), jnp.zeros((tq, LANES), F32))
                     for _ in range(2))
        per_tile = tq // tk
        n_full = qi * per_tile

        def group(first, c, size):
            for t in range(size):
                c = step(first + t, c, None)
            return c

        carry = lax.fori_loop(0, n_full // ATTN_UNROLL,
                              lambda j, c: group(ATTN_UNROLL * j, c, ATTN_UNROLL), init)
        done = (n_full // ATTN_UNROLL) * ATTN_UNROLL
        size = ATTN_UNROLL // 2
        while size >= 1:
            take = (n_full - done) >= size
            carry = lax.cond(take, lambda c, f=done, n=size: group(f, c, n), lambda c: c,
                             carry)
            done = done + jnp.where(take, size, 0)
            size //= 2
        for d in range(per_tile):
            carry = step(n_full + d, carry, d)
        (_, acc_a), (_, acc_b) = carry
        out_a = acc_a / pltpu.roll(acc_a, half, axis=1)
        out_b = acc_b / pltpu.roll(acc_b, half, axis=1)
        lane_o = lax.broadcasted_iota(jnp.int32, (tq, LANES), 1)
        o_ref[q_rows, :] = jnp.where(lane_o < half, out_a,
                                     pltpu.roll(out_b, half, axis=1)).astype(o_ref.dtype)

    pl.loop(0, q_ref.shape[0] // tq)(query_tile)


def _attention(q, k, v, cneg, packed):
    bsz, seq, v_w = v.shape
    pairs = v_w // (2 * PAD_HEAD)
    qk_w = q.shape[-1] // pairs
    in_specs = [pl.BlockSpec((None, seq, qk_w), lambda b, p: (b, 0, p)),
                pl.BlockSpec((None, seq, qk_w), lambda b, p: (b, 0, p)),
                pl.BlockSpec((None, seq, 2 * PAD_HEAD), lambda b, p: (b, 0, p))]
    args = [q, k, v]
    if cneg is not None:
        in_specs.append(pl.BlockSpec((None, 2, 1, seq), lambda b, p: (b, p, 0, 0)))
        args.append(cneg)
    return pl.pallas_call(
        functools.partial(_attn_kernel, packed=packed, has_bias=cneg is not None),
        grid=(bsz, pairs),
        in_specs=in_specs,
        out_specs=pl.BlockSpec((None, seq, LANES), lambda b, p: (b, 0, p)),
        out_shape=jax.ShapeDtypeStruct((bsz, seq, pairs * LANES), BF16),
        compiler_params=_params("arbitrary", "arbitrary"),
        name="attn_packed" if packed else "attn_padded",
    )(*args)


def _first_hit(hits, found):
    out = []
    for hcur in hits:
        take = jnp.logical_and(hcur, jnp.logical_not(found))
        found = jnp.logical_or(found, take)
        out.append(take)
    return out, found


def _mix_out_router_kernel(a_ref, b_ref, x_ref, wa_ref, wb_ref, bo_ref, g_ref, be_ref,
                           rw_ref, rwl_ref, rb_ref,
                           x1_ref, x1p_ref, e_ref, gate_ref, rank_ref, cnt_ref, cnt, gate_t):
    tm = x_ref.shape[0]

    @pl.when(pl.program_id(0) == 0)
    def _():
        cnt[...] = jnp.zeros_like(cnt)
        gate_t[...] = jnp.zeros_like(gate_t)

    mix = _dot(a_ref[...], wa_ref[...]) + _dot(b_ref[...], wb_ref[...]) + bo_ref[...]
    x1 = _layer_norm(DN_ALPHA * x_ref[...] + mix, g_ref[...], be_ref[...])
    x1_ref[...] = x1
    _store_packed(x1p_ref, _pack_rows(x1))

    x_hi = x1.astype(BF16)
    x_lo = (x1 - x_hi.astype(F32)).astype(BF16)
    logits = (_dot_nt(rw_ref[...], x_hi) + _dot_nt(rw_ref[...], x_lo)
              + _dot_nt(rwl_ref[...], x_hi))
    aff = jax.nn.sigmoid(logits)
    choice = aff + rb_ref[...]
    per_group = N_EXPERTS // N_GROUPS
    sub = lax.broadcasted_iota(jnp.int32, (per_group, tm), 0)
    groups = [choice[g * per_group:(g + 1) * per_group, :] for g in range(N_GROUPS)]

    gscore = []
    for cg in groups:
        m1 = jnp.max(cg, axis=0, keepdims=True)
        i1 = jnp.min(jnp.where(cg == m1, sub, per_group), axis=0, keepdims=True)
        m2 = jnp.max(jnp.where(sub == i1, NEG_INF, cg), axis=0, keepdims=True)
        gscore.append(m1 + m2)

    gsel = [jnp.zeros((1, tm), jnp.bool_) for _ in range(N_GROUPS)]
    for _ in range(TOPK_GROUPS):
        best = functools.reduce(jnp.maximum, gscore)
        takes, _ = _first_hit([gs == best for gs in gscore], jnp.zeros((1, tm), jnp.bool_))
        gsel = [jnp.logical_or(a, t) for a, t in zip(gsel, takes)]
        gscore = [jnp.where(t, NEG_INF, gs) for gs, t in zip(gscore, takes)]

    masked = [jnp.where(gs, cg, NEG_INF) for gs, cg in zip(gsel, groups)]
    eid = [sub + g * per_group for g in range(N_GROUPS)]
    affs = [aff[g * per_group:(g + 1) * per_group, :] for g in range(N_GROUPS)]
    sel = [jnp.zeros((per_group, tm), jnp.bool_) for _ in range(N_GROUPS)]
    picked_e, picked_w = [], []
    for _ in range(TOP_K):
        best = jnp.max(functools.reduce(jnp.maximum, masked), axis=0, keepdims=True)
        cand = [jnp.where(mg == best, ig, N_EXPERTS) for mg, ig in zip(masked, eid)]
        idx = jnp.min(functools.reduce(jnp.minimum, cand), axis=0, keepdims=True)
        onehot = [ig == idx for ig in eid]
        w = functools.reduce(
            jnp.add, [jnp.sum(jnp.where(oh, ag, 0.0), axis=0, keepdims=True)
                      for oh, ag in zip(onehot, affs)])
        picked_e.append(idx)
        picked_w.append(w)
        sel = [jnp.logical_or(sg, oh) for sg, oh in zip(sel, onehot)]
        masked = [jnp.where(oh, NEG_INF, mg) for mg, oh in zip(masked, onehot)]

    wsum = functools.reduce(jnp.add, picked_w)

    sel_f = jnp.concatenate([sg.astype(F32) for sg in sel], axis=0)
    r = lax.broadcasted_iota(jnp.int32, (tm, tm), 0)
    c = lax.broadcasted_iota(jnp.int32, (tm, tm), 1)
    before = _dot(sel_f.astype(BF16), (r < c).astype(BF16)) + cnt[...]
    eall = lax.broadcasted_iota(jnp.int32, (N_EXPERTS, tm), 0)
    for kk in range(TOP_K):
        rank = jnp.sum(jnp.where(eall == picked_e[kk], before, 0.0), axis=0, keepdims=True)
        e_ref[kk:kk + 1, :] = picked_e[kk]
        gate_t[kk:kk + 1, :] = picked_w[kk] / wsum * ROUTED_SCALE
        rank_ref[kk:kk + 1, :] = rank.astype(jnp.int32)
    pad = SUBLANES - TOP_K
    e_ref[TOP_K:, :] = jnp.zeros((pad, tm), jnp.int32)
    rank_ref[TOP_K:, :] = jnp.zeros((pad, tm), jnp.int32)
    gate_ref[...] = gate_t[...].T
    cnt[...] = cnt[...] + jnp.sum(sel_f, axis=1, keepdims=True)
    cnt_ref[...] = jnp.broadcast_to(cnt[...], cnt_ref.shape)


def _mix_out_router(a, b, x, w_out, b_out, ln_g, ln_b, router_w, router_bias):
    n_tok = x.shape[0]
    tm = TOKEN_TILE
    half = w_out.shape[0] // 2
    row = lambda w: pl.BlockSpec((tm, w), lambda i: (i, 0))
    b_spec = (pl.BlockSpec((tm, half), lambda i: (i, 1)) if b is a
              else pl.BlockSpec((tm, half), lambda i: (i, 0)))
    lane_row = pl.BlockSpec((SUBLANES, tm), lambda i: (0, i))
    packed = pl.BlockSpec((tm // SUBLANES, PACK_BLOCKS, SUBLANES, LANES),
                          lambda i: (i, 0, 0, 0))
    w_bf = w_out.astype(BF16)
    rw_t = router_w.T
    rw_hi = rw_t.astype(BF16)
    rw_lo = (rw_t - rw_hi.astype(F32)).astype(BF16)
    return pl.pallas_call(
        _mix_out_router_kernel,
        grid=(n_tok // tm,),
        in_specs=[row(half), b_spec, row(D_MODEL),
                  pl.BlockSpec((half, D_MODEL), lambda i: (0, 0)),
                  pl.BlockSpec((half, D_MODEL), lambda i: (1, 0)),
                  _resident((1, D_MODEL)), _resident((1, D_MODEL)), _resident((1, D_MODEL)),
                  _resident((N_EXPERTS, D_MODEL)), _resident((N_EXPERTS, D_MODEL)),
                  _resident((N_EXPERTS, 1))],
        out_specs=[row(D_MODEL), packed, lane_row, row(LANES), lane_row,
                   _resident((N_EXPERTS, LANES))],
        out_shape=[jax.ShapeDtypeStruct((n_tok, D_MODEL), F32),
                   jax.ShapeDtypeStruct((n_tok // SUBLANES, PACK_BLOCKS, SUBLANES, LANES),
                                        jnp.int32),
                   jax.ShapeDtypeStruct((SUBLANES, n_tok), jnp.int32),
                   jax.ShapeDtypeStruct((n_tok, LANES), F32),
                   jax.ShapeDtypeStruct((SUBLANES, n_tok), jnp.int32),
                   jax.ShapeDtypeStruct((N_EXPERTS, LANES), F32)],
        scratch_shapes=[pltpu.VMEM((N_EXPERTS, 1), F32), pltpu.VMEM((LANES, tm), F32)],
        compiler_params=_params("arbitrary"),
        name="mix_out_router",
    )(a, b, x, w_bf, w_bf, b_out[None, :], ln_g[None, :], ln_b[None, :],
      rw_hi, rw_lo, router_bias[:, None])


def _expert_ffn_kernel(tile_e_ref, tile_on_ref, xs_ref, *refs):
    w_refs = refs[:3 * FFN_SLOTS]
    ys_ref = refs[3 * FFN_SLOTS]
    wg_bf, wu_bf, wd_bf = refs[3 * FFN_SLOTS + 1:]
    i = pl.program_id(0)
    sub = GROUP_TILE // SUBLANES

    for slot in range(FFN_SLOTS):
        t = FFN_SLOTS * i + slot
        fresh = jnp.logical_or(
            i == 0, tile_e_ref[t] != tile_e_ref[jnp.maximum(t - FFN_SLOTS, 0)])

        @pl.when(fresh)
        def _(slot=slot):
            wg_ref, wu_ref, wd_ref = w_refs[3 * slot:3 * slot + 3]
            wg_bf[slot] = wg_ref[...].astype(BF16)
            wu_bf[slot] = wu_ref[...].astype(BF16)
            wd_bf[slot] = wd_ref[...].astype(BF16)

    on = tile_on_ref[FFN_SLOTS * i] > 0

    @pl.when(on)
    def _():
        for slot in range(FFN_SLOTS):
            rows = pl.ds(slot * sub, sub)
            lo, hi = _unpack_rows(_load_packed(xs_ref.at[rows]))
            lo = lo.astype(BF16)
            hi = hi.astype(BF16)
            gate = (_dot(lo, wg_bf[slot, :PACK_COLS, :]) + _dot(hi, wg_bf[slot, PACK_COLS:, :]))
            up = (_dot(lo, wu_bf[slot, :PACK_COLS, :]) + _dot(hi, wu_bf[slot, PACK_COLS:, :]))
            hid = (jax.nn.silu(gate) * up).astype(BF16)
            _store_packed(ys_ref.at[rows], _pack_rows(_dot(hid, wd_bf[slot])))

    @pl.when(jnp.logical_not(on))
    def _():
        ys_ref[...] = jnp.zeros_like(ys_ref)


def _expert_ffn(xs, tile_e, tile_on, w_gate, w_up, w_down):
    rows_per_step = GROUP_TILE * FFN_SLOTS
    packed = pl.BlockSpec((rows_per_step // SUBLANES, PACK_BLOCKS, SUBLANES, LANES),
                          lambda i, te, on: (i, 0, 0, 0))
    w_specs = []
    for slot in range(FFN_SLOTS):
        pick = lambda i, te, on, slot=slot: (te[FFN_SLOTS * i + slot], 0, 0)
        w_specs += [pl.BlockSpec((None, D_MODEL, EXPERT_HID), pick),
                    pl.BlockSpec((None, D_MODEL, EXPERT_HID), pick),
                    pl.BlockSpec((None, EXPERT_HID, D_MODEL), pick)]
    grid_spec = pltpu.PrefetchScalarGridSpec(
        num_scalar_prefetch=2,
        grid=(xs.shape[0] * SUBLANES // rows_per_step,),
        in_specs=[packed] + w_specs,
        out_specs=packed,
        scratch_shapes=[pltpu.VMEM((FFN_SLOTS, D_MODEL, EXPERT_HID), BF16),
                        pltpu.VMEM((FFN_SLOTS, D_MODEL, EXPERT_HID), BF16),
                        pltpu.VMEM((FFN_SLOTS, EXPERT_HID, D_MODEL), BF16)],
    )
    return pl.pallas_call(
        _expert_ffn_kernel,
        grid_spec=grid_spec,
        out_shape=jax.ShapeDtypeStruct(xs.shape, jnp.int32),
        compiler_params=_params("arbitrary"),
        name="expert_ffn",
    )(tile_e, tile_on, xs, *([w_gate, w_up, w_down] * FFN_SLOTS))


def _moe_out_kernel(x1_ref, yg_ref, gate_ref, sg_ref, su_ref, sd_ref, g_ref, b_ref, *rest):
    o_ref = rest[-1]
    x1 = x1_ref[...]
    xb = x1.astype(BF16)
    hid = jax.nn.silu(_dot(xb, sg_ref[...])) * _dot(xb, su_ref[...])
    shared = _dot(hid.astype(BF16), sd_ref[...])
    gates = gate_ref[...]
    lo_acc = shared[:, :PACK_COLS]
    hi_acc = shared[:, PACK_COLS:]
    for kk in range(TOP_K):
        lo, hi = _unpack_rows(_load_packed(yg_ref.at[kk]))
        w = gates[:, kk:kk + 1]
        lo_acc = lo_acc + w * lo
        hi_acc = hi_acc + w * hi
    moe = jnp.concatenate([lo_acc, hi_acc], axis=1)
    o_ref[...] = _layer_norm(DN_ALPHA * x1 + moe, g_ref[...], b_ref[...])


def _moe_out(x1, yg, gates, sh_gate, sh_up, sh_down, ln_g, ln_b, part, earlier):
    n_tok = x1.shape[0]
    tm = TOKEN_TILE
    steps = n_tok // tm // MOE_PARTS
    first = part * steps
    row = lambda w: pl.BlockSpec((tm, w), lambda i: (i + first, 0))
    in_specs = [row(D_MODEL),
                pl.BlockSpec((TOP_K, tm // SUBLANES, PACK_BLOCKS, SUBLANES, LANES),
                             lambda i: (0, i, 0, 0, 0)),
                row(LANES),
                _resident((D_MODEL, EXPERT_HID)), _resident((D_MODEL, EXPERT_HID)),
                _resident((EXPERT_HID, D_MODEL)), _resident((1, D_MODEL)),
                _resident((1, D_MODEL))]
    args = [x1, yg, gates, sh_gate.astype(BF16), sh_up.astype(BF16), sh_down.astype(BF16),
            ln_g[None, :], ln_b[None, :]]
    aliases = {}
    if earlier is not None:
        in_specs.append(pl.BlockSpec(memory_space=pl.ANY))
        args.append(earlier)
        aliases = {len(args) - 1: 0}
    return pl.pallas_call(
        _moe_out_kernel,
        grid=(steps,),
        in_specs=in_specs,
        out_specs=row(D_MODEL),
        out_shape=jax.ShapeDtypeStruct((n_tok, D_MODEL), F32),
        input_output_aliases=aliases,
        compiler_params=_params("arbitrary"),
        name="moe_out",
    )(*args)


def _slot_kernel(offset_ref, e_ref, rank_ref, pos_ref):
    e = e_ref[...]
    base = jnp.zeros(e.shape, jnp.int32)
    for ex in range(N_EXPERTS):
        base = jnp.where(e == ex, offset_ref[ex], base)
    pos_ref[...] = base + rank_ref[...]


def _slots(offset, e6, rank6):
    n_tok = e6.shape[1]
    tl = 4096
    lane_row = pl.BlockSpec((SUBLANES, tl), lambda i, off: (0, i))
    return pl.pallas_call(
        _slot_kernel,
        grid_spec=pltpu.PrefetchScalarGridSpec(
            num_scalar_prefetch=1, grid=(n_tok // tl,),
            in_specs=[lane_row, lane_row], out_specs=lane_row),
        out_shape=jax.ShapeDtypeStruct((SUBLANES, n_tok), jnp.int32),
        compiler_params=_params("arbitrary"),
        name="slots",
    )(offset, e6, rank6)


def _sc_mesh():
    return plsc.VectorSubcoreMesh(core_axis_name="c", subcore_axis_name="s")


def _sc_worker():
    return lax.axis_index("s") * SC_CORES + lax.axis_index("c")


def _sc_dispatch(x_sub, idx, n_out_sub):
    windows = x_sub.shape[0] // (SC_WORKERS * SC_WINDOW)

    @functools.partial(
        pl.kernel, mesh=_sc_mesh(),
        out_type=jax.ShapeDtypeStruct((n_out_sub, LANES), jnp.int32),
        scratch_types=[pltpu.VMEM((SC_WINDOW, LANES), jnp.int32),
                       pltpu.VMEM((TOP_K, SC_WINDOW), jnp.int32)],
        name="sc_dispatch")
    def run(x_hbm, idx_hbm, out_hbm, rows_v, idx_v):
        wid = _sc_worker()

        @pl.loop(0, windows)
        def _(c):
            base = (wid * windows + c) * SC_WINDOW
            pltpu.sync_copy(x_hbm.at[pl.ds(base, SC_WINDOW)], rows_v)
            pltpu.sync_copy(idx_hbm.at[wid, c], idx_v)
            for kk in range(TOP_K):
                pltpu.sync_copy(rows_v, out_hbm.at[idx_v.at[kk]])

    return run(x_sub, idx)


def _sc_combine(y_sub, idx, n_tok_sub):
    windows = n_tok_sub // (SC_WORKERS * SC_WINDOW)

    @functools.partial(
        pl.kernel, mesh=_sc_mesh(),
        out_type=jax.ShapeDtypeStruct((TOP_K, n_tok_sub, LANES), jnp.int32),
        scratch_types=[pltpu.VMEM((2, SC_WINDOW, LANES), jnp.int32),
                       pltpu.VMEM((TOP_K, SC_WINDOW), jnp.int32),
                       pltpu.SemaphoreType.DMA((2,)),
                       pltpu.SemaphoreType.DMA((2,))],
        name="sc_combine")
    def run(y_hbm, idx_hbm, out_hbm, rows_v, idx_v, gather_sem, write_sem):
        wid = _sc_worker()

        def gather(kk):
            buf = kk % 2
            return pltpu.async_copy(y_hbm.at[idx_v.at[kk]], rows_v.at[buf], gather_sem.at[buf])

        @pl.loop(0, windows)
        def _(c):
            base = (wid * windows + c) * SC_WINDOW
            pltpu.sync_copy(idx_hbm.at[wid, c], idx_v)
            writes = []
            pending = gather(0)
            for kk in range(TOP_K):
                pending.wait()
                if kk + 1 < TOP_K:
                    if kk >= 1:
                        writes[kk - 1].wait()
                    pending = gather(kk + 1)
                writes.append(pltpu.async_copy(
                    rows_v.at[kk % 2], out_hbm.at[kk, pl.ds(base, SC_WINDOW)],
                    write_sem.at[kk % 2]))
            writes[TOP_K - 2].wait()
            writes[TOP_K - 1].wait()

    return run(y_sub, idx)


def _moe(x1, x1p, e6, gates, rank6, counts, exp_w_gate, exp_w_up, exp_w_down,
         sh_w_gate, sh_w_up, sh_w_down, ln_g, ln_b):
    n_tok = x1.shape[0]
    tg = GROUP_TILE
    n_rows = n_tok * TOP_K + N_EXPERTS * tg
    n_tiles = n_rows // tg

    cnt = counts[:, 0].astype(jnp.int32)
    tiles_per_e = (cnt + tg - 1) // tg
    tile_end = jnp.cumsum(tiles_per_e)
    offset = (tile_end - tiles_per_e) * tg
    tile_ids = jnp.arange(n_tiles, dtype=jnp.int32)
    tile_on = (tile_ids < tile_end[-1]).astype(jnp.int32)
    tile_e = jnp.minimum(
        jnp.sum((tile_ids[:, None] >= tile_end[None, :]).astype(jnp.int32), axis=1),
        N_EXPERTS - 1)
    tile_e = jnp.where(tile_on > 0, tile_e, tile_e[jnp.maximum(tile_end[-1] - 1, 0)])

    pos = _slots(offset, e6, rank6)[:TOP_K]

    group = PACK_BLOCKS * SUBLANES
    sub0 = (pos // SUBLANES) * group + pos % SUBLANES
    lane = jnp.arange(SC_WINDOW)
    spread = (jnp.arange(group)[:, None]
              == (lane // group) * SUBLANES + lane % SUBLANES).astype(F32)
    idx = jnp.einsum("kws,sd->kwd", sub0.reshape(TOP_K, n_tok // group, group).astype(F32),
                     spread, precision=lax.Precision.HIGHEST).astype(jnp.int32)
    idx = idx + ((lane % group) // SUBLANES * SUBLANES).astype(jnp.int32)
    n_sub = n_tok * PACK_BLOCKS
    idx = idx.reshape(TOP_K, n_sub)

    def windowed(ix):
        windows = ix.shape[1] // (SC_WORKERS * SC_WINDOW)
        return ix.reshape(TOP_K, SC_WORKERS, windows, SC_WINDOW).transpose(1, 2, 0, 3)

    xs = _sc_dispatch(x1p.reshape(n_sub, LANES), windowed(idx), n_rows * PACK_BLOCKS)
    ys = _expert_ffn(xs.reshape(n_rows // SUBLANES, PACK_BLOCKS, SUBLANES, LANES),
                     tile_e, tile_on, exp_w_gate, exp_w_up, exp_w_down)
    ys = ys.reshape(n_rows * PACK_BLOCKS, LANES)
    part_sub = n_sub // MOE_PARTS
    out = None
    for part in range(MOE_PARTS):
        yg = _sc_combine(ys, windowed(idx[:, part * part_sub:(part + 1) * part_sub]), part_sub)
        yg = yg.reshape(TOP_K, part_sub // (PACK_BLOCKS * SUBLANES), PACK_BLOCKS,
                        SUBLANES, LANES)
        out = _moe_out(x1, yg, gates, sh_w_gate, sh_w_up, sh_w_down, ln_g, ln_b, part, out)
    return out


def kernel(x, positions,
           l0_w_in, l0_b_in, l0_conv_w, l0_conv_b, l0_conv_ln_g, l0_conv_ln_b,
           l0_q_norm_g, l0_w_uq, l0_kv_norm_g, l0_w_ukv, l0_w_out, l0_b_out,
           l0_ln1_g, l0_ln1_b,
           l0_router_w, l0_router_bias, l0_exp_w_gate, l0_exp_w_up, l0_exp_w_down,
           l0_sh_w_gate, l0_sh_w_up, l0_sh_w_down, l0_ln2_g, l0_ln2_b,
           l1_w_in, l1_b_in, l1_w_out, l1_b_out, l1_ln1_g, l1_ln1_b,
           l1_router_w, l1_router_bias, l1_exp_w_gate, l1_exp_w_up, l1_exp_w_down,
           l1_sh_w_gate, l1_sh_w_up, l1_sh_w_down, l1_ln2_g, l1_ln2_b):
    bsz, seq, d = x.shape
    n_tok = bsz * seq

    u, q, k, v = _l0_front(x, positions, l0_w_in, l0_b_in, l0_conv_w, l0_conv_b,
                           l0_conv_ln_g, l0_conv_ln_b, l0_q_norm_g, l0_w_uq,
                           l0_kv_norm_g, l0_w_ukv)
    o = _attention(q, k, v, None, packed=False)
    x_flat = x.reshape(n_tok, d)
    routing = _mix_out_router(u.reshape(n_tok, -1), o.reshape(n_tok, -1), x_flat,
                              l0_w_out, l0_b_out, l0_ln1_g, l0_ln1_b,
                              l0_router_w, l0_router_bias)
    x2 = _moe(*routing, l0_exp_w_gate, l0_exp_w_up, l0_exp_w_down,
              l0_sh_w_gate, l0_sh_w_up, l0_sh_w_down, l0_ln2_g, l0_ln2_b)

    q, k, v, cneg = _l1_front(x2.reshape(bsz, seq, d), l1_w_in, l1_b_in)
    o = _attention(q, k, v, cneg[:, :, None, :], packed=True).reshape(n_tok, -1)
    routing = _mix_out_router(o, o, x2, l1_w_out, l1_b_out, l1_ln1_g, l1_ln1_b,
                              l1_router_w, l1_router_bias)
    x3 = _moe(*routing, l1_exp_w_gate, l1_exp_w_up, l1_exp_w_down,
              l1_sh_w_gate, l1_sh_w_up, l1_sh_w_down, l1_ln2_g, l1_ln2_b)
    return x3.reshape(bsz, seq, d)
```

```python
import functools

import jax
import jax.numpy as jnp
from jax import lax
from jax.experimental import pallas as pl
from jax.experimental.pallas import tpu as pltpu
from jax.experimental.pallas import tpu_sc as plsc

D_MODEL = 1024
DEPTH = 2
DN_ALPHA = (2 * DEPTH) ** 0.25
LN_EPS = 1e-5
RMS_EPS = 1e-6

CONV_CH = 512
CONV_WIDTH = 31
MLA_HEADS = 8
QK_NOPE = 64
QK_ROPE = 32
V_DIM = 64
Q_LORA = 256
KV_LORA = 128
ROPE_THETA = 10000.0
FOX_HEADS = 16
FOX_HD = 64
N_EXPERTS = 64
TOP_K = 6
N_GROUPS = 8
TOPK_GROUPS = 4
EXPERT_HID = 256
ROUTED_SCALE = 2.5

LANES = 128
SUBLANES = 8
VMEM_LIMIT_BYTES = 56 * 1024 * 1024

SEQ_TILE = 512
TOKEN_TILE = 512
ATTN_TILE = 512
ATTN_KEY_TILE = 512
ATTN_UNROLL = 4
GROUP_TILE = 512
FFN_SLOTS = 2
CONV_ROWS = 32
CONV_HALO = 32
PAD_HEAD = 128

PACK_COLS = D_MODEL // 2
PACK_BLOCKS = PACK_COLS // LANES
SC_CORES = 2
SC_SUBCORES = 16
SC_WORKERS = SC_CORES * SC_SUBCORES
SC_WINDOW = 128
MOE_PARTS = 2
BF16 = jnp.bfloat16
F32 = jnp.float32
NEG_INF = float("-inf")
LOG2_E = 1.4426950408889634


def _params(*semantics):
    return pltpu.CompilerParams(dimension_semantics=semantics,
                                vmem_limit_bytes=VMEM_LIMIT_BYTES)


def _resident(shape):
    nd = len(shape)
    return pl.BlockSpec(shape, lambda *_: (0,) * nd)


def _dot(a, b):
    return jnp.dot(a, b, preferred_element_type=F32)


def _dot_nt(a, b, precision=None):
    return lax.dot_general(a, b, (((1,), (1,)), ((), ())),
                           precision=precision, preferred_element_type=F32)


def _layer_norm(x, g, b):
    mu = jnp.mean(x, axis=-1, keepdims=True)
    xc = x - mu
    var = jnp.mean(xc * xc, axis=-1, keepdims=True)
    return xc * lax.rsqrt(var + LN_EPS) * g + b


def _rms_norm(x, g):
    return x * lax.rsqrt(jnp.mean(x * x, axis=-1, keepdims=True) + RMS_EPS) * g


def _ones_upper_half(width):
    lane = lax.broadcasted_iota(jnp.int32, (1, width), 1)
    return jnp.where((lane & (PAD_HEAD - 1)) >= V_DIM, 1.0, 0.0).astype(F32)


def _pad_heads(w, heads):
    d = w.shape[-1] // heads
    w3 = w.reshape(w.shape[:-1] + (heads, d))
    pad = jnp.zeros(w.shape[:-1] + (heads, PAD_HEAD - d), w.dtype)
    return jnp.concatenate([w3, pad], axis=-1).reshape(w.shape[:-1] + (heads * PAD_HEAD,))


def _pack_rows(x):
    lo = lax.bitcast_convert_type(x[:, :PACK_COLS].astype(BF16).astype(F32), jnp.uint32)
    hi = lax.bitcast_convert_type(x[:, PACK_COLS:].astype(BF16).astype(F32), jnp.uint32)
    word = (lo >> 16) | (hi & jnp.uint32(0xFFFF0000))
    return lax.bitcast_convert_type(word, jnp.int32)


def _unpack_rows(words):
    u = lax.bitcast_convert_type(words, jnp.uint32)
    lo = lax.bitcast_convert_type(u << 16, F32)
    hi = lax.bitcast_convert_type(u & jnp.uint32(0xFFFF0000), F32)
    return lo, hi


def _store_packed(ref, words):
    groups = words.shape[0] // SUBLANES
    for cb in range(PACK_BLOCKS):
        ref[:, cb, :, :] = words[:, cb * LANES:(cb + 1) * LANES].reshape(groups, SUBLANES, LANES)


def _load_packed(ref):
    rows = ref.shape[0] * SUBLANES
    return jnp.concatenate(
        [ref[:, cb, :, :].reshape(rows, LANES) for cb in range(PACK_BLOCKS)], axis=1)


L0_A = 0
L0_G = CONV_CH
L0_Q = 2 * CONV_CH
L0_KV = L0_Q + Q_LORA
L0_KR = L0_KV + KV_LORA
L0_KRR = L0_KR + PAD_HEAD
L0_COLS = L0_KRR + PAD_HEAD


def _l0_front_kernel(x_ref, pos_ref, w_in_ref, b_in_ref, conv_w_ref, conv_b_ref,
                     cln_g_ref, cln_b_ref, qn_g_ref, wq_ref, wqr_ref, kvn_g_ref,
                     wk_ref, wv_ref, invf_ref,
                     u_ref, q_ref, k_ref, v_ref, ubuf):
    ts = x_ref.shape[0]

    @pl.when(pl.program_id(1) == 0)
    def _():
        ubuf[0:CONV_HALO, :] = jnp.zeros((CONV_HALO, CONV_CH), F32)

    h = _dot(x_ref[...].astype(BF16), w_in_ref[...]) + b_in_ref[...]

    ubuf[CONV_HALO:CONV_HALO + ts, :] = (
        h[:, L0_A:L0_A + CONV_CH] * jax.nn.sigmoid(h[:, L0_G:L0_G + CONV_CH]))
    first_tap = CONV_HALO - (CONV_WIDTH - 1)

    def conv_chunk(c, carry):
        base = pl.multiple_of(c * CONV_ROWS, CONV_ROWS)
        acc = jnp.broadcast_to(conv_b_ref[...], (CONV_ROWS, CONV_CH))
        for res in range(SUBLANES):
            rows = CONV_ROWS + (SUBLANES if res else 0)
            part = None
            for off in range(res, first_tap + CONV_WIDTH, SUBLANES):
                j = off - first_tap
                if j < 0:
                    continue
                term = ubuf[pl.ds(base + (off - res), rows), :] * conv_w_ref[j:j + 1, :]
                part = term if part is None else part + term
            acc = acc + part[res:res + CONV_ROWS, :]
        y = _layer_norm(acc, cln_g_ref[...], cln_b_ref[...])
        u_ref[pl.ds(base, CONV_ROWS), :] = (y * jax.nn.sigmoid(y)).astype(BF16)
        return carry

    lax.fori_loop(0, ts // CONV_ROWS, conv_chunk, 0)
    ubuf[0:CONV_HALO, :] = ubuf[ts:ts + CONV_HALO, :]

    ang = pos_ref[...].astype(F32) * invf_ref[...]
    cos = jnp.cos(ang)
    sin = jnp.sin(ang)
    scale = (QK_NOPE + QK_ROPE) ** -0.5 * LOG2_E

    qn = _rms_norm(h[:, L0_Q:L0_Q + Q_LORA], qn_g_ref[...]).astype(BF16)
    q = _dot(qn, wq_ref[...])
    q_rot = _dot(qn, wqr_ref[...])
    cos_s = cos * scale
    sin_s = sin * scale
    for hh in range(MLA_HEADS):
        blk = slice(hh * PAD_HEAD, (hh + 1) * PAD_HEAD)
        q_ref[:, blk] = (q[:, blk] * cos_s + q_rot[:, blk] * sin_s).astype(BF16)

    kvn = _rms_norm(h[:, L0_KV:L0_KV + KV_LORA], kvn_g_ref[...]).astype(BF16)
    k_nope = _dot(kvn, wk_ref[...])
    k_pe = h[:, L0_KR:L0_KR + PAD_HEAD] * cos + h[:, L0_KRR:L0_KRR + PAD_HEAD] * sin
    for hh in range(MLA_HEADS):
        blk = slice(hh * PAD_HEAD, (hh + 1) * PAD_HEAD)
        k_ref[:, blk] = (k_nope[:, blk] + k_pe).astype(BF16)
    v_ref[...] = (_dot(kvn, wv_ref[...]) + _ones_upper_half(v_ref.shape[1])).astype(BF16)


def _rope_rotate_cols(w):
    half = QK_ROPE // 2
    return jnp.concatenate([-w[..., half:], w[..., :half]], axis=-1)


def _l0_front(x, positions, w_in, b_in, conv_w, conv_b, cln_g, cln_b,
              qn_g, w_uq, kvn_g, w_ukv):
    bsz, seq, _ = x.shape
    pad_lo = jnp.zeros((D_MODEL, QK_NOPE), F32)
    pad_hi = jnp.zeros((D_MODEL, PAD_HEAD - QK_NOPE - QK_ROPE), F32)
    i3 = L0_KR
    w_kr = w_in[:, i3:i3 + QK_ROPE]
    w_in_p = jnp.concatenate(
        [w_in[:, :i3], pad_lo, w_kr, pad_hi, pad_lo, _rope_rotate_cols(w_kr), pad_hi],
        axis=1).astype(BF16)
    b_kr = b_in[i3:i3 + QK_ROPE]
    zlo = jnp.zeros((QK_NOPE,), F32)
    zhi = jnp.zeros((PAD_HEAD - QK_NOPE - QK_ROPE,), F32)
    b_in_p = jnp.concatenate(
        [b_in[:i3], zlo, b_kr, zhi, zlo, _rope_rotate_cols(b_kr), zhi])[None, :]

    dq = QK_NOPE + QK_ROPE
    wq3 = w_uq.reshape(Q_LORA, MLA_HEADS, dq)
    zq = jnp.zeros((Q_LORA, MLA_HEADS, PAD_HEAD - dq), F32)
    wq_p = jnp.concatenate([wq3, zq], axis=-1).reshape(Q_LORA, MLA_HEADS * PAD_HEAD)
    wqr_p = jnp.concatenate(
        [jnp.zeros((Q_LORA, MLA_HEADS, QK_NOPE), F32),
         _rope_rotate_cols(wq3[..., QK_NOPE:]), zq], axis=-1
    ).reshape(Q_LORA, MLA_HEADS * PAD_HEAD)
    wkv3 = w_ukv.reshape(KV_LORA, MLA_HEADS, QK_NOPE + V_DIM)
    wk_p = jnp.concatenate(
        [wkv3[..., :QK_NOPE], jnp.zeros((KV_LORA, MLA_HEADS, PAD_HEAD - QK_NOPE), F32)],
        axis=-1).reshape(KV_LORA, MLA_HEADS * PAD_HEAD)
    wv = _pad_heads(wkv3[..., QK_NOPE:].reshape(KV_LORA, MLA_HEADS * V_DIM), MLA_HEADS)

    inv_freq = 1.0 / (ROPE_THETA ** (jnp.arange(0, QK_ROPE, 2, dtype=F32) / QK_ROPE))
    invf = jnp.concatenate([jnp.zeros((QK_NOPE,), F32), inv_freq, inv_freq,
                            jnp.zeros((PAD_HEAD - QK_NOPE - QK_ROPE,), F32)])[None, :]

    ts = SEQ_TILE
    row = lambda w: pl.BlockSpec((None, ts, w), lambda b, s: (b, s, 0))
    qk_w = MLA_HEADS * PAD_HEAD
    v_w = MLA_HEADS * PAD_HEAD
    return pl.pallas_call(
        _l0_front_kernel,
        grid=(bsz, seq // ts),
        in_specs=[row(D_MODEL), row(1),
                  _resident((D_MODEL, L0_COLS)), _resident((1, L0_COLS)),
                  _resident((CONV_WIDTH, CONV_CH)), _resident((1, CONV_CH)),
                  _resident((1, CONV_CH)), _resident((1, CONV_CH)),
                  _resident((1, Q_LORA)), _resident((Q_LORA, qk_w)),
                  _resident((Q_LORA, qk_w)), _resident((1, KV_LORA)),
                  _resident((KV_LORA, qk_w)), _resident((KV_LORA, v_w)),
                  _resident((1, PAD_HEAD))],
        out_specs=[row(CONV_CH), row(qk_w), row(qk_w), row(v_w)],
        out_shape=[jax.ShapeDtypeStruct((bsz, seq, CONV_CH), BF16),
                   jax.ShapeDtypeStruct((bsz, seq, qk_w), BF16),
                   jax.ShapeDtypeStruct((bsz, seq, qk_w), BF16),
                   jax.ShapeDtypeStruct((bsz, seq, v_w), BF16)],
        scratch_shapes=[pltpu.VMEM((ts + CONV_HALO, CONV_CH), F32)],
        compiler_params=_params("arbitrary", "arbitrary"),
        name="l0_front",
    )(x, positions[..., None], w_in_p, b_in_p, conv_w, conv_b[None, :],
      cln_g[None, :], cln_b[None, :], qn_g[None, :], wq_p.astype(BF16),
      wqr_p.astype(BF16), kvn_g[None, :], wk_p.astype(BF16), wv.astype(BF16), invf)


def _l1_front_kernel(x_ref, w_ref, b_ref, wv_ref, bv_ref, wf_ref, bf_ref,
                     q_ref, k_ref, v_ref, cneg_ref, carry):
    ts = x_ref.shape[0]

    @pl.when(pl.program_id(1) == 0)
    def _():
        carry[...] = jnp.zeros_like(carry)

    xb = x_ref[...].astype(BF16)
    mix = FOX_HEADS * FOX_HD
    scale = FOX_HD ** -0.5 * LOG2_E
    q_ref[...] = ((_dot(xb, w_ref[:, 0:mix]) + b_ref[:, 0:mix]) * scale).astype(BF16)
    k_ref[...] = (_dot(xb, w_ref[:, mix:2 * mix]) + b_ref[:, mix:2 * mix]).astype(BF16)
    v_ref[...] = (_dot(xb, wv_ref[...]) + bv_ref[...]
                  + _ones_upper_half(v_ref.shape[1])).astype(BF16)

    log_f = jax.nn.log_sigmoid(_dot_nt(wf_ref[...], xb) + bf_ref[...])
    r = lax.broadcasted_iota(jnp.int32, (ts, ts), 0)
    c = lax.broadcasted_iota(jnp.int32, (ts, ts), 1)
    upper = (r <= c).astype(F32)
    csum = jnp.dot(log_f, upper, precision=lax.Precision.HIGHEST,
                   preferred_element_type=F32) + carry[...]
    cneg_ref[...] = csum * -LOG2_E
    carry[...] = carry[...] + jnp.sum(log_f, axis=1, keepdims=True)


def _l1_front(x, w_in, b_in):
    bsz, seq, _ = x.shape
    mix = FOX_HEADS * FOX_HD
    ts = SEQ_TILE
    row = lambda w: pl.BlockSpec((None, ts, w), lambda b, s: (b, s, 0))
    w_qk = w_in[:, :2 * mix].astype(BF16)
    b_qk = b_in[None, :2 * mix]
    w_v = _pad_heads(w_in[:, 2 * mix:3 * mix], FOX_HEADS).astype(BF16)
    b_v = _pad_heads(b_in[None, 2 * mix:3 * mix], FOX_HEADS)
    v_w = FOX_HEADS * PAD_HEAD
    wf_t = w_in[:, 3 * mix:].T.astype(BF16)
    bf_t = b_in[3 * mix:][:, None]
    return pl.pallas_call(
        _l1_front_kernel,
        grid=(bsz, seq // ts),
        in_specs=[row(D_MODEL), _resident((D_MODEL, 2 * mix)), _resident((1, 2 * mix)),
                  _resident((D_MODEL, v_w)), _resident((1, v_w)),
                  _resident((FOX_HEADS, D_MODEL)), _resident((FOX_HEADS, 1))],
        out_specs=[row(mix), row(mix), row(v_w),
                   pl.BlockSpec((None, FOX_HEADS, ts), lambda b, s: (b, 0, s))],
        out_shape=[jax.ShapeDtypeStruct((bsz, seq, mix), BF16)] * 2
        + [jax.ShapeDtypeStruct((bsz, seq, v_w), BF16),
           jax.ShapeDtypeStruct((bsz, FOX_HEADS, seq), F32)],
        scratch_shapes=[pltpu.VMEM((FOX_HEADS, 1), F32)],
        compiler_params=_params("arbitrary", "arbitrary"),
        name="l1_front",
    )(x, w_qk, b_qk, w_v, b_v, wf_t, bf_t)


def _attn_kernel(*refs, packed, has_bias):
    if has_bias:
        q_ref, k_ref, v_ref, cneg_ref, o_ref = refs
    else:
        q_ref, k_ref, v_ref, o_ref = refs
        cneg_ref = None
    tq = ATTN_TILE
    tk = ATTN_KEY_TILE
    half = LANES // 2

    def query_tile(qi):
        q_rows = pl.ds(pl.multiple_of(qi * tq, tq), tq)
        q2 = q_ref[q_rows, :]
        if packed:
            lane = lax.broadcasted_iota(jnp.int32, q2.shape, 1)
            zero = jnp.zeros_like(q2)
            qs = (jnp.where(lane < half, q2, zero), jnp.where(lane >= half, q2, zero))
        else:
            qs = (q2[:, :PAD_HEAD], q2[:, PAD_HEAD:])

        def k_block(kj, head):
            rows = pl.ds(pl.multiple_of(kj * tk, tk), tk)
            if packed:
                return k_ref[rows, :]
            return k_ref[rows, head * PAD_HEAD:(head + 1) * PAD_HEAD]

        def step(kj, carry, diag):
            rows = pl.ds(pl.multiple_of(kj * tk, tk), tk)
            out = []
            for head in range(2):
                m, acc = carry[head]
                s = _dot_nt(qs[head], k_block(kj, head))
                if has_bias:
                    s = s + cneg_ref[head, :, rows]
                if diag is not None:
                    r = lax.broadcasted_iota(jnp.int32, s.shape, 0)
                    c = lax.broadcasted_iota(jnp.int32, s.shape, 1)
                    s = jnp.where(r >= c + diag * tk, s, NEG_INF)
                m_new = jnp.maximum(m, jnp.max(s, axis=-1, keepdims=True))
                alpha = jnp.exp2(m - m_new)
                p = jnp.exp2((s - m_new).astype(BF16))
                vblk = v_ref[rows, head * PAD_HEAD:(head + 1) * PAD_HEAD]
                out.append((m_new, alpha * acc + _dot(p, vblk)))
            return tuple(out)

        init = tuple((jnp.full((tq, 1), NEG_INF, F32), jnp.zeros((tq, LANES), F32))
                     for _ in range(2))
        per_tile = tq // tk
        n_full = qi * per_tile

        def group(first, c, size):
            for t in range(size):
                c = step(first + t, c, None)
            return c

        carry = lax.fori_loop(0, n_full // ATTN_UNROLL,
                              lambda j, c: group(ATTN_UNROLL * j, c, ATTN_UNROLL), init)
        done = (n_full // ATTN_UNROLL) * ATTN_UNROLL
        size = ATTN_UNROLL // 2
        while size >= 1:
            take = (n_full - done) >= size
            carry = lax.cond(take, lambda c, f=done, n=size: group(f, c, n), lambda c: c,
                             carry)
            done = done + jnp.where(take, size, 0)
            size //= 2
        for d in range(per_tile):
            carry = step(n_full + d, carry, d)
        (_, acc_a), (_, acc_b) = carry
        out_a = acc_a / pltpu.roll(acc_a, half, axis=1)
        out_b = acc_b / pltpu.roll(acc_b, half, axis=1)
        lane_o = lax.broadcasted_iota(jnp.int32, (tq, LANES), 1)
        o_ref[q_rows, :] = jnp.where(lane_o < half, out_a,
                                     pltpu.roll(out_b, half, axis=1)).astype(o_ref.dtype)

    pl.loop(0, q_ref.shape[0] // tq)(query_tile)


def _attention(q, k, v, cneg, packed):
    bsz, seq, v_w = v.shape
    pairs = v_w // (2 * PAD_HEAD)
    qk_w = q.shape[-1] // pairs
    in_specs = [pl.BlockSpec((None, seq, qk_w), lambda b, p: (b, 0, p)),
                pl.BlockSpec((None, seq, qk_w), lambda b, p: (b, 0, p)),
                pl.BlockSpec((None, seq, 2 * PAD_HEAD), lambda b, p: (b, 0, p))]
    args = [q, k, v]
    if cneg is not None:
        in_specs.append(pl.BlockSpec((None, 2, 1, seq), lambda b, p: (b, p, 0, 0)))
        args.append(cneg)
    return pl.pallas_call(
        functools.partial(_attn_kernel, packed=packed, has_bias=cneg is not None),
        grid=(bsz, pairs),
        in_specs=in_specs,
        out_specs=pl.BlockSpec((None, seq, LANES), lambda b, p: (b, 0, p)),
        out_shape=jax.ShapeDtypeStruct((bsz, seq, pairs * LANES), BF16),
        compiler_params=_params("arbitrary", "arbitrary"),
        name="attn_packed" if packed else "attn_padded",
    )(*args)


def _first_hit(hits, found):
    out = []
    for hcur in hits:
        take = jnp.logical_and(hcur, jnp.logical_not(found))
        found = jnp.logical_or(found, take)
        out.append(take)
    return out, found


def _mix_out_router_kernel(a_ref, b_ref, x_ref, wa_ref, wb_ref, bo_ref, g_ref, be_ref,
                           rw_ref, rwl_ref, rb_ref,
                           x1_ref, x1p_ref, e_ref, gate_ref, rank_ref, cnt_ref, cnt, gate_t):
    tm = x_ref.shape[0]

    @pl.when(pl.program_id(0) == 0)
    def _():
        cnt[...] = jnp.zeros_like(cnt)
        gate_t[...] = jnp.zeros_like(gate_t)

    mix = _dot(a_ref[...], wa_ref[...]) + _dot(b_ref[...], wb_ref[...]) + bo_ref[...]
    x1 = _layer_norm(DN_ALPHA * x_ref[...] + mix, g_ref[...], be_ref[...])
    x1_ref[...] = x1
    _store_packed(x1p_ref, _pack_rows(x1))

    x_hi = x1.astype(BF16)
    x_lo = (x1 - x_hi.astype(F32)).astype(BF16)
    logits = (_dot_nt(rw_ref[...], x_hi) + _dot_nt(rw_ref[...], x_lo)
              + _dot_nt(rwl_ref[...], x_hi))
    aff = jax.nn.sigmoid(logits)
    choice = aff + rb_ref[...]
    per_group = N_EXPERTS // N_GROUPS
    sub = lax.broadcasted_iota(jnp.int32, (per_group, tm), 0)
    groups = [choice[g * per_group:(g + 1) * per_group, :] for g in range(N_GROUPS)]

    gscore = []
    for cg in groups:
        m1 = jnp.max(cg, axis=0, keepdims=True)
        i1 = jnp.min(jnp.where(cg == m1, sub, per_group), axis=0, keepdims=True)
        m2 = jnp.max(jnp.where(sub == i1, NEG_INF, cg), axis=0, keepdims=True)
        gscore.append(m1 + m2)

    gsel = [jnp.zeros((1, tm), jnp.bool_) for _ in range(N_GROUPS)]
    for _ in range(TOPK_GROUPS):
        best = functools.reduce(jnp.maximum, gscore)
        takes, _ = _first_hit([gs == best for gs in gscore], jnp.zeros((1, tm), jnp.bool_))
        gsel = [jnp.logical_or(a, t) for a, t in zip(gsel, takes)]
        gscore = [jnp.where(t, NEG_INF, gs) for gs, t in zip(gscore, takes)]

    masked = [jnp.where(gs, cg, NEG_INF) for gs, cg in zip(gsel, groups)]
    eid = [sub + g * per_group for g in range(N_GROUPS)]
    affs = [aff[g * per_group:(g + 1) * per_group, :] for g in range(N_GROUPS)]
    sel = [jnp.zeros((per_group, tm), jnp.bool_) for _ in range(N_GROUPS)]
    picked_e, picked_w = [], []
    for _ in range(TOP_K):
        best = jnp.max(functools.reduce(jnp.maximum, masked), axis=0, keepdims=True)
        cand = [jnp.where(mg == best, ig, N_EXPERTS) for mg, ig in zip(masked, eid)]
        idx = jnp.min(functools.reduce(jnp.minimum, cand), axis=0, keepdims=True)
        onehot = [ig == idx for ig in eid]
        w = functools.reduce(
            jnp.add, [jnp.sum(jnp.where(oh, ag, 0.0), axis=0, keepdims=True)
                      for oh, ag in zip(onehot, affs)])
        picked_e.append(idx)
        picked_w.append(w)
        sel = [jnp.logical_or(sg, oh) for sg, oh in zip(sel, onehot)]
        masked = [jnp.where(oh, NEG_INF, mg) for mg, oh in zip(masked, onehot)]

    wsum = functools.reduce(jnp.add, picked_w)

    sel_f = jnp.concatenate([sg.astype(F32) for sg in sel], axis=0)
    r = lax.broadcasted_iota(jnp.int32, (tm, tm), 0)
    c = lax.broadcasted_iota(jnp.int32, (tm, tm), 1)
    before = _dot(sel_f.astype(BF16), (r < c).astype(BF16)) + cnt[...]
    eall = lax.broadcasted_iota(jnp.int32, (N_EXPERTS, tm), 0)
    for kk in range(TOP_K):
        rank = jnp.sum(jnp.where(eall == picked_e[kk], before, 0.0), axis=0, keepdims=True)
        e_ref[kk:kk + 1, :] = picked_e[kk]
        gate_t[kk:kk + 1, :] = picked_w[kk] / wsum * ROUTED_SCALE
        rank_ref[kk:kk + 1, :] = rank.astype(jnp.int32)
    pad = SUBLANES - TOP_K
    e_ref[TOP_K:, :] = jnp.zeros((pad, tm), jnp.int32)
    rank_ref[TOP_K:, :] = jnp.zeros((pad, tm), jnp.int32)
    gate_ref[...] = gate_t[...].T
    cnt[...] = cnt[...] + jnp.sum(sel_f, axis=1, keepdims=True)
    cnt_ref[...] = jnp.broadcast_to(cnt[...], cnt_ref.shape)


def _mix_out_router(a, b, x, w_out, b_out, ln_g, ln_b, router_w, router_bias):
    n_tok = x.shape[0]
    tm = TOKEN_TILE
    half = w_out.shape[0] // 2
    row = lambda w: pl.BlockSpec((tm, w), lambda i: (i, 0))
    b_spec = (pl.BlockSpec((tm, half), lambda i: (i, 1)) if b is a
              else pl.BlockSpec((tm, half), lambda i: (i, 0)))
    lane_row = pl.BlockSpec((SUBLANES, tm), lambda i: (0, i))
    packed = pl.BlockSpec((tm // SUBLANES, PACK_BLOCKS, SUBLANES, LANES),
                          lambda i: (i, 0, 0, 0))
    w_bf = w_out.astype(BF16)
    rw_t = router_w.T
    rw_hi = rw_t.astype(BF16)
    rw_lo = (rw_t - rw_hi.astype(F32)).astype(BF16)
    return pl.pallas_call(
        _mix_out_router_kernel,
        grid=(n_tok // tm,),
        in_specs=[row(half), b_spec, row(D_MODEL),
                  pl.BlockSpec((half, D_MODEL), lambda i: (0, 0)),
                  pl.BlockSpec((half, D_MODEL), lambda i: (1, 0)),
                  _resident((1, D_MODEL)), _resident((1, D_MODEL)), _resident((1, D_MODEL)),
                  _resident((N_EXPERTS, D_MODEL)), _resident((N_EXPERTS, D_MODEL)),
                  _resident((N_EXPERTS, 1))],
        out_specs=[row(D_MODEL), packed, lane_row, row(LANES), lane_row,
                   _resident((N_EXPERTS, LANES))],
        out_shape=[jax.ShapeDtypeStruct((n_tok, D_MODEL), F32),
                   jax.ShapeDtypeStruct((n_tok // SUBLANES, PACK_BLOCKS, SUBLANES, LANES),
                                        jnp.int32),
                   jax.ShapeDtypeStruct((SUBLANES, n_tok), jnp.int32),
                   jax.ShapeDtypeStruct((n_tok, LANES), F32),
                   jax.ShapeDtypeStruct((SUBLANES, n_tok), jnp.int32),
                   jax.ShapeDtypeStruct((N_EXPERTS, LANES), F32)],
        scratch_shapes=[pltpu.VMEM((N_EXPERTS, 1), F32), pltpu.VMEM((LANES, tm), F32)],
        compiler_params=_params("arbitrary"),
        name="mix_out_router",
    )(a, b, x, w_bf, w_bf, b_out[None, :], ln_g[None, :], ln_b[None, :],
      rw_hi, rw_lo, router_bias[:, None])


def _expert_ffn_kernel(tile_e_ref, tile_on_ref, xs_ref, *refs):
    w_refs = refs[:3 * FFN_SLOTS]
    ys_ref = refs[3 * FFN_SLOTS]
    wg_bf, wu_bf, wd_bf = refs[3 * FFN_SLOTS + 1:]
    i = pl.program_id(0)
    sub = GROUP_TILE // SUBLANES

    for slot in range(FFN_SLOTS):
        t = FFN_SLOTS * i + slot
        fresh = jnp.logical_or(
            i == 0, tile_e_ref[t] != tile_e_ref[jnp.maximum(t - FFN_SLOTS, 0)])

        @pl.when(fresh)
        def _(slot=slot):
            wg_ref, wu_ref, wd_ref = w_refs[3 * slot:3 * slot + 3]
            wg_bf[slot] = wg_ref[...].astype(BF16)
            wu_bf[slot] = wu_ref[...].astype(BF16)
            wd_bf[slot] = wd_ref[...].astype(BF16)

    on = tile_on_ref[FFN_SLOTS * i] > 0

    @pl.when(on)
    def _():
        for slot in range(FFN_SLOTS):
            rows = pl.ds(slot * sub, sub)
            lo, hi = _unpack_rows(_load_packed(xs_ref.at[rows]))
            lo = lo.astype(BF16)
            hi = hi.astype(BF16)
            gate = (_dot(lo, wg_bf[slot, :PACK_COLS, :]) + _dot(hi, wg_bf[slot, PACK_COLS:, :]))
            up = (_dot(lo, wu_bf[slot, :PACK_COLS, :]) + _dot(hi, wu_bf[slot, PACK_COLS:, :]))
            hid = (jax.nn.silu(gate) * up).astype(BF16)
            _store_packed(ys_ref.at[rows], _pack_rows(_dot(hid, wd_bf[slot])))

    @pl.when(jnp.logical_not(on))
    def _():
        ys_ref[...] = jnp.zeros_like(ys_ref)


def _expert_ffn(xs, tile_e, tile_on, w_gate, w_up, w_down):
    rows_per_step = GROUP_TILE * FFN_SLOTS
    packed = pl.BlockSpec((rows_per_step // SUBLANES, PACK_BLOCKS, SUBLANES, LANES),
                          lambda i, te, on: (i, 0, 0, 0))
    w_specs = []
    for slot in range(FFN_SLOTS):
        pick = lambda i, te, on, slot=slot: (te[FFN_SLOTS * i + slot], 0, 0)
        w_specs += [pl.BlockSpec((None, D_MODEL, EXPERT_HID), pick),
                    pl.BlockSpec((None, D_MODEL, EXPERT_HID), pick),
                    pl.BlockSpec((None, EXPERT_HID, D_MODEL), pick)]
    grid_spec = pltpu.PrefetchScalarGridSpec(
        num_scalar_prefetch=2,
        grid=(xs.shape[0] * SUBLANES // rows_per_step,),
        in_specs=[packed] + w_specs,
        out_specs=packed,
        scratch_shapes=[pltpu.VMEM((FFN_SLOTS, D_MODEL, EXPERT_HID), BF16),
                        pltpu.VMEM((FFN_SLOTS, D_MODEL, EXPERT_HID), BF16),
                        pltpu.VMEM((FFN_SLOTS, EXPERT_HID, D_MODEL), BF16)],
    )
    return pl.pallas_call(
        _expert_ffn_kernel,
        grid_spec=grid_spec,
        out_shape=jax.ShapeDtypeStruct(xs.shape, jnp.int32),
        compiler_params=_params("arbitrary"),
        name="expert_ffn",
    )(tile_e, tile_on, xs, *([w_gate, w_up, w_down] * FFN_SLOTS))


def _moe_out_kernel(x1_ref, yg_ref, gate_ref, sg_ref, su_ref, sd_ref, g_ref, b_ref, *rest):
    o_ref = rest[-1]
    x1 = x1_ref[...]
    xb = x1.astype(BF16)
    hid = jax.nn.silu(_dot(xb, sg_ref[...])) * _dot(xb, su_ref[...])
    shared = _dot(hid.astype(BF16), sd_ref[...])
    gates = gate_ref[...]
    lo_acc = shared[:, :PACK_COLS]
    hi_acc = shared[:, PACK_COLS:]
    for kk in range(TOP_K):
        lo, hi = _unpack_rows(_load_packed(yg_ref.at[kk]))
        w = gates[:, kk:kk + 1]
        lo_acc = lo_acc + w * lo
        hi_acc = hi_acc + w * hi
    moe = jnp.concatenate([lo_acc, hi_acc], axis=1)
    o_ref[...] = _layer_norm(DN_ALPHA * x1 + moe, g_ref[...], b_ref[...])


def _moe_out(x1, yg, gates, sh_gate, sh_up, sh_down, ln_g, ln_b, part, earlier):
    n_tok = x1.shape[0]
    tm = TOKEN_TILE
    steps = n_tok // tm // MOE_PARTS
    first = part * steps
    row = lambda w: pl.BlockSpec((tm, w), lambda i: (i + first, 0))
    in_specs = [row(D_MODEL),
                pl.BlockSpec((TOP_K, tm // SUBLANES, PACK_BLOCKS, SUBLANES, LANES),
                             lambda i: (0, i, 0, 0, 0)),
                row(LANES),
                _resident((D_MODEL, EXPERT_HID)), _resident((D_MODEL, EXPERT_HID)),
                _resident((EXPERT_HID, D_MODEL)), _resident((1, D_MODEL)),
                _resident((1, D_MODEL))]
    args = [x1, yg, gates, sh_gate.astype(BF16), sh_up.astype(BF16), sh_down.astype(BF16),
            ln_g[None, :], ln_b[None, :]]
    aliases = {}
    if earlier is not None:
        in_specs.append(pl.BlockSpec(memory_space=pl.ANY))
        args.append(earlier)
        aliases = {len(args) - 1: 0}
    return pl.pallas_call(
        _moe_out_kernel,
        grid=(steps,),
        in_specs=in_specs,
        out_specs=row(D_MODEL),
        out_shape=jax.ShapeDtypeStruct((n_tok, D_MODEL), F32),
        input_output_aliases=aliases,
        compiler_params=_params("arbitrary"),
        name="moe_out",
    )(*args)


def _slot_kernel(offset_ref, e_ref, rank_ref, pos_ref):
    e = e_ref[...]
    base = jnp.zeros(e.shape, jnp.int32)
    for ex in range(N_EXPERTS):
        base = jnp.where(e == ex, offset_ref[ex], base)
    pos_ref[...] = base + rank_ref[...]


def _slots(offset, e6, rank6):
    n_tok = e6.shape[1]
    tl = 4096
    lane_row = pl.BlockSpec((SUBLANES, tl), lambda i, off: (0, i))
    return pl.pallas_call(
        _slot_kernel,
        grid_spec=pltpu.PrefetchScalarGridSpec(
            num_scalar_prefetch=1, grid=(n_tok // tl,),
            in_specs=[lane_row, lane_row], out_specs=lane_row),
        out_shape=jax.ShapeDtypeStruct((SUBLANES, n_tok), jnp.int32),
        compiler_params=_params("arbitrary"),
        name="slots",
    )(offset, e6, rank6)


def _sc_mesh():
    return plsc.VectorSubcoreMesh(core_axis_name="c", subcore_axis_name="s")


def _sc_worker():
    return lax.axis_index("s") * SC_CORES + lax.axis_index("c")


def _sc_dispatch(x_sub, idx, n_out_sub):
    windows = x_sub.shape[0] // (SC_WORKERS * SC_WINDOW)

    @functools.partial(
        pl.kernel, mesh=_sc_mesh(),
        out_type=jax.ShapeDtypeStruct((n_out_sub, LANES), jnp.int32),
        scratch_types=[pltpu.VMEM((SC_WINDOW, LANES), jnp.int32),
                       pltpu.VMEM((TOP_K, SC_WINDOW), jnp.int32)],
        name="sc_dispatch")
    def run(x_hbm, idx_hbm, out_hbm, rows_v, idx_v):
        wid = _sc_worker()

        @pl.loop(0, windows)
        def _(c):
            base = (wid * windows + c) * SC_WINDOW
            pltpu.sync_copy(x_hbm.at[pl.ds(base, SC_WINDOW)], rows_v)
            pltpu.sync_copy(idx_hbm.at[wid, c], idx_v)
            for kk in range(TOP_K):
                pltpu.sync_copy(rows_v, out_hbm.at[idx_v.at[kk]])

    return run(x_sub, idx)


def _sc_combine(y_sub, idx, n_tok_sub):
    windows = n_tok_sub // (SC_WORKERS * SC_WINDOW)

    @functools.partial(
        pl.kernel, mesh=_sc_mesh(),
        out_type=jax.ShapeDtypeStruct((TOP_K, n_tok_sub, LANES), jnp.int32),
        scratch_types=[pltpu.VMEM((2, SC_WINDOW, LANES), jnp.int32),
                       pltpu.VMEM((TOP_K, SC_WINDOW), jnp.int32),
                       pltpu.SemaphoreType.DMA((2,)),
                       pltpu.SemaphoreType.DMA((2,))],
        name="sc_combine")
    def run(y_hbm, idx_hbm, out_hbm, rows_v, idx_v, gather_sem, write_sem):
        wid = _sc_worker()

        def gather(kk):
            buf = kk % 2
            return pltpu.async_copy(y_hbm.at[idx_v.at[kk]], rows_v.at[buf], gather_sem.at[buf])

        @pl.loop(0, windows)
        def _(c):
            base = (wid * windows + c) * SC_WINDOW
            pltpu.sync_copy(idx_hbm.at[wid, c], idx_v)
            writes = []
            pending = gather(0)
            for kk in range(TOP_K):
                pending.wait()
                if kk + 1 < TOP_K:
                    if kk >= 1:
                        writes[kk - 1].wait()
                    pending = gather(kk + 1)
                writes.append(pltpu.async_copy(
                    rows_v.at[kk % 2], out_hbm.at[kk, pl.ds(base, SC_WINDOW)],
                    write_sem.at[kk % 2]))
            writes[TOP_K - 2].wait()
            writes[TOP_K - 1].wait()

    return run(y_sub, idx)


def _moe(x1, x1p, e6, gates, rank6, counts, exp_w_gate, exp_w_up, exp_w_down,
         sh_w_gate, sh_w_up, sh_w_down, ln_g, ln_b):
    n_tok = x1.shape[0]
    tg = GROUP_TILE
    n_rows = n_tok * TOP_K + N_EXPERTS * tg
    n_tiles = n_rows // tg

    cnt = counts[:, 0].astype(jnp.int32)
    tiles_per_e = (cnt + tg - 1) // tg
    tile_end = jnp.cumsum(tiles_per_e)
    offset = (tile_end - tiles_per_e) * tg
    tile_ids = jnp.arange(n_tiles, dtype=jnp.int32)
    tile_on = (tile_ids < tile_end[-1]).astype(jnp.int32)
    tile_e = jnp.minimum(
        jnp.sum((tile_ids[:, None] >= tile_end[None, :]).astype(jnp.int32), axis=1),
        N_EXPERTS - 1)
    tile_e = jnp.where(tile_on > 0, tile_e, tile_e[jnp.maximum(tile_end[-1] - 1, 0)])

    pos = _slots(offset, e6, rank6)[:TOP_K]

    group = PACK_BLOCKS * SUBLANES
    sub0 = (pos // SUBLANES) * group + pos % SUBLANES
    lane = jnp.arange(SC_WINDOW)
    spread = (jnp.arange(group)[:, None]
              == (lane // group) * SUBLANES + lane % SUBLANES).astype(F32)
    idx = jnp.einsum("kws,sd->kwd", sub0.reshape(TOP_K, n_tok // group, group).astype(F32),
                     spread, precision=lax.Precision.HIGHEST).astype(jnp.int32)
    idx = idx + ((lane % group) // SUBLANES * SUBLANES).astype(jnp.int32)
    n_sub = n_tok * PACK_BLOCKS
    idx = idx.reshape(TOP_K, n_sub)

    def windowed(ix):
        windows = ix.shape[1] // (SC_WORKERS * SC_WINDOW)
        return ix.reshape(TOP_K, SC_WORKERS, windows, SC_WINDOW).transpose(1, 2, 0, 3)

    xs = _sc_dispatch(x1p.reshape(n_sub, LANES), windowed(idx), n_rows * PACK_BLOCKS)
    ys = _expert_ffn(xs.reshape(n_rows // SUBLANES, PACK_BLOCKS, SUBLANES, LANES),
                     tile_e, tile_on, exp_w_gate, exp_w_up, exp_w_down)
    ys = ys.reshape(n_rows * PACK_BLOCKS, LANES)
    part_sub = n_sub // MOE_PARTS
    out = None
    for part in range(MOE_PARTS):
        yg = _sc_combine(ys, windowed(idx[:, part * part_sub:(part + 1) * part_sub]), part_sub)
        yg = yg.reshape(TOP_K, part_sub // (PACK_BLOCKS * SUBLANES), PACK_BLOCKS,
                        SUBLANES, LANES)
        out = _moe_out(x1, yg, gates, sh_w_gate, sh_w_up, sh_w_down, ln_g, ln_b, part, out)
    return out


def kernel(x, positions,
           l0_w_in, l0_b_in, l0_conv_w, l0_conv_b, l0_conv_ln_g, l0_conv_ln_b,
           l0_q_norm_g, l0_w_uq, l0_kv_norm_g, l0_w_ukv, l0_w_out, l0_b_out,
           l0_ln1_g, l0_ln1_b,
           l0_router_w, l0_router_bias, l0_exp_w_gate, l0_exp_w_up, l0_exp_w_down,
           l0_sh_w_gate, l0_sh_w_up, l0_sh_w_down, l0_ln2_g, l0_ln2_b,
           l1_w_in, l1_b_in, l1_w_out, l1_b_out, l1_ln1_g, l1_ln1_b,
           l1_router_w, l1_router_bias, l1_exp_w_gate, l1_exp_w_up, l1_exp_w_down,
           l1_sh_w_gate, l1_sh_w_up, l1_sh_w_down, l1_ln2_g, l1_ln2_b):
    bsz, seq, d = x.shape
    n_tok = bsz * seq

    u, q, k, v = _l0_front(x, positions, l0_w_in, l0_b_in, l0_conv_w, l0_conv_b,
                           l0_conv_ln_g, l0_conv_ln_b, l0_q_norm_g, l0_w_uq,
                           l0_kv_norm_g, l0_w_ukv)
    o = _attention(q, k, v, None, packed=False)
    x_flat = x.reshape(n_tok, d)
    routing = _mix_out_router(u.reshape(n_tok, -1), o.reshape(n_tok, -1), x_flat,
                              l0_w_out, l0_b_out, l0_ln1_g, l0_ln1_b,
                              l0_router_w, l0_router_bias)
    x2 = _moe(*routing, l0_exp_w_gate, l0_exp_w_up, l0_exp_w_down,
              l0_sh_w_gate, l0_sh_w_up, l0_sh_w_down, l0_ln2_g, l0_ln2_b)

    q, k, v, cneg = _l1_front(x2.reshape(bsz, seq, d), l1_w_in, l1_b_in)
    o = _attention(q, k, v, cneg[:, :, None, :], packed=True).reshape(n_tok, -1)
    routing = _mix_out_router(o, o, x2, l1_w_out, l1_b_out, l1_ln1_g, l1_ln1_b,
                              l1_router_w, l1_router_bias)
    x3 = _moe(*routing, l1_exp_w_gate, l1_exp_w_up, l1_exp_w_down,
              l1_sh_w_gate, l1_sh_w_up, l1_sh_w_down, l1_ln2_g, l1_ln2_b)
    return x3.reshape(bsz, seq, d)
```

```python
import functools

import jax
import jax.numpy as jnp
from jax import lax
from jax.experimental import pallas as pl
from jax.experimental.pallas import tpu as pltpu
from jax.experimental.pallas import tpu_sc as plsc

D_MODEL = 1024
DEPTH = 2
DN_ALPHA = (2 * DEPTH) ** 0.25
LN_EPS = 1e-5
RMS_EPS = 1e-6

CONV_CH = 512
CONV_WIDTH = 31
MLA_HEADS = 8
QK_NOPE = 64
QK_ROPE = 32
V_DIM = 64
Q_LORA = 256
KV_LORA = 128
ROPE_THETA = 10000.0
FOX_HEADS = 16
FOX_HD = 64
N_EXPERTS = 64
TOP_K = 6
N_GROUPS = 8
TOPK_GROUPS = 4
EXPERT_HID = 256
ROUTED_SCALE = 2.5

LANES = 128
SUBLANES = 8
VMEM_LIMIT_BYTES = 56 * 1024 * 1024

SEQ_TILE = 512
TOKEN_TILE = 512
ATTN_TILE = 512
ATTN_KEY_TILE = 512
ATTN_UNROLL = 4
GROUP_TILE = 512
FFN_SLOTS = 2
CONV_ROWS = 32
CONV_HALO = 32
PAD_HEAD = 128

PACK_COLS = D_MODEL // 2
PACK_BLOCKS = PACK_COLS // LANES
SC_CORES = 2
SC_SUBCORES = 16
SC_WORKERS = SC_CORES * SC_SUBCORES
SC_WINDOW = 128
MOE_PARTS = 4
BF16 = jnp.bfloat16
F32 = jnp.float32
NEG_INF = float("-inf")
LOG2_E = 1.4426950408889634


def _params(*semantics):
    return pltpu.CompilerParams(dimension_semantics=semantics,
                                vmem_limit_bytes=VMEM_LIMIT_BYTES)


def _resident(shape):
    nd = len(shape)
    return pl.BlockSpec(shape, lambda *_: (0,) * nd)


def _dot(a, b):
    return jnp.dot(a, b, preferred_element_type=F32)


def _dot_nt(a, b, precision=None):
    return lax.dot_general(a, b, (((1,), (1,)), ((), ())),
                           precision=precision, preferred_element_type=F32)


def _layer_norm(x, g, b):
    mu = jnp.mean(x, axis=-1, keepdims=True)
    xc = x - mu
    var = jnp.mean(xc * xc, axis=-1, keepdims=True)
    return xc * lax.rsqrt(var + LN_EPS) * g + b


def _rms_norm(x, g):
    return x * lax.rsqrt(jnp.mean(x * x, axis=-1, keepdims=True) + RMS_EPS) * g


def _ones_upper_half(width):
    lane = lax.broadcasted_iota(jnp.int32, (1, width), 1)
    return jnp.where((lane & (PAD_HEAD - 1)) >= V_DIM, 1.0, 0.0).astype(F32)


def _pad_heads(w, heads):
    d = w.shape[-1] // heads
    w3 = w.reshape(w.shape[:-1] + (heads, d))
    pad = jnp.zeros(w.shape[:-1] + (heads, PAD_HEAD - d), w.dtype)
    return jnp.concatenate([w3, pad], axis=-1).reshape(w.shape[:-1] + (heads * PAD_HEAD,))


def _pack_rows(x):
    lo = lax.bitcast_convert_type(x[:, :PACK_COLS].astype(BF16).astype(F32), jnp.uint32)
    hi = lax.bitcast_convert_type(x[:, PACK_COLS:].astype(BF16).astype(F32), jnp.uint32)
    word = (lo >> 16) | (hi & jnp.uint32(0xFFFF0000))
    return lax.bitcast_convert_type(word, jnp.int32)


def _unpack_rows(words):
    u = lax.bitcast_convert_type(words, jnp.uint32)
    lo = lax.bitcast_convert_type(u << 16, F32)
    hi = lax.bitcast_convert_type(u & jnp.uint32(0xFFFF0000), F32)
    return lo, hi


def _store_packed(ref, words):
    groups = words.shape[0] // SUBLANES
    for cb in range(PACK_BLOCKS):
        ref[:, cb, :, :] = words[:, cb * LANES:(cb + 1) * LANES].reshape(groups, SUBLANES, LANES)


def _load_packed(ref):
    rows = ref.shape[0] * SUBLANES
    return jnp.concatenate(
        [ref[:, cb, :, :].reshape(rows, LANES) for cb in range(PACK_BLOCKS)], axis=1)


L0_A = 0
L0_G = CONV_CH
L0_Q = 2 * CONV_CH
L0_KV = L0_Q + Q_LORA
L0_KR = L0_KV + KV_LORA
L0_KRR = L0_KR + PAD_HEAD
L0_COLS = L0_KRR + PAD_HEAD


def _l0_front_kernel(x_ref, pos_ref, w_in_ref, b_in_ref, conv_w_ref, conv_b_ref,
                     cln_g_ref, cln_b_ref, qn_g_ref, wq_ref, wqr_ref, kvn_g_ref,
                     wk_ref, wv_ref, invf_ref,
                     u_ref, q_ref, k_ref, v_ref, ubuf):
    ts = x_ref.shape[0]

    @pl.when(pl.program_id(1) == 0)
    def _():
        ubuf[0:CONV_HALO, :] = jnp.zeros((CONV_HALO, CONV_CH), F32)

    h = _dot(x_ref[...].astype(BF16), w_in_ref[...]) + b_in_ref[...]

    ubuf[CONV_HALO:CONV_HALO + ts, :] = (
        h[:, L0_A:L0_A + CONV_CH] * jax.nn.sigmoid(h[:, L0_G:L0_G + CONV_CH]))
    first_tap = CONV_HALO - (CONV_WIDTH - 1)

    def conv_chunk(c, carry):
        base = pl.multiple_of(c * CONV_ROWS, CONV_ROWS)
        acc = jnp.broadcast_to(conv_b_ref[...], (CONV_ROWS, CONV_CH))
        for res in range(SUBLANES):
            rows = CONV_ROWS + (SUBLANES if res else 0)
            part = None
            for off in range(res, first_tap + CONV_WIDTH, SUBLANES):
                j = off - first_tap
                if j < 0:
                    continue
                term = ubuf[pl.ds(base + (off - res), rows), :] * conv_w_ref[j:j + 1, :]
                part = term if part is None else part + term
            acc = acc + part[res:res + CONV_ROWS, :]
        y = _layer_norm(acc, cln_g_ref[...], cln_b_ref[...])
        u_ref[pl.ds(base, CONV_ROWS), :] = (y * jax.nn.sigmoid(y)).astype(BF16)
        return carry

    lax.fori_loop(0, ts // CONV_ROWS, conv_chunk, 0)
    ubuf[0:CONV_HALO, :] = ubuf[ts:ts + CONV_HALO, :]

    ang = pos_ref[...].astype(F32) * invf_ref[...]
    cos = jnp.cos(ang)
    sin = jnp.sin(ang)
    scale = (QK_NOPE + QK_ROPE) ** -0.5 * LOG2_E

    qn = _rms_norm(h[:, L0_Q:L0_Q + Q_LORA], qn_g_ref[...]).astype(BF16)
    q = _dot(qn, wq_ref[...])
    q_rot = _dot(qn, wqr_ref[...])
    cos_s = cos * scale
    sin_s = sin * scale
    for hh in range(MLA_HEADS):
        blk = slice(hh * PAD_HEAD, (hh + 1) * PAD_HEAD)
        q_ref[:, blk] = (q[:, blk] * cos_s + q_rot[:, blk] * sin_s).astype(BF16)

    kvn = _rms_norm(h[:, L0_KV:L0_KV + KV_LORA], kvn_g_ref[...]).astype(BF16)
    k_nope = _dot(kvn, wk_ref[...])
    k_pe = h[:, L0_KR:L0_KR + PAD_HEAD] * cos + h[:, L0_KRR:L0_KRR + PAD_HEAD] * sin
    for hh in range(MLA_HEADS):
        blk = slice(hh * PAD_HEAD, (hh + 1) * PAD_HEAD)
        k_ref[:, blk] = (k_nope[:, blk] + k_pe).astype(BF16)
    v_ref[...] = (_dot(kvn, wv_ref[...]) + _ones_upper_half(v_ref.shape[1])).astype(BF16)


def _rope_rotate_cols(w):
    half = QK_ROPE // 2
    return jnp.concatenate([-w[..., half:], w[..., :half]], axis=-1)


def _l0_front(x, positions, w_in, b_in, conv_w, conv_b, cln_g, cln_b,
              qn_g, w_uq, kvn_g, w_ukv):
    bsz, seq, _ = x.shape
    pad_lo = jnp.zeros((D_MODEL, QK_NOPE), F32)
    pad_hi = jnp.zeros((D_MODEL, PAD_HEAD - QK_NOPE - QK_ROPE), F32)
    i3 = L0_KR
    w_kr = w_in[:, i3:i3 + QK_ROPE]
    w_in_p = jnp.concatenate(
        [w_in[:, :i3], pad_lo, w_kr, pad_hi, pad_lo, _rope_rotate_cols(w_kr), pad_hi],
        axis=1).astype(BF16)
    b_kr = b_in[i3:i3 + QK_ROPE]
    zlo = jnp.zeros((QK_NOPE,), F32)
    zhi = jnp.zeros((PAD_HEAD - QK_NOPE - QK_ROPE,), F32)
    b_in_p = jnp.concatenate(
        [b_in[:i3], zlo, b_kr, zhi, zlo, _rope_rotate_cols(b_kr), zhi])[None, :]

    dq = QK_NOPE + QK_ROPE
    wq3 = w_uq.reshape(Q_LORA, MLA_HEADS, dq)
    zq = jnp.zeros((Q_LORA, MLA_HEADS, PAD_HEAD - dq), F32)
    wq_p = jnp.concatenate([wq3, zq], axis=-1).reshape(Q_LORA, MLA_HEADS * PAD_HEAD)
    wqr_p = jnp.concatenate(
        [jnp.zeros((Q_LORA, MLA_HEADS, QK_NOPE), F32),
         _rope_rotate_cols(wq3[..., QK_NOPE:]), zq], axis=-1
    ).reshape(Q_LORA, MLA_HEADS * PAD_HEAD)
    wkv3 = w_ukv.reshape(KV_LORA, MLA_HEADS, QK_NOPE + V_DIM)
    wk_p = jnp.concatenate(
        [wkv3[..., :QK_NOPE], jnp.zeros((KV_LORA, MLA_HEADS, PAD_HEAD - QK_NOPE), F32)],
        axis=-1).reshape(KV_LORA, MLA_HEADS * PAD_HEAD)
    wv = _pad_heads(wkv3[..., QK_NOPE:].reshape(KV_LORA, MLA_HEADS * V_DIM), MLA_HEADS)

    inv_freq = 1.0 / (ROPE_THETA ** (jnp.arange(0, QK_ROPE, 2, dtype=F32) / QK_ROPE))
    invf = jnp.concatenate([jnp.zeros((QK_NOPE,), F32), inv_freq, inv_freq,
                            jnp.zeros((PAD_HEAD - QK_NOPE - QK_ROPE,), F32)])[None, :]

    ts = SEQ_TILE
    row = lambda w: pl.BlockSpec((None, ts, w), lambda b, s: (b, s, 0))
    qk_w = MLA_HEADS * PAD_HEAD
    v_w = MLA_HEADS * PAD_HEAD
    return pl.pallas_call(
        _l0_front_kernel,
        grid=(bsz, seq // ts),
        in_specs=[row(D_MODEL), row(1),
                  _resident((D_MODEL, L0_COLS)), _resident((1, L0_COLS)),
                  _resident((CONV_WIDTH, CONV_CH)), _resident((1, CONV_CH)),
                  _resident((1, CONV_CH)), _resident((1, CONV_CH)),
                  _resident((1, Q_LORA)), _resident((Q_LORA, qk_w)),
                  _resident((Q_LORA, qk_w)), _resident((1, KV_LORA)),
                  _resident((KV_LORA, qk_w)), _resident((KV_LORA, v_w)),
                  _resident((1, PAD_HEAD))],
        out_specs=[row(CONV_CH), row(qk_w), row(qk_w), row(v_w)],
        out_shape=[jax.ShapeDtypeStruct((bsz, seq, CONV_CH), BF16),
                   jax.ShapeDtypeStruct((bsz, seq, qk_w), BF16),
                   jax.ShapeDtypeStruct((bsz, seq, qk_w), BF16),
                   jax.ShapeDtypeStruct((bsz, seq, v_w), BF16)],
        scratch_shapes=[pltpu.VMEM((ts + CONV_HALO, CONV_CH), F32)],
        compiler_params=_params("arbitrary", "arbitrary"),
        name="l0_front",
    )(x, positions[..., None], w_in_p, b_in_p, conv_w, conv_b[None, :],
      cln_g[None, :], cln_b[None, :], qn_g[None, :], wq_p.astype(BF16),
      wqr_p.astype(BF16), kvn_g[None, :], wk_p.astype(BF16), wv.astype(BF16), invf)


def _l1_front_kernel(x_ref, w_ref, b_ref, wv_ref, bv_ref, wf_ref, bf_ref,
                     q_ref, k_ref, v_ref, cneg_ref, carry):
    ts = x_ref.shape[0]

    @pl.when(pl.program_id(1) == 0)
    def _():
        carry[...] = jnp.zeros_like(carry)

    xb = x_ref[...].astype(BF16)
    mix = FOX_HEADS * FOX_HD
    scale = FOX_HD ** -0.5 * LOG2_E
    q_ref[...] = ((_dot(xb, w_ref[:, 0:mix]) + b_ref[:, 0:mix]) * scale).astype(BF16)
    k_ref[...] = (_dot(xb, w_ref[:, mix:2 * mix]) + b_ref[:, mix:2 * mix]).astype(BF16)
    v_ref[...] = (_dot(xb, wv_ref[...]) + bv_ref[...]
                  + _ones_upper_half(v_ref.shape[1])).astype(BF16)

    log_f = jax.nn.log_sigmoid(_dot_nt(wf_ref[...], xb) + bf_ref[...])
    r = lax.broadcasted_iota(jnp.int32, (ts, ts), 0)
    c = lax.broadcasted_iota(jnp.int32, (ts, ts), 1)
    upper = (r <= c).astype(F32)
    csum = jnp.dot(log_f, upper, precision=lax.Precision.HIGHEST,
                   preferred_element_type=F32) + carry[...]
    cneg_ref[...] = csum * -LOG2_E
    carry[...] = carry[...] + jnp.sum(log_f, axis=1, keepdims=True)


def _l1_front(x, w_in, b_in):
    bsz, seq, _ = x.shape
    mix = FOX_HEADS * FOX_HD
    ts = SEQ_TILE
    row = lambda w: pl.BlockSpec((None, ts, w), lambda b, s: (b, s, 0))
    w_qk = w_in[:, :2 * mix].astype(BF16)
    b_qk = b_in[None, :2 * mix]
    w_v = _pad_heads(w_in[:, 2 * mix:3 * mix], FOX_HEADS).astype(BF16)
    b_v = _pad_heads(b_in[None, 2 * mix:3 * mix], FOX_HEADS)
    v_w = FOX_HEADS * PAD_HEAD
    wf_t = w_in[:, 3 * mix:].T.astype(BF16)
    bf_t = b_in[3 * mix:][:, None]
    return pl.pallas_call(
        _l1_front_kernel,
        grid=(bsz, seq // ts),
        in_specs=[row(D_MODEL), _resident((D_MODEL, 2 * mix)), _resident((1, 2 * mix)),
                  _resident((D_MODEL, v_w)), _resident((1, v_w)),
                  _resident((FOX_HEADS, D_MODEL)), _resident((FOX_HEADS, 1))],
        out_specs=[row(mix), row(mix), row(v_w),
                   pl.BlockSpec((None, FOX_HEADS, ts), lambda b, s: (b, 0, s))],
        out_shape=[jax.ShapeDtypeStruct((bsz, seq, mix), BF16)] * 2
        + [jax.ShapeDtypeStruct((bsz, seq, v_w), BF16),
           jax.ShapeDtypeStruct((bsz, FOX_HEADS, seq), F32)],
        scratch_shapes=[pltpu.VMEM((FOX_HEADS, 1), F32)],
        compiler_params=_params("arbitrary", "arbitrary"),
        name="l1_front",
    )(x, w_qk, b_qk, w_v, b_v, wf_t, bf_t)


def _attn_kernel(*refs, packed, has_bias):
    if has_bias:
        q_ref, k_ref, v_ref, cneg_ref, o_ref = refs
    else:
        q_ref, k_ref, v_ref, o_ref = refs
        cneg_ref = None
    tq = ATTN_TILE
    tk = ATTN_KEY_TILE
    half = LANES // 2

    def query_tile(qi):
        q_rows = pl.ds(pl.multiple_of(qi * tq, tq), tq)
        q2 = q_ref[q_rows, :]
        if packed:
            lane = lax.broadcasted_iota(jnp.int32, q2.shape, 1)
            zero = jnp.zeros_like(q2)
            qs = (jnp.where(lane < half, q2, zero), jnp.where(lane >= half, q2, zero))
        else:
            qs = (q2[:, :PAD_HEAD], q2[:, PAD_HEAD:])

        def k_block(kj, head):
            rows = pl.ds(pl.multiple_of(kj * tk, tk), tk)
            if packed:
                return k_ref[rows, :]
            return k_ref[rows, head * PAD_HEAD:(head + 1) * PAD_HEAD]

        def step(kj, carry, diag):
            rows = pl.ds(pl.multiple_of(kj * tk, tk), tk)
            out = []
            for head in range(2):
                m, acc = carry[head]
                s = _dot_nt(qs[head], k_block(kj, head))
                if has_bias:
                    s = s + cneg_ref[head, :, rows]
                if diag is not None:
                    r = lax.broadcasted_iota(jnp.int32, s.shape, 0)
                    c = lax.broadcasted_iota(jnp.int32, s.shape, 1)
                    s = jnp.where(r >= c + diag * tk, s, NEG_INF)
                m_new = jnp.maximum(m, jnp.max(s, axis=-1, keepdims=True))
                alpha = jnp.exp2(m - m_new)
                p = jnp.exp2(s - m_new).astype(BF16)
                vblk = v_ref[rows, head * PAD_HEAD:(head + 1) * PAD_HEAD]
                out.append((m_new, alpha * acc + _dot(p, vblk)))
            return tuple(out)

        init = tuple((jnp.full((tq, 1), NEG_INF, F32), jnp.zeros((tq, LANES), F32))
                     for _ in range(2))
        per_tile = tq // tk
        n_full = qi * per_tile

        def group(first, c, size):
            for t in range(size):
                c = step(first + t, c, None)
            return c

        carry = lax.fori_loop(0, n_full // ATTN_UNROLL,
                              lambda j, c: group(ATTN_UNROLL * j, c, ATTN_UNROLL), init)
        done = (n_full // ATTN_UNROLL) * ATTN_UNROLL
        size = ATTN_UNROLL // 2
        while size >= 1:
            take = (n_full - done) >= size
            carry = lax.cond(take, lambda c, f=done, n=size: group(f, c, n), lambda c: c,
                             carry)
            done = done + jnp.where(take, size, 0)
            size //= 2
        for d in range(per_tile):
            carry = step(n_full + d, carry, d)
        (_, acc_a), (_, acc_b) = carry
        out_a = acc_a / pltpu.roll(acc_a, half, axis=1)
        out_b = acc_b / pltpu.roll(acc_b, half, axis=1)
        lane_o = lax.broadcasted_iota(jnp.int32, (tq, LANES), 1)
        o_ref[q_rows, :] = jnp.where(lane_o < half, out_a,
                                     pltpu.roll(out_b, half, axis=1)).astype(o_ref.dtype)

    pl.loop(0, q_ref.shape[0] // tq)(query_tile)


def _attention(q, k, v, cneg, packed):
    bsz, seq, v_w = v.shape
    pairs = v_w // (2 * PAD_HEAD)
    qk_w = q.shape[-1] // pairs
    in_specs = [pl.BlockSpec((None, seq, qk_w), lambda b, p: (b, 0, p)),
                pl.BlockSpec((None, seq, qk_w), lambda b, p: (b, 0, p)),
                pl.BlockSpec((None, seq, 2 * PAD_HEAD), lambda b, p: (b, 0, p))]
    args = [q, k, v]
    if cneg is not None:
        in_specs.append(pl.BlockSpec((None, 2, 1, seq), lambda b, p: (b, p, 0, 0)))
        args.append(cneg)
    return pl.pallas_call(
        functools.partial(_attn_kernel, packed=packed, has_bias=cneg is not None),
        grid=(bsz, pairs),
        in_specs=in_specs,
        out_specs=pl.BlockSpec((None, seq, LANES), lambda b, p: (b, 0, p)),
        out_shape=jax.ShapeDtypeStruct((bsz, seq, pairs * LANES), BF16),
        compiler_params=_params("arbitrary", "arbitrary"),
        name="attn_packed" if packed else "attn_padded",
    )(*args)


def _first_hit(hits, found):
    out = []
    for hcur in hits:
        take = jnp.logical_and(hcur, jnp.logical_not(found))
        found = jnp.logical_or(found, take)
        out.append(take)
    return out, found


def _mix_out_router_kernel(a_ref, b_ref, x_ref, wa_ref, wb_ref, bo_ref, g_ref, be_ref,
                           rw_ref, rwl_ref, rb_ref,
                           x1_ref, x1p_ref, e_ref, gate_ref, rank_ref, cnt_ref, cnt, gate_t):
    tm = x_ref.shape[0]

    @pl.when(pl.program_id(0) == 0)
    def _():
        cnt[...] = jnp.zeros_like(cnt)
        gate_t[...] = jnp.zeros_like(gate_t)

    mix = _dot(a_ref[...], wa_ref[...]) + _dot(b_ref[...], wb_ref[...]) + bo_ref[...]
    x1 = _layer_norm(DN_ALPHA * x_ref[...] + mix, g_ref[...], be_ref[...])
    x1_ref[...] = x1
    _store_packed(x1p_ref, _pack_rows(x1))

    x_hi = x1.astype(BF16)
    x_lo = (x1 - x_hi.astype(F32)).astype(BF16)
    logits = (_dot_nt(rw_ref[...], x_hi) + _dot_nt(rw_ref[...], x_lo)
              + _dot_nt(rwl_ref[...], x_hi))
    aff = jax.nn.sigmoid(logits)
    choice = aff + rb_ref[...]
    per_group = N_EXPERTS // N_GROUPS
    sub = lax.broadcasted_iota(jnp.int32, (per_group, tm), 0)
    groups = [choice[g * per_group:(g + 1) * per_group, :] for g in range(N_GROUPS)]

    gscore = []
    for cg in groups:
        m1 = jnp.max(cg, axis=0, keepdims=True)
        i1 = jnp.min(jnp.where(cg == m1, sub, per_group), axis=0, keepdims=True)
        m2 = jnp.max(jnp.where(sub == i1, NEG_INF, cg), axis=0, keepdims=True)
        gscore.append(m1 + m2)

    gsel = [jnp.zeros((1, tm), jnp.bool_) for _ in range(N_GROUPS)]
    for _ in range(TOPK_GROUPS):
        best = functools.reduce(jnp.maximum, gscore)
        takes, _ = _first_hit([gs == best for gs in gscore], jnp.zeros((1, tm), jnp.bool_))
        gsel = [jnp.logical_or(a, t) for a, t in zip(gsel, takes)]
        gscore = [jnp.where(t, NEG_INF, gs) for gs, t in zip(gscore, takes)]

    masked = [jnp.where(gs, cg, NEG_INF) for gs, cg in zip(gsel, groups)]
    eid = [sub + g * per_group for g in range(N_GROUPS)]
    affs = [aff[g * per_group:(g + 1) * per_group, :] for g in range(N_GROUPS)]
    sel = [jnp.zeros((per_group, tm), jnp.bool_) for _ in range(N_GROUPS)]
    picked_e, picked_w = [], []
    for _ in range(TOP_K):
        best = jnp.max(functools.reduce(jnp.maximum, masked), axis=0, keepdims=True)
        cand = [jnp.where(mg == best, ig, N_EXPERTS) for mg, ig in zip(masked, eid)]
        idx = jnp.min(functools.reduce(jnp.minimum, cand), axis=0, keepdims=True)
        onehot = [ig == idx for ig in eid]
        w = functools.reduce(
            jnp.add, [jnp.sum(jnp.where(oh, ag, 0.0), axis=0, keepdims=True)
                      for oh, ag in zip(onehot, affs)])
        picked_e.append(idx)
        picked_w.append(w)
        sel = [jnp.logical_or(sg, oh) for sg, oh in zip(sel, onehot)]
        masked = [jnp.where(oh, NEG_INF, mg) for mg, oh in zip(masked, onehot)]

    wsum = functools.reduce(jnp.add, picked_w)

    sel_f = jnp.concatenate([sg.astype(F32) for sg in sel], axis=0)
    r = lax.broadcasted_iota(jnp.int32, (tm, tm), 0)
    c = lax.broadcasted_iota(jnp.int32, (tm, tm), 1)
    before = _dot(sel_f.astype(BF16), (r < c).astype(BF16)) + cnt[...]
    eall = lax.broadcasted_iota(jnp.int32, (N_EXPERTS, tm), 0)
    for kk in range(TOP_K):
        rank = jnp.sum(jnp.where(eall == picked_e[kk], before, 0.0), axis=0, keepdims=True)
        e_ref[kk:kk + 1, :] = picked_e[kk]
        gate_t[kk:kk + 1, :] = picked_w[kk] / wsum * ROUTED_SCALE
        rank_ref[kk:kk + 1, :] = rank.astype(jnp.int32)
    pad = SUBLANES - TOP_K
    e_ref[TOP_K:, :] = jnp.zeros((pad, tm), jnp.int32)
    rank_ref[TOP_K:, :] = jnp.zeros((pad, tm), jnp.int32)
    gate_ref[...] = gate_t[...].T
    cnt[...] = cnt[...] + jnp.sum(sel_f, axis=1, keepdims=True)
    cnt_ref[...] = jnp.broadcast_to(cnt[...], cnt_ref.shape)


def _mix_out_router(a, b, x, w_out, b_out, ln_g, ln_b, router_w, router_bias):
    n_tok = x.shape[0]
    tm = TOKEN_TILE
    half = w_out.shape[0] // 2
    row = lambda w: pl.BlockSpec((tm, w), lambda i: (i, 0))
    b_spec = (pl.BlockSpec((tm, half), lambda i: (i, 1)) if b is a
              else pl.BlockSpec((tm, half), lambda i: (i, 0)))
    lane_row = pl.BlockSpec((SUBLANES, tm), lambda i: (0, i))
    packed = pl.BlockSpec((tm // SUBLANES, PACK_BLOCKS, SUBLANES, LANES),
                          lambda i: (i, 0, 0, 0))
    w_bf = w_out.astype(BF16)
    rw_t = router_w.T
    rw_hi = rw_t.astype(BF16)
    rw_lo = (rw_t - rw_hi.astype(F32)).astype(BF16)
    return pl.pallas_call(
        _mix_out_router_kernel,
        grid=(n_tok // tm,),
        in_specs=[row(half), b_spec, row(D_MODEL),
                  pl.BlockSpec((half, D_MODEL), lambda i: (0, 0)),
                  pl.BlockSpec((half, D_MODEL), lambda i: (1, 0)),
                  _resident((1, D_MODEL)), _resident((1, D_MODEL)), _resident((1, D_MODEL)),
                  _resident((N_EXPERTS, D_MODEL)), _resident((N_EXPERTS, D_MODEL)),
                  _resident((N_EXPERTS, 1))],
        out_specs=[row(D_MODEL), packed, lane_row, row(LANES), lane_row,
                   _resident((N_EXPERTS, LANES))],
        out_shape=[jax.ShapeDtypeStruct((n_tok, D_MODEL), F32),
                   jax.ShapeDtypeStruct((n_tok // SUBLANES, PACK_BLOCKS, SUBLANES, LANES),
                                        jnp.int32),
                   jax.ShapeDtypeStruct((SUBLANES, n_tok), jnp.int32),
                   jax.ShapeDtypeStruct((n_tok, LANES), F32),
                   jax.ShapeDtypeStruct((SUBLANES, n_tok), jnp.int32),
                   jax.ShapeDtypeStruct((N_EXPERTS, LANES), F32)],
        scratch_shapes=[pltpu.VMEM((N_EXPERTS, 1), F32), pltpu.VMEM((LANES, tm), F32)],
        compiler_params=_params("arbitrary"),
        name="mix_out_router",
    )(a, b, x, w_bf, w_bf, b_out[None, :], ln_g[None, :], ln_b[None, :],
      rw_hi, rw_lo, router_bias[:, None])


def _expert_ffn_kernel(tile_e_ref, tile_on_ref, xs_ref, *refs):
    w_refs = refs[:3 * FFN_SLOTS]
    ys_ref = refs[3 * FFN_SLOTS]
    wg_bf, wu_bf, wd_bf = refs[3 * FFN_SLOTS + 1:]
    i = pl.program_id(0)
    sub = GROUP_TILE // SUBLANES

    for slot in range(FFN_SLOTS):
        t = FFN_SLOTS * i + slot
        fresh = jnp.logical_or(
            i == 0, tile_e_ref[t] != tile_e_ref[jnp.maximum(t - FFN_SLOTS, 0)])

        @pl.when(fresh)
        def _(slot=slot):
            wg_ref, wu_ref, wd_ref = w_refs[3 * slot:3 * slot + 3]
            wg_bf[slot] = wg_ref[...].astype(BF16)
            wu_bf[slot] = wu_ref[...].astype(BF16)
            wd_bf[slot] = wd_ref[...].astype(BF16)

    on = tile_on_ref[FFN_SLOTS * i] > 0

    @pl.when(on)
    def _():
        for slot in range(FFN_SLOTS):
            rows = pl.ds(slot * sub, sub)
            lo, hi = _unpack_rows(_load_packed(xs_ref.at[rows]))
            lo = lo.astype(BF16)
            hi = hi.astype(BF16)
            gate = (_dot(lo, wg_bf[slot, :PACK_COLS, :]) + _dot(hi, wg_bf[slot, PACK_COLS:, :]))
            up = (_dot(lo, wu_bf[slot, :PACK_COLS, :]) + _dot(hi, wu_bf[slot, PACK_COLS:, :]))
            hid = (jax.nn.silu(gate) * up).astype(BF16)
            _store_packed(ys_ref.at[rows], _pack_rows(_dot(hid, wd_bf[slot])))

    @pl.when(jnp.logical_not(on))
    def _():
        ys_ref[...] = jnp.zeros_like(ys_ref)


def _expert_ffn(xs, tile_e, tile_on, w_gate, w_up, w_down):
    rows_per_step = GROUP_TILE * FFN_SLOTS
    packed = pl.BlockSpec((rows_per_step // SUBLANES, PACK_BLOCKS, SUBLANES, LANES),
                          lambda i, te, on: (i, 0, 0, 0))
    w_specs = []
    for slot in range(FFN_SLOTS):
        pick = lambda i, te, on, slot=slot: (te[FFN_SLOTS * i + slot], 0, 0)
        w_specs += [pl.BlockSpec((None, D_MODEL, EXPERT_HID), pick),
                    pl.BlockSpec((None, D_MODEL, EXPERT_HID), pick),
                    pl.BlockSpec((None, EXPERT_HID, D_MODEL), pick)]
    grid_spec = pltpu.PrefetchScalarGridSpec(
        num_scalar_prefetch=2,
        grid=(xs.shape[0] * SUBLANES // rows_per_step,),
        in_specs=[packed] + w_specs,
        out_specs=packed,
        scratch_shapes=[pltpu.VMEM((FFN_SLOTS, D_MODEL, EXPERT_HID), BF16),
                        pltpu.VMEM((FFN_SLOTS, D_MODEL, EXPERT_HID), BF16),
                        pltpu.VMEM((FFN_SLOTS, EXPERT_HID, D_MODEL), BF16)],
    )
    return pl.pallas_call(
        _expert_ffn_kernel,
        grid_spec=grid_spec,
        out_shape=jax.ShapeDtypeStruct(xs.shape, jnp.int32),
        compiler_params=_params("arbitrary"),
        name="expert_ffn",
    )(tile_e, tile_on, xs, *([w_gate, w_up, w_down] * FFN_SLOTS))


def _moe_out_kernel(x1_ref, yg_ref, gate_ref, sg_ref, su_ref, sd_ref, g_ref, b_ref, *rest):
    o_ref = rest[-1]
    x1 = x1_ref[...]
    xb = x1.astype(BF16)
    hid = jax.nn.silu(_dot(xb, sg_ref[...])) * _dot(xb, su_ref[...])
    shared = _dot(hid.astype(BF16), sd_ref[...])
    gates = gate_ref[...]
    lo_acc = shared[:, :PACK_COLS]
    hi_acc = shared[:, PACK_COLS:]
    for kk in range(TOP_K):
        lo, hi = _unpack_rows(_load_packed(yg_ref.at[kk]))
        w = gates[:, kk:kk + 1]
        lo_acc = lo_acc + w * lo
        hi_acc = hi_acc + w * hi
    moe = jnp.concatenate([lo_acc, hi_acc], axis=1)
    o_ref[...] = _layer_norm(DN_ALPHA * x1 + moe, g_ref[...], b_ref[...])


def _moe_out(x1, yg, gates, sh_gate, sh_up, sh_down, ln_g, ln_b, part, earlier):
    n_tok = x1.shape[0]
    tm = TOKEN_TILE
    steps = n_tok // tm // MOE_PARTS
    first = part * steps
    row = lambda w: pl.BlockSpec((tm, w), lambda i: (i + first, 0))
    in_specs = [row(D_MODEL),
                pl.BlockSpec((TOP_K, tm // SUBLANES, PACK_BLOCKS, SUBLANES, LANES),
                             lambda i: (0, i, 0, 0, 0)),
                row(LANES),
                _resident((D_MODEL, EXPERT_HID)), _resident((D_MODEL, EXPERT_HID)),
                _resident((EXPERT_HID, D_MODEL)), _resident((1, D_MODEL)),
                _resident((1, D_MODEL))]
    args = [x1, yg, gates, sh_gate.astype(BF16), sh_up.astype(BF16), sh_down.astype(BF16),
            ln_g[None, :], ln_b[None, :]]
    aliases = {}
    if earlier is not None:
        in_specs.append(pl.BlockSpec(memory_space=pl.ANY))
        args.append(earlier)
        aliases = {len(args) - 1: 0}
    return pl.pallas_call(
        _moe_out_kernel,
        grid=(steps,),
        in_specs=in_specs,
        out_specs=row(D_MODEL),
        out_shape=jax.ShapeDtypeStruct((n_tok, D_MODEL), F32),
        input_output_aliases=aliases,
        compiler_params=_params("arbitrary"),
        name="moe_out",
    )(*args)


def _slot_kernel(offset_ref, e_ref, rank_ref, pos_ref):
    e = e_ref[...]
    base = jnp.zeros(e.shape, jnp.int32)
    for ex in range(N_EXPERTS):
        base = jnp.where(e == ex, offset_ref[ex], base)
    pos_ref[...] = base + rank_ref[...]


def _slots(offset, e6, rank6):
    n_tok = e6.shape[1]
    tl = 4096
    lane_row = pl.BlockSpec((SUBLANES, tl), lambda i, off: (0, i))
    return pl.pallas_call(
        _slot_kernel,
        grid_spec=pltpu.PrefetchScalarGridSpec(
            num_scalar_prefetch=1, grid=(n_tok // tl,),
            in_specs=[lane_row, lane_row], out_specs=lane_row),
        out_shape=jax.ShapeDtypeStruct((SUBLANES, n_tok), jnp.int32),
        compiler_params=_params("arbitrary"),
        name="slots",
    )(offset, e6, rank6)


def _sc_mesh():
    return plsc.VectorSubcoreMesh(core_axis_name="c", subcore_axis_name="s")


def _sc_worker():
    return lax.axis_index("s") * SC_CORES + lax.axis_index("c")


def _sc_dispatch(x_sub, idx, n_out_sub):
    windows = x_sub.shape[0] // (SC_WORKERS * SC_WINDOW)

    @functools.partial(
        pl.kernel, mesh=_sc_mesh(),
        out_type=jax.ShapeDtypeStruct((n_out_sub, LANES), jnp.int32),
        scratch_types=[pltpu.VMEM((SC_WINDOW, LANES), jnp.int32),
                       pltpu.VMEM((TOP_K, SC_WINDOW), jnp.int32)],
        name="sc_dispatch")
    def run(x_hbm, idx_hbm, out_hbm, rows_v, idx_v):
        wid = _sc_worker()

        @pl.loop(0, windows)
        def _(c):
            base = (wid * windows + c) * SC_WINDOW
            pltpu.sync_copy(x_hbm.at[pl.ds(base, SC_WINDOW)], rows_v)
            pltpu.sync_copy(idx_hbm.at[wid, c], idx_v)
            for kk in range(TOP_K):
                pltpu.sync_copy(rows_v, out_hbm.at[idx_v.at[kk]])

    return run(x_sub, idx)


def _sc_combine(y_sub, idx, n_tok_sub):
    windows = n_tok_sub // (SC_WORKERS * SC_WINDOW)

    @functools.partial(
        pl.kernel, mesh=_sc_mesh(),
        out_type=jax.ShapeDtypeStruct((TOP_K, n_tok_sub, LANES), jnp.int32),
        scratch_types=[pltpu.VMEM((2, SC_WINDOW, LANES), jnp.int32),
                       pltpu.VMEM((TOP_K, SC_WINDOW), jnp.int32),
                       pltpu.SemaphoreType.DMA((2,)),
                       pltpu.SemaphoreType.DMA((2,))],
        name="sc_combine")
    def run(y_hbm, idx_hbm, out_hbm, rows_v, idx_v, gather_sem, write_sem):
        wid = _sc_worker()

        def gather(kk):
            buf = kk % 2
            return pltpu.async_copy(y_hbm.at[idx_v.at[kk]], rows_v.at[buf], gather_sem.at[buf])

        @pl.loop(0, windows)
        def _(c):
            base = (wid * windows + c) * SC_WINDOW
            pltpu.sync_copy(idx_hbm.at[wid, c], idx_v)
            writes = []
            pending = gather(0)
            for kk in range(TOP_K):
                pending.wait()
                if kk + 1 < TOP_K:
                    if kk >= 1:
                        writes[kk - 1].wait()
                    pending = gather(kk + 1)
                writes.append(pltpu.async_copy(
                    rows_v.at[kk % 2], out_hbm.at[kk, pl.ds(base, SC_WINDOW)],
                    write_sem.at[kk % 2]))
            writes[TOP_K - 2].wait()
            writes[TOP_K - 1].wait()

    return run(y_sub, idx)


def _moe(x1, x1p, e6, gates, rank6, counts, exp_w_gate, exp_w_up, exp_w_down,
         sh_w_gate, sh_w_up, sh_w_down, ln_g, ln_b):
    n_tok = x1.shape[0]
    tg = GROUP_TILE
    n_rows = n_tok * TOP_K + N_EXPERTS * tg
    n_tiles = n_rows // tg

    cnt = counts[:, 0].astype(jnp.int32)
    tiles_per_e = (cnt + tg - 1) // tg
    tile_end = jnp.cumsum(tiles_per_e)
    offset = (tile_end - tiles_per_e) * tg
    tile_ids = jnp.arange(n_tiles, dtype=jnp.int32)
    tile_on = (tile_ids < tile_end[-1]).astype(jnp.int32)
    tile_e = jnp.minimum(
        jnp.sum((tile_ids[:, None] >= tile_end[None, :]).astype(jnp.int32), axis=1),
        N_EXPERTS - 1)
    tile_e = jnp.where(tile_on > 0, tile_e, tile_e[jnp.maximum(tile_end[-1] - 1, 0)])

    pos = _slots(offset, e6, rank6)[:TOP_K]

    group = PACK_BLOCKS * SUBLANES
    sub0 = (pos // SUBLANES) * group + pos % SUBLANES
    lane = jnp.arange(SC_WINDOW)
    spread = (jnp.arange(group)[:, None]
              == (lane // group) * SUBLANES + lane % SUBLANES).astype(F32)
    idx = jnp.einsum("kws,sd->kwd", sub0.reshape(TOP_K, n_tok // group, group).astype(F32),
                     spread, precision=lax.Precision.HIGHEST).astype(jnp.int32)
    idx = idx + ((lane % group) // SUBLANES * SUBLANES).astype(jnp.int32)
    n_sub = n_tok * PACK_BLOCKS
    idx = idx.reshape(TOP_K, n_sub)

    def windowed(ix):
        windows = ix.shape[1] // (SC_WORKERS * SC_WINDOW)
        return ix.reshape(TOP_K, SC_WORKERS, windows, SC_WINDOW).transpose(1, 2, 0, 3)

    xs = _sc_dispatch(x1p.reshape(n_sub, LANES), windowed(idx), n_rows * PACK_BLOCKS)
    ys = _expert_ffn(xs.reshape(n_rows // SUBLANES, PACK_BLOCKS, SUBLANES, LANES),
                     tile_e, tile_on, exp_w_gate, exp_w_up, exp_w_down)
    ys = ys.reshape(n_rows * PACK_BLOCKS, LANES)
    part_sub = n_sub // MOE_PARTS
    out = None
    for part in range(MOE_PARTS):
        yg = _sc_combine(ys, windowed(idx[:, part * part_sub:(part + 1) * part_sub]), part_sub)
        yg = yg.reshape(TOP_K, part_sub // (PACK_BLOCKS * SUBLANES), PACK_BLOCKS,
                        SUBLANES, LANES)
        out = _moe_out(x1, yg, gates, sh_w_gate, sh_w_up, sh_w_down, ln_g, ln_b, part, out)
    return out


def kernel(x, positions,
           l0_w_in, l0_b_in, l0_conv_w, l0_conv_b, l0_conv_ln_g, l0_conv_ln_b,
           l0_q_norm_g, l0_w_uq, l0_kv_norm_g, l0_w_ukv, l0_w_out, l0_b_out,
           l0_ln1_g, l0_ln1_b,
           l0_router_w, l0_router_bias, l0_exp_w_gate, l0_exp_w_up, l0_exp_w_down,
           l0_sh_w_gate, l0_sh_w_up, l0_sh_w_down, l0_ln2_g, l0_ln2_b,
           l1_w_in, l1_b_in, l1_w_out, l1_b_out, l1_ln1_g, l1_ln1_b,
           l1_router_w, l1_router_bias, l1_exp_w_gate, l1_exp_w_up, l1_exp_w_down,
           l1_sh_w_gate, l1_sh_w_up, l1_sh_w_down, l1_ln2_g, l1_ln2_b):
    bsz, seq, d = x.shape
    n_tok = bsz * seq

    u, q, k, v = _l0_front(x, positions, l0_w_in, l0_b_in, l0_conv_w, l0_conv_b,
                           l0_conv_ln_g, l0_conv_ln_b, l0_q_norm_g, l0_w_uq,
                           l0_kv_norm_g, l0_w_ukv)
    o = _attention(q, k, v, None, packed=False)
    x_flat = x.reshape(n_tok, d)
    routing = _mix_out_router(u.reshape(n_tok, -1), o.reshape(n_tok, -1), x_flat,
                              l0_w_out, l0_b_out, l0_ln1_g, l0_ln1_b,
                              l0_router_w, l0_router_bias)
    x2 = _moe(*routing, l0_exp_w_gate, l0_exp_w_up, l0_exp_w_down,
              l0_sh_w_gate, l0_sh_w_up, l0_sh_w_down, l0_ln2_g, l0_ln2_b)

    q, k, v, cneg = _l1_front(x2.reshape(bsz, seq, d), l1_w_in, l1_b_in)
    o = _attention(q, k, v, cneg[:, :, None, :], packed=True).reshape(n_tok, -1)
    routing = _mix_out_router(o, o, x2, l1_w_out, l1_b_out, l1_ln1_g, l1_ln1_b,
                              l1_router_w, l1_router_bias)
    x3 = _moe(*routing, l1_exp_w_gate, l1_exp_w_up, l1_exp_w_down,
              l1_sh_w_gate, l1_sh_w_up, l1_sh_w_down, l1_ln2_g, l1_ln2_b)
    return x3.reshape(bsz, seq, d)
```

```python
import functools

import jax
import jax.numpy as jnp
from jax import lax
from jax.experimental import pallas as pl
from jax.experimental.pallas import tpu as pltpu
from jax.experimental.pallas import tpu_sc as plsc

D_MODEL = 1024
DEPTH = 2
DN_ALPHA = (2 * DEPTH) ** 0.25
LN_EPS = 1e-5
RMS_EPS = 1e-6

CONV_CH = 512
CONV_WIDTH = 31
MLA_HEADS = 8
QK_NOPE = 64
QK_ROPE = 32
V_DIM = 64
Q_LORA = 256
KV_LORA = 128
ROPE_THETA = 10000.0
FOX_HEADS = 16
FOX_HD = 64
N_EXPERTS = 64
TOP_K = 6
N_GROUPS = 8
TOPK_GROUPS = 4
EXPERT_HID = 256
ROUTED_SCALE = 2.5

LANES = 128
SUBLANES = 8
VMEM_LIMIT_BYTES = 56 * 1024 * 1024

SEQ_TILE = 512
TOKEN_TILE = 512
ATTN_TILE = 512
ATTN_KEY_TILE = 512
ATTN_UNROLL = 4
GROUP_TILE = 512
FFN_SLOTS = 2
CONV_ROWS = 32
CONV_HALO = 32
PAD_HEAD = 128

PACK_COLS = D_MODEL // 2
PACK_BLOCKS = PACK_COLS // LANES
SC_CORES = 2
SC_SUBCORES = 16
SC_WORKERS = SC_CORES * SC_SUBCORES
SC_WINDOW = 128
MOE_PARTS = 4
BF16 = jnp.bfloat16
F32 = jnp.float32
NEG_INF = float("-inf")
LOG2_E = 1.4426950408889634


def _params(*semantics):
    return pltpu.CompilerParams(dimension_semantics=semantics,
                                vmem_limit_bytes=VMEM_LIMIT_BYTES)


def _resident(shape):
    nd = len(shape)
    return pl.BlockSpec(shape, lambda *_: (0,) * nd)


def _dot(a, b):
    return jnp.dot(a, b, preferred_element_type=F32)


def _dot_nt(a, b, precision=None):
    return lax.dot_general(a, b, (((1,), (1,)), ((), ())),
                           precision=precision, preferred_element_type=F32)


def _layer_norm(x, g, b):
    mu = jnp.mean(x, axis=-1, keepdims=True)
    xc = x - mu
    var = jnp.mean(xc * xc, axis=-1, keepdims=True)
    return xc * lax.rsqrt(var + LN_EPS) * g + b


def _rms_norm(x, g):
    return x * lax.rsqrt(jnp.mean(x * x, axis=-1, keepdims=True) + RMS_EPS) * g


def _ones_upper_half(width):
    lane = lax.broadcasted_iota(jnp.int32, (1, width), 1)
    return jnp.where((lane & (PAD_HEAD - 1)) >= V_DIM, 1.0, 0.0).astype(F32)


def _pad_heads(w, heads):
    d = w.shape[-1] // heads
    w3 = w.reshape(w.shape[:-1] + (heads, d))
    pad = jnp.zeros(w.shape[:-1] + (heads, PAD_HEAD - d), w.dtype)
    return jnp.concatenate([w3, pad], axis=-1).reshape(w.shape[:-1] + (heads * PAD_HEAD,))


def _pack_rows(x):
    lo = lax.bitcast_convert_type(x[:, :PACK_COLS].astype(BF16).astype(F32), jnp.uint32)
    hi = lax.bitcast_convert_type(x[:, PACK_COLS:].astype(BF16).astype(F32), jnp.uint32)
    word = (lo >> 16) | (hi & jnp.uint32(0xFFFF0000))
    return lax.bitcast_convert_type(word, jnp.int32)


def _unpack_rows(words):
    u = lax.bitcast_convert_type(words, jnp.uint32)
    lo = lax.bitcast_convert_type(u << 16, F32)
    hi = lax.bitcast_convert_type(u & jnp.uint32(0xFFFF0000), F32)
    return lo, hi


def _store_packed(ref, words):
    groups = words.shape[0] // SUBLANES
    for cb in range(PACK_BLOCKS):
        ref[:, cb, :, :] = words[:, cb * LANES:(cb + 1) * LANES].reshape(groups, SUBLANES, LANES)


def _load_packed(ref):
    rows = ref.shape[0] * SUBLANES
    return jnp.concatenate(
        [ref[:, cb, :, :].reshape(rows, LANES) for cb in range(PACK_BLOCKS)], axis=1)


L0_A = 0
L0_G = CONV_CH
L0_Q = 2 * CONV_CH
L0_KV = L0_Q + Q_LORA
L0_KR = L0_KV + KV_LORA
L0_KRR = L0_KR + PAD_HEAD
L0_COLS = L0_KRR + PAD_HEAD


def _l0_front_kernel(x_ref, pos_ref, w_in_ref, b_in_ref, conv_w_ref, conv_b_ref,
                     cln_g_ref, cln_b_ref, qn_g_ref, wq_ref, wqr_ref, kvn_g_ref,
                     wk_ref, wv_ref, invf_ref,
                     u_ref, q_ref, k_ref, v_ref, ubuf):
    ts = x_ref.shape[0]

    @pl.when(pl.program_id(1) == 0)
    def _():
        ubuf[0:CONV_HALO, :] = jnp.zeros((CONV_HALO, CONV_CH), F32)

    h = _dot(x_ref[...].astype(BF16), w_in_ref[...]) + b_in_ref[...]

    ubuf[CONV_HALO:CONV_HALO + ts, :] = (
        h[:, L0_A:L0_A + CONV_CH] * jax.nn.sigmoid(h[:, L0_G:L0_G + CONV_CH]))
    first_tap = CONV_HALO - (CONV_WIDTH - 1)

    def conv_chunk(c, carry):
        base = pl.multiple_of(c * CONV_ROWS, CONV_ROWS)
        acc = jnp.broadcast_to(conv_b_ref[...], (CONV_ROWS, CONV_CH))
        for res in range(SUBLANES):
            rows = CONV_ROWS + (SUBLANES if res else 0)
            part = None
            for off in range(res, first_tap + CONV_WIDTH, SUBLANES):
                j = off - first_tap
                if j < 0:
                    continue
                term = ubuf[pl.ds(base + (off - res), rows), :] * conv_w_ref[j:j + 1, :]
                part = term if part is None else part + term
            acc = acc + part[res:res + CONV_ROWS, :]
        y = _layer_norm(acc, cln_g_ref[...], cln_b_ref[...])
        u_ref[pl.ds(base, CONV_ROWS), :] = (y * jax.nn.sigmoid(y)).astype(BF16)
        return carry

    lax.fori_loop(0, ts // CONV_ROWS, conv_chunk, 0)
    ubuf[0:CONV_HALO, :] = ubuf[ts:ts + CONV_HALO, :]

    ang = pos_ref[...].astype(F32) * invf_ref[...]
    cos = jnp.cos(ang)
    sin = jnp.sin(ang)
    scale = (QK_NOPE + QK_ROPE) ** -0.5 * LOG2_E

    qn = _rms_norm(h[:, L0_Q:L0_Q + Q_LORA], qn_g_ref[...]).astype(BF16)
    q = _dot(qn, wq_ref[...])
    q_rot = _dot(qn, wqr_ref[...])
    cos_s = cos * scale
    sin_s = sin * scale
    for hh in range(MLA_HEADS):
        blk = slice(hh * PAD_HEAD, (hh + 1) * PAD_HEAD)
        q_ref[:, blk] = (q[:, blk] * cos_s + q_rot[:, blk] * sin_s).astype(BF16)

    kvn = _rms_norm(h[:, L0_KV:L0_KV + KV_LORA], kvn_g_ref[...]).astype(BF16)
    k_nope = _dot(kvn, wk_ref[...])
    k_pe = h[:, L0_KR:L0_KR + PAD_HEAD] * cos + h[:, L0_KRR:L0_KRR + PAD_HEAD] * sin
    for hh in range(MLA_HEADS):
        blk = slice(hh * PAD_HEAD, (hh + 1) * PAD_HEAD)
        k_ref[:, blk] = (k_nope[:, blk] + k_pe).astype(BF16)
    v_ref[...] = (_dot(kvn, wv_ref[...]) + _ones_upper_half(v_ref.shape[1])).astype(BF16)


def _rope_rotate_cols(w):
    half = QK_ROPE // 2
    return jnp.concatenate([-w[..., half:], w[..., :half]], axis=-1)


def _l0_front(x, positions, w_in, b_in, conv_w, conv_b, cln_g, cln_b,
              qn_g, w_uq, kvn_g, w_ukv):
    bsz, seq, _ = x.shape
    pad_lo = jnp.zeros((D_MODEL, QK_NOPE), F32)
    pad_hi = jnp.zeros((D_MODEL, PAD_HEAD - QK_NOPE - QK_ROPE), F32)
    i3 = L0_KR
    w_kr = w_in[:, i3:i3 + QK_ROPE]
    w_in_p = jnp.concatenate(
        [w_in[:, :i3], pad_lo, w_kr, pad_hi, pad_lo, _rope_rotate_cols(w_kr), pad_hi],
        axis=1).astype(BF16)
    b_kr = b_in[i3:i3 + QK_ROPE]
    zlo = jnp.zeros((QK_NOPE,), F32)
    zhi = jnp.zeros((PAD_HEAD - QK_NOPE - QK_ROPE,), F32)
    b_in_p = jnp.concatenate(
        [b_in[:i3], zlo, b_kr, zhi, zlo, _rope_rotate_cols(b_kr), zhi])[None, :]

    dq = QK_NOPE + QK_ROPE
    wq3 = w_uq.reshape(Q_LORA, MLA_HEADS, dq)
    zq = jnp.zeros((Q_LORA, MLA_HEADS, PAD_HEAD - dq), F32)
    wq_p = jnp.concatenate([wq3, zq], axis=-1).reshape(Q_LORA, MLA_HEADS * PAD_HEAD)
    wqr_p = jnp.concatenate(
        [jnp.zeros((Q_LORA, MLA_HEADS, QK_NOPE), F32),
         _rope_rotate_cols(wq3[..., QK_NOPE:]), zq], axis=-1
    ).reshape(Q_LORA, MLA_HEADS * PAD_HEAD)
    wkv3 = w_ukv.reshape(KV_LORA, MLA_HEADS, QK_NOPE + V_DIM)
    wk_p = jnp.concatenate(
        [wkv3[..., :QK_NOPE], jnp.zeros((KV_LORA, MLA_HEADS, PAD_HEAD - QK_NOPE), F32)],
        axis=-1).reshape(KV_LORA, MLA_HEADS * PAD_HEAD)
    wv = _pad_heads(wkv3[..., QK_NOPE:].reshape(KV_LORA, MLA_HEADS * V_DIM), MLA_HEADS)

    inv_freq = 1.0 / (ROPE_THETA ** (jnp.arange(0, QK_ROPE, 2, dtype=F32) / QK_ROPE))
    invf = jnp.concatenate([jnp.zeros((QK_NOPE,), F32), inv_freq, inv_freq,
                            jnp.zeros((PAD_HEAD - QK_NOPE - QK_ROPE,), F32)])[None, :]

    ts = SEQ_TILE
    row = lambda w: pl.BlockSpec((None, ts, w), lambda b, s: (b, s, 0))
    qk_w = MLA_HEADS * PAD_HEAD
    v_w = MLA_HEADS * PAD_HEAD
    return pl.pallas_call(
        _l0_front_kernel,
        grid=(bsz, seq // ts),
        in_specs=[row(D_MODEL), row(1),
                  _resident((D_MODEL, L0_COLS)), _resident((1, L0_COLS)),
                  _resident((CONV_WIDTH, CONV_CH)), _resident((1, CONV_CH)),
                  _resident((1, CONV_CH)), _resident((1, CONV_CH)),
                  _resident((1, Q_LORA)), _resident((Q_LORA, qk_w)),
                  _resident((Q_LORA, qk_w)), _resident((1, KV_LORA)),
                  _resident((KV_LORA, qk_w)), _resident((KV_LORA, v_w)),
                  _resident((1, PAD_HEAD))],
        out_specs=[row(CONV_CH), row(qk_w), row(qk_w), row(v_w)],
        out_shape=[jax.ShapeDtypeStruct((bsz, seq, CONV_CH), BF16),
                   jax.ShapeDtypeStruct((bsz, seq, qk_w), BF16),
                   jax.ShapeDtypeStruct((bsz, seq, qk_w), BF16),
                   jax.ShapeDtypeStruct((bsz, seq, v_w), BF16)],
        scratch_shapes=[pltpu.VMEM((ts + CONV_HALO, CONV_CH), F32)],
        compiler_params=_params("arbitrary", "arbitrary"),
        name="l0_front",
    )(x, positions[..., None], w_in_p, b_in_p, conv_w, conv_b[None, :],
      cln_g[None, :], cln_b[None, :], qn_g[None, :], wq_p.astype(BF16),
      wqr_p.astype(BF16), kvn_g[None, :], wk_p.astype(BF16), wv.astype(BF16), invf)


def _l1_front_kernel(x_ref, w_ref, b_ref, wv_ref, bv_ref, wf_ref, bf_ref,
                     q_ref, k_ref, v_ref, cneg_ref, carry):
    ts = x_ref.shape[0]

    @pl.when(pl.program_id(1) == 0)
    def _():
        carry[...] = jnp.zeros_like(carry)

    xb = x_ref[...].astype(BF16)
    mix = FOX_HEADS * FOX_HD
    scale = FOX_HD ** -0.5 * LOG2_E
    q_ref[...] = ((_dot(xb, w_ref[:, 0:mix]) + b_ref[:, 0:mix]) * scale).astype(BF16)
    k_ref[...] = (_dot(xb, w_ref[:, mix:2 * mix]) + b_ref[:, mix:2 * mix]).astype(BF16)
    v = _dot(xb, wv_ref[...]) + bv_ref[...]
    head_lanes = lax.broadcasted_iota(jnp.int32, (ts, LANES), 1) < FOX_HD
    for pair in range(FOX_HEADS // 2):
        both = v[:, pair * LANES:(pair + 1) * LANES]
        swapped = pltpu.roll(both, LANES - FOX_HD, axis=1)
        for head, vals in enumerate((both, swapped)):
            blk = 2 * pair + head
            v_ref[:, blk * PAD_HEAD:(blk + 1) * PAD_HEAD] = (
                jnp.where(head_lanes, vals, 1.0).astype(BF16))

    log_f = jax.nn.log_sigmoid(_dot_nt(wf_ref[...], xb) + bf_ref[...])
    r = lax.broadcasted_iota(jnp.int32, (ts, ts), 0)
    c = lax.broadcasted_iota(jnp.int32, (ts, ts), 1)
    upper = (r <= c).astype(F32)
    csum = jnp.dot(log_f, upper, precision=lax.Precision.HIGHEST,
                   preferred_element_type=F32) + carry[...]
    cneg_ref[...] = csum * -LOG2_E
    carry[...] = carry[...] + jnp.sum(log_f, axis=1, keepdims=True)


def _l1_front(x, w_in, b_in):
    bsz, seq, _ = x.shape
    mix = FOX_HEADS * FOX_HD
    ts = SEQ_TILE
    row = lambda w: pl.BlockSpec((None, ts, w), lambda b, s: (b, s, 0))
    w_qk = w_in[:, :2 * mix].astype(BF16)
    b_qk = b_in[None, :2 * mix]
    w_v = w_in[:, 2 * mix:3 * mix].astype(BF16)
    b_v = b_in[None, 2 * mix:3 * mix]
    v_w = FOX_HEADS * PAD_HEAD
    wf_t = w_in[:, 3 * mix:].T.astype(BF16)
    bf_t = b_in[3 * mix:][:, None]
    return pl.pallas_call(
        _l1_front_kernel,
        grid=(bsz, seq // ts),
        in_specs=[row(D_MODEL), _resident((D_MODEL, 2 * mix)), _resident((1, 2 * mix)),
                  _resident((D_MODEL, mix)), _resident((1, mix)),
                  _resident((FOX_HEADS, D_MODEL)), _resident((FOX_HEADS, 1))],
        out_specs=[row(mix), row(mix), row(v_w),
                   pl.BlockSpec((None, FOX_HEADS, ts), lambda b, s: (b, 0, s))],
        out_shape=[jax.ShapeDtypeStruct((bsz, seq, mix), BF16)] * 2
        + [jax.ShapeDtypeStruct((bsz, seq, v_w), BF16),
           jax.ShapeDtypeStruct((bsz, FOX_HEADS, seq), F32)],
        scratch_shapes=[pltpu.VMEM((FOX_HEADS, 1), F32)],
        compiler_params=_params("arbitrary", "arbitrary"),
        name="l1_front",
    )(x, w_qk, b_qk, w_v, b_v, wf_t, bf_t)


def _attn_kernel(*refs, packed, has_bias):
    if has_bias:
        q_ref, k_ref, v_ref, cneg_ref, o_ref = refs
    else:
        q_ref, k_ref, v_ref, o_ref = refs
        cneg_ref = None
    tq = ATTN_TILE
    tk = ATTN_KEY_TILE
    half = LANES // 2

    def query_tile(qi):
        q_rows = pl.ds(pl.multiple_of(qi * tq, tq), tq)
        q2 = q_ref[q_rows, :]
        if packed:
            lane = lax.broadcasted_iota(jnp.int32, q2.shape, 1)
            zero = jnp.zeros_like(q2)
            qs = (jnp.where(lane < half, q2, zero), jnp.where(lane >= half, q2, zero))
        else:
            qs = (q2[:, :PAD_HEAD], q2[:, PAD_HEAD:])

        def k_block(kj, head):
            rows = pl.ds(pl.multiple_of(kj * tk, tk), tk)
            if packed:
                return k_ref[rows, :]
            return k_ref[rows, head * PAD_HEAD:(head + 1) * PAD_HEAD]

        def step(kj, carry, diag):
            rows = pl.ds(pl.multiple_of(kj * tk, tk), tk)
            out = []
            for head in range(2):
                m, acc = carry[head]
                s = _dot_nt(qs[head], k_block(kj, head))
                if has_bias:
                    s = s + cneg_ref[head, :, rows]
                if diag is not None:
                    r = lax.broadcasted_iota(jnp.int32, s.shape, 0)
                    c = lax.broadcasted_iota(jnp.int32, s.shape, 1)
                    s = jnp.where(r >= c + diag * tk, s, NEG_INF)
                m_new = jnp.maximum(m, jnp.max(s, axis=-1, keepdims=True))
                alpha = jnp.exp2(m - m_new)
                p = jnp.exp2(s - m_new).astype(BF16)
                vblk = v_ref[rows, head * PAD_HEAD:(head + 1) * PAD_HEAD]
                out.append((m_new, alpha * acc + _dot(p, vblk)))
            return tuple(out)

        init = tuple((jnp.full((tq, 1), NEG_INF, F32), jnp.zeros((tq, LANES), F32))
                     for _ in range(2))
        per_tile = tq // tk
        n_full = qi * per_tile

        def group(first, c, size):
            for t in range(size):
                c = step(first + t, c, None)
            return c

        carry = lax.fori_loop(0, n_full // ATTN_UNROLL,
                              lambda j, c: group(ATTN_UNROLL * j, c, ATTN_UNROLL), init)
        done = (n_full // ATTN_UNROLL) * ATTN_UNROLL
        size = ATTN_UNROLL // 2
        while size >= 1:
            take = (n_full - done) >= size
            carry = lax.cond(take, lambda c, f=done, n=size: group(f, c, n), lambda c: c,
                             carry)
            done = done + jnp.where(take, size, 0)
            size //= 2
        for d in range(per_tile):
            carry = step(n_full + d, carry, d)
        (_, acc_a), (_, acc_b) = carry
        out_a = acc_a / pltpu.roll(acc_a, half, axis=1)
        out_b = acc_b / pltpu.roll(acc_b, half, axis=1)
        lane_o = lax.broadcasted_iota(jnp.int32, (tq, LANES), 1)
        o_ref[q_rows, :] = jnp.where(lane_o < half, out_a,
                                     pltpu.roll(out_b, half, axis=1)).astype(o_ref.dtype)

    pl.loop(0, q_ref.shape[0] // tq)(query_tile)


def _attention(q, k, v, cneg, packed):
    bsz, seq, v_w = v.shape
    pairs = v_w // (2 * PAD_HEAD)
    qk_w = q.shape[-1] // pairs
    in_specs = [pl.BlockSpec((None, seq, qk_w), lambda b, p: (b, 0, p)),
                pl.BlockSpec((None, seq, qk_w), lambda b, p: (b, 0, p)),
                pl.BlockSpec((None, seq, 2 * PAD_HEAD), lambda b, p: (b, 0, p))]
    args = [q, k, v]
    if cneg is not None:
        in_specs.append(pl.BlockSpec((None, 2, 1, seq), lambda b, p: (b, p, 0, 0)))
        args.append(cneg)
    return pl.pallas_call(
        functools.partial(_attn_kernel, packed=packed, has_bias=cneg is not None),
        grid=(bsz, pairs),
        in_specs=in_specs,
        out_specs=pl.BlockSpec((None, seq, LANES), lambda b, p: (b, 0, p)),
        out_shape=jax.ShapeDtypeStruct((bsz, seq, pairs * LANES), BF16),
        compiler_params=_params("arbitrary", "arbitrary"),
        name="attn_packed" if packed else "attn_padded",
    )(*args)


def _first_hit(hits, found):
    out = []
    for hcur in hits:
        take = jnp.logical_and(hcur, jnp.logical_not(found))
        found = jnp.logical_or(found, take)
        out.append(take)
    return out, found


def _mix_out_router_kernel(a_ref, b_ref, x_ref, wa_ref, wb_ref, bo_ref, g_ref, be_ref,
                           rw_ref, rwl_ref, rb_ref,
                           x1_ref, x1p_ref, e_ref, gate_ref, rank_ref, cnt_ref, cnt, gate_t):
    tm = x_ref.shape[0]

    @pl.when(pl.program_id(0) == 0)
    def _():
        cnt[...] = jnp.zeros_like(cnt)
        gate_t[...] = jnp.zeros_like(gate_t)

    mix = _dot(a_ref[...], wa_ref[...]) + _dot(b_ref[...], wb_ref[...]) + bo_ref[...]
    x1 = _layer_norm(DN_ALPHA * x_ref[...] + mix, g_ref[...], be_ref[...])
    x1_ref[...] = x1
    _store_packed(x1p_ref, _pack_rows(x1))

    x_hi = x1.astype(BF16)
    x_lo = (x1 - x_hi.astype(F32)).astype(BF16)
    logits = (_dot_nt(rw_ref[...], x_hi) + _dot_nt(rw_ref[...], x_lo)
              + _dot_nt(rwl_ref[...], x_hi))
    aff = jax.nn.sigmoid(logits)
    choice = aff + rb_ref[...]
    per_group = N_EXPERTS // N_GROUPS
    sub = lax.broadcasted_iota(jnp.int32, (per_group, tm), 0)
    groups = [choice[g * per_group:(g + 1) * per_group, :] for g in range(N_GROUPS)]

    gscore = []
    for cg in groups:
        m1 = jnp.max(cg, axis=0, keepdims=True)
        i1 = jnp.min(jnp.where(cg == m1, sub, per_group), axis=0, keepdims=True)
        m2 = jnp.max(jnp.where(sub == i1, NEG_INF, cg), axis=0, keepdims=True)
        gscore.append(m1 + m2)

    gsel = [jnp.zeros((1, tm), jnp.bool_) for _ in range(N_GROUPS)]
    for _ in range(TOPK_GROUPS):
        best = functools.reduce(jnp.maximum, gscore)
        takes, _ = _first_hit([gs == best for gs in gscore], jnp.zeros((1, tm), jnp.bool_))
        gsel = [jnp.logical_or(a, t) for a, t in zip(gsel, takes)]
        gscore = [jnp.where(t, NEG_INF, gs) for gs, t in zip(gscore, takes)]

    masked = [jnp.where(gs, cg, NEG_INF) for gs, cg in zip(gsel, groups)]
    eid = [sub + g * per_group for g in range(N_GROUPS)]
    affs = [aff[g * per_group:(g + 1) * per_group, :] for g in range(N_GROUPS)]
    sel = [jnp.zeros((per_group, tm), jnp.bool_) for _ in range(N_GROUPS)]
    picked_e, picked_w = [], []
    for _ in range(TOP_K):
        best = jnp.max(functools.reduce(jnp.maximum, masked), axis=0, keepdims=True)
        cand = [jnp.where(mg == best, ig, N_EXPERTS) for mg, ig in zip(masked, eid)]
        idx = jnp.min(functools.reduce(jnp.minimum, cand), axis=0, keepdims=True)
        onehot = [ig == idx for ig in eid]
        w = functools.reduce(
            jnp.add, [jnp.sum(jnp.where(oh, ag, 0.0), axis=0, keepdims=True)
                      for oh, ag in zip(onehot, affs)])
        picked_e.append(idx)
        picked_w.append(w)
        sel = [jnp.logical_or(sg, oh) for sg, oh in zip(sel, onehot)]
        masked = [jnp.where(oh, NEG_INF, mg) for mg, oh in zip(masked, onehot)]

    wsum = functools.reduce(jnp.add, picked_w)

    sel_f = jnp.concatenate([sg.astype(F32) for sg in sel], axis=0)
    r = lax.broadcasted_iota(jnp.int32, (tm, tm), 0)
    c = lax.broadcasted_iota(jnp.int32, (tm, tm), 1)
    before = _dot(sel_f.astype(BF16), (r < c).astype(BF16)) + cnt[...]
    eall = lax.broadcasted_iota(jnp.int32, (N_EXPERTS, tm), 0)
    for kk in range(TOP_K):
        rank = jnp.sum(jnp.where(eall == picked_e[kk], before, 0.0), axis=0, keepdims=True)
        e_ref[kk:kk + 1, :] = picked_e[kk]
        gate_t[kk:kk + 1, :] = picked_w[kk] / wsum * ROUTED_SCALE
        rank_ref[kk:kk + 1, :] = rank.astype(jnp.int32)
    pad = SUBLANES - TOP_K
    e_ref[TOP_K:, :] = jnp.zeros((pad, tm), jnp.int32)
    rank_ref[TOP_K:, :] = jnp.zeros((pad, tm), jnp.int32)
    gate_ref[...] = gate_t[...].T
    cnt[...] = cnt[...] + jnp.sum(sel_f, axis=1, keepdims=True)
    cnt_ref[...] = jnp.broadcast_to(cnt[...], cnt_ref.shape)


def _mix_out_router(a, b, x, w_out, b_out, ln_g, ln_b, router_w, router_bias):
    n_tok = x.shape[0]
    tm = TOKEN_TILE
    half = w_out.shape[0] // 2
    row = lambda w: pl.BlockSpec((tm, w), lambda i: (i, 0))
    b_spec = (pl.BlockSpec((tm, half), lambda i: (i, 1)) if b is a
              else pl.BlockSpec((tm, half), lambda i: (i, 0)))
    lane_row = pl.BlockSpec((SUBLANES, tm), lambda i: (0, i))
    packed = pl.BlockSpec((tm // SUBLANES, PACK_BLOCKS, SUBLANES, LANES),
                          lambda i: (i, 0, 0, 0))
    w_bf = w_out.astype(BF16)
    rw_t = router_w.T
    rw_hi = rw_t.astype(BF16)
    rw_lo = (rw_t - rw_hi.astype(F32)).astype(BF16)
    return pl.pallas_call(
        _mix_out_router_kernel,
        grid=(n_tok // tm,),
        in_specs=[row(half), b_spec, row(D_MODEL),
                  pl.BlockSpec((half, D_MODEL), lambda i: (0, 0)),
                  pl.BlockSpec((half, D_MODEL), lambda i: (1, 0)),
                  _resident((1, D_MODEL)), _resident((1, D_MODEL)), _resident((1, D_MODEL)),
                  _resident((N_EXPERTS, D_MODEL)), _resident((N_EXPERTS, D_MODEL)),
                  _resident((N_EXPERTS, 1))],
        out_specs=[row(D_MODEL), packed, lane_row, row(LANES), lane_row,
                   _resident((N_EXPERTS, LANES))],
        out_shape=[jax.ShapeDtypeStruct((n_tok, D_MODEL), F32),
                   jax.ShapeDtypeStruct((n_tok // SUBLANES, PACK_BLOCKS, SUBLANES, LANES),
                                        jnp.int32),
                   jax.ShapeDtypeStruct((SUBLANES, n_tok), jnp.int32),
                   jax.ShapeDtypeStruct((n_tok, LANES), F32),
                   jax.ShapeDtypeStruct((SUBLANES, n_tok), jnp.int32),
                   jax.ShapeDtypeStruct((N_EXPERTS, LANES), F32)],
        scratch_shapes=[pltpu.VMEM((N_EXPERTS, 1), F32), pltpu.VMEM((LANES, tm), F32)],
        compiler_params=_params("arbitrary"),
        name="mix_out_router",
    )(a, b, x, w_bf, w_bf, b_out[None, :], ln_g[None, :], ln_b[None, :],
      rw_hi, rw_lo, router_bias[:, None])


def _expert_ffn_kernel(tile_e_ref, tile_on_ref, xs_ref, *refs):
    w_refs = refs[:3 * FFN_SLOTS]
    ys_ref = refs[3 * FFN_SLOTS]
    wg_bf, wu_bf, wd_bf = refs[3 * FFN_SLOTS + 1:]
    i = pl.program_id(0)
    sub = GROUP_TILE // SUBLANES

    for slot in range(FFN_SLOTS):
        t = FFN_SLOTS * i + slot
        fresh = jnp.logical_or(
            i == 0, tile_e_ref[t] != tile_e_ref[jnp.maximum(t - FFN_SLOTS, 0)])

        @pl.when(fresh)
        def _(slot=slot):
            wg_ref, wu_ref, wd_ref = w_refs[3 * slot:3 * slot + 3]
            wg_bf[slot] = wg_ref[...].astype(BF16)
            wu_bf[slot] = wu_ref[...].astype(BF16)
            wd_bf[slot] = wd_ref[...].astype(BF16)

    on = tile_on_ref[FFN_SLOTS * i] > 0

    @pl.when(on)
    def _():
        for slot in range(FFN_SLOTS):
            rows = pl.ds(slot * sub, sub)
            lo, hi = _unpack_rows(_load_packed(xs_ref.at[rows]))
            lo = lo.astype(BF16)
            hi = hi.astype(BF16)
            gate = (_dot(lo, wg_bf[slot, :PACK_COLS, :]) + _dot(hi, wg_bf[slot, PACK_COLS:, :]))
            up = (_dot(lo, wu_bf[slot, :PACK_COLS, :]) + _dot(hi, wu_bf[slot, PACK_COLS:, :]))
            hid = (jax.nn.silu(gate) * up).astype(BF16)
            _store_packed(ys_ref.at[rows], _pack_rows(_dot(hid, wd_bf[slot])))

    @pl.when(jnp.logical_not(on))
    def _():
        ys_ref[...] = jnp.zeros_like(ys_ref)


def _expert_ffn(xs, tile_e, tile_on, w_gate, w_up, w_down):
    rows_per_step = GROUP_TILE * FFN_SLOTS
    packed = pl.BlockSpec((rows_per_step // SUBLANES, PACK_BLOCKS, SUBLANES, LANES),
                          lambda i, te, on: (i, 0, 0, 0))
    w_specs = []
    for slot in range(FFN_SLOTS):
        pick = lambda i, te, on, slot=slot: (te[FFN_SLOTS * i + slot], 0, 0)
        w_specs += [pl.BlockSpec((None, D_MODEL, EXPERT_HID), pick),
                    pl.BlockSpec((None, D_MODEL, EXPERT_HID), pick),
                    pl.BlockSpec((None, EXPERT_HID, D_MODEL), pick)]
    grid_spec = pltpu.PrefetchScalarGridSpec(
        num_scalar_prefetch=2,
        grid=(xs.shape[0] * SUBLANES // rows_per_step,),
        in_specs=[packed] + w_specs,
        out_specs=packed,
        scratch_shapes=[pltpu.VMEM((FFN_SLOTS, D_MODEL, EXPERT_HID), BF16),
                        pltpu.VMEM((FFN_SLOTS, D_MODEL, EXPERT_HID), BF16),
                        pltpu.VMEM((FFN_SLOTS, EXPERT_HID, D_MODEL), BF16)],
    )
    return pl.pallas_call(
        _expert_ffn_kernel,
        grid_spec=grid_spec,
        out_shape=jax.ShapeDtypeStruct(xs.shape, jnp.int32),
        compiler_params=_params("arbitrary"),
        name="expert_ffn",
    )(tile_e, tile_on, xs, *([w_gate, w_up, w_down] * FFN_SLOTS))


def _moe_out_kernel(x1_ref, yg_ref, gate_ref, sg_ref, su_ref, sd_ref, g_ref, b_ref, *rest):
    o_ref = rest[-1]
    x1 = x1_ref[...]
    xb = x1.astype(BF16)
    hid = jax.nn.silu(_dot(xb, sg_ref[...])) * _dot(xb, su_ref[...])
    shared = _dot(hid.astype(BF16), sd_ref[...])
    gates = gate_ref[...]
    lo_acc = shared[:, :PACK_COLS]
    hi_acc = shared[:, PACK_COLS:]
    for kk in range(TOP_K):
        lo, hi = _unpack_rows(_load_packed(yg_ref.at[kk]))
        w = gates[:, kk:kk + 1]
        lo_acc = lo_acc + w * lo
        hi_acc = hi_acc + w * hi
    moe = jnp.concatenate([lo_acc, hi_acc], axis=1)
    o_ref[...] = _layer_norm(DN_ALPHA * x1 + moe, g_ref[...], b_ref[...])


def _moe_out(x1, yg, gates, sh_gate, sh_up, sh_down, ln_g, ln_b, part, earlier):
    n_tok = x1.shape[0]
    tm = TOKEN_TILE
    steps = n_tok // tm // MOE_PARTS
    first = part * steps
    row = lambda w: pl.BlockSpec((tm, w), lambda i: (i + first, 0))
    in_specs = [row(D_MODEL),
                pl.BlockSpec((TOP_K, tm // SUBLANES, PACK_BLOCKS, SUBLANES, LANES),
                             lambda i: (0, i, 0, 0, 0)),
                row(LANES),
                _resident((D_MODEL, EXPERT_HID)), _resident((D_MODEL, EXPERT_HID)),
                _resident((EXPERT_HID, D_MODEL)), _resident((1, D_MODEL)),
                _resident((1, D_MODEL))]
    args = [x1, yg, gates, sh_gate.astype(BF16), sh_up.astype(BF16), sh_down.astype(BF16),
            ln_g[None, :], ln_b[None, :]]
    aliases = {}
    if earlier is not None:
        in_specs.append(pl.BlockSpec(memory_space=pl.ANY))
        args.append(earlier)
        aliases = {len(args) - 1: 0}
    return pl.pallas_call(
        _moe_out_kernel,
        grid=(steps,),
        in_specs=in_specs,
        out_specs=row(D_MODEL),
        out_shape=jax.ShapeDtypeStruct((n_tok, D_MODEL), F32),
        input_output_aliases=aliases,
        compiler_params=_params("arbitrary"),
        name="moe_out",
    )(*args)


def _slot_kernel(offset_ref, e_ref, rank_ref, pos_ref):
    e = e_ref[...]
    base = jnp.zeros(e.shape, jnp.int32)
    for ex in range(N_EXPERTS):
        base = jnp.where(e == ex, offset_ref[ex], base)
    pos_ref[...] = base + rank_ref[...]


def _slots(offset, e6, rank6):
    n_tok = e6.shape[1]
    tl = 4096
    lane_row = pl.BlockSpec((SUBLANES, tl), lambda i, off: (0, i))
    return pl.pallas_call(
        _slot_kernel,
        grid_spec=pltpu.PrefetchScalarGridSpec(
            num_scalar_prefetch=1, grid=(n_tok // tl,),
            in_specs=[lane_row, lane_row], out_specs=lane_row),
        out_shape=jax.ShapeDtypeStruct((SUBLANES, n_tok), jnp.int32),
        compiler_params=_params("arbitrary"),
        name="slots",
    )(offset, e6, rank6)


def _sc_mesh():
    return plsc.VectorSubcoreMesh(core_axis_name="c", subcore_axis_name="s")


def _sc_worker():
    return lax.axis_index("s") * SC_CORES + lax.axis_index("c")


def _sc_dispatch(x_sub, idx, n_out_sub):
    windows = x_sub.shape[0] // (SC_WORKERS * SC_WINDOW)

    @functools.partial(
        pl.kernel, mesh=_sc_mesh(),
        out_type=jax.ShapeDtypeStruct((n_out_sub, LANES), jnp.int32),
        scratch_types=[pltpu.VMEM((SC_WINDOW, LANES), jnp.int32),
                       pltpu.VMEM((TOP_K, SC_WINDOW), jnp.int32)],
        name="sc_dispatch")
    def run(x_hbm, idx_hbm, out_hbm, rows_v, idx_v):
        wid = _sc_worker()

        @pl.loop(0, windows)
        def _(c):
            base = (wid * windows + c) * SC_WINDOW
            pltpu.sync_copy(x_hbm.at[pl.ds(base, SC_WINDOW)], rows_v)
            pltpu.sync_copy(idx_hbm.at[wid, c], idx_v)
            for kk in range(TOP_K):
                pltpu.sync_copy(rows_v, out_hbm.at[idx_v.at[kk]])

    return run(x_sub, idx)


def _sc_combine(y_sub, idx, n_tok_sub):
    windows = n_tok_sub // (SC_WORKERS * SC_WINDOW)

    @functools.partial(
        pl.kernel, mesh=_sc_mesh(),
        out_type=jax.ShapeDtypeStruct((TOP_K, n_tok_sub, LANES), jnp.int32),
        scratch_types=[pltpu.VMEM((2, SC_WINDOW, LANES), jnp.int32),
                       pltpu.VMEM((TOP_K, SC_WINDOW), jnp.int32),
                       pltpu.SemaphoreType.DMA((2,)),
                       pltpu.SemaphoreType.DMA((2,))],
        name="sc_combine")
    def run(y_hbm, idx_hbm, out_hbm, rows_v, idx_v, gather_sem, write_sem):
        wid = _sc_worker()

        def gather(kk):
            buf = kk % 2
            return pltpu.async_copy(y_hbm.at[idx_v.at[kk]], rows_v.at[buf], gather_sem.at[buf])

        @pl.loop(0, windows)
        def _(c):
            base = (wid * windows + c) * SC_WINDOW
            pltpu.sync_copy(idx_hbm.at[wid, c], idx_v)
            writes = []
            pending = gather(0)
            for kk in range(TOP_K):
                pending.wait()
                if kk + 1 < TOP_K:
                    if kk >= 1:
                        writes[kk - 1].wait()
                    pending = gather(kk + 1)
                writes.append(pltpu.async_copy(
                    rows_v.at[kk % 2], out_hbm.at[kk, pl.ds(base, SC_WINDOW)],
                    write_sem.at[kk % 2]))
            writes[TOP_K - 2].wait()
            writes[TOP_K - 1].wait()

    return run(y_sub, idx)


def _moe(x1, x1p, e6, gates, rank6, counts, exp_w_gate, exp_w_up, exp_w_down,
         sh_w_gate, sh_w_up, sh_w_down, ln_g, ln_b):
    n_tok = x1.shape[0]
    tg = GROUP_TILE
    n_rows = n_tok * TOP_K + N_EXPERTS * tg
    n_tiles = n_rows // tg

    cnt = counts[:, 0].astype(jnp.int32)
    tiles_per_e = (cnt + tg - 1) // tg
    tile_end = jnp.cumsum(tiles_per_e)
    offset = (tile_end - tiles_per_e) * tg
    tile_ids = jnp.arange(n_tiles, dtype=jnp.int32)
    tile_on = (tile_ids < tile_end[-1]).astype(jnp.int32)
    tile_e = jnp.minimum(
        jnp.sum((tile_ids[:, None] >= tile_end[None, :]).astype(jnp.int32), axis=1),
        N_EXPERTS - 1)
    tile_e = jnp.where(tile_on > 0, tile_e, tile_e[jnp.maximum(tile_end[-1] - 1, 0)])

    pos = _slots(offset, e6, rank6)[:TOP_K]

    group = PACK_BLOCKS * SUBLANES
    sub0 = (pos // SUBLANES) * group + pos % SUBLANES
    lane = jnp.arange(SC_WINDOW)
    spread = (jnp.arange(group)[:, None]
              == (lane // group) * SUBLANES + lane % SUBLANES).astype(F32)
    idx = jnp.einsum("kws,sd->kwd", sub0.reshape(TOP_K, n_tok // group, group).astype(F32),
                     spread, precision=lax.Precision.HIGHEST).astype(jnp.int32)
    idx = idx + ((lane % group) // SUBLANES * SUBLANES).astype(jnp.int32)
    n_sub = n_tok * PACK_BLOCKS
    idx = idx.reshape(TOP_K, n_sub)

    def windowed(ix):
        windows = ix.shape[1] // (SC_WORKERS * SC_WINDOW)
        return ix.reshape(TOP_K, SC_WORKERS, windows, SC_WINDOW).transpose(1, 2, 0, 3)

    xs = _sc_dispatch(x1p.reshape(n_sub, LANES), windowed(idx), n_rows * PACK_BLOCKS)
    ys = _expert_ffn(xs.reshape(n_rows // SUBLANES, PACK_BLOCKS, SUBLANES, LANES),
                     tile_e, tile_on, exp_w_gate, exp_w_up, exp_w_down)
    ys = ys.reshape(n_rows * PACK_BLOCKS, LANES)
    part_sub = n_sub // MOE_PARTS
    out = None
    for part in range(MOE_PARTS):
        yg = _sc_combine(ys, windowed(idx[:, part * part_sub:(part + 1) * part_sub]), part_sub)
        yg = yg.reshape(TOP_K, part_sub // (PACK_BLOCKS * SUBLANES), PACK_BLOCKS,
                        SUBLANES, LANES)
        out = _moe_out(x1, yg, gates, sh_w_gate, sh_w_up, sh_w_down, ln_g, ln_b, part, out)
    return out


def kernel(x, positions,
           l0_w_in, l0_b_in, l0_conv_w, l0_conv_b, l0_conv_ln_g, l0_conv_ln_b,
           l0_q_norm_g, l0_w_uq, l0_kv_norm_g, l0_w_ukv, l0_w_out, l0_b_out,
           l0_ln1_g, l0_ln1_b,
           l0_router_w, l0_router_bias, l0_exp_w_gate, l0_exp_w_up, l0_exp_w_down,
           l0_sh_w_gate, l0_sh_w_up, l0_sh_w_down, l0_ln2_g, l0_ln2_b,
           l1_w_in, l1_b_in, l1_w_out, l1_b_out, l1_ln1_g, l1_ln1_b,
           l1_router_w, l1_router_bias, l1_exp_w_gate, l1_exp_w_up, l1_exp_w_down,
           l1_sh_w_gate, l1_sh_w_up, l1_sh_w_down, l1_ln2_g, l1_ln2_b):
    bsz, seq, d = x.shape
    n_tok = bsz * seq

    u, q, k, v = _l0_front(x, positions, l0_w_in, l0_b_in, l0_conv_w, l0_conv_b,
                           l0_conv_ln_g, l0_conv_ln_b, l0_q_norm_g, l0_w_uq,
                           l0_kv_norm_g, l0_w_ukv)
    o = _attention(q, k, v, None, packed=False)
    x_flat = x.reshape(n_tok, d)
    routing = _mix_out_router(u.reshape(n_tok, -1), o.reshape(n_tok, -1), x_flat,
                              l0_w_out, l0_b_out, l0_ln1_g, l0_ln1_b,
                              l0_router_w, l0_router_bias)
    x2 = _moe(*routing, l0_exp_w_gate, l0_exp_w_up, l0_exp_w_down,
              l0_sh_w_gate, l0_sh_w_up, l0_sh_w_down, l0_ln2_g, l0_ln2_b)

    q, k, v, cneg = _l1_front(x2.reshape(bsz, seq, d), l1_w_in, l1_b_in)
    o = _attention(q, k, v, cneg[:, :, None, :], packed=True).reshape(n_tok, -1)
    routing = _mix_out_router(o, o, x2, l1_w_out, l1_b_out, l1_ln1_g, l1_ln1_b,
                              l1_router_w, l1_router_bias)
    x3 = _moe(*routing, l1_exp_w_gate, l1_exp_w_up, l1_exp_w_down,
              l1_sh_w_gate, l1_sh_w_up, l1_sh_w_down, l1_ln2_g, l1_ln2_b)
    return x3.reshape(bsz, seq, d)
```

```python
import functools

import jax
import jax.numpy as jnp
from jax import lax
from jax.experimental import pallas as pl
from jax.experimental.pallas import tpu as pltpu
from jax.experimental.pallas import tpu_sc as plsc

D_MODEL = 1024
DEPTH = 2
DN_ALPHA = (2 * DEPTH) ** 0.25
LN_EPS = 1e-5
RMS_EPS = 1e-6

CONV_CH = 512
CONV_WIDTH = 31
MLA_HEADS = 8
QK_NOPE = 64
QK_ROPE = 32
V_DIM = 64
Q_LORA = 256
KV_LORA = 128
ROPE_THETA = 10000.0
FOX_HEADS = 16
FOX_HD = 64
N_EXPERTS = 64
TOP_K = 6
N_GROUPS = 8
TOPK_GROUPS = 4
EXPERT_HID = 256
ROUTED_SCALE = 2.5

LANES = 128
SUBLANES = 8
VMEM_LIMIT_BYTES = 56 * 1024 * 1024

SEQ_TILE = 512
TOKEN_TILE = 512
ATTN_TILE = 512
ATTN_KEY_TILE = 512
ATTN_UNROLL = 4
GROUP_TILE = 512
FFN_SLOTS = 2
CONV_ROWS = 32
CONV_HALO = 32
PAD_HEAD = 128

PACK_COLS = D_MODEL // 2
PACK_BLOCKS = PACK_COLS // LANES
SC_CORES = 2
SC_SUBCORES = 16
SC_WORKERS = SC_CORES * SC_SUBCORES
SC_WINDOW = 128
MOE_PARTS = 4
BF16 = jnp.bfloat16
F32 = jnp.float32
NEG_INF = float("-inf")
LOG2_E = 1.4426950408889634


def _params(*semantics):
    return pltpu.CompilerParams(dimension_semantics=semantics,
                                vmem_limit_bytes=VMEM_LIMIT_BYTES)


def _resident(shape):
    nd = len(shape)
    return pl.BlockSpec(shape, lambda *_: (0,) * nd)


def _dot(a, b):
    return jnp.dot(a, b, preferred_element_type=F32)


def _dot_nt(a, b, precision=None):
    return lax.dot_general(a, b, (((1,), (1,)), ((), ())),
                           precision=precision, preferred_element_type=F32)


def _layer_norm(x, g, b):
    mu = jnp.mean(x, axis=-1, keepdims=True)
    xc = x - mu
    var = jnp.mean(xc * xc, axis=-1, keepdims=True)
    return xc * lax.rsqrt(var + LN_EPS) * g + b


def _rms_norm(x, g):
    return x * lax.rsqrt(jnp.mean(x * x, axis=-1, keepdims=True) + RMS_EPS) * g


def _ones_upper_half(width):
    lane = lax.broadcasted_iota(jnp.int32, (1, width), 1)
    return jnp.where((lane & (PAD_HEAD - 1)) >= V_DIM, 1.0, 0.0).astype(F32)


def _pad_heads(w, heads):
    d = w.shape[-1] // heads
    w3 = w.reshape(w.shape[:-1] + (heads, d))
    pad = jnp.zeros(w.shape[:-1] + (heads, PAD_HEAD - d), w.dtype)
    return jnp.concatenate([w3, pad], axis=-1).reshape(w.shape[:-1] + (heads * PAD_HEAD,))


def _pack_rows(x):
    lo = lax.bitcast_convert_type(x[:, :PACK_COLS].astype(BF16).astype(F32), jnp.uint32)
    hi = lax.bitcast_convert_type(x[:, PACK_COLS:].astype(BF16).astype(F32), jnp.uint32)
    word = (lo >> 16) | (hi & jnp.uint32(0xFFFF0000))
    return lax.bitcast_convert_type(word, jnp.int32)


def _unpack_rows(words):
    u = lax.bitcast_convert_type(words, jnp.uint32)
    lo = lax.bitcast_convert_type(u << 16, F32)
    hi = lax.bitcast_convert_type(u & jnp.uint32(0xFFFF0000), F32)
    return lo, hi


def _store_packed(ref, words):
    groups = words.shape[0] // SUBLANES
    for cb in range(PACK_BLOCKS):
        ref[:, cb, :, :] = words[:, cb * LANES:(cb + 1) * LANES].reshape(groups, SUBLANES, LANES)


def _load_packed(ref):
    rows = ref.shape[0] * SUBLANES
    return jnp.concatenate(
        [ref[:, cb, :, :].reshape(rows, LANES) for cb in range(PACK_BLOCKS)], axis=1)


L0_A = 0
L0_G = CONV_CH
L0_Q = 2 * CONV_CH
L0_KV = L0_Q + Q_LORA
L0_KR = L0_KV + KV_LORA
L0_KRR = L0_KR + PAD_HEAD
L0_COLS = L0_KRR + PAD_HEAD


def _l0_front_kernel(x_ref, pos_ref, w_in_ref, b_in_ref, conv_w_ref, conv_b_ref,
                     cln_g_ref, cln_b_ref, qn_g_ref, wq_ref, wqr_ref, kvn_g_ref,
                     wk_ref, wv_ref, invf_ref,
                     u_ref, q_ref, k_ref, v_ref, ubuf):
    ts = x_ref.shape[0]

    @pl.when(pl.program_id(1) == 0)
    def _():
        ubuf[0:CONV_HALO, :] = jnp.zeros((CONV_HALO, CONV_CH), F32)

    h = _dot(x_ref[...].astype(BF16), w_in_ref[...]) + b_in_ref[...]

    ubuf[CONV_HALO:CONV_HALO + ts, :] = (
        h[:, L0_A:L0_A + CONV_CH] * jax.nn.sigmoid(h[:, L0_G:L0_G + CONV_CH]))
    first_tap = CONV_HALO - (CONV_WIDTH - 1)

    def conv_chunk(c, carry):
        base = pl.multiple_of(c * CONV_ROWS, CONV_ROWS)
        acc = jnp.broadcast_to(conv_b_ref[...], (CONV_ROWS, CONV_CH))
        for res in range(SUBLANES):
            rows = CONV_ROWS + (SUBLANES if res else 0)
            part = None
            for off in range(res, first_tap + CONV_WIDTH, SUBLANES):
                j = off - first_tap
                if j < 0:
                    continue
                window = ubuf[pl.ds(base + (off - res), rows), :]
                term = window.reshape(rows // SUBLANES, SUBLANES, CONV_CH) * conv_w_ref[j][None]
                part = term if part is None else part + term
            acc = acc + part.reshape(rows, CONV_CH)[res:res + CONV_ROWS, :]
        y = _layer_norm(acc, cln_g_ref[...], cln_b_ref[...])
        u_ref[pl.ds(base, CONV_ROWS), :] = (y * jax.nn.sigmoid(y)).astype(BF16)
        return carry

    lax.fori_loop(0, ts // CONV_ROWS, conv_chunk, 0, unroll=4)
    ubuf[0:CONV_HALO, :] = ubuf[ts:ts + CONV_HALO, :]

    ang = pos_ref[...].astype(F32) * invf_ref[...]
    cos = jnp.cos(ang)
    sin = jnp.sin(ang)
    scale = (QK_NOPE + QK_ROPE) ** -0.5 * LOG2_E

    qn = _rms_norm(h[:, L0_Q:L0_Q + Q_LORA], qn_g_ref[...]).astype(BF16)
    q = _dot(qn, wq_ref[...])
    q_rot = _dot(qn, wqr_ref[...])
    cos_s = cos * scale
    sin_s = sin * scale
    for hh in range(MLA_HEADS):
        blk = slice(hh * PAD_HEAD, (hh + 1) * PAD_HEAD)
        q_ref[:, blk] = (q[:, blk] * cos_s + q_rot[:, blk] * sin_s).astype(BF16)

    kvn = _rms_norm(h[:, L0_KV:L0_KV + KV_LORA], kvn_g_ref[...]).astype(BF16)
    k_nope = _dot(kvn, wk_ref[...])
    k_pe = h[:, L0_KR:L0_KR + PAD_HEAD] * cos + h[:, L0_KRR:L0_KRR + PAD_HEAD] * sin
    for hh in range(MLA_HEADS):
        blk = slice(hh * PAD_HEAD, (hh + 1) * PAD_HEAD)
        k_ref[:, blk] = (k_nope[:, blk] + k_pe).astype(BF16)
    v_ref[...] = (_dot(kvn, wv_ref[...]) + _ones_upper_half(v_ref.shape[1])).astype(BF16)


def _rope_rotate_cols(w):
    half = QK_ROPE // 2
    return jnp.concatenate([-w[..., half:], w[..., :half]], axis=-1)


def _l0_front(x, positions, w_in, b_in, conv_w, conv_b, cln_g, cln_b,
              qn_g, w_uq, kvn_g, w_ukv):
    bsz, seq, _ = x.shape
    pad_lo = jnp.zeros((D_MODEL, QK_NOPE), F32)
    pad_hi = jnp.zeros((D_MODEL, PAD_HEAD - QK_NOPE - QK_ROPE), F32)
    i3 = L0_KR
    w_kr = w_in[:, i3:i3 + QK_ROPE]
    w_in_p = jnp.concatenate(
        [w_in[:, :i3], pad_lo, w_kr, pad_hi, pad_lo, _rope_rotate_cols(w_kr), pad_hi],
        axis=1).astype(BF16)
    b_kr = b_in[i3:i3 + QK_ROPE]
    zlo = jnp.zeros((QK_NOPE,), F32)
    zhi = jnp.zeros((PAD_HEAD - QK_NOPE - QK_ROPE,), F32)
    b_in_p = jnp.concatenate(
        [b_in[:i3], zlo, b_kr, zhi, zlo, _rope_rotate_cols(b_kr), zhi])[None, :]

    dq = QK_NOPE + QK_ROPE
    wq3 = w_uq.reshape(Q_LORA, MLA_HEADS, dq)
    zq = jnp.zeros((Q_LORA, MLA_HEADS, PAD_HEAD - dq), F32)
    wq_p = jnp.concatenate([wq3, zq], axis=-1).reshape(Q_LORA, MLA_HEADS * PAD_HEAD)
    wqr_p = jnp.concatenate(
        [jnp.zeros((Q_LORA, MLA_HEADS, QK_NOPE), F32),
         _rope_rotate_cols(wq3[..., QK_NOPE:]), zq], axis=-1
    ).reshape(Q_LORA, MLA_HEADS * PAD_HEAD)
    wkv3 = w_ukv.reshape(KV_LORA, MLA_HEADS, QK_NOPE + V_DIM)
    wk_p = jnp.concatenate(
        [wkv3[..., :QK_NOPE], jnp.zeros((KV_LORA, MLA_HEADS, PAD_HEAD - QK_NOPE), F32)],
        axis=-1).reshape(KV_LORA, MLA_HEADS * PAD_HEAD)
    wv = _pad_heads(wkv3[..., QK_NOPE:].reshape(KV_LORA, MLA_HEADS * V_DIM), MLA_HEADS)

    inv_freq = 1.0 / (ROPE_THETA ** (jnp.arange(0, QK_ROPE, 2, dtype=F32) / QK_ROPE))
    invf = jnp.concatenate([jnp.zeros((QK_NOPE,), F32), inv_freq, inv_freq,
                            jnp.zeros((PAD_HEAD - QK_NOPE - QK_ROPE,), F32)])[None, :]

    ts = SEQ_TILE
    row = lambda w: pl.BlockSpec((None, ts, w), lambda b, s: (b, s, 0))
    qk_w = MLA_HEADS * PAD_HEAD
    v_w = MLA_HEADS * PAD_HEAD
    return pl.pallas_call(
        _l0_front_kernel,
        grid=(bsz, seq // ts),
        in_specs=[row(D_MODEL), row(1),
                  _resident((D_MODEL, L0_COLS)), _resident((1, L0_COLS)),
                  _resident((CONV_WIDTH, SUBLANES, CONV_CH)), _resident((1, CONV_CH)),
                  _resident((1, CONV_CH)), _resident((1, CONV_CH)),
                  _resident((1, Q_LORA)), _resident((Q_LORA, qk_w)),
                  _resident((Q_LORA, qk_w)), _resident((1, KV_LORA)),
                  _resident((KV_LORA, qk_w)), _resident((KV_LORA, v_w)),
                  _resident((1, PAD_HEAD))],
        out_specs=[row(CONV_CH), row(qk_w), row(qk_w), row(v_w)],
        out_shape=[jax.ShapeDtypeStruct((bsz, seq, CONV_CH), BF16),
                   jax.ShapeDtypeStruct((bsz, seq, qk_w), BF16),
                   jax.ShapeDtypeStruct((bsz, seq, qk_w), BF16),
                   jax.ShapeDtypeStruct((bsz, seq, v_w), BF16)],
        scratch_shapes=[pltpu.VMEM((ts + CONV_HALO, CONV_CH), F32)],
        compiler_params=_params("arbitrary", "arbitrary"),
        name="l0_front",
    )(x, positions[..., None], w_in_p, b_in_p,
      jnp.broadcast_to(conv_w[:, None, :], (CONV_WIDTH, SUBLANES, CONV_CH)), conv_b[None, :],
      cln_g[None, :], cln_b[None, :], qn_g[None, :], wq_p.astype(BF16),
      wqr_p.astype(BF16), kvn_g[None, :], wk_p.astype(BF16), wv.astype(BF16), invf)


def _l1_front_kernel(x_ref, w_ref, b_ref, wv_ref, bv_ref, wf_ref, bf_ref,
                     q_ref, k_ref, v_ref, cneg_ref, carry):
    ts = x_ref.shape[0]

    @pl.when(pl.program_id(1) == 0)
    def _():
        carry[...] = jnp.zeros_like(carry)

    xb = x_ref[...].astype(BF16)
    mix = FOX_HEADS * FOX_HD
    scale = FOX_HD ** -0.5 * LOG2_E
    q_ref[...] = ((_dot(xb, w_ref[:, 0:mix]) + b_ref[:, 0:mix]) * scale).astype(BF16)
    k_ref[...] = (_dot(xb, w_ref[:, mix:2 * mix]) + b_ref[:, mix:2 * mix]).astype(BF16)
    v = _dot(xb, wv_ref[...]) + bv_ref[...]
    head_lanes = lax.broadcasted_iota(jnp.int32, (ts, LANES), 1) < FOX_HD
    for pair in range(FOX_HEADS // 2):
        both = v[:, pair * LANES:(pair + 1) * LANES]
        swapped = pltpu.roll(both, LANES - FOX_HD, axis=1)
        for head, vals in enumerate((both, swapped)):
            blk = 2 * pair + head
            v_ref[:, blk * PAD_HEAD:(blk + 1) * PAD_HEAD] = (
                jnp.where(head_lanes, vals, 1.0).astype(BF16))

    log_f = jax.nn.log_sigmoid(_dot_nt(wf_ref[...], xb) + bf_ref[...])
    r = lax.broadcasted_iota(jnp.int32, (ts, ts), 0)
    c = lax.broadcasted_iota(jnp.int32, (ts, ts), 1)
    upper = (r <= c).astype(F32)
    csum = jnp.dot(log_f, upper, precision=lax.Precision.HIGHEST,
                   preferred_element_type=F32) + carry[...]
    cneg_ref[...] = csum * -LOG2_E
    carry[...] = carry[...] + jnp.sum(log_f, axis=1, keepdims=True)


def _l1_front(x, w_in, b_in):
    bsz, seq, _ = x.shape
    mix = FOX_HEADS * FOX_HD
    ts = SEQ_TILE
    row = lambda w: pl.BlockSpec((None, ts, w), lambda b, s: (b, s, 0))
    w_qk = w_in[:, :2 * mix].astype(BF16)
    b_qk = b_in[None, :2 * mix]
    w_v = w_in[:, 2 * mix:3 * mix].astype(BF16)
    b_v = b_in[None, 2 * mix:3 * mix]
    v_w = FOX_HEADS * PAD_HEAD
    wf_t = w_in[:, 3 * mix:].T.astype(BF16)
    bf_t = b_in[3 * mix:][:, None]
    return pl.pallas_call(
        _l1_front_kernel,
        grid=(bsz, seq // ts),
        in_specs=[row(D_MODEL), _resident((D_MODEL, 2 * mix)), _resident((1, 2 * mix)),
                  _resident((D_MODEL, mix)), _resident((1, mix)),
                  _resident((FOX_HEADS, D_MODEL)), _resident((FOX_HEADS, 1))],
        out_specs=[row(mix), row(mix), row(v_w),
                   pl.BlockSpec((None, FOX_HEADS, ts), lambda b, s: (b, 0, s))],
        out_shape=[jax.ShapeDtypeStruct((bsz, seq, mix), BF16)] * 2
        + [jax.ShapeDtypeStruct((bsz, seq, v_w), BF16),
           jax.ShapeDtypeStruct((bsz, FOX_HEADS, seq), F32)],
        scratch_shapes=[pltpu.VMEM((FOX_HEADS, 1), F32)],
        compiler_params=_params("arbitrary", "arbitrary"),
        name="l1_front",
    )(x, w_qk, b_qk, w_v, b_v, wf_t, bf_t)


def _attn_kernel(*refs, packed, has_bias):
    if has_bias:
        q_ref, k_ref, v_ref, cneg_ref, o_ref = refs
    else:
        q_ref, k_ref, v_ref, o_ref = refs
        cneg_ref = None
    tq = ATTN_TILE
    tk = ATTN_KEY_TILE
    half = LANES // 2

    def query_tile(qi):
        q_rows = pl.ds(pl.multiple_of(qi * tq, tq), tq)
        q2 = q_ref[q_rows, :]
        if packed:
            lane = lax.broadcasted_iota(jnp.int32, q2.shape, 1)
            zero = jnp.zeros_like(q2)
            qs = (jnp.where(lane < half, q2, zero), jnp.where(lane >= half, q2, zero))
        else:
            qs = (q2[:, :PAD_HEAD], q2[:, PAD_HEAD:])

        def k_block(kj, head):
            rows = pl.ds(pl.multiple_of(kj * tk, tk), tk)
            if packed:
                return k_ref[rows, :]
            return k_ref[rows, head * PAD_HEAD:(head + 1) * PAD_HEAD]

        def step(kj, carry, diag):
            rows = pl.ds(pl.multiple_of(kj * tk, tk), tk)
            out = []
            for head in range(2):
                m, acc = carry[head]
                s = _dot_nt(qs[head], k_block(kj, head))
                if has_bias:
                    s = s + cneg_ref[head, :, rows]
                if diag is not None:
                    r = lax.broadcasted_iota(jnp.int32, s.shape, 0)
                    c = lax.broadcasted_iota(jnp.int32, s.shape, 1)
                    s = jnp.where(r >= c + diag * tk, s, NEG_INF)
                m_new = jnp.maximum(m, jnp.max(s, axis=-1, keepdims=True))
                alpha = jnp.exp2(m - m_new)
                p = jnp.exp2(s - m_new).astype(BF16)
                vblk = v_ref[rows, head * PAD_HEAD:(head + 1) * PAD_HEAD]
                out.append((m_new, alpha * acc + _dot(p, vblk)))
            return tuple(out)

        init = tuple((jnp.full((tq, 1), NEG_INF, F32), jnp.zeros((tq, LANES), F32))
                     for _ in range(2))
        per_tile = tq // tk
        n_full = qi * per_tile

        def group(first, c, size):
            for t in range(size):
                c = step(first + t, c, None)
            return c

        carry = lax.fori_loop(0, n_full // ATTN_UNROLL,
                              lambda j, c: group(ATTN_UNROLL * j, c, ATTN_UNROLL), init)
        done = (n_full // ATTN_UNROLL) * ATTN_UNROLL
        size = ATTN_UNROLL // 2
        while size >= 1:
            take = (n_full - done) >= size
            carry = lax.cond(take, lambda c, f=done, n=size: group(f, c, n), lambda c: c,
                             carry)
            done = done + jnp.where(take, size, 0)
            size //= 2
        for d in range(per_tile):
            carry = step(n_full + d, carry, d)
        (_, acc_a), (_, acc_b) = carry
        out_a = acc_a / pltpu.roll(acc_a, half, axis=1)
        out_b = acc_b / pltpu.roll(acc_b, half, axis=1)
        lane_o = lax.broadcasted_iota(jnp.int32, (tq, LANES), 1)
        o_ref[q_rows, :] = jnp.where(lane_o < half, out_a,
                                     pltpu.roll(out_b, half, axis=1)).astype(o_ref.dtype)

    pl.loop(0, q_ref.shape[0] // tq)(query_tile)


def _attention(q, k, v, cneg, packed):
    bsz, seq, v_w = v.shape
    pairs = v_w // (2 * PAD_HEAD)
    qk_w = q.shape[-1] // pairs
    in_specs = [pl.BlockSpec((None, seq, qk_w), lambda b, p: (b, 0, p)),
                pl.BlockSpec((None, seq, qk_w), lambda b, p: (b, 0, p)),
                pl.BlockSpec((None, seq, 2 * PAD_HEAD), lambda b, p: (b, 0, p))]
    args = [q, k, v]
    if cneg is not None:
        in_specs.append(pl.BlockSpec((None, 2, 1, seq), lambda b, p: (b, p, 0, 0)))
        args.append(cneg)
    return pl.pallas_call(
        functools.partial(_attn_kernel, packed=packed, has_bias=cneg is not None),
        grid=(bsz, pairs),
        in_specs=in_specs,
        out_specs=pl.BlockSpec((None, seq, LANES), lambda b, p: (b, 0, p)),
        out_shape=jax.ShapeDtypeStruct((bsz, seq, pairs * LANES), BF16),
        compiler_params=_params("arbitrary", "arbitrary"),
        name="attn_packed" if packed else "attn_padded",
    )(*args)


def _first_hit(hits, found):
    out = []
    for hcur in hits:
        take = jnp.logical_and(hcur, jnp.logical_not(found))
        found = jnp.logical_or(found, take)
        out.append(take)
    return out, found


def _mix_out_router_kernel(a_ref, b_ref, x_ref, wa_ref, wb_ref, bo_ref, g_ref, be_ref,
                           rw_ref, rwl_ref, rb_ref,
                           x1_ref, x1p_ref, e_ref, gate_ref, rank_ref, cnt_ref, cnt, gate_t):
    tm = x_ref.shape[0]

    @pl.when(pl.program_id(0) == 0)
    def _():
        cnt[...] = jnp.zeros_like(cnt)
        gate_t[...] = jnp.zeros_like(gate_t)

    mix = _dot(a_ref[...], wa_ref[...]) + _dot(b_ref[...], wb_ref[...]) + bo_ref[...]
    x1 = _layer_norm(DN_ALPHA * x_ref[...] + mix, g_ref[...], be_ref[...])
    x1_ref[...] = x1
    _store_packed(x1p_ref, _pack_rows(x1))

    x_hi = x1.astype(BF16)
    x_lo = (x1 - x_hi.astype(F32)).astype(BF16)
    logits = (_dot_nt(rw_ref[...], x_hi) + _dot_nt(rw_ref[...], x_lo)
              + _dot_nt(rwl_ref[...], x_hi))
    aff = jax.nn.sigmoid(logits)
    choice = aff + rb_ref[...]
    per_group = N_EXPERTS // N_GROUPS
    sub = lax.broadcasted_iota(jnp.int32, (per_group, tm), 0)
    groups = [choice[g * per_group:(g + 1) * per_group, :] for g in range(N_GROUPS)]

    gscore = []
    for cg in groups:
        m1 = jnp.max(cg, axis=0, keepdims=True)
        i1 = jnp.min(jnp.where(cg == m1, sub, per_group), axis=0, keepdims=True)
        m2 = jnp.max(jnp.where(sub == i1, NEG_INF, cg), axis=0, keepdims=True)
        gscore.append(m1 + m2)

    gsel = [jnp.zeros((1, tm), jnp.bool_) for _ in range(N_GROUPS)]
    for _ in range(TOPK_GROUPS):
        best = functools.reduce(jnp.maximum, gscore)
        takes, _ = _first_hit([gs == best for gs in gscore], jnp.zeros((1, tm), jnp.bool_))
        gsel = [jnp.logical_or(a, t) for a, t in zip(gsel, takes)]
        gscore = [jnp.where(t, NEG_INF, gs) for gs, t in zip(gscore, takes)]

    masked = [jnp.where(gs, cg, NEG_INF) for gs, cg in zip(gsel, groups)]
    eid = [sub + g * per_group for g in range(N_GROUPS)]
    affs = [aff[g * per_group:(g + 1) * per_group, :] for g in range(N_GROUPS)]
    sel = [jnp.zeros((per_group, tm), jnp.bool_) for _ in range(N_GROUPS)]
    picked_e, picked_w = [], []
    for _ in range(TOP_K):
        best = jnp.max(functools.reduce(jnp.maximum, masked), axis=0, keepdims=True)
        cand = [jnp.where(mg == best, ig, N_EXPERTS) for mg, ig in zip(masked, eid)]
        idx = jnp.min(functools.reduce(jnp.minimum, cand), axis=0, keepdims=True)
        onehot = [ig == idx for ig in eid]
        w = functools.reduce(
            jnp.add, [jnp.sum(jnp.where(oh, ag, 0.0), axis=0, keepdims=True)
                      for oh, ag in zip(onehot, affs)])
        picked_e.append(idx)
        picked_w.append(w)
        sel = [jnp.logical_or(sg, oh) for sg, oh in zip(sel, onehot)]
        masked = [jnp.where(oh, NEG_INF, mg) for mg, oh in zip(masked, onehot)]

    wsum = functools.reduce(jnp.add, picked_w)

    sel_f = jnp.concatenate([sg.astype(F32) for sg in sel], axis=0)
    r = lax.broadcasted_iota(jnp.int32, (tm, tm), 0)
    c = lax.broadcasted_iota(jnp.int32, (tm, tm), 1)
    before = _dot(sel_f.astype(BF16), (r < c).astype(BF16)) + cnt[...]
    eall = lax.broadcasted_iota(jnp.int32, (N_EXPERTS, tm), 0)
    for kk in range(TOP_K):
        rank = jnp.sum(jnp.where(eall == picked_e[kk], before, 0.0), axis=0, keepdims=True)
        e_ref[kk:kk + 1, :] = picked_e[kk]
        gate_t[kk:kk + 1, :] = picked_w[kk] / wsum * ROUTED_SCALE
        rank_ref[kk:kk + 1, :] = rank.astype(jnp.int32)
    pad = SUBLANES - TOP_K
    e_ref[TOP_K:, :] = jnp.zeros((pad, tm), jnp.int32)
    rank_ref[TOP_K:, :] = jnp.zeros((pad, tm), jnp.int32)
    gate_ref[...] = gate_t[...].T
    cnt[...] = cnt[...] + jnp.sum(sel_f, axis=1, keepdims=True)
    cnt_ref[...] = jnp.broadcast_to(cnt[...], cnt_ref.shape)


def _mix_out_router(a, b, x, w_out, b_out, ln_g, ln_b, router_w, router_bias):
    n_tok = x.shape[0]
    tm = TOKEN_TILE
    half = w_out.shape[0] // 2
    row = lambda w: pl.BlockSpec((tm, w), lambda i: (i, 0))
    b_spec = (pl.BlockSpec((tm, half), lambda i: (i, 1)) if b is a
              else pl.BlockSpec((tm, half), lambda i: (i, 0)))
    lane_row = pl.BlockSpec((SUBLANES, tm), lambda i: (0, i))
    packed = pl.BlockSpec((tm // SUBLANES, PACK_BLOCKS, SUBLANES, LANES),
                          lambda i: (i, 0, 0, 0))
    w_bf = w_out.astype(BF16)
    rw_t = router_w.T
    rw_hi = rw_t.astype(BF16)
    rw_lo = (rw_t - rw_hi.astype(F32)).astype(BF16)
    return pl.pallas_call(
        _mix_out_router_kernel,
        grid=(n_tok // tm,),
        in_specs=[row(half), b_spec, row(D_MODEL),
                  pl.BlockSpec((half, D_MODEL), lambda i: (0, 0)),
                  pl.BlockSpec((half, D_MODEL), lambda i: (1, 0)),
                  _resident((1, D_MODEL)), _resident((1, D_MODEL)), _resident((1, D_MODEL)),
                  _resident((N_EXPERTS, D_MODEL)), _resident((N_EXPERTS, D_MODEL)),
                  _resident((N_EXPERTS, 1))],
        out_specs=[row(D_MODEL), packed, lane_row, row(LANES), lane_row,
                   _resident((N_EXPERTS, LANES))],
        out_shape=[jax.ShapeDtypeStruct((n_tok, D_MODEL), F32),
                   jax.ShapeDtypeStruct((n_tok // SUBLANES, PACK_BLOCKS, SUBLANES, LANES),
                                        jnp.int32),
                   jax.ShapeDtypeStruct((SUBLANES, n_tok), jnp.int32),
                   jax.ShapeDtypeStruct((n_tok, LANES), F32),
                   jax.ShapeDtypeStruct((SUBLANES, n_tok), jnp.int32),
                   jax.ShapeDtypeStruct((N_EXPERTS, LANES), F32)],
        scratch_shapes=[pltpu.VMEM((N_EXPERTS, 1), F32), pltpu.VMEM((LANES, tm), F32)],
        compiler_params=_params("arbitrary"),
        name="mix_out_router",
    )(a, b, x, w_bf, w_bf, b_out[None, :], ln_g[None, :], ln_b[None, :],
      rw_hi, rw_lo, router_bias[:, None])


def _expert_ffn_kernel(tile_e_ref, tile_on_ref, xs_ref, *refs):
    w_refs = refs[:3 * FFN_SLOTS]
    ys_ref = refs[3 * FFN_SLOTS]
    wg_bf, wu_bf, wd_bf = refs[3 * FFN_SLOTS + 1:]
    i = pl.program_id(0)
    sub = GROUP_TILE // SUBLANES

    for slot in range(FFN_SLOTS):
        t = FFN_SLOTS * i + slot
        fresh = jnp.logical_or(
            i == 0, tile_e_ref[t] != tile_e_ref[jnp.maximum(t - FFN_SLOTS, 0)])

        @pl.when(fresh)
        def _(slot=slot):
            wg_ref, wu_ref, wd_ref = w_refs[3 * slot:3 * slot + 3]
            wg_bf[slot] = wg_ref[...].astype(BF16)
            wu_bf[slot] = wu_ref[...].astype(BF16)
            wd_bf[slot] = wd_ref[...].astype(BF16)

    on = tile_on_ref[FFN_SLOTS * i] > 0

    @pl.when(on)
    def _():
        for slot in range(FFN_SLOTS):
            rows = pl.ds(slot * sub, sub)
            lo, hi = _unpack_rows(_load_packed(xs_ref.at[rows]))
            lo = lo.astype(BF16)
            hi = hi.astype(BF16)
            gate = (_dot(lo, wg_bf[slot, :PACK_COLS, :]) + _dot(hi, wg_bf[slot, PACK_COLS:, :]))
            up = (_dot(lo, wu_bf[slot, :PACK_COLS, :]) + _dot(hi, wu_bf[slot, PACK_COLS:, :]))
            hid = (jax.nn.silu(gate) * up).astype(BF16)
            _store_packed(ys_ref.at[rows], _pack_rows(_dot(hid, wd_bf[slot])))

    @pl.when(jnp.logical_not(on))
    def _():
        ys_ref[...] = jnp.zeros_like(ys_ref)


def _expert_ffn(xs, tile_e, tile_on, w_gate, w_up, w_down):
    rows_per_step = GROUP_TILE * FFN_SLOTS
    packed = pl.BlockSpec((rows_per_step // SUBLANES, PACK_BLOCKS, SUBLANES, LANES),
                          lambda i, te, on: (i, 0, 0, 0))
    w_specs = []
    for slot in range(FFN_SLOTS):
        pick = lambda i, te, on, slot=slot: (te[FFN_SLOTS * i + slot], 0, 0)
        w_specs += [pl.BlockSpec((None, D_MODEL, EXPERT_HID), pick),
                    pl.BlockSpec((None, D_MODEL, EXPERT_HID), pick),
                    pl.BlockSpec((None, EXPERT_HID, D_MODEL), pick)]
    grid_spec = pltpu.PrefetchScalarGridSpec(
        num_scalar_prefetch=2,
        grid=(xs.shape[0] * SUBLANES // rows_per_step,),
        in_specs=[packed] + w_specs,
        out_specs=packed,
        scratch_shapes=[pltpu.VMEM((FFN_SLOTS, D_MODEL, EXPERT_HID), BF16),
                        pltpu.VMEM((FFN_SLOTS, D_MODEL, EXPERT_HID), BF16),
                        pltpu.VMEM((FFN_SLOTS, EXPERT_HID, D_MODEL), BF16)],
    )
    return pl.pallas_call(
        _expert_ffn_kernel,
        grid_spec=grid_spec,
        out_shape=jax.ShapeDtypeStruct(xs.shape, jnp.int32),
        compiler_params=_params("arbitrary"),
        name="expert_ffn",
    )(tile_e, tile_on, xs, *([w_gate, w_up, w_down] * FFN_SLOTS))


def _moe_out_kernel(x1_ref, yg_ref, gate_ref, sg_ref, su_ref, sd_ref, g_ref, b_ref, *rest):
    o_ref = rest[-1]
    x1 = x1_ref[...]
    xb = x1.astype(BF16)
    hid = jax.nn.silu(_dot(xb, sg_ref[...])) * _dot(xb, su_ref[...])
    shared = _dot(hid.astype(BF16), sd_ref[...])
    gates = gate_ref[...]
    lo_acc = shared[:, :PACK_COLS]
    hi_acc = shared[:, PACK_COLS:]
    for kk in range(TOP_K):
        lo, hi = _unpack_rows(_load_packed(yg_ref.at[kk]))
        w = gates[:, kk:kk + 1]
        lo_acc = lo_acc + w * lo
        hi_acc = hi_acc + w * hi
    moe = jnp.concatenate([lo_acc, hi_acc], axis=1)
    o_ref[...] = _layer_norm(DN_ALPHA * x1 + moe, g_ref[...], b_ref[...])


def _moe_out(x1, yg, gates, sh_gate, sh_up, sh_down, ln_g, ln_b, part, earlier):
    n_tok = x1.shape[0]
    tm = TOKEN_TILE
    steps = n_tok // tm // MOE_PARTS
    first = part * steps
    row = lambda w: pl.BlockSpec((tm, w), lambda i: (i + first, 0))
    in_specs = [row(D_MODEL),
                pl.BlockSpec((TOP_K, tm // SUBLANES, PACK_BLOCKS, SUBLANES, LANES),
                             lambda i: (0, i, 0, 0, 0)),
                row(LANES),
                _resident((D_MODEL, EXPERT_HID)), _resident((D_MODEL, EXPERT_HID)),
                _resident((EXPERT_HID, D_MODEL)), _resident((1, D_MODEL)),
                _resident((1, D_MODEL))]
    args = [x1, yg, gates, sh_gate.astype(BF16), sh_up.astype(BF16), sh_down.astype(BF16),
            ln_g[None, :], ln_b[None, :]]
    aliases = {}
    if earlier is not None:
        in_specs.append(pl.BlockSpec(memory_space=pl.ANY))
        args.append(earlier)
        aliases = {len(args) - 1: 0}
    return pl.pallas_call(
        _moe_out_kernel,
        grid=(steps,),
        in_specs=in_specs,
        out_specs=row(D_MODEL),
        out_shape=jax.ShapeDtypeStruct((n_tok, D_MODEL), F32),
        input_output_aliases=aliases,
        compiler_params=_params("arbitrary"),
        name="moe_out",
    )(*args)


def _slot_kernel(offset_ref, e_ref, rank_ref, pos_ref):
    e = e_ref[...]
    base = jnp.zeros(e.shape, jnp.int32)
    for ex in range(N_EXPERTS):
        base = jnp.where(e == ex, offset_ref[ex], base)
    pos_ref[...] = base + rank_ref[...]


def _slots(offset, e6, rank6):
    n_tok = e6.shape[1]
    tl = 4096
    lane_row = pl.BlockSpec((SUBLANES, tl), lambda i, off: (0, i))
    return pl.pallas_call(
        _slot_kernel,
        grid_spec=pltpu.PrefetchScalarGridSpec(
            num_scalar_prefetch=1, grid=(n_tok // tl,),
            in_specs=[lane_row, lane_row], out_specs=lane_row),
        out_shape=jax.ShapeDtypeStruct((SUBLANES, n_tok), jnp.int32),
        compiler_params=_params("arbitrary"),
        name="slots",
    )(offset, e6, rank6)


def _sc_mesh():
    return plsc.VectorSubcoreMesh(core_axis_name="c", subcore_axis_name="s")


def _sc_worker():
    return lax.axis_index("s") * SC_CORES + lax.axis_index("c")


def _sc_dispatch(x_sub, idx, n_out_sub):
    windows = x_sub.shape[0] // (SC_WORKERS * SC_WINDOW)

    @functools.partial(
        pl.kernel, mesh=_sc_mesh(),
        out_type=jax.ShapeDtypeStruct((n_out_sub, LANES), jnp.int32),
        scratch_types=[pltpu.VMEM((SC_WINDOW, LANES), jnp.int32),
                       pltpu.VMEM((TOP_K, SC_WINDOW), jnp.int32)],
        name="sc_dispatch")
    def run(x_hbm, idx_hbm, out_hbm, rows_v, idx_v):
        wid = _sc_worker()

        @pl.loop(0, windows)
        def _(c):
            base = (wid * windows + c) * SC_WINDOW
            pltpu.sync_copy(x_hbm.at[pl.ds(base, SC_WINDOW)], rows_v)
            pltpu.sync_copy(idx_hbm.at[wid, c], idx_v)
            for kk in range(TOP_K):
                pltpu.sync_copy(rows_v, out_hbm.at[idx_v.at[kk]])

    return run(x_sub, idx)


def _sc_combine(y_sub, idx, n_tok_sub):
    windows = n_tok_sub // (SC_WORKERS * SC_WINDOW)

    @functools.partial(
        pl.kernel, mesh=_sc_mesh(),
        out_type=jax.ShapeDtypeStruct((TOP_K, n_tok_sub, LANES), jnp.int32),
        scratch_types=[pltpu.VMEM((2, SC_WINDOW, LANES), jnp.int32),
                       pltpu.VMEM((TOP_K, SC_WINDOW), jnp.int32),
                       pltpu.SemaphoreType.DMA((2,)),
                       pltpu.SemaphoreType.DMA((2,))],
        name="sc_combine")
    def run(y_hbm, idx_hbm, out_hbm, rows_v, idx_v, gather_sem, write_sem):
        wid = _sc_worker()

        def gather(kk):
            buf = kk % 2
            return pltpu.async_copy(y_hbm.at[idx_v.at[kk]], rows_v.at[buf], gather_sem.at[buf])

        @pl.loop(0, windows)
        def _(c):
            base = (wid * windows + c) * SC_WINDOW
            pltpu.sync_copy(idx_hbm.at[wid, c], idx_v)
            writes = []
            pending = gather(0)
            for kk in range(TOP_K):
                pending.wait()
                if kk + 1 < TOP_K:
                    if kk >= 1:
                        writes[kk - 1].wait()
                    pending = gather(kk + 1)
                writes.append(pltpu.async_copy(
                    rows_v.at[kk % 2], out_hbm.at[kk, pl.ds(base, SC_WINDOW)],
                    write_sem.at[kk % 2]))
            writes[TOP_K - 2].wait()
            writes[TOP_K - 1].wait()

    return run(y_sub, idx)


def _moe(x1, x1p, e6, gates, rank6, counts, exp_w_gate, exp_w_up, exp_w_down,
         sh_w_gate, sh_w_up, sh_w_down, ln_g, ln_b):
    n_tok = x1.shape[0]
    tg = GROUP_TILE
    n_rows = n_tok * TOP_K + N_EXPERTS * tg
    n_tiles = n_rows // tg

    cnt = counts[:, 0].astype(jnp.int32)
    tiles_per_e = (cnt + tg - 1) // tg
    tile_end = jnp.cumsum(tiles_per_e)
    offset = (tile_end - tiles_per_e) * tg
    tile_ids = jnp.arange(n_tiles, dtype=jnp.int32)
    tile_on = (tile_ids < tile_end[-1]).astype(jnp.int32)
    tile_e = jnp.minimum(
        jnp.sum((tile_ids[:, None] >= tile_end[None, :]).astype(jnp.int32), axis=1),
        N_EXPERTS - 1)
    tile_e = jnp.where(tile_on > 0, tile_e, tile_e[jnp.maximum(tile_end[-1] - 1, 0)])

    pos = _slots(offset, e6, rank6)[:TOP_K]

    group = PACK_BLOCKS * SUBLANES
    sub0 = (pos // SUBLANES) * group + pos % SUBLANES
    lane = jnp.arange(SC_WINDOW)
    spread = (jnp.arange(group)[:, None]
              == (lane // group) * SUBLANES + lane % SUBLANES).astype(F32)
    idx = jnp.einsum("kws,sd->kwd", sub0.reshape(TOP_K, n_tok // group, group).astype(F32),
                     spread, precision=lax.Precision.HIGHEST).astype(jnp.int32)
    idx = idx + ((lane % group) // SUBLANES * SUBLANES).astype(jnp.int32)
    n_sub = n_tok * PACK_BLOCKS
    idx = idx.reshape(TOP_K, n_sub)

    def windowed(ix):
        windows = ix.shape[1] // (SC_WORKERS * SC_WINDOW)
        return ix.reshape(TOP_K, SC_WORKERS, windows, SC_WINDOW).transpose(1, 2, 0, 3)

    xs = _sc_dispatch(x1p.reshape(n_sub, LANES), windowed(idx), n_rows * PACK_BLOCKS)
    ys = _expert_ffn(xs.reshape(n_rows // SUBLANES, PACK_BLOCKS, SUBLANES, LANES),
                     tile_e, tile_on, exp_w_gate, exp_w_up, exp_w_down)
    ys = ys.reshape(n_rows * PACK_BLOCKS, LANES)
    part_sub = n_sub // MOE_PARTS
    out = None
    for part in range(MOE_PARTS):
        yg = _sc_combine(ys, windowed(idx[:, part * part_sub:(part + 1) * part_sub]), part_sub)
        yg = yg.reshape(TOP_K, part_sub // (PACK_BLOCKS * SUBLANES), PACK_BLOCKS,
                        SUBLANES, LANES)
        out = _moe_out(x1, yg, gates, sh_w_gate, sh_w_up, sh_w_down, ln_g, ln_b, part, out)
    return out


def kernel(x, positions,
           l0_w_in, l0_b_in, l0_conv_w, l0_conv_b, l0_conv_ln_g, l0_conv_ln_b,
           l0_q_norm_g, l0_w_uq, l0_kv_norm_g, l0_w_ukv, l0_w_out, l0_b_out,
           l0_ln1_g, l0_ln1_b,
           l0_router_w, l0_router_bias, l0_exp_w_gate, l0_exp_w_up, l0_exp_w_down,
           l0_sh_w_gate, l0_sh_w_up, l0_sh_w_down, l0_ln2_g, l0_ln2_b,
           l1_w_in, l1_b_in, l1_w_out, l1_b_out, l1_ln1_g, l1_ln1_b,
           l1_router_w, l1_router_bias, l1_exp_w_gate, l1_exp_w_up, l1_exp_w_down,
           l1_sh_w_gate, l1_sh_w_up, l1_sh_w_down, l1_ln2_g, l1_ln2_b):
    bsz, seq, d = x.shape
    n_tok = bsz * seq

    u, q, k, v = _l0_front(x, positions, l0_w_in, l0_b_in, l0_conv_w, l0_conv_b,
                           l0_conv_ln_g, l0_conv_ln_b, l0_q_norm_g, l0_w_uq,
                           l0_kv_norm_g, l0_w_ukv)
    o = _attention(q, k, v, None, packed=False)
    x_flat = x.reshape(n_tok, d)
    routing = _mix_out_router(u.reshape(n_tok, -1), o.reshape(n_tok, -1), x_flat,
                              l0_w_out, l0_b_out, l0_ln1_g, l0_ln1_b,
                              l0_router_w, l0_router_bias)
    x2 = _moe(*routing, l0_exp_w_gate, l0_exp_w_up, l0_exp_w_down,
              l0_sh_w_gate, l0_sh_w_up, l0_sh_w_down, l0_ln2_g, l0_ln2_b)

    q, k, v, cneg = _l1_front(x2.reshape(bsz, seq, d), l1_w_in, l1_b_in)
    o = _attention(q, k, v, cneg[:, :, None, :], packed=True).reshape(n_tok, -1)
    routing = _mix_out_router(o, o, x2, l1_w_out, l1_b_out, l1_ln1_g, l1_ln1_b,
                              l1_router_w, l1_router_bias)
    x3 = _moe(*routing, l1_exp_w_gate, l1_exp_w_up, l1_exp_w_down,
              l1_sh_w_gate, l1_sh_w_up, l1_sh_w_down, l1_ln2_g, l1_ln2_b)
    return x3.reshape(bsz, seq, d)
```

```python
import functools

import jax
import jax.numpy as jnp
from jax import lax
from jax.experimental import pallas as pl
from jax.experimental.pallas import tpu as pltpu
from jax.experimental.pallas import tpu_sc as plsc

D_MODEL = 1024
DEPTH = 2
DN_ALPHA = (2 * DEPTH) ** 0.25
LN_EPS = 1e-5
RMS_EPS = 1e-6

CONV_CH = 512
CONV_WIDTH = 31
MLA_HEADS = 8
QK_NOPE = 64
QK_ROPE = 32
V_DIM = 64
Q_LORA = 256
KV_LORA = 128
ROPE_THETA = 10000.0
FOX_HEADS = 16
FOX_HD = 64
N_EXPERTS = 64
TOP_K = 6
N_GROUPS = 8
TOPK_GROUPS = 4
EXPERT_HID = 256
ROUTED_SCALE = 2.5

LANES = 128
SUBLANES = 8
V7X_VMEM_BYTES = 64 * 1024 * 1024
VMEM_LIMIT_BYTES = V7X_VMEM_BYTES * 7 // 8

SEQ_TILE = 512
TOKEN_TILE = 512
SLOT_TILE = 4096
ATTN_TILE = 1024
ATTN_KEY_TILE = 512
ATTN_UNROLL = 2
GROUP_TILE = 512
FFN_SLOTS = 2
CONV_ROWS = 32
CONV_HALO = 32
PAD_HEAD = 128

PACK_COLS = D_MODEL // 2
PACK_BLOCKS = PACK_COLS // LANES
SC_CORES = 2
SC_SUBCORES = 16
SC_WORKERS = SC_CORES * SC_SUBCORES
SC_WINDOW = 128
MOE_PARTS = 4
BF16 = jnp.bfloat16
F32 = jnp.float32
NEG_INF = float("-inf")
LOG2_E = 1.4426950408889634


def _params(*semantics):
    return pltpu.CompilerParams(dimension_semantics=semantics,
                                vmem_limit_bytes=VMEM_LIMIT_BYTES)


def _resident(shape):
    nd = len(shape)
    return pl.BlockSpec(shape, lambda *_: (0,) * nd)


def _dot(a, b):
    return jnp.dot(a, b, preferred_element_type=F32)


def _dot_nt(a, b, precision=None):
    return lax.dot_general(a, b, (((1,), (1,)), ((), ())),
                           precision=precision, preferred_element_type=F32)


def _layer_norm(x, g, b):
    mu = jnp.mean(x, axis=-1, keepdims=True)
    xc = x - mu
    var = jnp.mean(xc * xc, axis=-1, keepdims=True)
    return xc * lax.rsqrt(var + LN_EPS) * g + b


def _rms_norm(x, g):
    return x * lax.rsqrt(jnp.mean(x * x, axis=-1, keepdims=True) + RMS_EPS) * g


def _ones_upper_half(width):
    lane = lax.broadcasted_iota(jnp.int32, (1, width), 1)
    return jnp.where((lane & (PAD_HEAD - 1)) >= V_DIM, 1.0, 0.0).astype(F32)


def _pad_heads(w, heads):
    d = w.shape[-1] // heads
    w3 = w.reshape(w.shape[:-1] + (heads, d))
    pad = jnp.zeros(w.shape[:-1] + (heads, PAD_HEAD - d), w.dtype)
    return jnp.concatenate([w3, pad], axis=-1).reshape(w.shape[:-1] + (heads * PAD_HEAD,))


def _pack_rows(x):
    lo = lax.bitcast_convert_type(x[:, :PACK_COLS].astype(BF16).astype(F32), jnp.uint32)
    hi = lax.bitcast_convert_type(x[:, PACK_COLS:].astype(BF16).astype(F32), jnp.uint32)
    word = (lo >> 16) | (hi & jnp.uint32(0xFFFF0000))
    return lax.bitcast_convert_type(word, jnp.int32)


def _unpack_rows(words):
    u = lax.bitcast_convert_type(words, jnp.uint32)
    lo = lax.bitcast_convert_type(u << 16, F32)
    hi = lax.bitcast_convert_type(u & jnp.uint32(0xFFFF0000), F32)
    return lo, hi


def _store_packed(ref, words):
    groups = words.shape[0] // SUBLANES
    for cb in range(PACK_BLOCKS):
        ref[:, cb, :, :] = words[:, cb * LANES:(cb + 1) * LANES].reshape(groups, SUBLANES, LANES)


def _load_packed(ref):
    rows = ref.shape[0] * SUBLANES
    return jnp.concatenate(
        [ref[:, cb, :, :].reshape(rows, LANES) for cb in range(PACK_BLOCKS)], axis=1)


L0_A = 0
L0_G = CONV_CH
L0_Q = 2 * CONV_CH
L0_KV = L0_Q + Q_LORA
L0_KR = L0_KV + KV_LORA
L0_KRR = L0_KR + PAD_HEAD
L0_COLS = L0_KRR + PAD_HEAD


def _l0_front_kernel(x_ref, pos_ref, w_in_ref, b_in_ref, conv_w_ref, conv_b_ref,
                     cln_g_ref, cln_b_ref, qn_g_ref, wq_ref, wqr_ref, kvn_g_ref,
                     wk_ref, wv_ref, invf_ref,
                     u_ref, q_ref, k_ref, v_ref, ubuf):
    ts = x_ref.shape[0]

    @pl.when(pl.program_id(1) == 0)
    def _():
        ubuf[0:CONV_HALO, :] = jnp.zeros((CONV_HALO, CONV_CH), F32)

    h = _dot(x_ref[...].astype(BF16), w_in_ref[...]) + b_in_ref[...]

    ubuf[CONV_HALO:CONV_HALO + ts, :] = (
        h[:, L0_A:L0_A + CONV_CH] * jax.nn.sigmoid(h[:, L0_G:L0_G + CONV_CH]))
    first_tap = CONV_HALO - (CONV_WIDTH - 1)

    def conv_chunk(c, carry):
        base = pl.multiple_of(c * CONV_ROWS, CONV_ROWS)
        acc = jnp.broadcast_to(conv_b_ref[...], (CONV_ROWS, CONV_CH))
        for res in range(SUBLANES):
            rows = CONV_ROWS + (SUBLANES if res else 0)
            part = None
            for off in range(res, first_tap + CONV_WIDTH, SUBLANES):
                j = off - first_tap
                if j < 0:
                    continue
                window = ubuf[pl.ds(base + (off - res), rows), :]
                term = window.reshape(rows // SUBLANES, SUBLANES, CONV_CH) * conv_w_ref[j][None]
                part = term if part is None else part + term
            acc = acc + part.reshape(rows, CONV_CH)[res:res + CONV_ROWS, :]
        y = _layer_norm(acc, cln_g_ref[...], cln_b_ref[...])
        u_ref[pl.ds(base, CONV_ROWS), :] = (y * jax.nn.sigmoid(y)).astype(BF16)
        return carry

    lax.fori_loop(0, ts // CONV_ROWS, conv_chunk, 0, unroll=4)
    ubuf[0:CONV_HALO, :] = ubuf[ts:ts + CONV_HALO, :]

    ang = pos_ref[...].astype(F32) * invf_ref[...]
    cos = jnp.cos(ang)
    sin = jnp.sin(ang)
    scale = (QK_NOPE + QK_ROPE) ** -0.5 * LOG2_E

    qn = _rms_norm(h[:, L0_Q:L0_Q + Q_LORA], qn_g_ref[...]).astype(BF16)
    q = _dot(qn, wq_ref[...])
    q_rot = _dot(qn, wqr_ref[...])
    cos_s = cos * scale
    sin_s = sin * scale
    for hh in range(MLA_HEADS):
        blk = slice(hh * PAD_HEAD, (hh + 1) * PAD_HEAD)
        q_ref[:, blk] = (q[:, blk] * cos_s + q_rot[:, blk] * sin_s).astype(BF16)

    kvn = _rms_norm(h[:, L0_KV:L0_KV + KV_LORA], kvn_g_ref[...]).astype(BF16)
    k_nope = _dot(kvn, wk_ref[...])
    k_pe = h[:, L0_KR:L0_KR + PAD_HEAD] * cos + h[:, L0_KRR:L0_KRR + PAD_HEAD] * sin
    for hh in range(MLA_HEADS):
        blk = slice(hh * PAD_HEAD, (hh + 1) * PAD_HEAD)
        k_ref[:, blk] = (k_nope[:, blk] + k_pe).astype(BF16)
    v_ref[...] = (_dot(kvn, wv_ref[...]) + _ones_upper_half(v_ref.shape[1])).astype(BF16)


def _rope_rotate_cols(w):
    half = QK_ROPE // 2
    return jnp.concatenate([-w[..., half:], w[..., :half]], axis=-1)


def _l0_front(x, positions, w_in, b_in, conv_w, conv_b, cln_g, cln_b,
              qn_g, w_uq, kvn_g, w_ukv):
    bsz, seq, _ = x.shape
    pad_lo = jnp.zeros((D_MODEL, QK_NOPE), F32)
    pad_hi = jnp.zeros((D_MODEL, PAD_HEAD - QK_NOPE - QK_ROPE), F32)
    i3 = L0_KR
    w_kr = w_in[:, i3:i3 + QK_ROPE]
    w_in_p = jnp.concatenate(
        [w_in[:, :i3], pad_lo, w_kr, pad_hi, pad_lo, _rope_rotate_cols(w_kr), pad_hi],
        axis=1).astype(BF16)
    b_kr = b_in[i3:i3 + QK_ROPE]
    zlo = jnp.zeros((QK_NOPE,), F32)
    zhi = jnp.zeros((PAD_HEAD - QK_NOPE - QK_ROPE,), F32)
    b_in_p = jnp.concatenate(
        [b_in[:i3], zlo, b_kr, zhi, zlo, _rope_rotate_cols(b_kr), zhi])[None, :]

    dq = QK_NOPE + QK_ROPE
    wq3 = w_uq.reshape(Q_LORA, MLA_HEADS, dq)
    zq = jnp.zeros((Q_LORA, MLA_HEADS, PAD_HEAD - dq), F32)
    wq_p = jnp.concatenate([wq3, zq], axis=-1).reshape(Q_LORA, MLA_HEADS * PAD_HEAD)
    wqr_p = jnp.concatenate(
        [jnp.zeros((Q_LORA, MLA_HEADS, QK_NOPE), F32),
         _rope_rotate_cols(wq3[..., QK_NOPE:]), zq], axis=-1
    ).reshape(Q_LORA, MLA_HEADS * PAD_HEAD)
    wkv3 = w_ukv.reshape(KV_LORA, MLA_HEADS, QK_NOPE + V_DIM)
    wk_p = jnp.concatenate(
        [wkv3[..., :QK_NOPE], jnp.zeros((KV_LORA, MLA_HEADS, PAD_HEAD - QK_NOPE), F32)],
        axis=-1).reshape(KV_LORA, MLA_HEADS * PAD_HEAD)
    wv = _pad_heads(wkv3[..., QK_NOPE:].reshape(KV_LORA, MLA_HEADS * V_DIM), MLA_HEADS)

    inv_freq = 1.0 / (ROPE_THETA ** (jnp.arange(0, QK_ROPE, 2, dtype=F32) / QK_ROPE))
    invf = jnp.concatenate([jnp.zeros((QK_NOPE,), F32), inv_freq, inv_freq,
                            jnp.zeros((PAD_HEAD - QK_NOPE - QK_ROPE,), F32)])[None, :]

    ts = SEQ_TILE
    row = lambda w: pl.BlockSpec((None, ts, w), lambda b, s: (b, s, 0))
    qk_w = MLA_HEADS * PAD_HEAD
    v_w = MLA_HEADS * PAD_HEAD
    return pl.pallas_call(
        _l0_front_kernel,
        grid=(bsz, seq // ts),
        in_specs=[row(D_MODEL), row(1),
                  _resident((D_MODEL, L0_COLS)), _resident((1, L0_COLS)),
                  _resident((CONV_WIDTH, SUBLANES, CONV_CH)), _resident((1, CONV_CH)),
                  _resident((1, CONV_CH)), _resident((1, CONV_CH)),
                  _resident((1, Q_LORA)), _resident((Q_LORA, qk_w)),
                  _resident((Q_LORA, qk_w)), _resident((1, KV_LORA)),
                  _resident((KV_LORA, qk_w)), _resident((KV_LORA, v_w)),
                  _resident((1, PAD_HEAD))],
        out_specs=[row(CONV_CH), row(qk_w), row(qk_w), row(v_w)],
        out_shape=[jax.ShapeDtypeStruct((bsz, seq, CONV_CH), BF16),
                   jax.ShapeDtypeStruct((bsz, seq, qk_w), BF16),
                   jax.ShapeDtypeStruct((bsz, seq, qk_w), BF16),
                   jax.ShapeDtypeStruct((bsz, seq, v_w), BF16)],
        scratch_shapes=[pltpu.VMEM((ts + CONV_HALO, CONV_CH), F32)],
        compiler_params=_params("arbitrary", "arbitrary"),
        name="l0_front",
    )(x, positions[..., None], w_in_p, b_in_p,
      jnp.broadcast_to(conv_w[:, None, :], (CONV_WIDTH, SUBLANES, CONV_CH)), conv_b[None, :],
      cln_g[None, :], cln_b[None, :], qn_g[None, :], wq_p.astype(BF16),
      wqr_p.astype(BF16), kvn_g[None, :], wk_p.astype(BF16), wv.astype(BF16), invf)


def _l1_front_kernel(x_ref, w_ref, b_ref, wv_ref, bv_ref, wf_ref, bf_ref,
                     q_ref, k_ref, v_ref, cneg_ref, carry):
    ts = x_ref.shape[0]

    @pl.when(pl.program_id(1) == 0)
    def _():
        carry[...] = jnp.zeros_like(carry)

    xb = x_ref[...].astype(BF16)
    mix = FOX_HEADS * FOX_HD
    scale = FOX_HD ** -0.5 * LOG2_E
    q_ref[...] = ((_dot(xb, w_ref[:, 0:mix]) + b_ref[:, 0:mix]) * scale).astype(BF16)
    k_ref[...] = (_dot(xb, w_ref[:, mix:2 * mix]) + b_ref[:, mix:2 * mix]).astype(BF16)
    v = _dot(xb, wv_ref[...]) + bv_ref[...]
    head_lanes = lax.broadcasted_iota(jnp.int32, (ts, LANES), 1) < FOX_HD
    for pair in range(FOX_HEADS // 2):
        both = v[:, pair * LANES:(pair + 1) * LANES]
        swapped = pltpu.roll(both, LANES - FOX_HD, axis=1)
        for head, vals in enumerate((both, swapped)):
            blk = 2 * pair + head
            v_ref[:, blk * PAD_HEAD:(blk + 1) * PAD_HEAD] = (
                jnp.where(head_lanes, vals, 1.0).astype(BF16))

    log_f = jax.nn.log_sigmoid(_dot_nt(wf_ref[...], xb) + bf_ref[...])
    r = lax.broadcasted_iota(jnp.int32, (ts, ts), 0)
    c = lax.broadcasted_iota(jnp.int32, (ts, ts), 1)
    upper = (r <= c).astype(F32)
    csum = jnp.dot(log_f, upper, precision=lax.Precision.HIGHEST,
                   preferred_element_type=F32) + carry[...]
    cneg_ref[...] = csum * -LOG2_E
    carry[...] = carry[...] + jnp.sum(log_f, axis=1, keepdims=True)


def _l1_front(x, w_in, b_in):
    bsz, seq, _ = x.shape
    mix = FOX_HEADS * FOX_HD
    ts = SEQ_TILE
    row = lambda w: pl.BlockSpec((None, ts, w), lambda b, s: (b, s, 0))
    w_qk = w_in[:, :2 * mix].astype(BF16)
    b_qk = b_in[None, :2 * mix]
    w_v = w_in[:, 2 * mix:3 * mix].astype(BF16)
    b_v = b_in[None, 2 * mix:3 * mix]
    v_w = FOX_HEADS * PAD_HEAD
    wf_t = w_in[:, 3 * mix:].T.astype(BF16)
    bf_t = b_in[3 * mix:][:, None]
    return pl.pallas_call(
        _l1_front_kernel,
        grid=(bsz, seq // ts),
        in_specs=[row(D_MODEL), _resident((D_MODEL, 2 * mix)), _resident((1, 2 * mix)),
                  _resident((D_MODEL, mix)), _resident((1, mix)),
                  _resident((FOX_HEADS, D_MODEL)), _resident((FOX_HEADS, 1))],
        out_specs=[row(mix), row(mix), row(v_w),
                   pl.BlockSpec((None, FOX_HEADS, ts), lambda b, s: (b, 0, s))],
        out_shape=[jax.ShapeDtypeStruct((bsz, seq, mix), BF16)] * 2
        + [jax.ShapeDtypeStruct((bsz, seq, v_w), BF16),
           jax.ShapeDtypeStruct((bsz, FOX_HEADS, seq), F32)],
        scratch_shapes=[pltpu.VMEM((FOX_HEADS, 1), F32)],
        compiler_params=_params("arbitrary", "arbitrary"),
        name="l1_front",
    )(x, w_qk, b_qk, w_v, b_v, wf_t, bf_t)


def _attn_kernel(*refs, packed, has_bias):
    if has_bias:
        q_ref, k_ref, v_ref, cneg_ref, o_ref = refs
    else:
        q_ref, k_ref, v_ref, o_ref = refs
        cneg_ref = None
    tq = ATTN_TILE
    tk = ATTN_KEY_TILE
    half = LANES // 2

    def query_tile(qi):
        q_rows = pl.ds(pl.multiple_of(qi * tq, tq), tq)
        q2 = q_ref[q_rows, :]
        if packed:
            lane = lax.broadcasted_iota(jnp.int32, q2.shape, 1)
            zero = jnp.zeros_like(q2)
            qs = (jnp.where(lane < half, q2, zero), jnp.where(lane >= half, q2, zero))
        else:
            qs = (q2[:, :PAD_HEAD], q2[:, PAD_HEAD:])

        def k_block(kj, head):
            rows = pl.ds(pl.multiple_of(kj * tk, tk), tk)
            if packed:
                return k_ref[rows, :]
            return k_ref[rows, head * PAD_HEAD:(head + 1) * PAD_HEAD]

        def step(kj, carry, diag):
            rows = pl.ds(pl.multiple_of(kj * tk, tk), tk)
            out = []
            for head in range(2):
                m, acc = carry[head]
                s = _dot_nt(qs[head], k_block(kj, head))
                if has_bias:
                    s = s + cneg_ref[head, :, rows]
                if diag is not None:
                    r = lax.broadcasted_iota(jnp.int32, s.shape, 0)
                    c = lax.broadcasted_iota(jnp.int32, s.shape, 1)
                    s = jnp.where(r >= c + diag * tk, s, NEG_INF)
                m_new = jnp.maximum(m, jnp.max(s, axis=-1, keepdims=True))
                alpha = jnp.exp2(m - m_new)
                p = jnp.exp2(s - m_new).astype(BF16)
                vblk = v_ref[rows, head * PAD_HEAD:(head + 1) * PAD_HEAD]
                out.append((m_new, alpha * acc + _dot(p, vblk)))
            return tuple(out)

        init = tuple((jnp.full((tq, 1), NEG_INF, F32), jnp.zeros((tq, LANES), F32))
                     for _ in range(2))
        per_tile = tq // tk
        n_full = qi * per_tile

        def group(first, c, size):
            for t in range(size):
                c = step(first + t, c, None)
            return c

        carry = lax.fori_loop(0, n_full // ATTN_UNROLL,
                              lambda j, c: group(ATTN_UNROLL * j, c, ATTN_UNROLL), init)
        done = (n_full // ATTN_UNROLL) * ATTN_UNROLL
        size = ATTN_UNROLL // 2
        while size >= 1:
            take = (n_full - done) >= size
            carry = lax.cond(take, lambda c, f=done, n=size: group(f, c, n), lambda c: c,
                             carry)
            done = done + jnp.where(take, size, 0)
            size //= 2
        for d in range(per_tile):
            carry = step(n_full + d, carry, d)
        (_, acc_a), (_, acc_b) = carry
        out_a = acc_a / pltpu.roll(acc_a, half, axis=1)
        out_b = acc_b / pltpu.roll(acc_b, half, axis=1)
        lane_o = lax.broadcasted_iota(jnp.int32, (tq, LANES), 1)
        o_ref[q_rows, :] = jnp.where(lane_o < half, out_a,
                                     pltpu.roll(out_b, half, axis=1)).astype(o_ref.dtype)

    pl.loop(0, q_ref.shape[0] // tq)(query_tile)


def _attention(q, k, v, cneg, packed):
    bsz, seq, v_w = v.shape
    pairs = v_w // (2 * PAD_HEAD)
    qk_w = q.shape[-1] // pairs
    in_specs = [pl.BlockSpec((None, seq, qk_w), lambda b, p: (b, 0, p)),
                pl.BlockSpec((None, seq, qk_w), lambda b, p: (b, 0, p)),
                pl.BlockSpec((None, seq, 2 * PAD_HEAD), lambda b, p: (b, 0, p))]
    args = [q, k, v]
    if cneg is not None:
        in_specs.append(pl.BlockSpec((None, 2, 1, seq), lambda b, p: (b, p, 0, 0)))
        args.append(cneg)
    return pl.pallas_call(
        functools.partial(_attn_kernel, packed=packed, has_bias=cneg is not None),
        grid=(bsz, pairs),
        in_specs=in_specs,
        out_specs=pl.BlockSpec((None, seq, LANES), lambda b, p: (b, 0, p)),
        out_shape=jax.ShapeDtypeStruct((bsz, seq, pairs * LANES), BF16),
        compiler_params=_params("arbitrary", "arbitrary"),
        name="attn_packed" if packed else "attn_padded",
    )(*args)


def _first_hit(hits, found):
    out = []
    for hcur in hits:
        take = jnp.logical_and(hcur, jnp.logical_not(found))
        found = jnp.logical_or(found, take)
        out.append(take)
    return out, found


def _mix_out_router_kernel(a_ref, b_ref, x_ref, wa_ref, wb_ref, bo_ref, g_ref, be_ref,
                           rw_ref, rwl_ref, rb_ref,
                           x1_ref, x1p_ref, e_ref, gate_ref, rank_ref, cnt_ref, cnt, gate_t):
    tm = x_ref.shape[0]

    @pl.when(pl.program_id(0) == 0)
    def _():
        cnt[...] = jnp.zeros_like(cnt)
        gate_t[...] = jnp.zeros_like(gate_t)

    mix = _dot(a_ref[...], wa_ref[...]) + _dot(b_ref[...], wb_ref[...]) + bo_ref[...]
    x1 = _layer_norm(DN_ALPHA * x_ref[...] + mix, g_ref[...], be_ref[...])
    x1_ref[...] = x1
    _store_packed(x1p_ref, _pack_rows(x1))

    x_hi = x1.astype(BF16)
    x_lo = (x1 - x_hi.astype(F32)).astype(BF16)
    logits = (_dot_nt(rw_ref[...], x_hi) + _dot_nt(rw_ref[...], x_lo)
              + _dot_nt(rwl_ref[...], x_hi))
    aff = jax.nn.sigmoid(logits)
    choice = aff + rb_ref[...]
    per_group = N_EXPERTS // N_GROUPS
    sub = lax.broadcasted_iota(jnp.int32, (per_group, tm), 0)
    groups = [choice[g * per_group:(g + 1) * per_group, :] for g in range(N_GROUPS)]

    gscore = []
    for cg in groups:
        m1 = jnp.max(cg, axis=0, keepdims=True)
        i1 = jnp.min(jnp.where(cg == m1, sub, per_group), axis=0, keepdims=True)
        m2 = jnp.max(jnp.where(sub == i1, NEG_INF, cg), axis=0, keepdims=True)
        gscore.append(m1 + m2)

    gsel = [jnp.zeros((1, tm), jnp.bool_) for _ in range(N_GROUPS)]
    for _ in range(TOPK_GROUPS):
        best = functools.reduce(jnp.maximum, gscore)
        takes, _ = _first_hit([gs == best for gs in gscore], jnp.zeros((1, tm), jnp.bool_))
        gsel = [jnp.logical_or(a, t) for a, t in zip(gsel, takes)]
        gscore = [jnp.where(t, NEG_INF, gs) for gs, t in zip(gscore, takes)]

    masked = [jnp.where(gs, cg, NEG_INF) for gs, cg in zip(gsel, groups)]
    eid = [sub + g * per_group for g in range(N_GROUPS)]
    affs = [aff[g * per_group:(g + 1) * per_group, :] for g in range(N_GROUPS)]
    sel = [jnp.zeros((per_group, tm), jnp.bool_) for _ in range(N_GROUPS)]
    picked_e, picked_w = [], []
    for _ in range(TOP_K):
        best = jnp.max(functools.reduce(jnp.maximum, masked), axis=0, keepdims=True)
        cand = [jnp.where(mg == best, ig, N_EXPERTS) for mg, ig in zip(masked, eid)]
        idx = jnp.min(functools.reduce(jnp.minimum, cand), axis=0, keepdims=True)
        onehot = [ig == idx for ig in eid]
        w = functools.reduce(
            jnp.add, [jnp.sum(jnp.where(oh, ag, 0.0), axis=0, keepdims=True)
                      for oh, ag in zip(onehot, affs)])
        picked_e.append(idx)
        picked_w.append(w)
        sel = [jnp.logical_or(sg, oh) for sg, oh in zip(sel, onehot)]
        masked = [jnp.where(oh, NEG_INF, mg) for mg, oh in zip(masked, onehot)]

    wsum = functools.reduce(jnp.add, picked_w)

    sel_f = jnp.concatenate([sg.astype(F32) for sg in sel], axis=0)
    r = lax.broadcasted_iota(jnp.int32, (tm, tm), 0)
    c = lax.broadcasted_iota(jnp.int32, (tm, tm), 1)
    before = _dot(sel_f.astype(BF16), (r < c).astype(BF16)) + cnt[...]
    eall = lax.broadcasted_iota(jnp.int32, (N_EXPERTS, tm), 0)
    for kk in range(TOP_K):
        rank = jnp.sum(jnp.where(eall == picked_e[kk], before, 0.0), axis=0, keepdims=True)
        e_ref[kk:kk + 1, :] = picked_e[kk]
        gate_t[kk:kk + 1, :] = picked_w[kk] / wsum * ROUTED_SCALE
        rank_ref[kk:kk + 1, :] = rank.astype(jnp.int32)
    pad = SUBLANES - TOP_K
    e_ref[TOP_K:, :] = jnp.zeros((pad, tm), jnp.int32)
    rank_ref[TOP_K:, :] = jnp.zeros((pad, tm), jnp.int32)
    gate_ref[...] = gate_t[...].T
    cnt[...] = cnt[...] + jnp.sum(sel_f, axis=1, keepdims=True)
    cnt_ref[...] = jnp.broadcast_to(cnt[...], cnt_ref.shape)


def _mix_out_router(a, b, x, w_out, b_out, ln_g, ln_b, router_w, router_bias):
    n_tok = x.shape[0]
    tm = TOKEN_TILE
    half = w_out.shape[0] // 2
    row = lambda w: pl.BlockSpec((tm, w), lambda i: (i, 0))
    b_spec = (pl.BlockSpec((tm, half), lambda i: (i, 1)) if b is a
              else pl.BlockSpec((tm, half), lambda i: (i, 0)))
    lane_row = pl.BlockSpec((SUBLANES, tm), lambda i: (0, i))
    packed = pl.BlockSpec((tm // SUBLANES, PACK_BLOCKS, SUBLANES, LANES),
                          lambda i: (i, 0, 0, 0))
    w_bf = w_out.astype(BF16)
    rw_t = router_w.T
    rw_hi = rw_t.astype(BF16)
    rw_lo = (rw_t - rw_hi.astype(F32)).astype(BF16)
    return pl.pallas_call(
        _mix_out_router_kernel,
        grid=(n_tok // tm,),
        in_specs=[row(half), b_spec, row(D_MODEL),
                  pl.BlockSpec((half, D_MODEL), lambda i: (0, 0)),
                  pl.BlockSpec((half, D_MODEL), lambda i: (1, 0)),
                  _resident((1, D_MODEL)), _resident((1, D_MODEL)), _resident((1, D_MODEL)),
                  _resident((N_EXPERTS, D_MODEL)), _resident((N_EXPERTS, D_MODEL)),
                  _resident((N_EXPERTS, 1))],
        out_specs=[row(D_MODEL), packed, lane_row, row(LANES), lane_row,
                   _resident((N_EXPERTS, LANES))],
        out_shape=[jax.ShapeDtypeStruct((n_tok, D_MODEL), F32),
                   jax.ShapeDtypeStruct((n_tok // SUBLANES, PACK_BLOCKS, SUBLANES, LANES),
                                        jnp.int32),
                   jax.ShapeDtypeStruct((SUBLANES, n_tok), jnp.int32),
                   jax.ShapeDtypeStruct((n_tok, LANES), F32),
                   jax.ShapeDtypeStruct((SUBLANES, n_tok), jnp.int32),
                   jax.ShapeDtypeStruct((N_EXPERTS, LANES), F32)],
        scratch_shapes=[pltpu.VMEM((N_EXPERTS, 1), F32), pltpu.VMEM((LANES, tm), F32)],
        compiler_params=_params("arbitrary"),
        name="mix_out_router",
    )(a, b, x, w_bf, w_bf, b_out[None, :], ln_g[None, :], ln_b[None, :],
      rw_hi, rw_lo, router_bias[:, None])


def _expert_ffn_kernel(tile_e_ref, tile_on_ref, xs_ref, *refs):
    w_refs = refs[:3 * FFN_SLOTS]
    ys_ref = refs[3 * FFN_SLOTS]
    wg_bf, wu_bf, wd_bf = refs[3 * FFN_SLOTS + 1:]
    i = pl.program_id(0)
    sub = GROUP_TILE // SUBLANES

    for slot in range(FFN_SLOTS):
        t = FFN_SLOTS * i + slot
        fresh = jnp.logical_or(
            i == 0, tile_e_ref[t] != tile_e_ref[jnp.maximum(t - FFN_SLOTS, 0)])

        @pl.when(fresh)
        def _(slot=slot):
            wg_ref, wu_ref, wd_ref = w_refs[3 * slot:3 * slot + 3]
            wg_bf[slot] = wg_ref[...].astype(BF16)
            wu_bf[slot] = wu_ref[...].astype(BF16)
            wd_bf[slot] = wd_ref[...].astype(BF16)

    on = tile_on_ref[FFN_SLOTS * i] > 0

    @pl.when(on)
    def _():
        for slot in range(FFN_SLOTS):
            rows = pl.ds(slot * sub, sub)
            lo, hi = _unpack_rows(_load_packed(xs_ref.at[rows]))
            lo = lo.astype(BF16)
            hi = hi.astype(BF16)
            gate = (_dot(lo, wg_bf[slot, :PACK_COLS, :]) + _dot(hi, wg_bf[slot, PACK_COLS:, :]))
            up = (_dot(lo, wu_bf[slot, :PACK_COLS, :]) + _dot(hi, wu_bf[slot, PACK_COLS:, :]))
            hid = (jax.nn.silu(gate) * up).astype(BF16)
            _store_packed(ys_ref.at[rows], _pack_rows(_dot(hid, wd_bf[slot])))

    @pl.when(jnp.logical_not(on))
    def _():
        ys_ref[...] = jnp.zeros_like(ys_ref)


def _expert_ffn(xs, tile_e, tile_on, w_gate, w_up, w_down):
    rows_per_step = GROUP_TILE * FFN_SLOTS
    packed = pl.BlockSpec((rows_per_step // SUBLANES, PACK_BLOCKS, SUBLANES, LANES),
                          lambda i, te, on: (i, 0, 0, 0))
    w_specs = []
    for slot in range(FFN_SLOTS):
        pick = lambda i, te, on, slot=slot: (te[FFN_SLOTS * i + slot], 0, 0)
        w_specs += [pl.BlockSpec((None, D_MODEL, EXPERT_HID), pick),
                    pl.BlockSpec((None, D_MODEL, EXPERT_HID), pick),
                    pl.BlockSpec((None, EXPERT_HID, D_MODEL), pick)]
    grid_spec = pltpu.PrefetchScalarGridSpec(
        num_scalar_prefetch=2,
        grid=(xs.shape[0] * SUBLANES // rows_per_step,),
        in_specs=[packed] + w_specs,
        out_specs=packed,
        scratch_shapes=[pltpu.VMEM((FFN_SLOTS, D_MODEL, EXPERT_HID), BF16),
                        pltpu.VMEM((FFN_SLOTS, D_MODEL, EXPERT_HID), BF16),
                        pltpu.VMEM((FFN_SLOTS, EXPERT_HID, D_MODEL), BF16)],
    )
    return pl.pallas_call(
        _expert_ffn_kernel,
        grid_spec=grid_spec,
        out_shape=jax.ShapeDtypeStruct(xs.shape, jnp.int32),
        compiler_params=_params("arbitrary"),
        name="expert_ffn",
    )(tile_e, tile_on, xs, *([w_gate, w_up, w_down] * FFN_SLOTS))


def _moe_out_kernel(x1_ref, yg_ref, gate_ref, sg_ref, su_ref, sd_ref, g_ref, b_ref, *rest):
    o_ref = rest[-1]
    x1 = x1_ref[...]
    xb = x1.astype(BF16)
    hid = jax.nn.silu(_dot(xb, sg_ref[...])) * _dot(xb, su_ref[...])
    shared = _dot(hid.astype(BF16), sd_ref[...])
    gates = gate_ref[...]
    lo_acc = shared[:, :PACK_COLS]
    hi_acc = shared[:, PACK_COLS:]
    for kk in range(TOP_K):
        lo, hi = _unpack_rows(_load_packed(yg_ref.at[kk]))
        w = gates[:, kk:kk + 1]
        lo_acc = lo_acc + w * lo
        hi_acc = hi_acc + w * hi
    moe = jnp.concatenate([lo_acc, hi_acc], axis=1)
    o_ref[...] = _layer_norm(DN_ALPHA * x1 + moe, g_ref[...], b_ref[...])


def _moe_out(x1, yg, gates, sh_gate, sh_up, sh_down, ln_g, ln_b, part, earlier):
    n_tok = x1.shape[0]
    tm = TOKEN_TILE
    steps = n_tok // tm // MOE_PARTS
    first = part * steps
    row = lambda w: pl.BlockSpec((tm, w), lambda i: (i + first, 0))
    in_specs = [row(D_MODEL),
                pl.BlockSpec((TOP_K, tm // SUBLANES, PACK_BLOCKS, SUBLANES, LANES),
                             lambda i: (0, i, 0, 0, 0)),
                row(LANES),
                _resident((D_MODEL, EXPERT_HID)), _resident((D_MODEL, EXPERT_HID)),
                _resident((EXPERT_HID, D_MODEL)), _resident((1, D_MODEL)),
                _resident((1, D_MODEL))]
    args = [x1, yg, gates, sh_gate.astype(BF16), sh_up.astype(BF16), sh_down.astype(BF16),
            ln_g[None, :], ln_b[None, :]]
    aliases = {}
    if earlier is not None:
        in_specs.append(pl.BlockSpec(memory_space=pl.ANY))
        args.append(earlier)
        aliases = {len(args) - 1: 0}
    return pl.pallas_call(
        _moe_out_kernel,
        grid=(steps,),
        in_specs=in_specs,
        out_specs=row(D_MODEL),
        out_shape=jax.ShapeDtypeStruct((n_tok, D_MODEL), F32),
        input_output_aliases=aliases,
        compiler_params=_params("arbitrary"),
        name="moe_out",
    )(*args)


def _slot_kernel(offset_ref, e_ref, rank_ref, pos_ref):
    e = e_ref[...]
    base = jnp.zeros(e.shape, jnp.int32)
    for ex in range(N_EXPERTS):
        base = jnp.where(e == ex, offset_ref[ex], base)
    pos_ref[...] = base + rank_ref[...]


def _slots(offset, e6, rank6):
    n_tok = e6.shape[1]
    tl = SLOT_TILE
    lane_row = pl.BlockSpec((SUBLANES, tl), lambda i, off: (0, i))
    return pl.pallas_call(
        _slot_kernel,
        grid_spec=pltpu.PrefetchScalarGridSpec(
            num_scalar_prefetch=1, grid=(n_tok // tl,),
            in_specs=[lane_row, lane_row], out_specs=lane_row),
        out_shape=jax.ShapeDtypeStruct((SUBLANES, n_tok), jnp.int32),
        compiler_params=_params("arbitrary"),
        name="slots",
    )(offset, e6, rank6)


def _sc_mesh():
    return plsc.VectorSubcoreMesh(core_axis_name="c", subcore_axis_name="s")


def _sc_worker():
    return lax.axis_index("s") * SC_CORES + lax.axis_index("c")


def _sc_dispatch(x_sub, idx, n_out_sub):
    windows = x_sub.shape[0] // (SC_WORKERS * SC_WINDOW)

    @functools.partial(
        pl.kernel, mesh=_sc_mesh(),
        out_type=jax.ShapeDtypeStruct((n_out_sub, LANES), jnp.int32),
        scratch_types=[pltpu.VMEM((SC_WINDOW, LANES), jnp.int32),
                       pltpu.VMEM((TOP_K, SC_WINDOW), jnp.int32)],
        name="sc_dispatch")
    def run(x_hbm, idx_hbm, out_hbm, rows_v, idx_v):
        wid = _sc_worker()

        @pl.loop(0, windows)
        def _(c):
            base = (wid * windows + c) * SC_WINDOW
            pltpu.sync_copy(x_hbm.at[pl.ds(base, SC_WINDOW)], rows_v)
            pltpu.sync_copy(idx_hbm.at[wid, c], idx_v)
            for kk in range(TOP_K):
                pltpu.sync_copy(rows_v, out_hbm.at[idx_v.at[kk]])

    return run(x_sub, idx)


def _sc_combine(y_sub, idx, n_tok_sub):
    windows = n_tok_sub // (SC_WORKERS * SC_WINDOW)

    @functools.partial(
        pl.kernel, mesh=_sc_mesh(),
        out_type=jax.ShapeDtypeStruct((TOP_K, n_tok_sub, LANES), jnp.int32),
        scratch_types=[pltpu.VMEM((2, SC_WINDOW, LANES), jnp.int32),
                       pltpu.VMEM((TOP_K, SC_WINDOW), jnp.int32),
                       pltpu.SemaphoreType.DMA((2,)),
                       pltpu.SemaphoreType.DMA((2,))],
        name="sc_combine")
    def run(y_hbm, idx_hbm, out_hbm, rows_v, idx_v, gather_sem, write_sem):
        wid = _sc_worker()

        def gather(kk):
            buf = kk % 2
            return pltpu.async_copy(y_hbm.at[idx_v.at[kk]], rows_v.at[buf], gather_sem.at[buf])

        @pl.loop(0, windows)
        def _(c):
            base = (wid * windows + c) * SC_WINDOW
            pltpu.sync_copy(idx_hbm.at[wid, c], idx_v)
            writes = []
            pending = gather(0)
            for kk in range(TOP_K):
                pending.wait()
                if kk + 1 < TOP_K:
                    if kk >= 1:
                        writes[kk - 1].wait()
                    pending = gather(kk + 1)
                writes.append(pltpu.async_copy(
                    rows_v.at[kk % 2], out_hbm.at[kk, pl.ds(base, SC_WINDOW)],
                    write_sem.at[kk % 2]))
            writes[TOP_K - 2].wait()
            writes[TOP_K - 1].wait()

    return run(y_sub, idx)


def _moe(x1, x1p, e6, gates, rank6, counts, exp_w_gate, exp_w_up, exp_w_down,
         sh_w_gate, sh_w_up, sh_w_down, ln_g, ln_b):
    n_tok = x1.shape[0]
    tg = GROUP_TILE
    n_rows = n_tok * TOP_K + N_EXPERTS * tg
    n_tiles = n_rows // tg

    cnt = counts[:, 0].astype(jnp.int32)
    tiles_per_e = (cnt + tg - 1) // tg
    tile_end = jnp.cumsum(tiles_per_e)
    offset = (tile_end - tiles_per_e) * tg
    tile_ids = jnp.arange(n_tiles, dtype=jnp.int32)
    tile_on = (tile_ids < tile_end[-1]).astype(jnp.int32)
    tile_e = jnp.minimum(
        jnp.sum((tile_ids[:, None] >= tile_end[None, :]).astype(jnp.int32), axis=1),
        N_EXPERTS - 1)
    tile_e = jnp.where(tile_on > 0, tile_e, tile_e[jnp.maximum(tile_end[-1] - 1, 0)])

    pos = _slots(offset, e6, rank6)[:TOP_K]

    group = PACK_BLOCKS * SUBLANES
    sub0 = (pos // SUBLANES) * group + pos % SUBLANES
    lane = jnp.arange(SC_WINDOW)
    spread = (jnp.arange(group)[:, None]
              == (lane // group) * SUBLANES + lane % SUBLANES).astype(F32)
    idx = jnp.einsum("kws,sd->kwd", sub0.reshape(TOP_K, n_tok // group, group).astype(F32),
                     spread, precision=lax.Precision.HIGHEST).astype(jnp.int32)
    idx = idx + ((lane % group) // SUBLANES * SUBLANES).astype(jnp.int32)
    n_sub = n_tok * PACK_BLOCKS
    idx = idx.reshape(TOP_K, n_sub)

    def windowed(ix):
        windows = ix.shape[1] // (SC_WORKERS * SC_WINDOW)
        return ix.reshape(TOP_K, SC_WORKERS, windows, SC_WINDOW).transpose(1, 2, 0, 3)

    xs = _sc_dispatch(x1p.reshape(n_sub, LANES), windowed(idx), n_rows * PACK_BLOCKS)
    ys = _expert_ffn(xs.reshape(n_rows // SUBLANES, PACK_BLOCKS, SUBLANES, LANES),
                     tile_e, tile_on, exp_w_gate, exp_w_up, exp_w_down)
    ys = ys.reshape(n_rows * PACK_BLOCKS, LANES)
    part_sub = n_sub // MOE_PARTS
    out = None
    for part in range(MOE_PARTS):
        yg = _sc_combine(ys, windowed(idx[:, part * part_sub:(part + 1) * part_sub]), part_sub)
        yg = yg.reshape(TOP_K, part_sub // (PACK_BLOCKS * SUBLANES), PACK_BLOCKS,
                        SUBLANES, LANES)
        out = _moe_out(x1, yg, gates, sh_w_gate, sh_w_up, sh_w_down, ln_g, ln_b, part, out)
    return out


def kernel(x, positions,
           l0_w_in, l0_b_in, l0_conv_w, l0_conv_b, l0_conv_ln_g, l0_conv_ln_b,
           l0_q_norm_g, l0_w_uq, l0_kv_norm_g, l0_w_ukv, l0_w_out, l0_b_out,
           l0_ln1_g, l0_ln1_b,
           l0_router_w, l0_router_bias, l0_exp_w_gate, l0_exp_w_up, l0_exp_w_down,
           l0_sh_w_gate, l0_sh_w_up, l0_sh_w_down, l0_ln2_g, l0_ln2_b,
           l1_w_in, l1_b_in, l1_w_out, l1_b_out, l1_ln1_g, l1_ln1_b,
           l1_router_w, l1_router_bias, l1_exp_w_gate, l1_exp_w_up, l1_exp_w_down,
           l1_sh_w_gate, l1_sh_w_up, l1_sh_w_down, l1_ln2_g, l1_ln2_b):
    bsz, seq, d = x.shape
    n_tok = bsz * seq

    u, q, k, v = _l0_front(x, positions, l0_w_in, l0_b_in, l0_conv_w, l0_conv_b,
                           l0_conv_ln_g, l0_conv_ln_b, l0_q_norm_g, l0_w_uq,
                           l0_kv_norm_g, l0_w_ukv)
    o = _attention(q, k, v, None, packed=False)
    x_flat = x.reshape(n_tok, d)
    routing = _mix_out_router(u.reshape(n_tok, -1), o.reshape(n_tok, -1), x_flat,
                              l0_w_out, l0_b_out, l0_ln1_g, l0_ln1_b,
                              l0_router_w, l0_router_bias)
    x2 = _moe(*routing, l0_exp_w_gate, l0_exp_w_up, l0_exp_w_down,
              l0_sh_w_gate, l0_sh_w_up, l0_sh_w_down, l0_ln2_g, l0_ln2_b)

    q, k, v, cneg = _l1_front(x2.reshape(bsz, seq, d), l1_w_in, l1_b_in)
    o = _attention(q, k, v, cneg[:, :, None, :], packed=True).reshape(n_tok, -1)
    routing = _mix_out_router(o, o, x2, l1_w_out, l1_b_out, l1_ln1_g, l1_ln1_b,
                              l1_router_w, l1_router_bias)
    x3 = _moe(*routing, l1_exp_w_gate, l1_exp_w_up, l1_exp_w_down,
              l1_sh_w_gate, l1_sh_w_up, l1_sh_w_down, l1_ln2_g, l1_ln2_b)
    return x3.reshape(bsz, seq, d)
```

```python
import functools

import jax
import jax.numpy as jnp
from jax import lax
from jax.experimental import pallas as pl
from jax.experimental.pallas import tpu as pltpu
from jax.experimental.pallas import tpu_sc as plsc

D_MODEL = 1024
DEPTH = 2
DN_ALPHA = (2 * DEPTH) ** 0.25
LN_EPS = 1e-5
RMS_EPS = 1e-6

CONV_CH = 512
CONV_WIDTH = 31
MLA_HEADS = 8
QK_NOPE = 64
QK_ROPE = 32
V_DIM = 64
Q_LORA = 256
KV_LORA = 128
ROPE_THETA = 10000.0
FOX_HEADS = 16
FOX_HD = 64
N_EXPERTS = 64
TOP_K = 6
N_GROUPS = 8
TOPK_GROUPS = 4
EXPERT_HID = 256
ROUTED_SCALE = 2.5

LANES = 128
SUBLANES = 8
V7X_VMEM_BYTES = 64 * 1024 * 1024
VMEM_LIMIT_BYTES = V7X_VMEM_BYTES * 7 // 8

SEQ_TILE = 512
TOKEN_TILE = 512
SLOT_TILE = 4096
ATTN_TILE = 1024
ATTN_KEY_TILE = 512
ATTN_UNROLL = 2
GROUP_TILE = 512
FFN_SLOTS = 2
CONV_ROWS = 32
CONV_HALO = 32
PAD_HEAD = 128

PACK_COLS = D_MODEL // 2
PACK_BLOCKS = PACK_COLS // LANES
SC_CORES = 2
SC_SUBCORES = 16
SC_WORKERS = SC_CORES * SC_SUBCORES
SC_WINDOW = 128
MOE_PARTS = 4
BF16 = jnp.bfloat16
F32 = jnp.float32
NEG_INF = float("-inf")
LOG2_E = 1.4426950408889634


def _params(*semantics):
    return pltpu.CompilerParams(dimension_semantics=semantics,
                                vmem_limit_bytes=VMEM_LIMIT_BYTES)


def _resident(shape):
    nd = len(shape)
    return pl.BlockSpec(shape, lambda *_: (0,) * nd)


def _dot(a, b):
    return jnp.dot(a, b, preferred_element_type=F32)


def _dot_nt(a, b, precision=None):
    return lax.dot_general(a, b, (((1,), (1,)), ((), ())),
                           precision=precision, preferred_element_type=F32)


def _layer_norm(x, g, b):
    mu = jnp.mean(x, axis=-1, keepdims=True)
    xc = x - mu
    var = jnp.mean(xc * xc, axis=-1, keepdims=True)
    return xc * lax.rsqrt(var + LN_EPS) * g + b


def _rms_norm(x, g):
    return x * lax.rsqrt(jnp.mean(x * x, axis=-1, keepdims=True) + RMS_EPS) * g


def _ones_upper_half(width):
    lane = lax.broadcasted_iota(jnp.int32, (1, width), 1)
    return jnp.where((lane & (PAD_HEAD - 1)) >= V_DIM, 1.0, 0.0).astype(F32)


def _pad_heads(w, heads):
    d = w.shape[-1] // heads
    w3 = w.reshape(w.shape[:-1] + (heads, d))
    pad = jnp.zeros(w.shape[:-1] + (heads, PAD_HEAD - d), w.dtype)
    return jnp.concatenate([w3, pad], axis=-1).reshape(w.shape[:-1] + (heads * PAD_HEAD,))


def _pack_rows(x):
    lo = lax.bitcast_convert_type(x[:, :PACK_COLS].astype(BF16).astype(F32), jnp.uint32)
    hi = lax.bitcast_convert_type(x[:, PACK_COLS:].astype(BF16).astype(F32), jnp.uint32)
    word = (lo >> 16) | (hi & jnp.uint32(0xFFFF0000))
    return lax.bitcast_convert_type(word, jnp.int32)


def _unpack_rows(words):
    u = lax.bitcast_convert_type(words, jnp.uint32)
    lo = lax.bitcast_convert_type(u << 16, F32)
    hi = lax.bitcast_convert_type(u & jnp.uint32(0xFFFF0000), F32)
    return lo, hi


def _store_packed(ref, words):
    groups = words.shape[0] // SUBLANES
    for cb in range(PACK_BLOCKS):
        ref[:, cb, :, :] = words[:, cb * LANES:(cb + 1) * LANES].reshape(groups, SUBLANES, LANES)


def _load_packed(ref):
    rows = ref.shape[0] * SUBLANES
    return jnp.concatenate(
        [ref[:, cb, :, :].reshape(rows, LANES) for cb in range(PACK_BLOCKS)], axis=1)


L0_A = 0
L0_G = CONV_CH
L0_Q = 2 * CONV_CH
L0_KV = L0_Q + Q_LORA
L0_KR = L0_KV + KV_LORA
L0_KRR = L0_KR + PAD_HEAD
L0_COLS = L0_KRR + PAD_HEAD


def _l0_front_kernel(x_ref, pos_ref, w_in_ref, b_in_ref, conv_w_ref, conv_b_ref,
                     cln_g_ref, cln_b_ref, qn_g_ref, wq_ref, wqr_ref, kvn_g_ref,
                     wk_ref, wv_ref, invf_ref,
                     u_ref, q_ref, k_ref, v_ref, ubuf):
    ts = x_ref.shape[0]

    @pl.when(pl.program_id(1) == 0)
    def _():
        ubuf[0:CONV_HALO, :] = jnp.zeros((CONV_HALO, CONV_CH), F32)

    h = _dot(x_ref[...].astype(BF16), w_in_ref[...]) + b_in_ref[...]

    ubuf[CONV_HALO:CONV_HALO + ts, :] = (
        h[:, L0_A:L0_A + CONV_CH] * jax.nn.sigmoid(h[:, L0_G:L0_G + CONV_CH]))
    first_tap = CONV_HALO - (CONV_WIDTH - 1)

    def conv_chunk(c, carry):
        base = pl.multiple_of(c * CONV_ROWS, CONV_ROWS)
        acc = jnp.broadcast_to(conv_b_ref[...], (CONV_ROWS, CONV_CH))
        for res in range(SUBLANES):
            rows = CONV_ROWS + (SUBLANES if res else 0)
            part = None
            for off in range(res, first_tap + CONV_WIDTH, SUBLANES):
                j = off - first_tap
                if j < 0:
                    continue
                window = ubuf[pl.ds(base + (off - res), rows), :]
                term = window.reshape(rows // SUBLANES, SUBLANES, CONV_CH) * conv_w_ref[j][None]
                part = term if part is None else part + term
            acc = acc + part.reshape(rows, CONV_CH)[res:res + CONV_ROWS, :]
        y = _layer_norm(acc, cln_g_ref[...], cln_b_ref[...])
        u_ref[pl.ds(base, CONV_ROWS), :] = (y * jax.nn.sigmoid(y)).astype(BF16)
        return carry

    lax.fori_loop(0, ts // CONV_ROWS, conv_chunk, 0, unroll=4)
    ubuf[0:CONV_HALO, :] = ubuf[ts:ts + CONV_HALO, :]

    ang = pos_ref[...].astype(F32) * invf_ref[...]
    cos = jnp.cos(ang)
    sin = jnp.sin(ang)
    scale = (QK_NOPE + QK_ROPE) ** -0.5 * LOG2_E

    qn = _rms_norm(h[:, L0_Q:L0_Q + Q_LORA], qn_g_ref[...]).astype(BF16)
    q = _dot(qn, wq_ref[...])
    q_rot = _dot(qn, wqr_ref[...])
    cos_s = cos * scale
    sin_s = sin * scale
    for hh in range(MLA_HEADS):
        blk = slice(hh * PAD_HEAD, (hh + 1) * PAD_HEAD)
        q_ref[:, blk] = (q[:, blk] * cos_s + q_rot[:, blk] * sin_s).astype(BF16)

    kvn = _rms_norm(h[:, L0_KV:L0_KV + KV_LORA], kvn_g_ref[...]).astype(BF16)
    k_nope = _dot(kvn, wk_ref[...])
    k_pe = h[:, L0_KR:L0_KR + PAD_HEAD] * cos + h[:, L0_KRR:L0_KRR + PAD_HEAD] * sin
    for hh in range(MLA_HEADS):
        blk = slice(hh * PAD_HEAD, (hh + 1) * PAD_HEAD)
        k_ref[:, blk] = (k_nope[:, blk] + k_pe).astype(BF16)
    v_ref[...] = (_dot(kvn, wv_ref[...]) + _ones_upper_half(v_ref.shape[1])).astype(BF16)


def _rope_rotate_cols(w):
    half = QK_ROPE // 2
    return jnp.concatenate([-w[..., half:], w[..., :half]], axis=-1)


def _l0_front(x, positions, w_in, b_in, conv_w, conv_b, cln_g, cln_b,
              qn_g, w_uq, kvn_g, w_ukv):
    bsz, seq, _ = x.shape
    pad_lo = jnp.zeros((D_MODEL, QK_NOPE), F32)
    pad_hi = jnp.zeros((D_MODEL, PAD_HEAD - QK_NOPE - QK_ROPE), F32)
    i3 = L0_KR
    w_kr = w_in[:, i3:i3 + QK_ROPE]
    w_in_p = jnp.concatenate(
        [w_in[:, :i3], pad_lo, w_kr, pad_hi, pad_lo, _rope_rotate_cols(w_kr), pad_hi],
        axis=1).astype(BF16)
    b_kr = b_in[i3:i3 + QK_ROPE]
    zlo = jnp.zeros((QK_NOPE,), F32)
    zhi = jnp.zeros((PAD_HEAD - QK_NOPE - QK_ROPE,), F32)
    b_in_p = jnp.concatenate(
        [b_in[:i3], zlo, b_kr, zhi, zlo, _rope_rotate_cols(b_kr), zhi])[None, :]

    dq = QK_NOPE + QK_ROPE
    wq3 = w_uq.reshape(Q_LORA, MLA_HEADS, dq)
    zq = jnp.zeros((Q_LORA, MLA_HEADS, PAD_HEAD - dq), F32)
    wq_p = jnp.concatenate([wq3, zq], axis=-1).reshape(Q_LORA, MLA_HEADS * PAD_HEAD)
    wqr_p = jnp.concatenate(
        [jnp.zeros((Q_LORA, MLA_HEADS, QK_NOPE), F32),
         _rope_rotate_cols(wq3[..., QK_NOPE:]), zq], axis=-1
    ).reshape(Q_LORA, MLA_HEADS * PAD_HEAD)
    wkv3 = w_ukv.reshape(KV_LORA, MLA_HEADS, QK_NOPE + V_DIM)
    wk_p = jnp.concatenate(
        [wkv3[..., :QK_NOPE], jnp.zeros((KV_LORA, MLA_HEADS, PAD_HEAD - QK_NOPE), F32)],
        axis=-1).reshape(KV_LORA, MLA_HEADS * PAD_HEAD)
    wv = _pad_heads(wkv3[..., QK_NOPE:].reshape(KV_LORA, MLA_HEADS * V_DIM), MLA_HEADS)

    inv_freq = 1.0 / (ROPE_THETA ** (jnp.arange(0, QK_ROPE, 2, dtype=F32) / QK_ROPE))
    invf = jnp.concatenate([jnp.zeros((QK_NOPE,), F32), inv_freq, inv_freq,
                            jnp.zeros((PAD_HEAD - QK_NOPE - QK_ROPE,), F32)])[None, :]

    ts = SEQ_TILE
    row = lambda w: pl.BlockSpec((None, ts, w), lambda b, s: (b, s, 0))
    qk_w = MLA_HEADS * PAD_HEAD
    v_w = MLA_HEADS * PAD_HEAD
    return pl.pallas_call(
        _l0_front_kernel,
        grid=(bsz, seq // ts),
        in_specs=[row(D_MODEL), row(1),
                  _resident((D_MODEL, L0_COLS)), _resident((1, L0_COLS)),
                  _resident((CONV_WIDTH, SUBLANES, CONV_CH)), _resident((1, CONV_CH)),
                  _resident((1, CONV_CH)), _resident((1, CONV_CH)),
                  _resident((1, Q_LORA)), _resident((Q_LORA, qk_w)),
                  _resident((Q_LORA, qk_w)), _resident((1, KV_LORA)),
                  _resident((KV_LORA, qk_w)), _resident((KV_LORA, v_w)),
                  _resident((1, PAD_HEAD))],
        out_specs=[row(CONV_CH), row(qk_w), row(qk_w), row(v_w)],
        out_shape=[jax.ShapeDtypeStruct((bsz, seq, CONV_CH), BF16),
                   jax.ShapeDtypeStruct((bsz, seq, qk_w), BF16),
                   jax.ShapeDtypeStruct((bsz, seq, qk_w), BF16),
                   jax.ShapeDtypeStruct((bsz, seq, v_w), BF16)],
        scratch_shapes=[pltpu.VMEM((ts + CONV_HALO, CONV_CH), F32)],
        compiler_params=_params("arbitrary", "arbitrary"),
        name="l0_front",
    )(x, positions[..., None], w_in_p, b_in_p,
      jnp.broadcast_to(conv_w[:, None, :], (CONV_WIDTH, SUBLANES, CONV_CH)), conv_b[None, :],
      cln_g[None, :], cln_b[None, :], qn_g[None, :], wq_p.astype(BF16),
      wqr_p.astype(BF16), kvn_g[None, :], wk_p.astype(BF16), wv.astype(BF16), invf)


def _l1_front_kernel(x_ref, w_ref, b_ref, wv_ref, bv_ref, wf_ref, bf_ref,
                     q_ref, k_ref, v_ref, cneg_ref, carry):
    ts = x_ref.shape[0]

    @pl.when(pl.program_id(1) == 0)
    def _():
        carry[...] = jnp.zeros_like(carry)

    xb = x_ref[...].astype(BF16)
    mix = FOX_HEADS * FOX_HD
    scale = FOX_HD ** -0.5 * LOG2_E
    q_ref[...] = ((_dot(xb, w_ref[:, 0:mix]) + b_ref[:, 0:mix]) * scale).astype(BF16)
    k_ref[...] = (_dot(xb, w_ref[:, mix:2 * mix]) + b_ref[:, mix:2 * mix]).astype(BF16)
    v = _dot(xb, wv_ref[...]) + bv_ref[...]
    head_lanes = lax.broadcasted_iota(jnp.int32, (ts, LANES), 1) < FOX_HD
    for pair in range(FOX_HEADS // 2):
        both = v[:, pair * LANES:(pair + 1) * LANES]
        swapped = pltpu.roll(both, LANES - FOX_HD, axis=1)
        for head, vals in enumerate((both, swapped)):
            blk = 2 * pair + head
            v_ref[:, blk * PAD_HEAD:(blk + 1) * PAD_HEAD] = (
                jnp.where(head_lanes, vals, 1.0).astype(BF16))

    log_f = jax.nn.log_sigmoid(_dot_nt(wf_ref[...], xb) + bf_ref[...])
    r = lax.broadcasted_iota(jnp.int32, (ts, ts), 0)
    c = lax.broadcasted_iota(jnp.int32, (ts, ts), 1)
    upper = (r <= c).astype(F32)
    csum = jnp.dot(log_f, upper, precision=lax.Precision.HIGHEST,
                   preferred_element_type=F32) + carry[...]
    cneg_ref[...] = csum * -LOG2_E
    carry[...] = carry[...] + jnp.sum(log_f, axis=1, keepdims=True)


def _l1_front(x, w_in, b_in):
    bsz, seq, _ = x.shape
    mix = FOX_HEADS * FOX_HD
    ts = SEQ_TILE
    row = lambda w: pl.BlockSpec((None, ts, w), lambda b, s: (b, s, 0))
    w_qk = w_in[:, :2 * mix].astype(BF16)
    b_qk = b_in[None, :2 * mix]
    w_v = w_in[:, 2 * mix:3 * mix].astype(BF16)
    b_v = b_in[None, 2 * mix:3 * mix]
    v_w = FOX_HEADS * PAD_HEAD
    wf_t = w_in[:, 3 * mix:].T.astype(BF16)
    bf_t = b_in[3 * mix:][:, None]
    return pl.pallas_call(
        _l1_front_kernel,
        grid=(bsz, seq // ts),
        in_specs=[row(D_MODEL), _resident((D_MODEL, 2 * mix)), _resident((1, 2 * mix)),
                  _resident((D_MODEL, mix)), _resident((1, mix)),
                  _resident((FOX_HEADS, D_MODEL)), _resident((FOX_HEADS, 1))],
        out_specs=[row(mix), row(mix), row(v_w),
                   pl.BlockSpec((None, FOX_HEADS, ts), lambda b, s: (b, 0, s))],
        out_shape=[jax.ShapeDtypeStruct((bsz, seq, mix), BF16)] * 2
        + [jax.ShapeDtypeStruct((bsz, seq, v_w), BF16),
           jax.ShapeDtypeStruct((bsz, FOX_HEADS, seq), F32)],
        scratch_shapes=[pltpu.VMEM((FOX_HEADS, 1), F32)],
        compiler_params=_params("arbitrary", "arbitrary"),
        name="l1_front",
    )(x, w_qk, b_qk, w_v, b_v, wf_t, bf_t)


def _attn_kernel(*refs, packed, has_bias):
    if has_bias:
        q_ref, k_ref, v_ref, cneg_ref, o_ref = refs
    else:
        q_ref, k_ref, v_ref, o_ref = refs
        cneg_ref = None
    tq = ATTN_TILE
    tk = ATTN_KEY_TILE
    half = LANES // 2

    def query_tile(qi):
        q_rows = pl.ds(pl.multiple_of(qi * tq, tq), tq)
        q2 = q_ref[q_rows, :]
        if packed:
            lane = lax.broadcasted_iota(jnp.int32, q2.shape, 1)
            zero = jnp.zeros_like(q2)
            qs = (jnp.where(lane < half, q2, zero), jnp.where(lane >= half, q2, zero))
        else:
            qs = (q2[:, :PAD_HEAD], q2[:, PAD_HEAD:])

        def k_block(kj, head):
            rows = pl.ds(pl.multiple_of(kj * tk, tk), tk)
            if packed:
                return k_ref[rows, :]
            return k_ref[rows, head * PAD_HEAD:(head + 1) * PAD_HEAD]

        def step(kj, carry, diag):
            rows = pl.ds(pl.multiple_of(kj * tk, tk), tk)
            skip = 0 if diag is None else diag * tk
            out = []
            for head in range(2):
                m_all, acc_all = carry[head]
                m, acc = m_all[skip:], acc_all[skip:]
                s = _dot_nt(qs[head][skip:], k_block(kj, head))
                if has_bias:
                    s = s + cneg_ref[head, :, rows]
                if diag is not None:
                    r = lax.broadcasted_iota(jnp.int32, s.shape, 0)
                    c = lax.broadcasted_iota(jnp.int32, s.shape, 1)
                    s = jnp.where(r >= c, s, NEG_INF)
                m_new = jnp.maximum(m, jnp.max(s, axis=-1, keepdims=True))
                alpha = jnp.exp2(m - m_new)
                p = jnp.exp2(s - m_new).astype(BF16)
                vblk = v_ref[rows, head * PAD_HEAD:(head + 1) * PAD_HEAD]
                acc = alpha * acc + _dot(p, vblk)
                if skip:
                    m_new = jnp.concatenate([m_all[:skip], m_new], axis=0)
                    acc = jnp.concatenate([acc_all[:skip], acc], axis=0)
                out.append((m_new, acc))
            return tuple(out)

        init = tuple((jnp.full((tq, 1), NEG_INF, F32), jnp.zeros((tq, LANES), F32))
                     for _ in range(2))
        per_tile = tq // tk
        n_full = qi * per_tile

        def group(first, c, size):
            for t in range(size):
                c = step(first + t, c, None)
            return c

        carry = lax.fori_loop(0, n_full // ATTN_UNROLL,
                              lambda j, c: group(ATTN_UNROLL * j, c, ATTN_UNROLL), init)
        done = (n_full // ATTN_UNROLL) * ATTN_UNROLL
        size = ATTN_UNROLL // 2
        while size >= 1:
            take = (n_full - done) >= size
            carry = lax.cond(take, lambda c, f=done, n=size: group(f, c, n), lambda c: c,
                             carry)
            done = done + jnp.where(take, size, 0)
            size //= 2
        for d in range(per_tile):
            carry = step(n_full + d, carry, d)
        (_, acc_a), (_, acc_b) = carry
        out_a = acc_a / pltpu.roll(acc_a, half, axis=1)
        out_b = acc_b / pltpu.roll(acc_b, half, axis=1)
        lane_o = lax.broadcasted_iota(jnp.int32, (tq, LANES), 1)
        o_ref[q_rows, :] = jnp.where(lane_o < half, out_a,
                                     pltpu.roll(out_b, half, axis=1)).astype(o_ref.dtype)

    pl.loop(0, q_ref.shape[0] // tq)(query_tile)


def _attention(q, k, v, cneg, packed):
    bsz, seq, v_w = v.shape
    pairs = v_w // (2 * PAD_HEAD)
    qk_w = q.shape[-1] // pairs
    in_specs = [pl.BlockSpec((None, seq, qk_w), lambda b, p: (b, 0, p)),
                pl.BlockSpec((None, seq, qk_w), lambda b, p: (b, 0, p)),
                pl.BlockSpec((None, seq, 2 * PAD_HEAD), lambda b, p: (b, 0, p))]
    args = [q, k, v]
    if cneg is not None:
        in_specs.append(pl.BlockSpec((None, 2, 1, seq), lambda b, p: (b, p, 0, 0)))
        args.append(cneg)
    return pl.pallas_call(
        functools.partial(_attn_kernel, packed=packed, has_bias=cneg is not None),
        grid=(bsz, pairs),
        in_specs=in_specs,
        out_specs=pl.BlockSpec((None, seq, LANES), lambda b, p: (b, 0, p)),
        out_shape=jax.ShapeDtypeStruct((bsz, seq, pairs * LANES), BF16),
        compiler_params=_params("arbitrary", "arbitrary"),
        name="attn_packed" if packed else "attn_padded",
    )(*args)


def _first_hit(hits, found):
    out = []
    for hcur in hits:
        take = jnp.logical_and(hcur, jnp.logical_not(found))
        found = jnp.logical_or(found, take)
        out.append(take)
    return out, found


def _mix_out_router_kernel(a_ref, b_ref, x_ref, wa_ref, wb_ref, bo_ref, g_ref, be_ref,
                           rw_ref, rwl_ref, rb_ref,
                           x1_ref, x1p_ref, e_ref, gate_ref, rank_ref, cnt_ref, cnt, gate_t):
    tm = x_ref.shape[0]

    @pl.when(pl.program_id(0) == 0)
    def _():
        cnt[...] = jnp.zeros_like(cnt)
        gate_t[...] = jnp.zeros_like(gate_t)

    mix = _dot(a_ref[...], wa_ref[...]) + _dot(b_ref[...], wb_ref[...]) + bo_ref[...]
    x1 = _layer_norm(DN_ALPHA * x_ref[...] + mix, g_ref[...], be_ref[...])
    x1_ref[...] = x1
    _store_packed(x1p_ref, _pack_rows(x1))

    x_hi = x1.astype(BF16)
    x_lo = (x1 - x_hi.astype(F32)).astype(BF16)
    logits = (_dot_nt(rw_ref[...], x_hi) + _dot_nt(rw_ref[...], x_lo)
              + _dot_nt(rwl_ref[...], x_hi))
    aff = jax.nn.sigmoid(logits)
    choice = aff + rb_ref[...]
    per_group = N_EXPERTS // N_GROUPS
    sub = lax.broadcasted_iota(jnp.int32, (per_group, tm), 0)
    groups = [choice[g * per_group:(g + 1) * per_group, :] for g in range(N_GROUPS)]

    gscore = []
    for cg in groups:
        m1 = jnp.max(cg, axis=0, keepdims=True)
        i1 = jnp.min(jnp.where(cg == m1, sub, per_group), axis=0, keepdims=True)
        m2 = jnp.max(jnp.where(sub == i1, NEG_INF, cg), axis=0, keepdims=True)
        gscore.append(m1 + m2)

    gsel = [jnp.zeros((1, tm), jnp.bool_) for _ in range(N_GROUPS)]
    for _ in range(TOPK_GROUPS):
        best = functools.reduce(jnp.maximum, gscore)
        takes, _ = _first_hit([gs == best for gs in gscore], jnp.zeros((1, tm), jnp.bool_))
        gsel = [jnp.logical_or(a, t) for a, t in zip(gsel, takes)]
        gscore = [jnp.where(t, NEG_INF, gs) for gs, t in zip(gscore, takes)]

    masked = [jnp.where(gs, cg, NEG_INF) for gs, cg in zip(gsel, groups)]
    eid = [sub + g * per_group for g in range(N_GROUPS)]
    affs = [aff[g * per_group:(g + 1) * per_group, :] for g in range(N_GROUPS)]
    sel = [jnp.zeros((per_group, tm), jnp.bool_) for _ in range(N_GROUPS)]
    picked_e, picked_w = [], []
    for _ in range(TOP_K):
        best = jnp.max(functools.reduce(jnp.maximum, masked), axis=0, keepdims=True)
        cand = [jnp.where(mg == best, ig, N_EXPERTS) for mg, ig in zip(masked, eid)]
        idx = jnp.min(functools.reduce(jnp.minimum, cand), axis=0, keepdims=True)
        onehot = [ig == idx for ig in eid]
        w = functools.reduce(
            jnp.add, [jnp.sum(jnp.where(oh, ag, 0.0), axis=0, keepdims=True)
                      for oh, ag in zip(onehot, affs)])
        picked_e.append(idx)
        picked_w.append(w)
        sel = [jnp.logical_or(sg, oh) for sg, oh in zip(sel, onehot)]
        masked = [jnp.where(oh, NEG_INF, mg) for mg, oh in zip(masked, onehot)]

    wsum = functools.reduce(jnp.add, picked_w)

    sel_f = jnp.concatenate([sg.astype(F32) for sg in sel], axis=0)
    r = lax.broadcasted_iota(jnp.int32, (tm, tm), 0)
    c = lax.broadcasted_iota(jnp.int32, (tm, tm), 1)
    before = _dot(sel_f.astype(BF16), (r < c).astype(BF16)) + cnt[...]
    eall = lax.broadcasted_iota(jnp.int32, (N_EXPERTS, tm), 0)
    for kk in range(TOP_K):
        rank = jnp.sum(jnp.where(eall == picked_e[kk], before, 0.0), axis=0, keepdims=True)
        e_ref[kk:kk + 1, :] = picked_e[kk]
        gate_t[kk:kk + 1, :] = picked_w[kk] / wsum * ROUTED_SCALE
        rank_ref[kk:kk + 1, :] = rank.astype(jnp.int32)
    pad = SUBLANES - TOP_K
    e_ref[TOP_K:, :] = jnp.zeros((pad, tm), jnp.int32)
    rank_ref[TOP_K:, :] = jnp.zeros((pad, tm), jnp.int32)
    gate_ref[...] = gate_t[...].T
    cnt[...] = cnt[...] + jnp.sum(sel_f, axis=1, keepdims=True)
    cnt_ref[...] = jnp.broadcast_to(cnt[...], cnt_ref.shape)


def _mix_out_router(a, b, x, w_out, b_out, ln_g, ln_b, router_w, router_bias):
    n_tok = x.shape[0]
    tm = TOKEN_TILE
    half = w_out.shape[0] // 2
    row = lambda w: pl.BlockSpec((tm, w), lambda i: (i, 0))
    b_spec = (pl.BlockSpec((tm, half), lambda i: (i, 1)) if b is a
              else pl.BlockSpec((tm, half), lambda i: (i, 0)))
    lane_row = pl.BlockSpec((SUBLANES, tm), lambda i: (0, i))
    packed = pl.BlockSpec((tm // SUBLANES, PACK_BLOCKS, SUBLANES, LANES),
                          lambda i: (i, 0, 0, 0))
    w_bf = w_out.astype(BF16)
    rw_t = router_w.T
    rw_hi = rw_t.astype(BF16)
    rw_lo = (rw_t - rw_hi.astype(F32)).astype(BF16)
    return pl.pallas_call(
        _mix_out_router_kernel,
        grid=(n_tok // tm,),
        in_specs=[row(half), b_spec, row(D_MODEL),
                  pl.BlockSpec((half, D_MODEL), lambda i: (0, 0)),
                  pl.BlockSpec((half, D_MODEL), lambda i: (1, 0)),
                  _resident((1, D_MODEL)), _resident((1, D_MODEL)), _resident((1, D_MODEL)),
                  _resident((N_EXPERTS, D_MODEL)), _resident((N_EXPERTS, D_MODEL)),
                  _resident((N_EXPERTS, 1))],
        out_specs=[row(D_MODEL), packed, lane_row, row(LANES), lane_row,
                   _resident((N_EXPERTS, LANES))],
        out_shape=[jax.ShapeDtypeStruct((n_tok, D_MODEL), F32),
                   jax.ShapeDtypeStruct((n_tok // SUBLANES, PACK_BLOCKS, SUBLANES, LANES),
                                        jnp.int32),
                   jax.ShapeDtypeStruct((SUBLANES, n_tok), jnp.int32),
                   jax.ShapeDtypeStruct((n_tok, LANES), F32),
                   jax.ShapeDtypeStruct((SUBLANES, n_tok), jnp.int32),
                   jax.ShapeDtypeStruct((N_EXPERTS, LANES), F32)],
        scratch_shapes=[pltpu.VMEM((N_EXPERTS, 1), F32), pltpu.VMEM((LANES, tm), F32)],
        compiler_params=_params("arbitrary"),
        name="mix_out_router",
    )(a, b, x, w_bf, w_bf, b_out[None, :], ln_g[None, :], ln_b[None, :],
      rw_hi, rw_lo, router_bias[:, None])


def _expert_ffn_kernel(tile_e_ref, tile_on_ref, xs_ref, *refs):
    w_refs = refs[:3 * FFN_SLOTS]
    ys_ref = refs[3 * FFN_SLOTS]
    wg_bf, wu_bf, wd_bf = refs[3 * FFN_SLOTS + 1:]
    i = pl.program_id(0)
    sub = GROUP_TILE // SUBLANES

    for slot in range(FFN_SLOTS):
        t = FFN_SLOTS * i + slot
        fresh = jnp.logical_or(
            i == 0, tile_e_ref[t] != tile_e_ref[jnp.maximum(t - FFN_SLOTS, 0)])

        @pl.when(fresh)
        def _(slot=slot):
            wg_ref, wu_ref, wd_ref = w_refs[3 * slot:3 * slot + 3]
            wg_bf[slot] = wg_ref[...].astype(BF16)
            wu_bf[slot] = wu_ref[...].astype(BF16)
            wd_bf[slot] = wd_ref[...].astype(BF16)

    on = tile_on_ref[FFN_SLOTS * i] > 0

    @pl.when(on)
    def _():
        for slot in range(FFN_SLOTS):
            rows = pl.ds(slot * sub, sub)
            lo, hi = _unpack_rows(_load_packed(xs_ref.at[rows]))
            lo = lo.astype(BF16)
            hi = hi.astype(BF16)
            gate = (_dot(lo, wg_bf[slot, :PACK_COLS, :]) + _dot(hi, wg_bf[slot, PACK_COLS:, :]))
            up = (_dot(lo, wu_bf[slot, :PACK_COLS, :]) + _dot(hi, wu_bf[slot, PACK_COLS:, :]))
            hid = (jax.nn.silu(gate) * up).astype(BF16)
            _store_packed(ys_ref.at[rows], _pack_rows(_dot(hid, wd_bf[slot])))

    @pl.when(jnp.logical_not(on))
    def _():
        ys_ref[...] = jnp.zeros_like(ys_ref)


def _expert_ffn(xs, tile_e, tile_on, w_gate, w_up, w_down):
    rows_per_step = GROUP_TILE * FFN_SLOTS
    packed = pl.BlockSpec((rows_per_step // SUBLANES, PACK_BLOCKS, SUBLANES, LANES),
                          lambda i, te, on: (i, 0, 0, 0))
    w_specs = []
    for slot in range(FFN_SLOTS):
        pick = lambda i, te, on, slot=slot: (te[FFN_SLOTS * i + slot], 0, 0)
        w_specs += [pl.BlockSpec((None, D_MODEL, EXPERT_HID), pick),
                    pl.BlockSpec((None, D_MODEL, EXPERT_HID), pick),
                    pl.BlockSpec((None, EXPERT_HID, D_MODEL), pick)]
    grid_spec = pltpu.PrefetchScalarGridSpec(
        num_scalar_prefetch=2,
        grid=(xs.shape[0] * SUBLANES // rows_per_step,),
        in_specs=[packed] + w_specs,
        out_specs=packed,
        scratch_shapes=[pltpu.VMEM((FFN_SLOTS, D_MODEL, EXPERT_HID), BF16),
                        pltpu.VMEM((FFN_SLOTS, D_MODEL, EXPERT_HID), BF16),
                        pltpu.VMEM((FFN_SLOTS, EXPERT_HID, D_MODEL), BF16)],
    )
    return pl.pallas_call(
        _expert_ffn_kernel,
        grid_spec=grid_spec,
        out_shape=jax.ShapeDtypeStruct(xs.shape, jnp.int32),
        compiler_params=_params("arbitrary"),
        name="expert_ffn",
    )(tile_e, tile_on, xs, *([w_gate, w_up, w_down] * FFN_SLOTS))


def _moe_out_kernel(x1_ref, yg_ref, gate_ref, sg_ref, su_ref, sd_ref, g_ref, b_ref, *rest):
    o_ref = rest[-1]
    x1 = x1_ref[...]
    xb = x1.astype(BF16)
    hid = jax.nn.silu(_dot(xb, sg_ref[...])) * _dot(xb, su_ref[...])
    shared = _dot(hid.astype(BF16), sd_ref[...])
    gates = gate_ref[...]
    lo_acc = shared[:, :PACK_COLS]
    hi_acc = shared[:, PACK_COLS:]
    for kk in range(TOP_K):
        lo, hi = _unpack_rows(_load_packed(yg_ref.at[kk]))
        w = gates[:, kk:kk + 1]
        lo_acc = lo_acc + w * lo
        hi_acc = hi_acc + w * hi
    moe = jnp.concatenate([lo_acc, hi_acc], axis=1)
    o_ref[...] = _layer_norm(DN_ALPHA * x1 + moe, g_ref[...], b_ref[...])


def _moe_out(x1, yg, gates, sh_gate, sh_up, sh_down, ln_g, ln_b, part, earlier):
    n_tok = x1.shape[0]
    tm = TOKEN_TILE
    steps = n_tok // tm // MOE_PARTS
    first = part * steps
    row = lambda w: pl.BlockSpec((tm, w), lambda i: (i + first, 0))
    in_specs = [row(D_MODEL),
                pl.BlockSpec((TOP_K, tm // SUBLANES, PACK_BLOCKS, SUBLANES, LANES),
                             lambda i: (0, i, 0, 0, 0)),
                row(LANES),
                _resident((D_MODEL, EXPERT_HID)), _resident((D_MODEL, EXPERT_HID)),
                _resident((EXPERT_HID, D_MODEL)), _resident((1, D_MODEL)),
                _resident((1, D_MODEL))]
    args = [x1, yg, gates, sh_gate.astype(BF16), sh_up.astype(BF16), sh_down.astype(BF16),
            ln_g[None, :], ln_b[None, :]]
    aliases = {}
    if earlier is not None:
        in_specs.append(pl.BlockSpec(memory_space=pl.ANY))
        args.append(earlier)
        aliases = {len(args) - 1: 0}
    return pl.pallas_call(
        _moe_out_kernel,
        grid=(steps,),
        in_specs=in_specs,
        out_specs=row(D_MODEL),
        out_shape=jax.ShapeDtypeStruct((n_tok, D_MODEL), F32),
        input_output_aliases=aliases,
        compiler_params=_params("arbitrary"),
        name="moe_out",
    )(*args)


def _slot_kernel(offset_ref, e_ref, rank_ref, pos_ref):
    e = e_ref[...]
    base = jnp.zeros(e.shape, jnp.int32)
    for ex in range(N_EXPERTS):
        base = jnp.where(e == ex, offset_ref[ex], base)
    pos_ref[...] = base + rank_ref[...]


def _slots(offset, e6, rank6):
    n_tok = e6.shape[1]
    tl = SLOT_TILE
    lane_row = pl.BlockSpec((SUBLANES, tl), lambda i, off: (0, i))
    return pl.pallas_call(
        _slot_kernel,
        grid_spec=pltpu.PrefetchScalarGridSpec(
            num_scalar_prefetch=1, grid=(n_tok // tl,),
            in_specs=[lane_row, lane_row], out_specs=lane_row),
        out_shape=jax.ShapeDtypeStruct((SUBLANES, n_tok), jnp.int32),
        compiler_params=_params("arbitrary"),
        name="slots",
    )(offset, e6, rank6)


def _sc_mesh():
    return plsc.VectorSubcoreMesh(core_axis_name="c", subcore_axis_name="s")


def _sc_worker():
    return lax.axis_index("s") * SC_CORES + lax.axis_index("c")


def _sc_dispatch(x_sub, idx, n_out_sub):
    windows = x_sub.shape[0] // (SC_WORKERS * SC_WINDOW)

    @functools.partial(
        pl.kernel, mesh=_sc_mesh(),
        out_type=jax.ShapeDtypeStruct((n_out_sub, LANES), jnp.int32),
        scratch_types=[pltpu.VMEM((SC_WINDOW, LANES), jnp.int32),
                       pltpu.VMEM((TOP_K, SC_WINDOW), jnp.int32)],
        name="sc_dispatch")
    def run(x_hbm, idx_hbm, out_hbm, rows_v, idx_v):
        wid = _sc_worker()

        @pl.loop(0, windows)
        def _(c):
            base = (wid * windows + c) * SC_WINDOW
            pltpu.sync_copy(x_hbm.at[pl.ds(base, SC_WINDOW)], rows_v)
            pltpu.sync_copy(idx_hbm.at[wid, c], idx_v)
            for kk in range(TOP_K):
                pltpu.sync_copy(rows_v, out_hbm.at[idx_v.at[kk]])

    return run(x_sub, idx)


def _sc_combine(y_sub, idx, n_tok_sub):
    windows = n_tok_sub // (SC_WORKERS * SC_WINDOW)

    @functools.partial(
        pl.kernel, mesh=_sc_mesh(),
        out_type=jax.ShapeDtypeStruct((TOP_K, n_tok_sub, LANES), jnp.int32),
        scratch_types=[pltpu.VMEM((2, SC_WINDOW, LANES), jnp.int32),
                       pltpu.VMEM((TOP_K, SC_WINDOW), jnp.int32),
                       pltpu.SemaphoreType.DMA((2,)),
                       pltpu.SemaphoreType.DMA((2,))],
        name="sc_combine")
    def run(y_hbm, idx_hbm, out_hbm, rows_v, idx_v, gather_sem, write_sem):
        wid = _sc_worker()

        def gather(kk):
            buf = kk % 2
            return pltpu.async_copy(y_hbm.at[idx_v.at[kk]], rows_v.at[buf], gather_sem.at[buf])

        @pl.loop(0, windows)
        def _(c):
            base = (wid * windows + c) * SC_WINDOW
            pltpu.sync_copy(idx_hbm.at[wid, c], idx_v)
            writes = []
            pending = gather(0)
            for kk in range(TOP_K):
                pending.wait()
                if kk + 1 < TOP_K:
                    if kk >= 1:
                        writes[kk - 1].wait()
                    pending = gather(kk + 1)
                writes.append(pltpu.async_copy(
                    rows_v.at[kk % 2], out_hbm.at[kk, pl.ds(base, SC_WINDOW)],
                    write_sem.at[kk % 2]))
            writes[TOP_K - 2].wait()
            writes[TOP_K - 1].wait()

    return run(y_sub, idx)


def _moe(x1, x1p, e6, gates, rank6, counts, exp_w_gate, exp_w_up, exp_w_down,
         sh_w_gate, sh_w_up, sh_w_down, ln_g, ln_b):
    n_tok = x1.shape[0]
    tg = GROUP_TILE
    n_rows = n_tok * TOP_K + N_EXPERTS * tg
    n_tiles = n_rows // tg

    cnt = counts[:, 0].astype(jnp.int32)
    tiles_per_e = (cnt + tg - 1) // tg
    tile_end = jnp.cumsum(tiles_per_e)
    offset = (tile_end - tiles_per_e) * tg
    tile_ids = jnp.arange(n_tiles, dtype=jnp.int32)
    tile_on = (tile_ids < tile_end[-1]).astype(jnp.int32)
    tile_e = jnp.minimum(
        jnp.sum((tile_ids[:, None] >= tile_end[None, :]).astype(jnp.int32), axis=1),
        N_EXPERTS - 1)
    tile_e = jnp.where(tile_on > 0, tile_e, tile_e[jnp.maximum(tile_end[-1] - 1, 0)])

    pos = _slots(offset, e6, rank6)[:TOP_K]

    group = PACK_BLOCKS * SUBLANES
    sub0 = (pos // SUBLANES) * group + pos % SUBLANES
    lane = jnp.arange(SC_WINDOW)
    spread = (jnp.arange(group)[:, None]
              == (lane // group) * SUBLANES + lane % SUBLANES).astype(F32)
    idx = jnp.einsum("kws,sd->kwd", sub0.reshape(TOP_K, n_tok // group, group).astype(F32),
                     spread, precision=lax.Precision.HIGHEST).astype(jnp.int32)
    idx = idx + ((lane % group) // SUBLANES * SUBLANES).astype(jnp.int32)
    n_sub = n_tok * PACK_BLOCKS
    idx = idx.reshape(TOP_K, n_sub)

    def windowed(ix):
        windows = ix.shape[1] // (SC_WORKERS * SC_WINDOW)
        return ix.reshape(TOP_K, SC_WORKERS, windows, SC_WINDOW).transpose(1, 2, 0, 3)

    xs = _sc_dispatch(x1p.reshape(n_sub, LANES), windowed(idx), n_rows * PACK_BLOCKS)
    ys = _expert_ffn(xs.reshape(n_rows // SUBLANES, PACK_BLOCKS, SUBLANES, LANES),
                     tile_e, tile_on, exp_w_gate, exp_w_up, exp_w_down)
    ys = ys.reshape(n_rows * PACK_BLOCKS, LANES)
    part_sub = n_sub // MOE_PARTS
    out = None
    for part in range(MOE_PARTS):
        yg = _sc_combine(ys, windowed(idx[:, part * part_sub:(part + 1) * part_sub]), part_sub)
        yg = yg.reshape(TOP_K, part_sub // (PACK_BLOCKS * SUBLANES), PACK_BLOCKS,
                        SUBLANES, LANES)
        out = _moe_out(x1, yg, gates, sh_w_gate, sh_w_up, sh_w_down, ln_g, ln_b, part, out)
    return out


def kernel(x, positions,
           l0_w_in, l0_b_in, l0_conv_w, l0_conv_b, l0_conv_ln_g, l0_conv_ln_b,
           l0_q_norm_g, l0_w_uq, l0_kv_norm_g, l0_w_ukv, l0_w_out, l0_b_out,
           l0_ln1_g, l0_ln1_b,
           l0_router_w, l0_router_bias, l0_exp_w_gate, l0_exp_w_up, l0_exp_w_down,
           l0_sh_w_gate, l0_sh_w_up, l0_sh_w_down, l0_ln2_g, l0_ln2_b,
           l1_w_in, l1_b_in, l1_w_out, l1_b_out, l1_ln1_g, l1_ln1_b,
           l1_router_w, l1_router_bias, l1_exp_w_gate, l1_exp_w_up, l1_exp_w_down,
           l1_sh_w_gate, l1_sh_w_up, l1_sh_w_down, l1_ln2_g, l1_ln2_b):
    bsz, seq, d = x.shape
    n_tok = bsz * seq

    u, q, k, v = _l0_front(x, positions, l0_w_in, l0_b_in, l0_conv_w, l0_conv_b,
                           l0_conv_ln_g, l0_conv_ln_b, l0_q_norm_g, l0_w_uq,
                           l0_kv_norm_g, l0_w_ukv)
    o = _attention(q, k, v, None, packed=False)
    x_flat = x.reshape(n_tok, d)
    routing = _mix_out_router(u.reshape(n_tok, -1), o.reshape(n_tok, -1), x_flat,
                              l0_w_out, l0_b_out, l0_ln1_g, l0_ln1_b,
                              l0_router_w, l0_router_bias)
    x2 = _moe(*routing, l0_exp_w_gate, l0_exp_w_up, l0_exp_w_down,
              l0_sh_w_gate, l0_sh_w_up, l0_sh_w_down, l0_ln2_g, l0_ln2_b)

    q, k, v, cneg = _l1_front(x2.reshape(bsz, seq, d), l1_w_in, l1_b_in)
    o = _attention(q, k, v, cneg[:, :, None, :], packed=True).reshape(n_tok, -1)
    routing = _mix_out_router(o, o, x2, l1_w_out, l1_b_out, l1_ln1_g, l1_ln1_b,
                              l1_router_w, l1_router_bias)
    x3 = _moe(*routing, l1_exp_w_gate, l1_exp_w_up, l1_exp_w_down,
              l1_sh_w_gate, l1_sh_w_up, l1_sh_w_down, l1_ln2_g, l1_ln2_b)
    return x3.reshape(bsz, seq, d)
```

```python
import functools

import jax
import jax.numpy as jnp
from jax import lax
from jax.experimental import pallas as pl
from jax.experimental.pallas import tpu as pltpu
from jax.experimental.pallas import tpu_sc as plsc

D_MODEL = 1024
DEPTH = 2
DN_ALPHA = (2 * DEPTH) ** 0.25
LN_EPS = 1e-5
RMS_EPS = 1e-6

CONV_CH = 512
CONV_WIDTH = 31
MLA_HEADS = 8
QK_NOPE = 64
QK_ROPE = 32
V_DIM = 64
Q_LORA = 256
KV_LORA = 128
ROPE_THETA = 10000.0
FOX_HEADS = 16
FOX_HD = 64
N_EXPERTS = 64
TOP_K = 6
N_GROUPS = 8
TOPK_GROUPS = 4
EXPERT_HID = 256
ROUTED_SCALE = 2.5

LANES = 128
SUBLANES = 8
V7X_VMEM_BYTES = 64 * 1024 * 1024
VMEM_LIMIT_BYTES = V7X_VMEM_BYTES * 7 // 8

SEQ_TILE = 512
TOKEN_TILE = 512
SLOT_TILE = 4096
ATTN_TILE = 1024
ATTN_KEY_TILE = 512
ATTN_UNROLL = 2
GROUP_TILE = 512
FFN_SLOTS = 2
CONV_ROWS = 32
CONV_HALO = 32
PAD_HEAD = 128

PACK_COLS = D_MODEL // 2
PACK_BLOCKS = PACK_COLS // LANES
SC_CORES = 2
SC_SUBCORES = 16
SC_WORKERS = SC_CORES * SC_SUBCORES
SC_WINDOW = 128
MOE_PARTS = 4
BF16 = jnp.bfloat16
F32 = jnp.float32
NEG_INF = float("-inf")
LOG2_E = 1.4426950408889634


def _params(*semantics):
    return pltpu.CompilerParams(dimension_semantics=semantics,
                                vmem_limit_bytes=VMEM_LIMIT_BYTES)


def _resident(shape):
    nd = len(shape)
    return pl.BlockSpec(shape, lambda *_: (0,) * nd)


def _dot(a, b):
    return jnp.dot(a, b, preferred_element_type=F32)


def _dot_nt(a, b, precision=None):
    return lax.dot_general(a, b, (((1,), (1,)), ((), ())),
                           precision=precision, preferred_element_type=F32)


def _layer_norm(x, g, b):
    mu = jnp.mean(x, axis=-1, keepdims=True)
    xc = x - mu
    var = jnp.mean(xc * xc, axis=-1, keepdims=True)
    return xc * lax.rsqrt(var + LN_EPS) * g + b


def _rms_norm(x, g):
    return x * lax.rsqrt(jnp.mean(x * x, axis=-1, keepdims=True) + RMS_EPS) * g


def _ones_upper_half(width):
    lane = lax.broadcasted_iota(jnp.int32, (1, width), 1)
    return jnp.where((lane & (PAD_HEAD - 1)) >= V_DIM, 1.0, 0.0).astype(F32)


def _pad_heads(w, heads):
    d = w.shape[-1] // heads
    w3 = w.reshape(w.shape[:-1] + (heads, d))
    pad = jnp.zeros(w.shape[:-1] + (heads, PAD_HEAD - d), w.dtype)
    return jnp.concatenate([w3, pad], axis=-1).reshape(w.shape[:-1] + (heads * PAD_HEAD,))


def _pack_rows(x):
    lo = lax.bitcast_convert_type(x[:, :PACK_COLS].astype(BF16).astype(F32), jnp.uint32)
    hi = lax.bitcast_convert_type(x[:, PACK_COLS:].astype(BF16).astype(F32), jnp.uint32)
    word = (lo >> 16) | (hi & jnp.uint32(0xFFFF0000))
    return lax.bitcast_convert_type(word, jnp.int32)


def _unpack_rows(words):
    u = lax.bitcast_convert_type(words, jnp.uint32)
    lo = lax.bitcast_convert_type(u << 16, F32)
    hi = lax.bitcast_convert_type(u & jnp.uint32(0xFFFF0000), F32)
    return lo, hi


def _store_packed(ref, words):
    groups = words.shape[0] // SUBLANES
    for cb in range(PACK_BLOCKS):
        ref[:, cb, :, :] = words[:, cb * LANES:(cb + 1) * LANES].reshape(groups, SUBLANES, LANES)


def _load_packed(ref):
    rows = ref.shape[0] * SUBLANES
    return jnp.concatenate(
        [ref[:, cb, :, :].reshape(rows, LANES) for cb in range(PACK_BLOCKS)], axis=1)


L0_A = 0
L0_G = CONV_CH
L0_Q = 2 * CONV_CH
L0_KV = L0_Q + Q_LORA
L0_KR = L0_KV + KV_LORA
L0_KRR = L0_KR + PAD_HEAD
L0_COLS = L0_KRR + PAD_HEAD


def _l0_front_kernel(x_ref, pos_ref, w_in_ref, b_in_ref, conv_w_ref, conv_b_ref,
                     cln_g_ref, cln_b_ref, qn_g_ref, wq_ref, wqr_ref, kvn_g_ref,
                     wk_ref, wv_ref, invf_ref,
                     u_ref, q_ref, k_ref, v_ref, ubuf):
    ts = x_ref.shape[0]

    @pl.when(pl.program_id(1) == 0)
    def _():
        ubuf[0:CONV_HALO, :] = jnp.zeros((CONV_HALO, CONV_CH), F32)

    h = _dot(x_ref[...].astype(BF16), w_in_ref[...]) + b_in_ref[...]

    ubuf[CONV_HALO:CONV_HALO + ts, :] = (
        h[:, L0_A:L0_A + CONV_CH] * jax.nn.sigmoid(h[:, L0_G:L0_G + CONV_CH]))
    first_tap = CONV_HALO - (CONV_WIDTH - 1)

    def conv_chunk(c, carry):
        base = pl.multiple_of(c * CONV_ROWS, CONV_ROWS)
        acc = jnp.broadcast_to(conv_b_ref[...], (CONV_ROWS, CONV_CH))
        for res in range(SUBLANES):
            rows = CONV_ROWS + (SUBLANES if res else 0)
            part = None
            for off in range(res, first_tap + CONV_WIDTH, SUBLANES):
                j = off - first_tap
                if j < 0:
                    continue
                window = ubuf[pl.ds(base + (off - res), rows), :]
                term = window.reshape(rows // SUBLANES, SUBLANES, CONV_CH) * conv_w_ref[j][None]
                part = term if part is None else part + term
            acc = acc + part.reshape(rows, CONV_CH)[res:res + CONV_ROWS, :]
        y = _layer_norm(acc, cln_g_ref[...], cln_b_ref[...])
        u_ref[pl.ds(base, CONV_ROWS), :] = (y * jax.nn.sigmoid(y)).astype(BF16)
        return carry

    lax.fori_loop(0, ts // CONV_ROWS, conv_chunk, 0, unroll=4)
    ubuf[0:CONV_HALO, :] = ubuf[ts:ts + CONV_HALO, :]

    ang = pos_ref[...].astype(F32) * invf_ref[...]
    cos = jnp.cos(ang)
    sin = jnp.sin(ang)
    scale = (QK_NOPE + QK_ROPE) ** -0.5 * LOG2_E

    qn = _rms_norm(h[:, L0_Q:L0_Q + Q_LORA], qn_g_ref[...]).astype(BF16)
    q = _dot(qn, wq_ref[...])
    q_rot = _dot(qn, wqr_ref[...])
    cos_s = cos * scale
    sin_s = sin * scale
    for hh in range(MLA_HEADS):
        blk = slice(hh * PAD_HEAD, (hh + 1) * PAD_HEAD)
        q_ref[:, blk] = (q[:, blk] * cos_s + q_rot[:, blk] * sin_s).astype(BF16)

    kvn = _rms_norm(h[:, L0_KV:L0_KV + KV_LORA], kvn_g_ref[...]).astype(BF16)
    k_nope = _dot(kvn, wk_ref[...])
    k_pe = h[:, L0_KR:L0_KR + PAD_HEAD] * cos + h[:, L0_KRR:L0_KRR + PAD_HEAD] * sin
    for hh in range(MLA_HEADS):
        blk = slice(hh * PAD_HEAD, (hh + 1) * PAD_HEAD)
        k_ref[:, blk] = (k_nope[:, blk] + k_pe).astype(BF16)
    v_ref[...] = (_dot(kvn, wv_ref[...]) + _ones_upper_half(v_ref.shape[1])).astype(BF16)


def _rope_rotate_cols(w):
    half = QK_ROPE // 2
    return jnp.concatenate([-w[..., half:], w[..., :half]], axis=-1)


def _l0_front(x, positions, w_in, b_in, conv_w, conv_b, cln_g, cln_b,
              qn_g, w_uq, kvn_g, w_ukv):
    bsz, seq, _ = x.shape
    pad_lo = jnp.zeros((D_MODEL, QK_NOPE), F32)
    pad_hi = jnp.zeros((D_MODEL, PAD_HEAD - QK_NOPE - QK_ROPE), F32)
    i3 = L0_KR
    w_kr = w_in[:, i3:i3 + QK_ROPE]
    w_in_p = jnp.concatenate(
        [w_in[:, :i3], pad_lo, w_kr, pad_hi, pad_lo, _rope_rotate_cols(w_kr), pad_hi],
        axis=1).astype(BF16)
    b_kr = b_in[i3:i3 + QK_ROPE]
    zlo = jnp.zeros((QK_NOPE,), F32)
    zhi = jnp.zeros((PAD_HEAD - QK_NOPE - QK_ROPE,), F32)
    b_in_p = jnp.concatenate(
        [b_in[:i3], zlo, b_kr, zhi, zlo, _rope_rotate_cols(b_kr), zhi])[None, :]

    dq = QK_NOPE + QK_ROPE
    wq3 = w_uq.reshape(Q_LORA, MLA_HEADS, dq)
    zq = jnp.zeros((Q_LORA, MLA_HEADS, PAD_HEAD - dq), F32)
    wq_p = jnp.concatenate([wq3, zq], axis=-1).reshape(Q_LORA, MLA_HEADS * PAD_HEAD)
    wqr_p = jnp.concatenate(
        [jnp.zeros((Q_LORA, MLA_HEADS, QK_NOPE), F32),
         _rope_rotate_cols(wq3[..., QK_NOPE:]), zq], axis=-1
    ).reshape(Q_LORA, MLA_HEADS * PAD_HEAD)
    wkv3 = w_ukv.reshape(KV_LORA, MLA_HEADS, QK_NOPE + V_DIM)
    wk_p = jnp.concatenate(
        [wkv3[..., :QK_NOPE], jnp.zeros((KV_LORA, MLA_HEADS, PAD_HEAD - QK_NOPE), F32)],
        axis=-1).reshape(KV_LORA, MLA_HEADS * PAD_HEAD)
    wv = _pad_heads(wkv3[..., QK_NOPE:].reshape(KV_LORA, MLA_HEADS * V_DIM), MLA_HEADS)

    inv_freq = 1.0 / (ROPE_THETA ** (jnp.arange(0, QK_ROPE, 2, dtype=F32) / QK_ROPE))
    invf = jnp.concatenate([jnp.zeros((QK_NOPE,), F32), inv_freq, inv_freq,
                            jnp.zeros((PAD_HEAD - QK_NOPE - QK_ROPE,), F32)])[None, :]

    ts = SEQ_TILE
    row = lambda w: pl.BlockSpec((None, ts, w), lambda b, s: (b, s, 0))
    qk_w = MLA_HEADS * PAD_HEAD
    v_w = MLA_HEADS * PAD_HEAD
    return pl.pallas_call(
        _l0_front_kernel,
        grid=(bsz, seq // ts),
        in_specs=[row(D_MODEL), row(1),
                  _resident((D_MODEL, L0_COLS)), _resident((1, L0_COLS)),
                  _resident((CONV_WIDTH, SUBLANES, CONV_CH)), _resident((1, CONV_CH)),
                  _resident((1, CONV_CH)), _resident((1, CONV_CH)),
                  _resident((1, Q_LORA)), _resident((Q_LORA, qk_w)),
                  _resident((Q_LORA, qk_w)), _resident((1, KV_LORA)),
                  _resident((KV_LORA, qk_w)), _resident((KV_LORA, v_w)),
                  _resident((1, PAD_HEAD))],
        out_specs=[row(CONV_CH), row(qk_w), row(qk_w), row(v_w)],
        out_shape=[jax.ShapeDtypeStruct((bsz, seq, CONV_CH), BF16),
                   jax.ShapeDtypeStruct((bsz, seq, qk_w), BF16),
                   jax.ShapeDtypeStruct((bsz, seq, qk_w), BF16),
                   jax.ShapeDtypeStruct((bsz, seq, v_w), BF16)],
        scratch_shapes=[pltpu.VMEM((ts + CONV_HALO, CONV_CH), F32)],
        compiler_params=_params("arbitrary", "arbitrary"),
        name="l0_front",
    )(x, positions[..., None], w_in_p, b_in_p,
      jnp.broadcast_to(conv_w[:, None, :], (CONV_WIDTH, SUBLANES, CONV_CH)), conv_b[None, :],
      cln_g[None, :], cln_b[None, :], qn_g[None, :], wq_p.astype(BF16),
      wqr_p.astype(BF16), kvn_g[None, :], wk_p.astype(BF16), wv.astype(BF16), invf)


def _l1_front_kernel(x_ref, w_ref, b_ref, wv_ref, bv_ref, wf_ref, bf_ref,
                     q_ref, k_ref, v_ref, cneg_ref, carry):
    ts = x_ref.shape[0]

    @pl.when(pl.program_id(1) == 0)
    def _():
        carry[...] = jnp.zeros_like(carry)

    xb = x_ref[...].astype(BF16)
    mix = FOX_HEADS * FOX_HD
    scale = FOX_HD ** -0.5 * LOG2_E
    q_ref[...] = ((_dot(xb, w_ref[:, 0:mix]) + b_ref[:, 0:mix]) * scale).astype(BF16)
    k_ref[...] = (_dot(xb, w_ref[:, mix:2 * mix]) + b_ref[:, mix:2 * mix]).astype(BF16)
    v = _dot(xb, wv_ref[...]) + bv_ref[...]
    head_lanes = lax.broadcasted_iota(jnp.int32, (ts, LANES), 1) < FOX_HD
    for pair in range(FOX_HEADS // 2):
        both = v[:, pair * LANES:(pair + 1) * LANES]
        swapped = pltpu.roll(both, LANES - FOX_HD, axis=1)
        for head, vals in enumerate((both, swapped)):
            blk = 2 * pair + head
            v_ref[:, blk * PAD_HEAD:(blk + 1) * PAD_HEAD] = (
                jnp.where(head_lanes, vals, 1.0).astype(BF16))

    log_f = jax.nn.log_sigmoid(_dot_nt(wf_ref[...], xb) + bf_ref[...])
    r = lax.broadcasted_iota(jnp.int32, (ts, ts), 0)
    c = lax.broadcasted_iota(jnp.int32, (ts, ts), 1)
    upper = (r <= c).astype(F32)
    csum = jnp.dot(log_f, upper, precision=lax.Precision.HIGHEST,
                   preferred_element_type=F32) + carry[...]
    cneg_ref[...] = csum * -LOG2_E
    carry[...] = carry[...] + jnp.sum(log_f, axis=1, keepdims=True)


def _l1_front(x, w_in, b_in):
    bsz, seq, _ = x.shape
    mix = FOX_HEADS * FOX_HD
    ts = SEQ_TILE
    row = lambda w: pl.BlockSpec((None, ts, w), lambda b, s: (b, s, 0))
    w_qk = w_in[:, :2 * mix].astype(BF16)
    b_qk = b_in[None, :2 * mix]
    w_v = w_in[:, 2 * mix:3 * mix].astype(BF16)
    b_v = b_in[None, 2 * mix:3 * mix]
    v_w = FOX_HEADS * PAD_HEAD
    wf_t = w_in[:, 3 * mix:].T.astype(BF16)
    bf_t = b_in[3 * mix:][:, None]
    return pl.pallas_call(
        _l1_front_kernel,
        grid=(bsz, seq // ts),
        in_specs=[row(D_MODEL), _resident((D_MODEL, 2 * mix)), _resident((1, 2 * mix)),
                  _resident((D_MODEL, mix)), _resident((1, mix)),
                  _resident((FOX_HEADS, D_MODEL)), _resident((FOX_HEADS, 1))],
        out_specs=[row(mix), row(mix), row(v_w),
                   pl.BlockSpec((None, FOX_HEADS, ts), lambda b, s: (b, 0, s))],
        out_shape=[jax.ShapeDtypeStruct((bsz, seq, mix), BF16)] * 2
        + [jax.ShapeDtypeStruct((bsz, seq, v_w), BF16),
           jax.ShapeDtypeStruct((bsz, FOX_HEADS, seq), F32)],
        scratch_shapes=[pltpu.VMEM((FOX_HEADS, 1), F32)],
        compiler_params=_params("arbitrary", "arbitrary"),
        name="l1_front",
    )(x, w_qk, b_qk, w_v, b_v, wf_t, bf_t)


def _attn_kernel(*refs, packed, has_bias):
    if has_bias:
        q_ref, k_ref, v_ref, cneg_ref, o_ref = refs
    else:
        q_ref, k_ref, v_ref, o_ref = refs
        cneg_ref = None
    tq = ATTN_TILE
    tk = ATTN_KEY_TILE
    half = LANES // 2

    def query_tile(qi):
        q_rows = pl.ds(pl.multiple_of(qi * tq, tq), tq)
        q2 = q_ref[q_rows, :]
        if packed:
            lane = lax.broadcasted_iota(jnp.int32, q2.shape, 1)
            zero = jnp.zeros_like(q2)
            qs = (jnp.where(lane < half, q2, zero), jnp.where(lane >= half, q2, zero))
        else:
            qs = (q2[:, :PAD_HEAD], q2[:, PAD_HEAD:])

        def k_block(kj, head):
            rows = pl.ds(pl.multiple_of(kj * tk, tk), tk)
            if packed:
                return k_ref[rows, :]
            return k_ref[rows, head * PAD_HEAD:(head + 1) * PAD_HEAD]

        def step(kj, carry, diag):
            rows = pl.ds(pl.multiple_of(kj * tk, tk), tk)
            skip = 0 if diag is None else diag * tk
            out = []
            for head in range(2):
                m_all, acc_all = carry[head]
                m, acc = m_all[skip:], acc_all[skip:]
                s = _dot_nt(qs[head][skip:], k_block(kj, head))
                if has_bias:
                    s = s + cneg_ref[head, :, rows]
                if diag is not None:
                    r = lax.broadcasted_iota(jnp.int32, s.shape, 0)
                    c = lax.broadcasted_iota(jnp.int32, s.shape, 1)
                    s = jnp.where(r >= c, s, NEG_INF)
                m_new = jnp.maximum(m, jnp.max(s, axis=-1, keepdims=True))
                alpha = jnp.exp2(m - m_new)
                p = jnp.exp2(s - m_new).astype(BF16)
                vblk = v_ref[rows, head * PAD_HEAD:(head + 1) * PAD_HEAD]
                acc = alpha * acc + _dot(p, vblk)
                if skip:
                    m_new = jnp.concatenate([m_all[:skip], m_new], axis=0)
                    acc = jnp.concatenate([acc_all[:skip], acc], axis=0)
                out.append((m_new, acc))
            return tuple(out)

        init = tuple((jnp.full((tq, 1), NEG_INF, F32), jnp.zeros((tq, LANES), F32))
                     for _ in range(2))
        per_tile = tq // tk
        n_full = qi * per_tile

        def group(first, c, size):
            for t in range(size):
                c = step(first + t, c, None)
            return c

        def diagonal(c):
            for d in range(per_tile):
                c = step(n_full + d, c, d)
            return c

        trips = n_full // ATTN_UNROLL
        if per_tile % ATTN_UNROLL == 0:
            carry = lax.fori_loop(0, jnp.maximum(trips - 1, 0),
                                  lambda j, c: group(ATTN_UNROLL * j, c, ATTN_UNROLL), init)
            carry = lax.cond(
                trips >= 1,
                lambda c: diagonal(group(n_full - ATTN_UNROLL, c, ATTN_UNROLL)),
                diagonal, carry)
        else:
            carry = lax.fori_loop(0, trips,
                                  lambda j, c: group(ATTN_UNROLL * j, c, ATTN_UNROLL), init)
            done = trips * ATTN_UNROLL
            size = ATTN_UNROLL // 2
            while size >= 1:
                take = (n_full - done) >= size
                carry = lax.cond(take, lambda c, f=done, n=size: group(f, c, n),
                                 lambda c: c, carry)
                done = done + jnp.where(take, size, 0)
                size //= 2
            carry = diagonal(carry)
        (_, acc_a), (_, acc_b) = carry
        out_a = acc_a / pltpu.roll(acc_a, half, axis=1)
        out_b = acc_b / pltpu.roll(acc_b, half, axis=1)
        lane_o = lax.broadcasted_iota(jnp.int32, (tq, LANES), 1)
        o_ref[q_rows, :] = jnp.where(lane_o < half, out_a,
                                     pltpu.roll(out_b, half, axis=1)).astype(o_ref.dtype)

    pl.loop(0, q_ref.shape[0] // tq)(query_tile)


def _attention(q, k, v, cneg, packed):
    bsz, seq, v_w = v.shape
    pairs = v_w // (2 * PAD_HEAD)
    qk_w = q.shape[-1] // pairs
    in_specs = [pl.BlockSpec((None, seq, qk_w), lambda b, p: (b, 0, p)),
                pl.BlockSpec((None, seq, qk_w), lambda b, p: (b, 0, p)),
                pl.BlockSpec((None, seq, 2 * PAD_HEAD), lambda b, p: (b, 0, p))]
    args = [q, k, v]
    if cneg is not None:
        in_specs.append(pl.BlockSpec((None, 2, 1, seq), lambda b, p: (b, p, 0, 0)))
        args.append(cneg)
    return pl.pallas_call(
        functools.partial(_attn_kernel, packed=packed, has_bias=cneg is not None),
        grid=(bsz, pairs),
        in_specs=in_specs,
        out_specs=pl.BlockSpec((None, seq, LANES), lambda b, p: (b, 0, p)),
        out_shape=jax.ShapeDtypeStruct((bsz, seq, pairs * LANES), BF16),
        compiler_params=_params("arbitrary", "arbitrary"),
        name="attn_packed" if packed else "attn_padded",
    )(*args)


def _first_hit(hits, found):
    out = []
    for hcur in hits:
        take = jnp.logical_and(hcur, jnp.logical_not(found))
        found = jnp.logical_or(found, take)
        out.append(take)
    return out, found


def _mix_out_router_kernel(a_ref, b_ref, x_ref, wa_ref, wb_ref, bo_ref, g_ref, be_ref,
                           rw_ref, rwl_ref, rb_ref,
                           x1_ref, x1p_ref, e_ref, gate_ref, rank_ref, cnt_ref, cnt, gate_t):
    tm = x_ref.shape[0]

    @pl.when(pl.program_id(0) == 0)
    def _():
        cnt[...] = jnp.zeros_like(cnt)
        gate_t[...] = jnp.zeros_like(gate_t)

    mix = _dot(a_ref[...], wa_ref[...]) + _dot(b_ref[...], wb_ref[...]) + bo_ref[...]
    x1 = _layer_norm(DN_ALPHA * x_ref[...] + mix, g_ref[...], be_ref[...])
    x1_ref[...] = x1
    _store_packed(x1p_ref, _pack_rows(x1))

    x_hi = x1.astype(BF16)
    x_lo = (x1 - x_hi.astype(F32)).astype(BF16)
    logits = (_dot_nt(rw_ref[...], x_hi) + _dot_nt(rw_ref[...], x_lo)
              + _dot_nt(rwl_ref[...], x_hi))
    aff = jax.nn.sigmoid(logits)
    choice = aff + rb_ref[...]
    per_group = N_EXPERTS // N_GROUPS
    sub = lax.broadcasted_iota(jnp.int32, (per_group, tm), 0)
    groups = [choice[g * per_group:(g + 1) * per_group, :] for g in range(N_GROUPS)]

    gscore = []
    for cg in groups:
        m1 = jnp.max(cg, axis=0, keepdims=True)
        i1 = jnp.min(jnp.where(cg == m1, sub, per_group), axis=0, keepdims=True)
        m2 = jnp.max(jnp.where(sub == i1, NEG_INF, cg), axis=0, keepdims=True)
        gscore.append(m1 + m2)

    gsel = [jnp.zeros((1, tm), jnp.bool_) for _ in range(N_GROUPS)]
    for _ in range(TOPK_GROUPS):
        best = functools.reduce(jnp.maximum, gscore)
        takes, _ = _first_hit([gs == best for gs in gscore], jnp.zeros((1, tm), jnp.bool_))
        gsel = [jnp.logical_or(a, t) for a, t in zip(gsel, takes)]
        gscore = [jnp.where(t, NEG_INF, gs) for gs, t in zip(gscore, takes)]

    masked = [jnp.where(gs, cg, NEG_INF) for gs, cg in zip(gsel, groups)]
    eid = [sub + g * per_group for g in range(N_GROUPS)]
    affs = [aff[g * per_group:(g + 1) * per_group, :] for g in range(N_GROUPS)]
    sel = [jnp.zeros((per_group, tm), jnp.bool_) for _ in range(N_GROUPS)]
    picked_e, picked_w = [], []
    for _ in range(TOP_K):
        best = jnp.max(functools.reduce(jnp.maximum, masked), axis=0, keepdims=True)
        cand = [jnp.where(mg == best, ig, N_EXPERTS) for mg, ig in zip(masked, eid)]
        idx = jnp.min(functools.reduce(jnp.minimum, cand), axis=0, keepdims=True)
        onehot = [ig == idx for ig in eid]
        w = functools.reduce(
            jnp.add, [jnp.sum(jnp.where(oh, ag, 0.0), axis=0, keepdims=True)
                      for oh, ag in zip(onehot, affs)])
        picked_e.append(idx)
        picked_w.append(w)
        sel = [jnp.logical_or(sg, oh) for sg, oh in zip(sel, onehot)]
        masked = [jnp.where(oh, NEG_INF, mg) for mg, oh in zip(masked, onehot)]

    wsum = functools.reduce(jnp.add, picked_w)

    sel_f = jnp.concatenate([sg.astype(F32) for sg in sel], axis=0)
    r = lax.broadcasted_iota(jnp.int32, (tm, tm), 0)
    c = lax.broadcasted_iota(jnp.int32, (tm, tm), 1)
    before = _dot(sel_f.astype(BF16), (r < c).astype(BF16)) + cnt[...]
    eall = lax.broadcasted_iota(jnp.int32, (N_EXPERTS, tm), 0)
    for kk in range(TOP_K):
        rank = jnp.sum(jnp.where(eall == picked_e[kk], before, 0.0), axis=0, keepdims=True)
        e_ref[kk:kk + 1, :] = picked_e[kk]
        gate_t[kk:kk + 1, :] = picked_w[kk] / wsum * ROUTED_SCALE
        rank_ref[kk:kk + 1, :] = rank.astype(jnp.int32)
    pad = SUBLANES - TOP_K
    e_ref[TOP_K:, :] = jnp.zeros((pad, tm), jnp.int32)
    rank_ref[TOP_K:, :] = jnp.zeros((pad, tm), jnp.int32)
    gate_ref[...] = gate_t[...].T
    cnt[...] = cnt[...] + jnp.sum(sel_f, axis=1, keepdims=True)
    cnt_ref[...] = jnp.broadcast_to(cnt[...], cnt_ref.shape)


def _mix_out_router(a, b, x, w_out, b_out, ln_g, ln_b, router_w, router_bias):
    n_tok = x.shape[0]
    tm = TOKEN_TILE
    half = w_out.shape[0] // 2
    row = lambda w: pl.BlockSpec((tm, w), lambda i: (i, 0))
    b_spec = (pl.BlockSpec((tm, half), lambda i: (i, 1)) if b is a
              else pl.BlockSpec((tm, half), lambda i: (i, 0)))
    lane_row = pl.BlockSpec((SUBLANES, tm), lambda i: (0, i))
    packed = pl.BlockSpec((tm // SUBLANES, PACK_BLOCKS, SUBLANES, LANES),
                          lambda i: (i, 0, 0, 0))
    w_bf = w_out.astype(BF16)
    rw_t = router_w.T
    rw_hi = rw_t.astype(BF16)
    rw_lo = (rw_t - rw_hi.astype(F32)).astype(BF16)
    return pl.pallas_call(
        _mix_out_router_kernel,
        grid=(n_tok // tm,),
        in_specs=[row(half), b_spec, row(D_MODEL),
                  pl.BlockSpec((half, D_MODEL), lambda i: (0, 0)),
                  pl.BlockSpec((half, D_MODEL), lambda i: (1, 0)),
                  _resident((1, D_MODEL)), _resident((1, D_MODEL)), _resident((1, D_MODEL)),
                  _resident((N_EXPERTS, D_MODEL)), _resident((N_EXPERTS, D_MODEL)),
                  _resident((N_EXPERTS, 1))],
        out_specs=[row(D_MODEL), packed, lane_row, row(LANES), lane_row,
                   _resident((N_EXPERTS, LANES))],
        out_shape=[jax.ShapeDtypeStruct((n_tok, D_MODEL), F32),
                   jax.ShapeDtypeStruct((n_tok // SUBLANES, PACK_BLOCKS, SUBLANES, LANES),
                                        jnp.int32),
                   jax.ShapeDtypeStruct((SUBLANES, n_tok), jnp.int32),
                   jax.ShapeDtypeStruct((n_tok, LANES), F32),
                   jax.ShapeDtypeStruct((SUBLANES, n_tok), jnp.int32),
                   jax.ShapeDtypeStruct((N_EXPERTS, LANES), F32)],
        scratch_shapes=[pltpu.VMEM((N_EXPERTS, 1), F32), pltpu.VMEM((LANES, tm), F32)],
        compiler_params=_params("arbitrary"),
        name="mix_out_router",
    )(a, b, x, w_bf, w_bf, b_out[None, :], ln_g[None, :], ln_b[None, :],
      rw_hi, rw_lo, router_bias[:, None])


def _expert_ffn_kernel(tile_e_ref, tile_on_ref, xs_ref, *refs):
    w_refs = refs[:3 * FFN_SLOTS]
    ys_ref = refs[3 * FFN_SLOTS]
    wg_bf, wu_bf, wd_bf = refs[3 * FFN_SLOTS + 1:]
    i = pl.program_id(0)
    sub = GROUP_TILE // SUBLANES

    for slot in range(FFN_SLOTS):
        t = FFN_SLOTS * i + slot
        fresh = jnp.logical_or(
            i == 0, tile_e_ref[t] != tile_e_ref[jnp.maximum(t - FFN_SLOTS, 0)])

        @pl.when(fresh)
        def _(slot=slot):
            wg_ref, wu_ref, wd_ref = w_refs[3 * slot:3 * slot + 3]
            wg_bf[slot] = wg_ref[...].astype(BF16)
            wu_bf[slot] = wu_ref[...].astype(BF16)
            wd_bf[slot] = wd_ref[...].astype(BF16)

    on = tile_on_ref[FFN_SLOTS * i] > 0

    @pl.when(on)
    def _():
        for slot in range(FFN_SLOTS):
            rows = pl.ds(slot * sub, sub)
            lo, hi = _unpack_rows(_load_packed(xs_ref.at[rows]))
            lo = lo.astype(BF16)
            hi = hi.astype(BF16)
            gate = (_dot(lo, wg_bf[slot, :PACK_COLS, :]) + _dot(hi, wg_bf[slot, PACK_COLS:, :]))
            up = (_dot(lo, wu_bf[slot, :PACK_COLS, :]) + _dot(hi, wu_bf[slot, PACK_COLS:, :]))
            hid = (jax.nn.silu(gate) * up).astype(BF16)
            _store_packed(ys_ref.at[rows], _pack_rows(_dot(hid, wd_bf[slot])))

    @pl.when(jnp.logical_not(on))
    def _():
        ys_ref[...] = jnp.zeros_like(ys_ref)


def _expert_ffn(xs, tile_e, tile_on, w_gate, w_up, w_down):
    rows_per_step = GROUP_TILE * FFN_SLOTS
    packed = pl.BlockSpec((rows_per_step // SUBLANES, PACK_BLOCKS, SUBLANES, LANES),
                          lambda i, te, on: (i, 0, 0, 0))
    w_specs = []
    for slot in range(FFN_SLOTS):
        pick = lambda i, te, on, slot=slot: (te[FFN_SLOTS * i + slot], 0, 0)
        w_specs += [pl.BlockSpec((None, D_MODEL, EXPERT_HID), pick),
                    pl.BlockSpec((None, D_MODEL, EXPERT_HID), pick),
                    pl.BlockSpec((None, EXPERT_HID, D_MODEL), pick)]
    grid_spec = pltpu.PrefetchScalarGridSpec(
        num_scalar_prefetch=2,
        grid=(xs.shape[0] * SUBLANES // rows_per_step,),
        in_specs=[packed] + w_specs,
        out_specs=packed,
        scratch_shapes=[pltpu.VMEM((FFN_SLOTS, D_MODEL, EXPERT_HID), BF16),
                        pltpu.VMEM((FFN_SLOTS, D_MODEL, EXPERT_HID), BF16),
                        pltpu.VMEM((FFN_SLOTS, EXPERT_HID, D_MODEL), BF16)],
    )
    return pl.pallas_call(
        _expert_ffn_kernel,
        grid_spec=grid_spec,
        out_shape=jax.ShapeDtypeStruct(xs.shape, jnp.int32),
        compiler_params=_params("arbitrary"),
        name="expert_ffn",
    )(tile_e, tile_on, xs, *([w_gate, w_up, w_down] * FFN_SLOTS))


def _moe_out_kernel(x1_ref, yg_ref, gate_ref, sg_ref, su_ref, sd_ref, g_ref, b_ref, *rest):
    o_ref = rest[-1]
    x1 = x1_ref[...]
    xb = x1.astype(BF16)
    hid = jax.nn.silu(_dot(xb, sg_ref[...])) * _dot(xb, su_ref[...])
    shared = _dot(hid.astype(BF16), sd_ref[...])
    gates = gate_ref[...]
    lo_acc = shared[:, :PACK_COLS]
    hi_acc = shared[:, PACK_COLS:]
    for kk in range(TOP_K):
        lo, hi = _unpack_rows(_load_packed(yg_ref.at[kk]))
        w = gates[:, kk:kk + 1]
        lo_acc = lo_acc + w * lo
        hi_acc = hi_acc + w * hi
    moe = jnp.concatenate([lo_acc, hi_acc], axis=1)
    o_ref[...] = _layer_norm(DN_ALPHA * x1 + moe, g_ref[...], b_ref[...])


def _moe_out(x1, yg, gates, sh_gate, sh_up, sh_down, ln_g, ln_b, part, earlier):
    n_tok = x1.shape[0]
    tm = TOKEN_TILE
    steps = n_tok // tm // MOE_PARTS
    first = part * steps
    row = lambda w: pl.BlockSpec((tm, w), lambda i: (i + first, 0))
    in_specs = [row(D_MODEL),
                pl.BlockSpec((TOP_K, tm // SUBLANES, PACK_BLOCKS, SUBLANES, LANES),
                             lambda i: (0, i, 0, 0, 0)),
                row(LANES),
                _resident((D_MODEL, EXPERT_HID)), _resident((D_MODEL, EXPERT_HID)),
                _resident((EXPERT_HID, D_MODEL)), _resident((1, D_MODEL)),
                _resident((1, D_MODEL))]
    args = [x1, yg, gates, sh_gate.astype(BF16), sh_up.astype(BF16), sh_down.astype(BF16),
            ln_g[None, :], ln_b[None, :]]
    aliases = {}
    if earlier is not None:
        in_specs.append(pl.BlockSpec(memory_space=pl.ANY))
        args.append(earlier)
        aliases = {len(args) - 1: 0}
    return pl.pallas_call(
        _moe_out_kernel,
        grid=(steps,),
        in_specs=in_specs,
        out_specs=row(D_MODEL),
        out_shape=jax.ShapeDtypeStruct((n_tok, D_MODEL), F32),
        input_output_aliases=aliases,
        compiler_params=_params("arbitrary"),
        name="moe_out",
    )(*args)


def _slot_kernel(offset_ref, e_ref, rank_ref, pos_ref):
    e = e_ref[...]
    base = jnp.zeros(e.shape, jnp.int32)
    for ex in range(N_EXPERTS):
        base = jnp.where(e == ex, offset_ref[ex], base)
    pos_ref[...] = base + rank_ref[...]


def _slots(offset, e6, rank6):
    n_tok = e6.shape[1]
    tl = SLOT_TILE
    lane_row = pl.BlockSpec((SUBLANES, tl), lambda i, off: (0, i))
    return pl.pallas_call(
        _slot_kernel,
        grid_spec=pltpu.PrefetchScalarGridSpec(
            num_scalar_prefetch=1, grid=(n_tok // tl,),
            in_specs=[lane_row, lane_row], out_specs=lane_row),
        out_shape=jax.ShapeDtypeStruct((SUBLANES, n_tok), jnp.int32),
        compiler_params=_params("arbitrary"),
        name="slots",
    )(offset, e6, rank6)


def _sc_mesh():
    return plsc.VectorSubcoreMesh(core_axis_name="c", subcore_axis_name="s")


def _sc_worker():
    return lax.axis_index("s") * SC_CORES + lax.axis_index("c")


def _sc_dispatch(x_sub, idx, n_out_sub):
    windows = x_sub.shape[0] // (SC_WORKERS * SC_WINDOW)

    @functools.partial(
        pl.kernel, mesh=_sc_mesh(),
        out_type=jax.ShapeDtypeStruct((n_out_sub, LANES), jnp.int32),
        scratch_types=[pltpu.VMEM((SC_WINDOW, LANES), jnp.int32),
                       pltpu.VMEM((TOP_K, SC_WINDOW), jnp.int32)],
        name="sc_dispatch")
    def run(x_hbm, idx_hbm, out_hbm, rows_v, idx_v):
        wid = _sc_worker()

        @pl.loop(0, windows)
        def _(c):
            base = (wid * windows + c) * SC_WINDOW
            pltpu.sync_copy(x_hbm.at[pl.ds(base, SC_WINDOW)], rows_v)
            pltpu.sync_copy(idx_hbm.at[wid, c], idx_v)
            for kk in range(TOP_K):
                pltpu.sync_copy(rows_v, out_hbm.at[idx_v.at[kk]])

    return run(x_sub, idx)


def _sc_combine(y_sub, idx, n_tok_sub):
    windows = n_tok_sub // (SC_WORKERS * SC_WINDOW)

    @functools.partial(
        pl.kernel, mesh=_sc_mesh(),
        out_type=jax.ShapeDtypeStruct((TOP_K, n_tok_sub, LANES), jnp.int32),
        scratch_types=[pltpu.VMEM((2, SC_WINDOW, LANES), jnp.int32),
                       pltpu.VMEM((TOP_K, SC_WINDOW), jnp.int32),
                       pltpu.SemaphoreType.DMA((2,)),
                       pltpu.SemaphoreType.DMA((2,))],
        name="sc_combine")
    def run(y_hbm, idx_hbm, out_hbm, rows_v, idx_v, gather_sem, write_sem):
        wid = _sc_worker()

        def gather(kk):
            buf = kk % 2
            return pltpu.async_copy(y_hbm.at[idx_v.at[kk]], rows_v.at[buf], gather_sem.at[buf])

        @pl.loop(0, windows)
        def _(c):
            base = (wid * windows + c) * SC_WINDOW
            pltpu.sync_copy(idx_hbm.at[wid, c], idx_v)
            writes = []
            pending = gather(0)
            for kk in range(TOP_K):
                pending.wait()
                if kk + 1 < TOP_K:
                    if kk >= 1:
                        writes[kk - 1].wait()
                    pending = gather(kk + 1)
                writes.append(pltpu.async_copy(
                    rows_v.at[kk % 2], out_hbm.at[kk, pl.ds(base, SC_WINDOW)],
                    write_sem.at[kk % 2]))
            writes[TOP_K - 2].wait()
            writes[TOP_K - 1].wait()

    return run(y_sub, idx)


def _moe(x1, x1p, e6, gates, rank6, counts, exp_w_gate, exp_w_up, exp_w_down,
         sh_w_gate, sh_w_up, sh_w_down, ln_g, ln_b):
    n_tok = x1.shape[0]
    tg = GROUP_TILE
    n_rows = n_tok * TOP_K + N_EXPERTS * tg
    n_tiles = n_rows // tg

    cnt = counts[:, 0].astype(jnp.int32)
    tiles_per_e = (cnt + tg - 1) // tg
    tile_end = jnp.cumsum(tiles_per_e)
    offset = (tile_end - tiles_per_e) * tg
    tile_ids = jnp.arange(n_tiles, dtype=jnp.int32)
    tile_on = (tile_ids < tile_end[-1]).astype(jnp.int32)
    tile_e = jnp.minimum(
        jnp.sum((tile_ids[:, None] >= tile_end[None, :]).astype(jnp.int32), axis=1),
        N_EXPERTS - 1)
    tile_e = jnp.where(tile_on > 0, tile_e, tile_e[jnp.maximum(tile_end[-1] - 1, 0)])

    pos = _slots(offset, e6, rank6)[:TOP_K]

    group = PACK_BLOCKS * SUBLANES
    sub0 = (pos // SUBLANES) * group + pos % SUBLANES
    lane = jnp.arange(SC_WINDOW)
    spread = (jnp.arange(group)[:, None]
              == (lane // group) * SUBLANES + lane % SUBLANES).astype(F32)
    idx = jnp.einsum("kws,sd->kwd", sub0.reshape(TOP_K, n_tok // group, group).astype(F32),
                     spread, precision=lax.Precision.HIGHEST).astype(jnp.int32)
    idx = idx + ((lane % group) // SUBLANES * SUBLANES).astype(jnp.int32)
    n_sub = n_tok * PACK_BLOCKS
    idx = idx.reshape(TOP_K, n_sub)

    def windowed(ix):
        windows = ix.shape[1] // (SC_WORKERS * SC_WINDOW)
        return ix.reshape(TOP_K, SC_WORKERS, windows, SC_WINDOW).transpose(1, 2, 0, 3)

    xs = _sc_dispatch(x1p.reshape(n_sub, LANES), windowed(idx), n_rows * PACK_BLOCKS)
    ys = _expert_ffn(xs.reshape(n_rows // SUBLANES, PACK_BLOCKS, SUBLANES, LANES),
                     tile_e, tile_on, exp_w_gate, exp_w_up, exp_w_down)
    ys = ys.reshape(n_rows * PACK_BLOCKS, LANES)
    part_sub = n_sub // MOE_PARTS
    out = None
    for part in range(MOE_PARTS):
        yg = _sc_combine(ys, windowed(idx[:, part * part_sub:(part + 1) * part_sub]), part_sub)
        yg = yg.reshape(TOP_K, part_sub // (PACK_BLOCKS * SUBLANES), PACK_BLOCKS,
                        SUBLANES, LANES)
        out = _moe_out(x1, yg, gates, sh_w_gate, sh_w_up, sh_w_down, ln_g, ln_b, part, out)
    return out


def kernel(x, positions,
           l0_w_in, l0_b_in, l0_conv_w, l0_conv_b, l0_conv_ln_g, l0_conv_ln_b,
           l0_q_norm_g, l0_w_uq, l0_kv_norm_g, l0_w_ukv, l0_w_out, l0_b_out,
           l0_ln1_g, l0_ln1_b,
           l0_router_w, l0_router_bias, l0_exp_w_gate, l0_exp_w_up, l0_exp_w_down,
           l0_sh_w_gate, l0_sh_w_up, l0_sh_w_down, l0_ln2_g, l0_ln2_b,
           l1_w_in, l1_b_in, l1_w_out, l1_b_out, l1_ln1_g, l1_ln1_b,
           l1_router_w, l1_router_bias, l1_exp_w_gate, l1_exp_w_up, l1_exp_w_down,
           l1_sh_w_gate, l1_sh_w_up, l1_sh_w_down, l1_ln2_g, l1_ln2_b):
    bsz, seq, d = x.shape
    n_tok = bsz * seq

    u, q, k, v = _l0_front(x, positions, l0_w_in, l0_b_in, l0_conv_w, l0_conv_b,
                           l0_conv_ln_g, l0_conv_ln_b, l0_q_norm_g, l0_w_uq,
                           l0_kv_norm_g, l0_w_ukv)
    o = _attention(q, k, v, None, packed=False)
    x_flat = x.reshape(n_tok, d)
    routing = _mix_out_router(u.reshape(n_tok, -1), o.reshape(n_tok, -1), x_flat,
                              l0_w_out, l0_b_out, l0_ln1_g, l0_ln1_b,
                              l0_router_w, l0_router_bias)
    x2 = _moe(*routing, l0_exp_w_gate, l0_exp_w_up, l0_exp_w_down,
              l0_sh_w_gate, l0_sh_w_up, l0_sh_w_down, l0_ln2_g, l0_ln2_b)

    q, k, v, cneg = _l1_front(x2.reshape(bsz, seq, d), l1_w_in, l1_b_in)
    o = _attention(q, k, v, cneg[:, :, None, :], packed=True).reshape(n_tok, -1)
    routing = _mix_out_router(o, o, x2, l1_w_out, l1_b_out, l1_ln1_g, l1_ln1_b,
                              l1_router_w, l1_router_bias)
    x3 = _moe(*routing, l1_exp_w_gate, l1_exp_w_up, l1_exp_w_down,
              l1_sh_w_gate, l1_sh_w_up, l1_sh_w_down, l1_ln2_g, l1_ln2_b)
    return x3.reshape(bsz, seq, d)
```

```python
import functools

import jax
import jax.numpy as jnp
from jax import lax
from jax.experimental import pallas as pl
from jax.experimental.pallas import tpu as pltpu
from jax.experimental.pallas import tpu_sc as plsc

D_MODEL = 1024
DEPTH = 2
DN_ALPHA = (2 * DEPTH) ** 0.25
LN_EPS = 1e-5
RMS_EPS = 1e-6

CONV_CH = 512
CONV_WIDTH = 31
MLA_HEADS = 8
QK_NOPE = 64
QK_ROPE = 32
V_DIM = 64
Q_LORA = 256
KV_LORA = 128
ROPE_THETA = 10000.0
FOX_HEADS = 16
FOX_HD = 64
N_EXPERTS = 64
TOP_K = 6
N_GROUPS = 8
TOPK_GROUPS = 4
EXPERT_HID = 256
ROUTED_SCALE = 2.5

LANES = 128
SUBLANES = 8
V7X_VMEM_BYTES = 64 * 1024 * 1024
VMEM_LIMIT_BYTES = V7X_VMEM_BYTES * 7 // 8

SEQ_TILE = 512
TOKEN_TILE = 512
SLOT_TILE = 4096
ATTN_TILE = 1024
ATTN_KEY_TILE = 512
ATTN_UNROLL = 2
GROUP_TILE = 512
FFN_SLOTS = 2
CONV_ROWS = 32
CONV_HALO = 32
PAD_HEAD = 128

PACK_COLS = D_MODEL // 2
PACK_BLOCKS = PACK_COLS // LANES
SC_CORES = 2
SC_SUBCORES = 16
SC_WORKERS = SC_CORES * SC_SUBCORES
SC_WINDOW = 128
MOE_PARTS = 4
BF16 = jnp.bfloat16
F32 = jnp.float32
NEG_INF = float("-inf")
LOG2_E = 1.4426950408889634


def _params(*semantics):
    return pltpu.CompilerParams(dimension_semantics=semantics,
                                vmem_limit_bytes=VMEM_LIMIT_BYTES)


def _resident(shape):
    nd = len(shape)
    return pl.BlockSpec(shape, lambda *_: (0,) * nd)


def _dot(a, b):
    return jnp.dot(a, b, preferred_element_type=F32)


def _dot_nt(a, b, precision=None):
    return lax.dot_general(a, b, (((1,), (1,)), ((), ())),
                           precision=precision, preferred_element_type=F32)


def _layer_norm(x, g, b):
    mu = jnp.mean(x, axis=-1, keepdims=True)
    xc = x - mu
    var = jnp.mean(xc * xc, axis=-1, keepdims=True)
    return xc * lax.rsqrt(var + LN_EPS) * g + b


def _rms_norm(x, g):
    return x * lax.rsqrt(jnp.mean(x * x, axis=-1, keepdims=True) + RMS_EPS) * g


def _ones_upper_half(width):
    lane = lax.broadcasted_iota(jnp.int32, (1, width), 1)
    return jnp.where((lane & (PAD_HEAD - 1)) >= V_DIM, 1.0, 0.0).astype(F32)


def _pad_heads(w, heads):
    d = w.shape[-1] // heads
    w3 = w.reshape(w.shape[:-1] + (heads, d))
    pad = jnp.zeros(w.shape[:-1] + (heads, PAD_HEAD - d), w.dtype)
    return jnp.concatenate([w3, pad], axis=-1).reshape(w.shape[:-1] + (heads * PAD_HEAD,))


def _pack_rows(x):
    lo = lax.bitcast_convert_type(x[:, :PACK_COLS].astype(BF16).astype(F32), jnp.uint32)
    hi = lax.bitcast_convert_type(x[:, PACK_COLS:].astype(BF16).astype(F32), jnp.uint32)
    word = (lo >> 16) | (hi & jnp.uint32(0xFFFF0000))
    return lax.bitcast_convert_type(word, jnp.int32)


def _unpack_rows(words):
    u = lax.bitcast_convert_type(words, jnp.uint32)
    lo = lax.bitcast_convert_type(u << 16, F32)
    hi = lax.bitcast_convert_type(u & jnp.uint32(0xFFFF0000), F32)
    return lo, hi


def _store_packed(ref, words):
    groups = words.shape[0] // SUBLANES
    for cb in range(PACK_BLOCKS):
        ref[:, cb, :, :] = words[:, cb * LANES:(cb + 1) * LANES].reshape(groups, SUBLANES, LANES)


def _load_packed(ref):
    rows = ref.shape[0] * SUBLANES
    return jnp.concatenate(
        [ref[:, cb, :, :].reshape(rows, LANES) for cb in range(PACK_BLOCKS)], axis=1)


L0_A = 0
L0_G = CONV_CH
L0_Q = 2 * CONV_CH
L0_KV = L0_Q + Q_LORA
L0_KR = L0_KV + KV_LORA
L0_KRR = L0_KR + PAD_HEAD
L0_COLS = L0_KRR + PAD_HEAD


def _l0_front_kernel(x_ref, pos_ref, w_in_ref, b_in_ref, conv_w_ref, conv_b_ref,
                     cln_g_ref, cln_b_ref, qn_g_ref, wq_ref, wqr_ref, kvn_g_ref,
                     wk_ref, wv_ref, invf_ref,
                     u_ref, q_ref, k_ref, v_ref, ubuf):
    ts = x_ref.shape[0]

    @pl.when(pl.program_id(1) == 0)
    def _():
        ubuf[0:CONV_HALO, :] = jnp.zeros((CONV_HALO, CONV_CH), F32)

    h = _dot(x_ref[...].astype(BF16), w_in_ref[...]) + b_in_ref[...]

    ubuf[CONV_HALO:CONV_HALO + ts, :] = (
        h[:, L0_A:L0_A + CONV_CH] * jax.nn.sigmoid(h[:, L0_G:L0_G + CONV_CH]))
    first_tap = CONV_HALO - (CONV_WIDTH - 1)

    def conv_chunk(c, carry):
        base = pl.multiple_of(c * CONV_ROWS, CONV_ROWS)
        acc = jnp.broadcast_to(conv_b_ref[...], (CONV_ROWS, CONV_CH))
        for res in range(SUBLANES):
            rows = CONV_ROWS + (SUBLANES if res else 0)
            part = None
            for off in range(res, first_tap + CONV_WIDTH, SUBLANES):
                j = off - first_tap
                if j < 0:
                    continue
                window = ubuf[pl.ds(base + (off - res), rows), :]
                term = window.reshape(rows // SUBLANES, SUBLANES, CONV_CH) * conv_w_ref[j][None]
                part = term if part is None else part + term
            acc = acc + part.reshape(rows, CONV_CH)[res:res + CONV_ROWS, :]
        y = _layer_norm(acc, cln_g_ref[...], cln_b_ref[...])
        u_ref[pl.ds(base, CONV_ROWS), :] = (y * jax.nn.sigmoid(y)).astype(BF16)
        return carry

    lax.fori_loop(0, ts // CONV_ROWS, conv_chunk, 0, unroll=4)
    ubuf[0:CONV_HALO, :] = ubuf[ts:ts + CONV_HALO, :]

    ang = pos_ref[...].astype(F32) * invf_ref[...]
    cos = jnp.cos(ang)
    sin = jnp.sin(ang)
    scale = (QK_NOPE + QK_ROPE) ** -0.5 * LOG2_E

    qn = _rms_norm(h[:, L0_Q:L0_Q + Q_LORA], qn_g_ref[...]).astype(BF16)
    q = _dot(qn, wq_ref[...])
    q_rot = _dot(qn, wqr_ref[...])
    cos_s = cos * scale
    sin_s = sin * scale
    for hh in range(MLA_HEADS):
        blk = slice(hh * PAD_HEAD, (hh + 1) * PAD_HEAD)
        q_ref[:, blk] = (q[:, blk] * cos_s + q_rot[:, blk] * sin_s).astype(BF16)

    kvn = _rms_norm(h[:, L0_KV:L0_KV + KV_LORA], kvn_g_ref[...]).astype(BF16)
    k_nope = _dot(kvn, wk_ref[...])
    k_pe = h[:, L0_KR:L0_KR + PAD_HEAD] * cos + h[:, L0_KRR:L0_KRR + PAD_HEAD] * sin
    for hh in range(MLA_HEADS):
        blk = slice(hh * PAD_HEAD, (hh + 1) * PAD_HEAD)
        k_ref[:, blk] = (k_nope[:, blk] + k_pe).astype(BF16)
    v_ref[...] = (_dot(kvn, wv_ref[...]) + _ones_upper_half(v_ref.shape[1])).astype(BF16)


def _rope_rotate_cols(w):
    half = QK_ROPE // 2
    return jnp.concatenate([-w[..., half:], w[..., :half]], axis=-1)


def _l0_front(x, positions, w_in, b_in, conv_w, conv_b, cln_g, cln_b,
              qn_g, w_uq, kvn_g, w_ukv):
    bsz, seq, _ = x.shape
    pad_lo = jnp.zeros((D_MODEL, QK_NOPE), F32)
    pad_hi = jnp.zeros((D_MODEL, PAD_HEAD - QK_NOPE - QK_ROPE), F32)
    i3 = L0_KR
    w_kr = w_in[:, i3:i3 + QK_ROPE]
    w_in_p = jnp.concatenate(
        [w_in[:, :i3], pad_lo, w_kr, pad_hi, pad_lo, _rope_rotate_cols(w_kr), pad_hi],
        axis=1).astype(BF16)
    b_kr = b_in[i3:i3 + QK_ROPE]
    zlo = jnp.zeros((QK_NOPE,), F32)
    zhi = jnp.zeros((PAD_HEAD - QK_NOPE - QK_ROPE,), F32)
    b_in_p = jnp.concatenate(
        [b_in[:i3], zlo, b_kr, zhi, zlo, _rope_rotate_cols(b_kr), zhi])[None, :]

    dq = QK_NOPE + QK_ROPE
    wq3 = w_uq.reshape(Q_LORA, MLA_HEADS, dq)
    zq = jnp.zeros((Q_LORA, MLA_HEADS, PAD_HEAD - dq), F32)
    wq_p = jnp.concatenate([wq3, zq], axis=-1).reshape(Q_LORA, MLA_HEADS * PAD_HEAD)
    wqr_p = jnp.concatenate(
        [jnp.zeros((Q_LORA, MLA_HEADS, QK_NOPE), F32),
         _rope_rotate_cols(wq3[..., QK_NOPE:]), zq], axis=-1
    ).reshape(Q_LORA, MLA_HEADS * PAD_HEAD)
    wkv3 = w_ukv.reshape(KV_LORA, MLA_HEADS, QK_NOPE + V_DIM)
    wk_p = jnp.concatenate(
        [wkv3[..., :QK_NOPE], jnp.zeros((KV_LORA, MLA_HEADS, PAD_HEAD - QK_NOPE), F32)],
        axis=-1).reshape(KV_LORA, MLA_HEADS * PAD_HEAD)
    wv = _pad_heads(wkv3[..., QK_NOPE:].reshape(KV_LORA, MLA_HEADS * V_DIM), MLA_HEADS)

    inv_freq = 1.0 / (ROPE_THETA ** (jnp.arange(0, QK_ROPE, 2, dtype=F32) / QK_ROPE))
    invf = jnp.concatenate([jnp.zeros((QK_NOPE,), F32), inv_freq, inv_freq,
                            jnp.zeros((PAD_HEAD - QK_NOPE - QK_ROPE,), F32)])[None, :]

    ts = SEQ_TILE
    row = lambda w: pl.BlockSpec((None, ts, w), lambda b, s: (b, s, 0))
    qk_w = MLA_HEADS * PAD_HEAD
    v_w = MLA_HEADS * PAD_HEAD
    return pl.pallas_call(
        _l0_front_kernel,
        grid=(bsz, seq // ts),
        in_specs=[row(D_MODEL), row(1),
                  _resident((D_MODEL, L0_COLS)), _resident((1, L0_COLS)),
                  _resident((CONV_WIDTH, SUBLANES, CONV_CH)), _resident((1, CONV_CH)),
                  _resident((1, CONV_CH)), _resident((1, CONV_CH)),
                  _resident((1, Q_LORA)), _resident((Q_LORA, qk_w)),
                  _resident((Q_LORA, qk_w)), _resident((1, KV_LORA)),
                  _resident((KV_LORA, qk_w)), _resident((KV_LORA, v_w)),
                  _resident((1, PAD_HEAD))],
        out_specs=[row(CONV_CH), row(qk_w), row(qk_w), row(v_w)],
        out_shape=[jax.ShapeDtypeStruct((bsz, seq, CONV_CH), BF16),
                   jax.ShapeDtypeStruct((bsz, seq, qk_w), BF16),
                   jax.ShapeDtypeStruct((bsz, seq, qk_w), BF16),
                   jax.ShapeDtypeStruct((bsz, seq, v_w), BF16)],
        scratch_shapes=[pltpu.VMEM((ts + CONV_HALO, CONV_CH), F32)],
        compiler_params=_params("arbitrary", "arbitrary"),
        name="l0_front",
    )(x, positions[..., None], w_in_p, b_in_p,
      jnp.broadcast_to(conv_w[:, None, :], (CONV_WIDTH, SUBLANES, CONV_CH)), conv_b[None, :],
      cln_g[None, :], cln_b[None, :], qn_g[None, :], wq_p.astype(BF16),
      wqr_p.astype(BF16), kvn_g[None, :], wk_p.astype(BF16), wv.astype(BF16), invf)


def _l1_front_kernel(x_ref, w_ref, b_ref, wv_ref, bv_ref, wf_ref, bf_ref,
                     q_ref, k_ref, v_ref, cneg_ref, carry):
    ts = x_ref.shape[0]

    @pl.when(pl.program_id(1) == 0)
    def _():
        carry[...] = jnp.zeros_like(carry)

    xb = x_ref[...].astype(BF16)
    mix = FOX_HEADS * FOX_HD
    scale = FOX_HD ** -0.5 * LOG2_E
    q_ref[...] = ((_dot(xb, w_ref[:, 0:mix]) + b_ref[:, 0:mix]) * scale).astype(BF16)
    k_ref[...] = (_dot(xb, w_ref[:, mix:2 * mix]) + b_ref[:, mix:2 * mix]).astype(BF16)
    v = _dot(xb, wv_ref[...]) + bv_ref[...]
    head_lanes = lax.broadcasted_iota(jnp.int32, (ts, LANES), 1) < FOX_HD
    for pair in range(FOX_HEADS // 2):
        both = v[:, pair * LANES:(pair + 1) * LANES]
        swapped = pltpu.roll(both, LANES - FOX_HD, axis=1)
        for head, vals in enumerate((both, swapped)):
            blk = 2 * pair + head
            v_ref[:, blk * PAD_HEAD:(blk + 1) * PAD_HEAD] = (
                jnp.where(head_lanes, vals, 1.0).astype(BF16))

    log_f = jax.nn.log_sigmoid(_dot_nt(wf_ref[...], xb) + bf_ref[...])
    r = lax.broadcasted_iota(jnp.int32, (ts, ts), 0)
    c = lax.broadcasted_iota(jnp.int32, (ts, ts), 1)
    upper = (r <= c).astype(F32)
    csum = jnp.dot(log_f, upper, precision=lax.Precision.HIGHEST,
                   preferred_element_type=F32) + carry[...]
    cneg_ref[...] = csum * -LOG2_E
    carry[...] = carry[...] + jnp.sum(log_f, axis=1, keepdims=True)


def _l1_front(x, w_in, b_in):
    bsz, seq, _ = x.shape
    mix = FOX_HEADS * FOX_HD
    ts = SEQ_TILE
    row = lambda w: pl.BlockSpec((None, ts, w), lambda b, s: (b, s, 0))
    w_qk = w_in[:, :2 * mix].astype(BF16)
    b_qk = b_in[None, :2 * mix]
    w_v = w_in[:, 2 * mix:3 * mix].astype(BF16)
    b_v = b_in[None, 2 * mix:3 * mix]
    v_w = FOX_HEADS * PAD_HEAD
    wf_t = w_in[:, 3 * mix:].T.astype(BF16)
    bf_t = b_in[3 * mix:][:, None]
    return pl.pallas_call(
        _l1_front_kernel,
        grid=(bsz, seq // ts),
        in_specs=[row(D_MODEL), _resident((D_MODEL, 2 * mix)), _resident((1, 2 * mix)),
                  _resident((D_MODEL, mix)), _resident((1, mix)),
                  _resident((FOX_HEADS, D_MODEL)), _resident((FOX_HEADS, 1))],
        out_specs=[row(mix), row(mix), row(v_w),
                   pl.BlockSpec((None, FOX_HEADS, ts), lambda b, s: (b, 0, s))],
        out_shape=[jax.ShapeDtypeStruct((bsz, seq, mix), BF16)] * 2
        + [jax.ShapeDtypeStruct((bsz, seq, v_w), BF16),
           jax.ShapeDtypeStruct((bsz, FOX_HEADS, seq), F32)],
        scratch_shapes=[pltpu.VMEM((FOX_HEADS, 1), F32)],
        compiler_params=_params("arbitrary", "arbitrary"),
        name="l1_front",
    )(x, w_qk, b_qk, w_v, b_v, wf_t, bf_t)


def _attn_kernel(*refs, packed, has_bias):
    if has_bias:
        q_ref, k_ref, v_ref, cneg_ref, o_ref = refs
    else:
        q_ref, k_ref, v_ref, o_ref = refs
        cneg_ref = None
    tq = ATTN_TILE
    tk = ATTN_KEY_TILE
    half = LANES // 2

    def query_tile(qi):
        q_rows = pl.ds(pl.multiple_of(qi * tq, tq), tq)
        q2 = q_ref[q_rows, :]
        if packed:
            lane = lax.broadcasted_iota(jnp.int32, q2.shape, 1)
            zero = jnp.zeros_like(q2)
            qs = (jnp.where(lane < half, q2, zero), jnp.where(lane >= half, q2, zero))
        else:
            qs = (q2[:, :PAD_HEAD], q2[:, PAD_HEAD:])

        def k_block(kj, head):
            rows = pl.ds(pl.multiple_of(kj * tk, tk), tk)
            if packed:
                return k_ref[rows, :]
            return k_ref[rows, head * PAD_HEAD:(head + 1) * PAD_HEAD]

        def step(kj, carry, diag):
            rows = pl.ds(pl.multiple_of(kj * tk, tk), tk)
            skip = 0 if diag is None else diag * tk
            out = []
            for head in range(2):
                m_all, acc_all = carry[head]
                m, acc = m_all[skip:], acc_all[skip:]
                s = _dot_nt(qs[head][skip:], k_block(kj, head))
                if has_bias:
                    s = s + cneg_ref[head, :, rows]
                if diag is not None:
                    r = lax.broadcasted_iota(jnp.int32, s.shape, 0)
                    c = lax.broadcasted_iota(jnp.int32, s.shape, 1)
                    s = jnp.where(r >= c, s, NEG_INF)
                m_new = jnp.maximum(m, jnp.max(s, axis=-1, keepdims=True))
                alpha = jnp.exp2(m - m_new)
                p = jnp.exp2(s - m_new).astype(BF16)
                vblk = v_ref[rows, head * PAD_HEAD:(head + 1) * PAD_HEAD]
                acc = alpha * acc + _dot(p, vblk)
                if skip:
                    m_new = jnp.concatenate([m_all[:skip], m_new], axis=0)
                    acc = jnp.concatenate([acc_all[:skip], acc], axis=0)
                out.append((m_new, acc))
            return tuple(out)

        init = tuple((jnp.full((tq, 1), NEG_INF, F32), jnp.zeros((tq, LANES), F32))
                     for _ in range(2))
        per_tile = tq // tk
        n_full = qi * per_tile

        def group(first, c, size):
            for t in range(size):
                c = step(first + t, c, None)
            return c

        def diagonal(c):
            for d in range(per_tile):
                c = step(n_full + d, c, d)
            return c

        trips = n_full // ATTN_UNROLL
        carry = lax.fori_loop(0, jnp.maximum(trips - 1, 0),
                              lambda j, c: group(ATTN_UNROLL * j, c, ATTN_UNROLL), init)
        carry = lax.cond(
            trips >= 1,
            lambda c: diagonal(group(n_full - ATTN_UNROLL, c, ATTN_UNROLL)),
            diagonal, carry)
        (_, acc_a), (_, acc_b) = carry
        out_a = acc_a / pltpu.roll(acc_a, half, axis=1)
        out_b = acc_b / pltpu.roll(acc_b, half, axis=1)
        lane_o = lax.broadcasted_iota(jnp.int32, (tq, LANES), 1)
        o_ref[q_rows, :] = jnp.where(lane_o < half, out_a,
                                     pltpu.roll(out_b, half, axis=1)).astype(o_ref.dtype)

    pl.loop(0, q_ref.shape[0] // tq)(query_tile)


def _attention(q, k, v, cneg, packed):
    bsz, seq, v_w = v.shape
    assert ATTN_TILE % (ATTN_KEY_TILE * ATTN_UNROLL) == 0 and seq % ATTN_TILE == 0
    pairs = v_w // (2 * PAD_HEAD)
    qk_w = q.shape[-1] // pairs
    in_specs = [pl.BlockSpec((None, seq, qk_w), lambda b, p: (b, 0, p)),
                pl.BlockSpec((None, seq, qk_w), lambda b, p: (b, 0, p)),
                pl.BlockSpec((None, seq, 2 * PAD_HEAD), lambda b, p: (b, 0, p))]
    args = [q, k, v]
    if cneg is not None:
        in_specs.append(pl.BlockSpec((None, 2, 1, seq), lambda b, p: (b, p, 0, 0)))
        args.append(cneg)
    return pl.pallas_call(
        functools.partial(_attn_kernel, packed=packed, has_bias=cneg is not None),
        grid=(bsz, pairs),
        in_specs=in_specs,
        out_specs=pl.BlockSpec((None, seq, LANES), lambda b, p: (b, 0, p)),
        out_shape=jax.ShapeDtypeStruct((bsz, seq, pairs * LANES), BF16),
        compiler_params=_params("arbitrary", "arbitrary"),
        name="attn_packed" if packed else "attn_padded",
    )(*args)


def _first_hit(hits, found):
    out = []
    for hcur in hits:
        take = jnp.logical_and(hcur, jnp.logical_not(found))
        found = jnp.logical_or(found, take)
        out.append(take)
    return out, found


def _mix_out_router_kernel(a_ref, b_ref, x_ref, wa_ref, wb_ref, bo_ref, g_ref, be_ref,
                           rw_ref, rwl_ref, rb_ref,
                           x1_ref, x1p_ref, e_ref, gate_ref, rank_ref, cnt_ref, cnt, gate_t):
    tm = x_ref.shape[0]

    @pl.when(pl.program_id(0) == 0)
    def _():
        cnt[...] = jnp.zeros_like(cnt)
        gate_t[...] = jnp.zeros_like(gate_t)

    mix = _dot(a_ref[...], wa_ref[...]) + _dot(b_ref[...], wb_ref[...]) + bo_ref[...]
    x1 = _layer_norm(DN_ALPHA * x_ref[...] + mix, g_ref[...], be_ref[...])
    x1_ref[...] = x1
    _store_packed(x1p_ref, _pack_rows(x1))

    x_hi = x1.astype(BF16)
    x_lo = (x1 - x_hi.astype(F32)).astype(BF16)
    logits = (_dot_nt(rw_ref[...], x_hi) + _dot_nt(rw_ref[...], x_lo)
              + _dot_nt(rwl_ref[...], x_hi))
    aff = jax.nn.sigmoid(logits)
    choice = aff + rb_ref[...]
    per_group = N_EXPERTS // N_GROUPS
    sub = lax.broadcasted_iota(jnp.int32, (per_group, tm), 0)
    groups = [choice[g * per_group:(g + 1) * per_group, :] for g in range(N_GROUPS)]

    gscore = []
    for cg in groups:
        m1 = jnp.max(cg, axis=0, keepdims=True)
        i1 = jnp.min(jnp.where(cg == m1, sub, per_group), axis=0, keepdims=True)
        m2 = jnp.max(jnp.where(sub == i1, NEG_INF, cg), axis=0, keepdims=True)
        gscore.append(m1 + m2)

    gsel = [jnp.zeros((1, tm), jnp.bool_) for _ in range(N_GROUPS)]
    for _ in range(TOPK_GROUPS):
        best = functools.reduce(jnp.maximum, gscore)
        takes, _ = _first_hit([gs == best for gs in gscore], jnp.zeros((1, tm), jnp.bool_))
        gsel = [jnp.logical_or(a, t) for a, t in zip(gsel, takes)]
        gscore = [jnp.where(t, NEG_INF, gs) for gs, t in zip(gscore, takes)]

    masked = [jnp.where(gs, cg, NEG_INF) for gs, cg in zip(gsel, groups)]
    eid = [sub + g * per_group for g in range(N_GROUPS)]
    affs = [aff[g * per_group:(g + 1) * per_group, :] for g in range(N_GROUPS)]
    sel = [jnp.zeros((per_group, tm), jnp.bool_) for _ in range(N_GROUPS)]
    picked_e, picked_w = [], []
    for _ in range(TOP_K):
        best = jnp.max(functools.reduce(jnp.maximum, masked), axis=0, keepdims=True)
        cand = [jnp.where(mg == best, ig, N_EXPERTS) for mg, ig in zip(masked, eid)]
        idx = jnp.min(functools.reduce(jnp.minimum, cand), axis=0, keepdims=True)
        onehot = [ig == idx for ig in eid]
        w = functools.reduce(
            jnp.add, [jnp.sum(jnp.where(oh, ag, 0.0), axis=0, keepdims=True)
                      for oh, ag in zip(onehot, affs)])
        picked_e.append(idx)
        picked_w.append(w)
        sel = [jnp.logical_or(sg, oh) for sg, oh in zip(sel, onehot)]
        masked = [jnp.where(oh, NEG_INF, mg) for mg, oh in zip(masked, onehot)]

    wsum = functools.reduce(jnp.add, picked_w)

    sel_f = jnp.concatenate([sg.astype(F32) for sg in sel], axis=0)
    r = lax.broadcasted_iota(jnp.int32, (tm, tm), 0)
    c = lax.broadcasted_iota(jnp.int32, (tm, tm), 1)
    before = _dot(sel_f.astype(BF16), (r < c).astype(BF16)) + cnt[...]
    eall = lax.broadcasted_iota(jnp.int32, (N_EXPERTS, tm), 0)
    for kk in range(TOP_K):
        rank = jnp.sum(jnp.where(eall == picked_e[kk], before, 0.0), axis=0, keepdims=True)
        e_ref[kk:kk + 1, :] = picked_e[kk]
        gate_t[kk:kk + 1, :] = picked_w[kk] / wsum * ROUTED_SCALE
        rank_ref[kk:kk + 1, :] = rank.astype(jnp.int32)
    pad = SUBLANES - TOP_K
    e_ref[TOP_K:, :] = jnp.zeros((pad, tm), jnp.int32)
    rank_ref[TOP_K:, :] = jnp.zeros((pad, tm), jnp.int32)
    gate_ref[...] = gate_t[...].T
    cnt[...] = cnt[...] + jnp.sum(sel_f, axis=1, keepdims=True)
    cnt_ref[...] = jnp.broadcast_to(cnt[...], cnt_ref.shape)


def _mix_out_router(a, b, x, w_out, b_out, ln_g, ln_b, router_w, router_bias):
    n_tok = x.shape[0]
    tm = TOKEN_TILE
    half = w_out.shape[0] // 2
    row = lambda w: pl.BlockSpec((tm, w), lambda i: (i, 0))
    b_spec = (pl.BlockSpec((tm, half), lambda i: (i, 1)) if b is a
              else pl.BlockSpec((tm, half), lambda i: (i, 0)))
    lane_row = pl.BlockSpec((SUBLANES, tm), lambda i: (0, i))
    packed = pl.BlockSpec((tm // SUBLANES, PACK_BLOCKS, SUBLANES, LANES),
                          lambda i: (i, 0, 0, 0))
    w_bf = w_out.astype(BF16)
    rw_t = router_w.T
    rw_hi = rw_t.astype(BF16)
    rw_lo = (rw_t - rw_hi.astype(F32)).astype(BF16)
    return pl.pallas_call(
        _mix_out_router_kernel,
        grid=(n_tok // tm,),
        in_specs=[row(half), b_spec, row(D_MODEL),
                  pl.BlockSpec((half, D_MODEL), lambda i: (0, 0)),
                  pl.BlockSpec((half, D_MODEL), lambda i: (1, 0)),
                  _resident((1, D_MODEL)), _resident((1, D_MODEL)), _resident((1, D_MODEL)),
                  _resident((N_EXPERTS, D_MODEL)), _resident((N_EXPERTS, D_MODEL)),
                  _resident((N_EXPERTS, 1))],
        out_specs=[row(D_MODEL), packed, lane_row, row(LANES), lane_row,
                   _resident((N_EXPERTS, LANES))],
        out_shape=[jax.ShapeDtypeStruct((n_tok, D_MODEL), F32),
                   jax.ShapeDtypeStruct((n_tok // SUBLANES, PACK_BLOCKS, SUBLANES, LANES),
                                        jnp.int32),
                   jax.ShapeDtypeStruct((SUBLANES, n_tok), jnp.int32),
                   jax.ShapeDtypeStruct((n_tok, LANES), F32),
                   jax.ShapeDtypeStruct((SUBLANES, n_tok), jnp.int32),
                   jax.ShapeDtypeStruct((N_EXPERTS, LANES), F32)],
        scratch_shapes=[pltpu.VMEM((N_EXPERTS, 1), F32), pltpu.VMEM((LANES, tm), F32)],
        compiler_params=_params("arbitrary"),
        name="mix_out_router",
    )(a, b, x, w_bf, w_bf, b_out[None, :], ln_g[None, :], ln_b[None, :],
      rw_hi, rw_lo, router_bias[:, None])


def _expert_ffn_kernel(tile_e_ref, tile_on_ref, xs_ref, *refs):
    w_refs = refs[:3 * FFN_SLOTS]
    ys_ref = refs[3 * FFN_SLOTS]
    wg_bf, wu_bf, wd_bf = refs[3 * FFN_SLOTS + 1:]
    i = pl.program_id(0)
    sub = GROUP_TILE // SUBLANES

    for slot in range(FFN_SLOTS):
        t = FFN_SLOTS * i + slot
        fresh = jnp.logical_or(
            i == 0, tile_e_ref[t] != tile_e_ref[jnp.maximum(t - FFN_SLOTS, 0)])

        @pl.when(fresh)
        def _(slot=slot):
            wg_ref, wu_ref, wd_ref = w_refs[3 * slot:3 * slot + 3]
            wg_bf[slot] = wg_ref[...].astype(BF16)
            wu_bf[slot] = wu_ref[...].astype(BF16)
            wd_bf[slot] = wd_ref[...].astype(BF16)

    on = tile_on_ref[FFN_SLOTS * i] > 0

    @pl.when(on)
    def _():
        for slot in range(FFN_SLOTS):
            rows = pl.ds(slot * sub, sub)
            lo, hi = _unpack_rows(_load_packed(xs_ref.at[rows]))
            lo = lo.astype(BF16)
            hi = hi.astype(BF16)
            gate = (_dot(lo, wg_bf[slot, :PACK_COLS, :]) + _dot(hi, wg_bf[slot, PACK_COLS:, :]))
            up = (_dot(lo, wu_bf[slot, :PACK_COLS, :]) + _dot(hi, wu_bf[slot, PACK_COLS:, :]))
            hid = (jax.nn.silu(gate) * up).astype(BF16)
            _store_packed(ys_ref.at[rows], _pack_rows(_dot(hid, wd_bf[slot])))

    @pl.when(jnp.logical_not(on))
    def _():
        ys_ref[...] = jnp.zeros_like(ys_ref)


def _expert_ffn(xs, tile_e, tile_on, w_gate, w_up, w_down):
    rows_per_step = GROUP_TILE * FFN_SLOTS
    packed = pl.BlockSpec((rows_per_step // SUBLANES, PACK_BLOCKS, SUBLANES, LANES),
                          lambda i, te, on: (i, 0, 0, 0))
    w_specs = []
    for slot in range(FFN_SLOTS):
        pick = lambda i, te, on, slot=slot: (te[FFN_SLOTS * i + slot], 0, 0)
        w_specs += [pl.BlockSpec((None, D_MODEL, EXPERT_HID), pick),
                    pl.BlockSpec((None, D_MODEL, EXPERT_HID), pick),
                    pl.BlockSpec((None, EXPERT_HID, D_MODEL), pick)]
    grid_spec = pltpu.PrefetchScalarGridSpec(
        num_scalar_prefetch=2,
        grid=(xs.shape[0] * SUBLANES // rows_per_step,),
        in_specs=[packed] + w_specs,
        out_specs=packed,
        scratch_shapes=[pltpu.VMEM((FFN_SLOTS, D_MODEL, EXPERT_HID), BF16),
                        pltpu.VMEM((FFN_SLOTS, D_MODEL, EXPERT_HID), BF16),
                        pltpu.VMEM((FFN_SLOTS, EXPERT_HID, D_MODEL), BF16)],
    )
    return pl.pallas_call(
        _expert_ffn_kernel,
        grid_spec=grid_spec,
        out_shape=jax.ShapeDtypeStruct(xs.shape, jnp.int32),
        compiler_params=_params("arbitrary"),
        name="expert_ffn",
    )(tile_e, tile_on, xs, *([w_gate, w_up, w_down] * FFN_SLOTS))


def _moe_out_kernel(x1_ref, yg_ref, gate_ref, sg_ref, su_ref, sd_ref, g_ref, b_ref, *rest):
    o_ref = rest[-1]
    x1 = x1_ref[...]
    xb = x1.astype(BF16)
    hid = jax.nn.silu(_dot(xb, sg_ref[...])) * _dot(xb, su_ref[...])
    shared = _dot(hid.astype(BF16), sd_ref[...])
    gates = gate_ref[...]
    lo_acc = shared[:, :PACK_COLS]
    hi_acc = shared[:, PACK_COLS:]
    for kk in range(TOP_K):
        lo, hi = _unpack_rows(_load_packed(yg_ref.at[kk]))
        w = gates[:, kk:kk + 1]
        lo_acc = lo_acc + w * lo
        hi_acc = hi_acc + w * hi
    moe = jnp.concatenate([lo_acc, hi_acc], axis=1)
    o_ref[...] = _layer_norm(DN_ALPHA * x1 + moe, g_ref[...], b_ref[...])


def _moe_out(x1, yg, gates, sh_gate, sh_up, sh_down, ln_g, ln_b, part, earlier):
    n_tok = x1.shape[0]
    tm = TOKEN_TILE
    steps = n_tok // tm // MOE_PARTS
    first = part * steps
    row = lambda w: pl.BlockSpec((tm, w), lambda i: (i + first, 0))
    in_specs = [row(D_MODEL),
                pl.BlockSpec((TOP_K, tm // SUBLANES, PACK_BLOCKS, SUBLANES, LANES),
                             lambda i: (0, i, 0, 0, 0)),
                row(LANES),
                _resident((D_MODEL, EXPERT_HID)), _resident((D_MODEL, EXPERT_HID)),
                _resident((EXPERT_HID, D_MODEL)), _resident((1, D_MODEL)),
                _resident((1, D_MODEL))]
    args = [x1, yg, gates, sh_gate.astype(BF16), sh_up.astype(BF16), sh_down.astype(BF16),
            ln_g[None, :], ln_b[None, :]]
    aliases = {}
    if earlier is not None:
        in_specs.append(pl.BlockSpec(memory_space=pl.ANY))
        args.append(earlier)
        aliases = {len(args) - 1: 0}
    return pl.pallas_call(
        _moe_out_kernel,
        grid=(steps,),
        in_specs=in_specs,
        out_specs=row(D_MODEL),
        out_shape=jax.ShapeDtypeStruct((n_tok, D_MODEL), F32),
        input_output_aliases=aliases,
        compiler_params=_params("arbitrary"),
        name="moe_out",
    )(*args)


def _slot_kernel(offset_ref, e_ref, rank_ref, pos_ref):
    e = e_ref[...]
    base = jnp.zeros(e.shape, jnp.int32)
    for ex in range(N_EXPERTS):
        base = jnp.where(e == ex, offset_ref[ex], base)
    pos_ref[...] = base + rank_ref[...]


def _slots(offset, e6, rank6):
    n_tok = e6.shape[1]
    tl = SLOT_TILE
    lane_row = pl.BlockSpec((SUBLANES, tl), lambda i, off: (0, i))
    return pl.pallas_call(
        _slot_kernel,
        grid_spec=pltpu.PrefetchScalarGridSpec(
            num_scalar_prefetch=1, grid=(n_tok // tl,),
            in_specs=[lane_row, lane_row], out_specs=lane_row),
        out_shape=jax.ShapeDtypeStruct((SUBLANES, n_tok), jnp.int32),
        compiler_params=_params("arbitrary"),
        name="slots",
    )(offset, e6, rank6)


def _sc_mesh():
    return plsc.VectorSubcoreMesh(core_axis_name="c", subcore_axis_name="s")


def _sc_worker():
    return lax.axis_index("s") * SC_CORES + lax.axis_index("c")


def _sc_dispatch(x_sub, idx, n_out_sub):
    windows = x_sub.shape[0] // (SC_WORKERS * SC_WINDOW)

    @functools.partial(
        pl.kernel, mesh=_sc_mesh(),
        out_type=jax.ShapeDtypeStruct((n_out_sub, LANES), jnp.int32),
        scratch_types=[pltpu.VMEM((SC_WINDOW, LANES), jnp.int32),
                       pltpu.VMEM((TOP_K, SC_WINDOW), jnp.int32)],
        name="sc_dispatch")
    def run(x_hbm, idx_hbm, out_hbm, rows_v, idx_v):
        wid = _sc_worker()

        @pl.loop(0, windows)
        def _(c):
            base = (wid * windows + c) * SC_WINDOW
            pltpu.sync_copy(x_hbm.at[pl.ds(base, SC_WINDOW)], rows_v)
            pltpu.sync_copy(idx_hbm.at[wid, c], idx_v)
            for kk in range(TOP_K):
                pltpu.sync_copy(rows_v, out_hbm.at[idx_v.at[kk]])

    return run(x_sub, idx)


def _sc_combine(y_sub, idx, n_tok_sub):
    windows = n_tok_sub // (SC_WORKERS * SC_WINDOW)

    @functools.partial(
        pl.kernel, mesh=_sc_mesh(),
        out_type=jax.ShapeDtypeStruct((TOP_K, n_tok_sub, LANES), jnp.int32),
        scratch_types=[pltpu.VMEM((2, SC_WINDOW, LANES), jnp.int32),
                       pltpu.VMEM((TOP_K, SC_WINDOW), jnp.int32),
                       pltpu.SemaphoreType.DMA((2,)),
                       pltpu.SemaphoreType.DMA((2,))],
        name="sc_combine")
    def run(y_hbm, idx_hbm, out_hbm, rows_v, idx_v, gather_sem, write_sem):
        wid = _sc_worker()

        def gather(kk):
            buf = kk % 2
            return pltpu.async_copy(y_hbm.at[idx_v.at[kk]], rows_v.at[buf], gather_sem.at[buf])

        @pl.loop(0, windows)
        def _(c):
            base = (wid * windows + c) * SC_WINDOW
            pltpu.sync_copy(idx_hbm.at[wid, c], idx_v)
            writes = []
            pending = gather(0)
            for kk in range(TOP_K):
                pending.wait()
                if kk + 1 < TOP_K:
                    if kk >= 1:
                        writes[kk - 1].wait()
                    pending = gather(kk + 1)
                writes.append(pltpu.async_copy(
                    rows_v.at[kk % 2], out_hbm.at[kk, pl.ds(base, SC_WINDOW)],
                    write_sem.at[kk % 2]))
            writes[TOP_K - 2].wait()
            writes[TOP_K - 1].wait()

    return run(y_sub, idx)


def _moe(x1, x1p, e6, gates, rank6, counts, exp_w_gate, exp_w_up, exp_w_down,
         sh_w_gate, sh_w_up, sh_w_down, ln_g, ln_b):
    n_tok = x1.shape[0]
    tg = GROUP_TILE
    n_rows = n_tok * TOP_K + N_EXPERTS * tg
    n_tiles = n_rows // tg

    cnt = counts[:, 0].astype(jnp.int32)
    tiles_per_e = (cnt + tg - 1) // tg
    tile_end = jnp.cumsum(tiles_per_e)
    offset = (tile_end - tiles_per_e) * tg
    tile_ids = jnp.arange(n_tiles, dtype=jnp.int32)
    tile_on = (tile_ids < tile_end[-1]).astype(jnp.int32)
    tile_e = jnp.minimum(
        jnp.sum((tile_ids[:, None] >= tile_end[None, :]).astype(jnp.int32), axis=1),
        N_EXPERTS - 1)
    tile_e = jnp.where(tile_on > 0, tile_e, tile_e[jnp.maximum(tile_end[-1] - 1, 0)])

    pos = _slots(offset, e6, rank6)[:TOP_K]

    group = PACK_BLOCKS * SUBLANES
    sub0 = (pos // SUBLANES) * group + pos % SUBLANES
    lane = jnp.arange(SC_WINDOW)
    spread = (jnp.arange(group)[:, None]
              == (lane // group) * SUBLANES + lane % SUBLANES).astype(F32)
    idx = jnp.einsum("kws,sd->kwd", sub0.reshape(TOP_K, n_tok // group, group).astype(F32),
                     spread, precision=lax.Precision.HIGHEST).astype(jnp.int32)
    idx = idx + ((lane % group) // SUBLANES * SUBLANES).astype(jnp.int32)
    n_sub = n_tok * PACK_BLOCKS
    idx = idx.reshape(TOP_K, n_sub)

    def windowed(ix):
        windows = ix.shape[1] // (SC_WORKERS * SC_WINDOW)
        return ix.reshape(TOP_K, SC_WORKERS, windows, SC_WINDOW).transpose(1, 2, 0, 3)

    xs = _sc_dispatch(x1p.reshape(n_sub, LANES), windowed(idx), n_rows * PACK_BLOCKS)
    ys = _expert_ffn(xs.reshape(n_rows // SUBLANES, PACK_BLOCKS, SUBLANES, LANES),
                     tile_e, tile_on, exp_w_gate, exp_w_up, exp_w_down)
    ys = ys.reshape(n_rows * PACK_BLOCKS, LANES)
    part_sub = n_sub // MOE_PARTS
    out = None
    for part in range(MOE_PARTS):
        yg = _sc_combine(ys, windowed(idx[:, part * part_sub:(part + 1) * part_sub]), part_sub)
        yg = yg.reshape(TOP_K, part_sub // (PACK_BLOCKS * SUBLANES), PACK_BLOCKS,
                        SUBLANES, LANES)
        out = _moe_out(x1, yg, gates, sh_w_gate, sh_w_up, sh_w_down, ln_g, ln_b, part, out)
    return out


def kernel(x, positions,
           l0_w_in, l0_b_in, l0_conv_w, l0_conv_b, l0_conv_ln_g, l0_conv_ln_b,
           l0_q_norm_g, l0_w_uq, l0_kv_norm_g, l0_w_ukv, l0_w_out, l0_b_out,
           l0_ln1_g, l0_ln1_b,
           l0_router_w, l0_router_bias, l0_exp_w_gate, l0_exp_w_up, l0_exp_w_down,
           l0_sh_w_gate, l0_sh_w_up, l0_sh_w_down, l0_ln2_g, l0_ln2_b,
           l1_w_in, l1_b_in, l1_w_out, l1_b_out, l1_ln1_g, l1_ln1_b,
           l1_router_w, l1_router_bias, l1_exp_w_gate, l1_exp_w_up, l1_exp_w_down,
           l1_sh_w_gate, l1_sh_w_up, l1_sh_w_down, l1_ln2_g, l1_ln2_b):
    bsz, seq, d = x.shape
    n_tok = bsz * seq

    u, q, k, v = _l0_front(x, positions, l0_w_in, l0_b_in, l0_conv_w, l0_conv_b,
                           l0_conv_ln_g, l0_conv_ln_b, l0_q_norm_g, l0_w_uq,
                           l0_kv_norm_g, l0_w_ukv)
    o = _attention(q, k, v, None, packed=False)
    x_flat = x.reshape(n_tok, d)
    routing = _mix_out_router(u.reshape(n_tok, -1), o.reshape(n_tok, -1), x_flat,
                              l0_w_out, l0_b_out, l0_ln1_g, l0_ln1_b,
                              l0_router_w, l0_router_bias)
    x2 = _moe(*routing, l0_exp_w_gate, l0_exp_w_up, l0_exp_w_down,
              l0_sh_w_gate, l0_sh_w_up, l0_sh_w_down, l0_ln2_g, l0_ln2_b)

    q, k, v, cneg = _l1_front(x2.reshape(bsz, seq, d), l1_w_in, l1_b_in)
    o = _attention(q, k, v, cneg[:, :, None, :], packed=True).reshape(n_tok, -1)
    routing = _mix_out_router(o, o, x2, l1_w_out, l1_b_out, l1_ln1_g, l1_ln1_b,
                              l1_router_w, l1_router_bias)
    x3 = _moe(*routing, l1_exp_w_gate, l1_exp_w_up, l1_exp_w_down,
              l1_sh_w_gate, l1_sh_w_up, l1_sh_w_down, l1_ln2_g, l1_ln2_b)
    return x3.reshape(bsz, seq, d)
```

```python
import functools

import jax
import jax.numpy as jnp
from jax import lax
from jax.experimental import pallas as pl
from jax.experimental.pallas import tpu as pltpu
from jax.experimental.pallas import tpu_sc as plsc

D_MODEL = 1024
DEPTH = 2
DN_ALPHA = (2 * DEPTH) ** 0.25
LN_EPS = 1e-5
RMS_EPS = 1e-6

CONV_CH = 512
CONV_WIDTH = 31
MLA_HEADS = 8
QK_NOPE = 64
QK_ROPE = 32
V_DIM = 64
Q_LORA = 256
KV_LORA = 128
ROPE_THETA = 10000.0
FOX_HEADS = 16
FOX_HD = 64
N_EXPERTS = 64
TOP_K = 6
N_GROUPS = 8
TOPK_GROUPS = 4
EXPERT_HID = 256
ROUTED_SCALE = 2.5

LANES = 128
SUBLANES = 8
V7X_VMEM_BYTES = 64 * 1024 * 1024
VMEM_LIMIT_BYTES = V7X_VMEM_BYTES * 7 // 8

SEQ_TILE = 512
TOKEN_TILE = 512
SLOT_TILE = 4096
ATTN_TILE = 1024
ATTN_KEY_TILE = 512
ATTN_UNROLL = 2
GROUP_TILE = 512
FFN_SLOTS = 2
CONV_ROWS = 32
CONV_HALO = 32
PAD_HEAD = 128

PACK_COLS = D_MODEL // 2
PACK_BLOCKS = PACK_COLS // LANES
SC_CORES = 2
SC_SUBCORES = 16
SC_WORKERS = SC_CORES * SC_SUBCORES
SC_WINDOW = 128
MOE_PARTS = 8
BF16 = jnp.bfloat16
F32 = jnp.float32
NEG_INF = float("-inf")
LOG2_E = 1.4426950408889634


def _params(*semantics):
    return pltpu.CompilerParams(dimension_semantics=semantics,
                                vmem_limit_bytes=VMEM_LIMIT_BYTES)


def _resident(shape):
    nd = len(shape)
    return pl.BlockSpec(shape, lambda *_: (0,) * nd)


def _dot(a, b):
    return jnp.dot(a, b, preferred_element_type=F32)


def _dot_nt(a, b, precision=None):
    return lax.dot_general(a, b, (((1,), (1,)), ((), ())),
                           precision=precision, preferred_element_type=F32)


def _layer_norm(x, g, b):
    mu = jnp.mean(x, axis=-1, keepdims=True)
    xc = x - mu
    var = jnp.mean(xc * xc, axis=-1, keepdims=True)
    return xc * lax.rsqrt(var + LN_EPS) * g + b


def _rms_norm(x, g):
    return x * lax.rsqrt(jnp.mean(x * x, axis=-1, keepdims=True) + RMS_EPS) * g


def _ones_upper_half(width):
    lane = lax.broadcasted_iota(jnp.int32, (1, width), 1)
    return jnp.where((lane & (PAD_HEAD - 1)) >= V_DIM, 1.0, 0.0).astype(F32)


def _pad_heads(w, heads):
    d = w.shape[-1] // heads
    w3 = w.reshape(w.shape[:-1] + (heads, d))
    pad = jnp.zeros(w.shape[:-1] + (heads, PAD_HEAD - d), w.dtype)
    return jnp.concatenate([w3, pad], axis=-1).reshape(w.shape[:-1] + (heads * PAD_HEAD,))


def _pack_rows(x):
    lo = lax.bitcast_convert_type(x[:, :PACK_COLS].astype(BF16).astype(F32), jnp.uint32)
    hi = lax.bitcast_convert_type(x[:, PACK_COLS:].astype(BF16).astype(F32), jnp.uint32)
    word = (lo >> 16) | (hi & jnp.uint32(0xFFFF0000))
    return lax.bitcast_convert_type(word, jnp.int32)


def _unpack_rows(words):
    u = lax.bitcast_convert_type(words, jnp.uint32)
    lo = lax.bitcast_convert_type(u << 16, F32)
    hi = lax.bitcast_convert_type(u & jnp.uint32(0xFFFF0000), F32)
    return lo, hi


def _store_packed(ref, words):
    groups = words.shape[0] // SUBLANES
    for cb in range(PACK_BLOCKS):
        ref[:, cb, :, :] = words[:, cb * LANES:(cb + 1) * LANES].reshape(groups, SUBLANES, LANES)


def _load_packed(ref):
    rows = ref.shape[0] * SUBLANES
    return jnp.concatenate(
        [ref[:, cb, :, :].reshape(rows, LANES) for cb in range(PACK_BLOCKS)], axis=1)


L0_A = 0
L0_G = CONV_CH
L0_Q = 2 * CONV_CH
L0_KV = L0_Q + Q_LORA
L0_KR = L0_KV + KV_LORA
L0_KRR = L0_KR + PAD_HEAD
L0_COLS = L0_KRR + PAD_HEAD


def _l0_front_kernel(x_ref, pos_ref, w_in_ref, b_in_ref, conv_w_ref, conv_b_ref,
                     cln_g_ref, cln_b_ref, qn_g_ref, wq_ref, wqr_ref, kvn_g_ref,
                     wk_ref, wv_ref, invf_ref,
                     u_ref, q_ref, k_ref, v_ref, ubuf):
    ts = x_ref.shape[0]

    @pl.when(pl.program_id(1) == 0)
    def _():
        ubuf[0:CONV_HALO, :] = jnp.zeros((CONV_HALO, CONV_CH), F32)

    h = _dot(x_ref[...].astype(BF16), w_in_ref[...]) + b_in_ref[...]

    ubuf[CONV_HALO:CONV_HALO + ts, :] = (
        h[:, L0_A:L0_A + CONV_CH] * jax.nn.sigmoid(h[:, L0_G:L0_G + CONV_CH]))
    first_tap = CONV_HALO - (CONV_WIDTH - 1)

    def conv_chunk(c, carry):
        base = pl.multiple_of(c * CONV_ROWS, CONV_ROWS)
        acc = jnp.broadcast_to(conv_b_ref[...], (CONV_ROWS, CONV_CH))
        for res in range(SUBLANES):
            rows = CONV_ROWS + (SUBLANES if res else 0)
            part = None
            for off in range(res, first_tap + CONV_WIDTH, SUBLANES):
                j = off - first_tap
                if j < 0:
                    continue
                window = ubuf[pl.ds(base + (off - res), rows), :]
                term = window.reshape(rows // SUBLANES, SUBLANES, CONV_CH) * conv_w_ref[j][None]
                part = term if part is None else part + term
            acc = acc + part.reshape(rows, CONV_CH)[res:res + CONV_ROWS, :]
        y = _layer_norm(acc, cln_g_ref[...], cln_b_ref[...])
        u_ref[pl.ds(base, CONV_ROWS), :] = (y * jax.nn.sigmoid(y)).astype(BF16)
        return carry

    lax.fori_loop(0, ts // CONV_ROWS, conv_chunk, 0, unroll=4)
    ubuf[0:CONV_HALO, :] = ubuf[ts:ts + CONV_HALO, :]

    ang = pos_ref[...].astype(F32) * invf_ref[...]
    cos = jnp.cos(ang)
    sin = jnp.sin(ang)
    scale = (QK_NOPE + QK_ROPE) ** -0.5 * LOG2_E

    qn = _rms_norm(h[:, L0_Q:L0_Q + Q_LORA], qn_g_ref[...]).astype(BF16)
    q = _dot(qn, wq_ref[...])
    q_rot = _dot(qn, wqr_ref[...])
    cos_s = cos * scale
    sin_s = sin * scale
    for hh in range(MLA_HEADS):
        blk = slice(hh * PAD_HEAD, (hh + 1) * PAD_HEAD)
        q_ref[:, blk] = (q[:, blk] * cos_s + q_rot[:, blk] * sin_s).astype(BF16)

    kvn = _rms_norm(h[:, L0_KV:L0_KV + KV_LORA], kvn_g_ref[...]).astype(BF16)
    k_nope = _dot(kvn, wk_ref[...])
    k_pe = h[:, L0_KR:L0_KR + PAD_HEAD] * cos + h[:, L0_KRR:L0_KRR + PAD_HEAD] * sin
    for hh in range(MLA_HEADS):
        blk = slice(hh * PAD_HEAD, (hh + 1) * PAD_HEAD)
        k_ref[:, blk] = (k_nope[:, blk] + k_pe).astype(BF16)
    v_ref[...] = (_dot(kvn, wv_ref[...]) + _ones_upper_half(v_ref.shape[1])).astype(BF16)


def _rope_rotate_cols(w):
    half = QK_ROPE // 2
    return jnp.concatenate([-w[..., half:], w[..., :half]], axis=-1)


def _l0_front(x, positions, w_in, b_in, conv_w, conv_b, cln_g, cln_b,
              qn_g, w_uq, kvn_g, w_ukv):
    bsz, seq, _ = x.shape
    pad_lo = jnp.zeros((D_MODEL, QK_NOPE), F32)
    pad_hi = jnp.zeros((D_MODEL, PAD_HEAD - QK_NOPE - QK_ROPE), F32)
    i3 = L0_KR
    w_kr = w_in[:, i3:i3 + QK_ROPE]
    w_in_p = jnp.concatenate(
        [w_in[:, :i3], pad_lo, w_kr, pad_hi, pad_lo, _rope_rotate_cols(w_kr), pad_hi],
        axis=1).astype(BF16)
    b_kr = b_in[i3:i3 + QK_ROPE]
    zlo = jnp.zeros((QK_NOPE,), F32)
    zhi = jnp.zeros((PAD_HEAD - QK_NOPE - QK_ROPE,), F32)
    b_in_p = jnp.concatenate(
        [b_in[:i3], zlo, b_kr, zhi, zlo, _rope_rotate_cols(b_kr), zhi])[None, :]

    dq = QK_NOPE + QK_ROPE
    wq3 = w_uq.reshape(Q_LORA, MLA_HEADS, dq)
    zq = jnp.zeros((Q_LORA, MLA_HEADS, PAD_HEAD - dq), F32)
    wq_p = jnp.concatenate([wq3, zq], axis=-1).reshape(Q_LORA, MLA_HEADS * PAD_HEAD)
    wqr_p = jnp.concatenate(
        [jnp.zeros((Q_LORA, MLA_HEADS, QK_NOPE), F32),
         _rope_rotate_cols(wq3[..., QK_NOPE:]), zq], axis=-1
    ).reshape(Q_LORA, MLA_HEADS * PAD_HEAD)
    wkv3 = w_ukv.reshape(KV_LORA, MLA_HEADS, QK_NOPE + V_DIM)
    wk_p = jnp.concatenate(
        [wkv3[..., :QK_NOPE], jnp.zeros((KV_LORA, MLA_HEADS, PAD_HEAD - QK_NOPE), F32)],
        axis=-1).reshape(KV_LORA, MLA_HEADS * PAD_HEAD)
    wv = _pad_heads(wkv3[..., QK_NOPE:].reshape(KV_LORA, MLA_HEADS * V_DIM), MLA_HEADS)

    inv_freq = 1.0 / (ROPE_THETA ** (jnp.arange(0, QK_ROPE, 2, dtype=F32) / QK_ROPE))
    invf = jnp.concatenate([jnp.zeros((QK_NOPE,), F32), inv_freq, inv_freq,
                            jnp.zeros((PAD_HEAD - QK_NOPE - QK_ROPE,), F32)])[None, :]

    ts = SEQ_TILE
    row = lambda w: pl.BlockSpec((None, ts, w), lambda b, s: (b, s, 0))
    qk_w = MLA_HEADS * PAD_HEAD
    v_w = MLA_HEADS * PAD_HEAD
    return pl.pallas_call(
        _l0_front_kernel,
        grid=(bsz, seq // ts),
        in_specs=[row(D_MODEL), row(1),
                  _resident((D_MODEL, L0_COLS)), _resident((1, L0_COLS)),
                  _resident((CONV_WIDTH, SUBLANES, CONV_CH)), _resident((1, CONV_CH)),
                  _resident((1, CONV_CH)), _resident((1, CONV_CH)),
                  _resident((1, Q_LORA)), _resident((Q_LORA, qk_w)),
                  _resident((Q_LORA, qk_w)), _resident((1, KV_LORA)),
                  _resident((KV_LORA, qk_w)), _resident((KV_LORA, v_w)),
                  _resident((1, PAD_HEAD))],
        out_specs=[row(CONV_CH), row(qk_w), row(qk_w), row(v_w)],
        out_shape=[jax.ShapeDtypeStruct((bsz, seq, CONV_CH), BF16),
                   jax.ShapeDtypeStruct((bsz, seq, qk_w), BF16),
                   jax.ShapeDtypeStruct((bsz, seq, qk_w), BF16),
                   jax.ShapeDtypeStruct((bsz, seq, v_w), BF16)],
        scratch_shapes=[pltpu.VMEM((ts + CONV_HALO, CONV_CH), F32)],
        compiler_params=_params("arbitrary", "arbitrary"),
        name="l0_front",
    )(x, positions[..., None], w_in_p, b_in_p,
      jnp.broadcast_to(conv_w[:, None, :], (CONV_WIDTH, SUBLANES, CONV_CH)), conv_b[None, :],
      cln_g[None, :], cln_b[None, :], qn_g[None, :], wq_p.astype(BF16),
      wqr_p.astype(BF16), kvn_g[None, :], wk_p.astype(BF16), wv.astype(BF16), invf)


def _l1_front_kernel(x_ref, w_ref, b_ref, wv_ref, bv_ref, wf_ref, bf_ref,
                     q_ref, k_ref, v_ref, cneg_ref, carry):
    ts = x_ref.shape[0]

    @pl.when(pl.program_id(1) == 0)
    def _():
        carry[...] = jnp.zeros_like(carry)

    xb = x_ref[...].astype(BF16)
    mix = FOX_HEADS * FOX_HD
    scale = FOX_HD ** -0.5 * LOG2_E
    q_ref[...] = ((_dot(xb, w_ref[:, 0:mix]) + b_ref[:, 0:mix]) * scale).astype(BF16)
    k_ref[...] = (_dot(xb, w_ref[:, mix:2 * mix]) + b_ref[:, mix:2 * mix]).astype(BF16)
    v = _dot(xb, wv_ref[...]) + bv_ref[...]
    head_lanes = lax.broadcasted_iota(jnp.int32, (ts, LANES), 1) < FOX_HD
    for pair in range(FOX_HEADS // 2):
        both = v[:, pair * LANES:(pair + 1) * LANES]
        swapped = pltpu.roll(both, LANES - FOX_HD, axis=1)
        for head, vals in enumerate((both, swapped)):
            blk = 2 * pair + head
            v_ref[:, blk * PAD_HEAD:(blk + 1) * PAD_HEAD] = (
                jnp.where(head_lanes, vals, 1.0).astype(BF16))

    log_f = jax.nn.log_sigmoid(_dot_nt(wf_ref[...], xb) + bf_ref[...])
    r = lax.broadcasted_iota(jnp.int32, (ts, ts), 0)
    c = lax.broadcasted_iota(jnp.int32, (ts, ts), 1)
    upper = (r <= c).astype(F32)
    csum = jnp.dot(log_f, upper, precision=lax.Precision.HIGHEST,
                   preferred_element_type=F32) + carry[...]
    cneg_ref[...] = csum * -LOG2_E
    carry[...] = carry[...] + jnp.sum(log_f, axis=1, keepdims=True)


def _l1_front(x, w_in, b_in):
    bsz, seq, _ = x.shape
    mix = FOX_HEADS * FOX_HD
    ts = SEQ_TILE
    row = lambda w: pl.BlockSpec((None, ts, w), lambda b, s: (b, s, 0))
    w_qk = w_in[:, :2 * mix].astype(BF16)
    b_qk = b_in[None, :2 * mix]
    w_v = w_in[:, 2 * mix:3 * mix].astype(BF16)
    b_v = b_in[None, 2 * mix:3 * mix]
    v_w = FOX_HEADS * PAD_HEAD
    wf_t = w_in[:, 3 * mix:].T.astype(BF16)
    bf_t = b_in[3 * mix:][:, None]
    return pl.pallas_call(
        _l1_front_kernel,
        grid=(bsz, seq // ts),
        in_specs=[row(D_MODEL), _resident((D_MODEL, 2 * mix)), _resident((1, 2 * mix)),
                  _resident((D_MODEL, mix)), _resident((1, mix)),
                  _resident((FOX_HEADS, D_MODEL)), _resident((FOX_HEADS, 1))],
        out_specs=[row(mix), row(mix), row(v_w),
                   pl.BlockSpec((None, FOX_HEADS, ts), lambda b, s: (b, 0, s))],
        out_shape=[jax.ShapeDtypeStruct((bsz, seq, mix), BF16)] * 2
        + [jax.ShapeDtypeStruct((bsz, seq, v_w), BF16),
           jax.ShapeDtypeStruct((bsz, FOX_HEADS, seq), F32)],
        scratch_shapes=[pltpu.VMEM((FOX_HEADS, 1), F32)],
        compiler_params=_params("arbitrary", "arbitrary"),
        name="l1_front",
    )(x, w_qk, b_qk, w_v, b_v, wf_t, bf_t)


def _attn_kernel(*refs, packed, has_bias):
    if has_bias:
        q_ref, k_ref, v_ref, cneg_ref, o_ref = refs
    else:
        q_ref, k_ref, v_ref, o_ref = refs
        cneg_ref = None
    tq = ATTN_TILE
    tk = ATTN_KEY_TILE
    half = LANES // 2

    def query_tile(qi):
        q_rows = pl.ds(pl.multiple_of(qi * tq, tq), tq)
        q2 = q_ref[q_rows, :]
        if packed:
            lane = lax.broadcasted_iota(jnp.int32, q2.shape, 1)
            zero = jnp.zeros_like(q2)
            qs = (jnp.where(lane < half, q2, zero), jnp.where(lane >= half, q2, zero))
        else:
            qs = (q2[:, :PAD_HEAD], q2[:, PAD_HEAD:])

        def k_block(kj, head):
            rows = pl.ds(pl.multiple_of(kj * tk, tk), tk)
            if packed:
                return k_ref[rows, :]
            return k_ref[rows, head * PAD_HEAD:(head + 1) * PAD_HEAD]

        def step(kj, carry, diag):
            rows = pl.ds(pl.multiple_of(kj * tk, tk), tk)
            skip = 0 if diag is None else diag * tk
            out = []
            for head in range(2):
                m_all, acc_all = carry[head]
                m, acc = m_all[skip:], acc_all[skip:]
                s = _dot_nt(qs[head][skip:], k_block(kj, head))
                if has_bias:
                    s = s + cneg_ref[head, :, rows]
                if diag is not None:
                    r = lax.broadcasted_iota(jnp.int32, s.shape, 0)
                    c = lax.broadcasted_iota(jnp.int32, s.shape, 1)
                    s = jnp.where(r >= c, s, NEG_INF)
                m_new = jnp.maximum(m, jnp.max(s, axis=-1, keepdims=True))
                alpha = jnp.exp2(m - m_new)
                p = jnp.exp2(s - m_new).astype(BF16)
                vblk = v_ref[rows, head * PAD_HEAD:(head + 1) * PAD_HEAD]
                acc = alpha * acc + _dot(p, vblk)
                if skip:
                    m_new = jnp.concatenate([m_all[:skip], m_new], axis=0)
                    acc = jnp.concatenate([acc_all[:skip], acc], axis=0)
                out.append((m_new, acc))
            return tuple(out)

        init = tuple((jnp.full((tq, 1), NEG_INF, F32), jnp.zeros((tq, LANES), F32))
                     for _ in range(2))
        per_tile = tq // tk
        n_full = qi * per_tile

        def group(first, c, size):
            for t in range(size):
                c = step(first + t, c, None)
            return c

        def diagonal(c):
            for d in range(per_tile):
                c = step(n_full + d, c, d)
            return c

        trips = n_full // ATTN_UNROLL
        carry = lax.fori_loop(0, jnp.maximum(trips - 1, 0),
                              lambda j, c: group(ATTN_UNROLL * j, c, ATTN_UNROLL), init)
        carry = lax.cond(
            trips >= 1,
            lambda c: diagonal(group(n_full - ATTN_UNROLL, c, ATTN_UNROLL)),
            diagonal, carry)
        (_, acc_a), (_, acc_b) = carry
        out_a = acc_a / pltpu.roll(acc_a, half, axis=1)
        out_b = acc_b / pltpu.roll(acc_b, half, axis=1)
        lane_o = lax.broadcasted_iota(jnp.int32, (tq, LANES), 1)
        o_ref[q_rows, :] = jnp.where(lane_o < half, out_a,
                                     pltpu.roll(out_b, half, axis=1)).astype(o_ref.dtype)

    pl.loop(0, q_ref.shape[0] // tq)(query_tile)


def _attention(q, k, v, cneg, packed):
    bsz, seq, v_w = v.shape
    assert ATTN_TILE % (ATTN_KEY_TILE * ATTN_UNROLL) == 0 and seq % ATTN_TILE == 0
    pairs = v_w // (2 * PAD_HEAD)
    qk_w = q.shape[-1] // pairs
    in_specs = [pl.BlockSpec((None, seq, qk_w), lambda b, p: (b, 0, p)),
                pl.BlockSpec((None, seq, qk_w), lambda b, p: (b, 0, p)),
                pl.BlockSpec((None, seq, 2 * PAD_HEAD), lambda b, p: (b, 0, p))]
    args = [q, k, v]
    if cneg is not None:
        in_specs.append(pl.BlockSpec((None, 2, 1, seq), lambda b, p: (b, p, 0, 0)))
        args.append(cneg)
    return pl.pallas_call(
        functools.partial(_attn_kernel, packed=packed, has_bias=cneg is not None),
        grid=(bsz, pairs),
        in_specs=in_specs,
        out_specs=pl.BlockSpec((None, seq, LANES), lambda b, p: (b, 0, p)),
        out_shape=jax.ShapeDtypeStruct((bsz, seq, pairs * LANES), BF16),
        compiler_params=_params("arbitrary", "arbitrary"),
        name="attn_packed" if packed else "attn_padded",
    )(*args)


def _first_hit(hits, found):
    out = []
    for hcur in hits:
        take = jnp.logical_and(hcur, jnp.logical_not(found))
        found = jnp.logical_or(found, take)
        out.append(take)
    return out, found


def _mix_out_router_kernel(a_ref, b_ref, x_ref, wa_ref, wb_ref, bo_ref, g_ref, be_ref,
                           rw_ref, rwl_ref, rb_ref,
                           x1_ref, x1p_ref, e_ref, gate_ref, rank_ref, cnt_ref, cnt, gate_t):
    tm = x_ref.shape[0]

    @pl.when(pl.program_id(0) == 0)
    def _():
        cnt[...] = jnp.zeros_like(cnt)
        gate_t[...] = jnp.zeros_like(gate_t)

    mix = _dot(a_ref[...], wa_ref[...]) + _dot(b_ref[...], wb_ref[...]) + bo_ref[...]
    x1 = _layer_norm(DN_ALPHA * x_ref[...] + mix, g_ref[...], be_ref[...])
    x1_ref[...] = x1
    _store_packed(x1p_ref, _pack_rows(x1))

    x_hi = x1.astype(BF16)
    x_lo = (x1 - x_hi.astype(F32)).astype(BF16)
    logits = (_dot_nt(rw_ref[...], x_hi) + _dot_nt(rw_ref[...], x_lo)
              + _dot_nt(rwl_ref[...], x_hi))
    aff = jax.nn.sigmoid(logits)
    choice = aff + rb_ref[...]
    per_group = N_EXPERTS // N_GROUPS
    sub = lax.broadcasted_iota(jnp.int32, (per_group, tm), 0)
    groups = [choice[g * per_group:(g + 1) * per_group, :] for g in range(N_GROUPS)]

    gscore = []
    for cg in groups:
        m1 = jnp.max(cg, axis=0, keepdims=True)
        i1 = jnp.min(jnp.where(cg == m1, sub, per_group), axis=0, keepdims=True)
        m2 = jnp.max(jnp.where(sub == i1, NEG_INF, cg), axis=0, keepdims=True)
        gscore.append(m1 + m2)

    gsel = [jnp.zeros((1, tm), jnp.bool_) for _ in range(N_GROUPS)]
    for _ in range(TOPK_GROUPS):
        best = functools.reduce(jnp.maximum, gscore)
        takes, _ = _first_hit([gs == best for gs in gscore], jnp.zeros((1, tm), jnp.bool_))
        gsel = [jnp.logical_or(a, t) for a, t in zip(gsel, takes)]
        gscore = [jnp.where(t, NEG_INF, gs) for gs, t in zip(gscore, takes)]

    masked = [jnp.where(gs, cg, NEG_INF) for gs, cg in zip(gsel, groups)]
    eid = [sub + g * per_group for g in range(N_GROUPS)]
    affs = [aff[g * per_group:(g + 1) * per_group, :] for g in range(N_GROUPS)]
    sel = [jnp.zeros((per_group, tm), jnp.bool_) for _ in range(N_GROUPS)]
    picked_e, picked_w = [], []
    for _ in range(TOP_K):
        best = jnp.max(functools.reduce(jnp.maximum, masked), axis=0, keepdims=True)
        cand = [jnp.where(mg == best, ig, N_EXPERTS) for mg, ig in zip(masked, eid)]
        idx = jnp.min(functools.reduce(jnp.minimum, cand), axis=0, keepdims=True)
        onehot = [ig == idx for ig in eid]
        w = functools.reduce(
            jnp.add, [jnp.sum(jnp.where(oh, ag, 0.0), axis=0, keepdims=True)
                      for oh, ag in zip(onehot, affs)])
        picked_e.append(idx)
        picked_w.append(w)
        sel = [jnp.logical_or(sg, oh) for sg, oh in zip(sel, onehot)]
        masked = [jnp.where(oh, NEG_INF, mg) for mg, oh in zip(masked, onehot)]

    wsum = functools.reduce(jnp.add, picked_w)

    sel_f = jnp.concatenate([sg.astype(F32) for sg in sel], axis=0)
    r = lax.broadcasted_iota(jnp.int32, (tm, tm), 0)
    c = lax.broadcasted_iota(jnp.int32, (tm, tm), 1)
    before = _dot(sel_f.astype(BF16), (r < c).astype(BF16)) + cnt[...]
    eall = lax.broadcasted_iota(jnp.int32, (N_EXPERTS, tm), 0)
    for kk in range(TOP_K):
        rank = jnp.sum(jnp.where(eall == picked_e[kk], before, 0.0), axis=0, keepdims=True)
        e_ref[kk:kk + 1, :] = picked_e[kk]
        gate_t[kk:kk + 1, :] = picked_w[kk] / wsum * ROUTED_SCALE
        rank_ref[kk:kk + 1, :] = rank.astype(jnp.int32)
    pad = SUBLANES - TOP_K
    e_ref[TOP_K:, :] = jnp.zeros((pad, tm), jnp.int32)
    rank_ref[TOP_K:, :] = jnp.zeros((pad, tm), jnp.int32)
    gate_ref[...] = gate_t[...].T
    cnt[...] = cnt[...] + jnp.sum(sel_f, axis=1, keepdims=True)
    cnt_ref[...] = jnp.broadcast_to(cnt[...], cnt_ref.shape)


def _mix_out_router(a, b, x, w_out, b_out, ln_g, ln_b, router_w, router_bias):
    n_tok = x.shape[0]
    tm = TOKEN_TILE
    half = w_out.shape[0] // 2
    row = lambda w: pl.BlockSpec((tm, w), lambda i: (i, 0))
    b_spec = (pl.BlockSpec((tm, half), lambda i: (i, 1)) if b is a
              else pl.BlockSpec((tm, half), lambda i: (i, 0)))
    lane_row = pl.BlockSpec((SUBLANES, tm), lambda i: (0, i))
    packed = pl.BlockSpec((tm // SUBLANES, PACK_BLOCKS, SUBLANES, LANES),
                          lambda i: (i, 0, 0, 0))
    w_bf = w_out.astype(BF16)
    rw_t = router_w.T
    rw_hi = rw_t.astype(BF16)
    rw_lo = (rw_t - rw_hi.astype(F32)).astype(BF16)
    return pl.pallas_call(
        _mix_out_router_kernel,
        grid=(n_tok // tm,),
        in_specs=[row(half), b_spec, row(D_MODEL),
                  pl.BlockSpec((half, D_MODEL), lambda i: (0, 0)),
                  pl.BlockSpec((half, D_MODEL), lambda i: (1, 0)),
                  _resident((1, D_MODEL)), _resident((1, D_MODEL)), _resident((1, D_MODEL)),
                  _resident((N_EXPERTS, D_MODEL)), _resident((N_EXPERTS, D_MODEL)),
                  _resident((N_EXPERTS, 1))],
        out_specs=[row(D_MODEL), packed, lane_row, row(LANES), lane_row,
                   _resident((N_EXPERTS, LANES))],
        out_shape=[jax.ShapeDtypeStruct((n_tok, D_MODEL), F32),
                   jax.ShapeDtypeStruct((n_tok // SUBLANES, PACK_BLOCKS, SUBLANES, LANES),
                                        jnp.int32),
                   jax.ShapeDtypeStruct((SUBLANES, n_tok), jnp.int32),
                   jax.ShapeDtypeStruct((n_tok, LANES), F32),
                   jax.ShapeDtypeStruct((SUBLANES, n_tok), jnp.int32),
                   jax.ShapeDtypeStruct((N_EXPERTS, LANES), F32)],
        scratch_shapes=[pltpu.VMEM((N_EXPERTS, 1), F32), pltpu.VMEM((LANES, tm), F32)],
        compiler_params=_params("arbitrary"),
        name="mix_out_router",
    )(a, b, x, w_bf, w_bf, b_out[None, :], ln_g[None, :], ln_b[None, :],
      rw_hi, rw_lo, router_bias[:, None])


def _expert_ffn_kernel(tile_e_ref, tile_on_ref, xs_ref, *refs):
    w_refs = refs[:3 * FFN_SLOTS]
    ys_ref = refs[3 * FFN_SLOTS]
    wg_bf, wu_bf, wd_bf = refs[3 * FFN_SLOTS + 1:]
    i = pl.program_id(0)
    sub = GROUP_TILE // SUBLANES

    for slot in range(FFN_SLOTS):
        t = FFN_SLOTS * i + slot
        fresh = jnp.logical_or(
            i == 0, tile_e_ref[t] != tile_e_ref[jnp.maximum(t - FFN_SLOTS, 0)])

        @pl.when(fresh)
        def _(slot=slot):
            wg_ref, wu_ref, wd_ref = w_refs[3 * slot:3 * slot + 3]
            wg_bf[slot] = wg_ref[...].astype(BF16)
            wu_bf[slot] = wu_ref[...].astype(BF16)
            wd_bf[slot] = wd_ref[...].astype(BF16)

    on = tile_on_ref[FFN_SLOTS * i] > 0

    @pl.when(on)
    def _():
        for slot in range(FFN_SLOTS):
            rows = pl.ds(slot * sub, sub)
            lo, hi = _unpack_rows(_load_packed(xs_ref.at[rows]))
            lo = lo.astype(BF16)
            hi = hi.astype(BF16)
            gate = (_dot(lo, wg_bf[slot, :PACK_COLS, :]) + _dot(hi, wg_bf[slot, PACK_COLS:, :]))
            up = (_dot(lo, wu_bf[slot, :PACK_COLS, :]) + _dot(hi, wu_bf[slot, PACK_COLS:, :]))
            hid = (jax.nn.silu(gate) * up).astype(BF16)
            _store_packed(ys_ref.at[rows], _pack_rows(_dot(hid, wd_bf[slot])))

    @pl.when(jnp.logical_not(on))
    def _():
        ys_ref[...] = jnp.zeros_like(ys_ref)


def _expert_ffn(xs, tile_e, tile_on, w_gate, w_up, w_down):
    rows_per_step = GROUP_TILE * FFN_SLOTS
    packed = pl.BlockSpec((rows_per_step // SUBLANES, PACK_BLOCKS, SUBLANES, LANES),
                          lambda i, te, on: (i, 0, 0, 0))
    w_specs = []
    for slot in range(FFN_SLOTS):
        pick = lambda i, te, on, slot=slot: (te[FFN_SLOTS * i + slot], 0, 0)
        w_specs += [pl.BlockSpec((None, D_MODEL, EXPERT_HID), pick),
                    pl.BlockSpec((None, D_MODEL, EXPERT_HID), pick),
                    pl.BlockSpec((None, EXPERT_HID, D_MODEL), pick)]
    grid_spec = pltpu.PrefetchScalarGridSpec(
        num_scalar_prefetch=2,
        grid=(xs.shape[0] * SUBLANES // rows_per_step,),
        in_specs=[packed] + w_specs,
        out_specs=packed,
        scratch_shapes=[pltpu.VMEM((FFN_SLOTS, D_MODEL, EXPERT_HID), BF16),
                        pltpu.VMEM((FFN_SLOTS, D_MODEL, EXPERT_HID), BF16),
                        pltpu.VMEM((FFN_SLOTS, EXPERT_HID, D_MODEL), BF16)],
    )
    return pl.pallas_call(
        _expert_ffn_kernel,
        grid_spec=grid_spec,
        out_shape=jax.ShapeDtypeStruct(xs.shape, jnp.int32),
        compiler_params=_params("arbitrary"),
        name="expert_ffn",
    )(tile_e, tile_on, xs, *([w_gate, w_up, w_down] * FFN_SLOTS))


def _moe_out_kernel(x1_ref, yg_ref, gate_ref, sg_ref, su_ref, sd_ref, g_ref, b_ref, *rest):
    o_ref = rest[-1]
    x1 = x1_ref[...]
    xb = x1.astype(BF16)
    hid = jax.nn.silu(_dot(xb, sg_ref[...])) * _dot(xb, su_ref[...])
    shared = _dot(hid.astype(BF16), sd_ref[...])
    gates = gate_ref[...]
    lo_acc = shared[:, :PACK_COLS]
    hi_acc = shared[:, PACK_COLS:]
    for kk in range(TOP_K):
        lo, hi = _unpack_rows(_load_packed(yg_ref.at[kk]))
        w = gates[:, kk:kk + 1]
        lo_acc = lo_acc + w * lo
        hi_acc = hi_acc + w * hi
    moe = jnp.concatenate([lo_acc, hi_acc], axis=1)
    o_ref[...] = _layer_norm(DN_ALPHA * x1 + moe, g_ref[...], b_ref[...])


def _moe_out(x1, yg, gates, sh_gate, sh_up, sh_down, ln_g, ln_b, part, earlier):
    n_tok = x1.shape[0]
    tm = TOKEN_TILE
    steps = n_tok // tm // MOE_PARTS
    first = part * steps
    row = lambda w: pl.BlockSpec((tm, w), lambda i: (i + first, 0))
    in_specs = [row(D_MODEL),
                pl.BlockSpec((TOP_K, tm // SUBLANES, PACK_BLOCKS, SUBLANES, LANES),
                             lambda i: (0, i, 0, 0, 0)),
                row(LANES),
                _resident((D_MODEL, EXPERT_HID)), _resident((D_MODEL, EXPERT_HID)),
                _resident((EXPERT_HID, D_MODEL)), _resident((1, D_MODEL)),
                _resident((1, D_MODEL))]
    args = [x1, yg, gates, sh_gate.astype(BF16), sh_up.astype(BF16), sh_down.astype(BF16),
            ln_g[None, :], ln_b[None, :]]
    aliases = {}
    if earlier is not None:
        in_specs.append(pl.BlockSpec(memory_space=pl.ANY))
        args.append(earlier)
        aliases = {len(args) - 1: 0}
    return pl.pallas_call(
        _moe_out_kernel,
        grid=(steps,),
        in_specs=in_specs,
        out_specs=row(D_MODEL),
        out_shape=jax.ShapeDtypeStruct((n_tok, D_MODEL), F32),
        input_output_aliases=aliases,
        compiler_params=_params("arbitrary"),
        name="moe_out",
    )(*args)


def _slot_kernel(offset_ref, e_ref, rank_ref, pos_ref):
    e = e_ref[...]
    base = jnp.zeros(e.shape, jnp.int32)
    for ex in range(N_EXPERTS):
        base = jnp.where(e == ex, offset_ref[ex], base)
    pos_ref[...] = base + rank_ref[...]


def _slots(offset, e6, rank6):
    n_tok = e6.shape[1]
    tl = SLOT_TILE
    lane_row = pl.BlockSpec((SUBLANES, tl), lambda i, off: (0, i))
    return pl.pallas_call(
        _slot_kernel,
        grid_spec=pltpu.PrefetchScalarGridSpec(
            num_scalar_prefetch=1, grid=(n_tok // tl,),
            in_specs=[lane_row, lane_row], out_specs=lane_row),
        out_shape=jax.ShapeDtypeStruct((SUBLANES, n_tok), jnp.int32),
        compiler_params=_params("arbitrary"),
        name="slots",
    )(offset, e6, rank6)


def _sc_mesh():
    return plsc.VectorSubcoreMesh(core_axis_name="c", subcore_axis_name="s")


def _sc_worker():
    return lax.axis_index("s") * SC_CORES + lax.axis_index("c")


def _sc_dispatch(x_sub, idx, n_out_sub):
    windows = x_sub.shape[0] // (SC_WORKERS * SC_WINDOW)

    @functools.partial(
        pl.kernel, mesh=_sc_mesh(),
        out_type=jax.ShapeDtypeStruct((n_out_sub, LANES), jnp.int32),
        scratch_types=[pltpu.VMEM((SC_WINDOW, LANES), jnp.int32),
                       pltpu.VMEM((TOP_K, SC_WINDOW), jnp.int32)],
        name="sc_dispatch")
    def run(x_hbm, idx_hbm, out_hbm, rows_v, idx_v):
        wid = _sc_worker()

        @pl.loop(0, windows)
        def _(c):
            base = (wid * windows + c) * SC_WINDOW
            pltpu.sync_copy(x_hbm.at[pl.ds(base, SC_WINDOW)], rows_v)
            pltpu.sync_copy(idx_hbm.at[wid, c], idx_v)
            for kk in range(TOP_K):
                pltpu.sync_copy(rows_v, out_hbm.at[idx_v.at[kk]])

    return run(x_sub, idx)


def _sc_combine(y_sub, idx, n_tok_sub):
    windows = n_tok_sub // (SC_WORKERS * SC_WINDOW)

    @functools.partial(
        pl.kernel, mesh=_sc_mesh(),
        out_type=jax.ShapeDtypeStruct((TOP_K, n_tok_sub, LANES), jnp.int32),
        scratch_types=[pltpu.VMEM((2, SC_WINDOW, LANES), jnp.int32),
                       pltpu.VMEM((TOP_K, SC_WINDOW), jnp.int32),
                       pltpu.SemaphoreType.DMA((2,)),
                       pltpu.SemaphoreType.DMA((2,))],
        name="sc_combine")
    def run(y_hbm, idx_hbm, out_hbm, rows_v, idx_v, gather_sem, write_sem):
        wid = _sc_worker()

        def gather(kk):
            buf = kk % 2
            return pltpu.async_copy(y_hbm.at[idx_v.at[kk]], rows_v.at[buf], gather_sem.at[buf])

        @pl.loop(0, windows)
        def _(c):
            base = (wid * windows + c) * SC_WINDOW
            pltpu.sync_copy(idx_hbm.at[wid, c], idx_v)
            writes = []
            pending = gather(0)
            for kk in range(TOP_K):
                pending.wait()
                if kk + 1 < TOP_K:
                    if kk >= 1:
                        writes[kk - 1].wait()
                    pending = gather(kk + 1)
                writes.append(pltpu.async_copy(
                    rows_v.at[kk % 2], out_hbm.at[kk, pl.ds(base, SC_WINDOW)],
                    write_sem.at[kk % 2]))
            writes[TOP_K - 2].wait()
            writes[TOP_K - 1].wait()

    return run(y_sub, idx)


def _moe(x1, x1p, e6, gates, rank6, counts, exp_w_gate, exp_w_up, exp_w_down,
         sh_w_gate, sh_w_up, sh_w_down, ln_g, ln_b):
    n_tok = x1.shape[0]
    tg = GROUP_TILE
    n_rows = n_tok * TOP_K + N_EXPERTS * tg
    n_tiles = n_rows // tg

    cnt = counts[:, 0].astype(jnp.int32)
    tiles_per_e = (cnt + tg - 1) // tg
    tile_end = jnp.cumsum(tiles_per_e)
    offset = (tile_end - tiles_per_e) * tg
    tile_ids = jnp.arange(n_tiles, dtype=jnp.int32)
    tile_on = (tile_ids < tile_end[-1]).astype(jnp.int32)
    tile_e = jnp.minimum(
        jnp.sum((tile_ids[:, None] >= tile_end[None, :]).astype(jnp.int32), axis=1),
        N_EXPERTS - 1)
    tile_e = jnp.where(tile_on > 0, tile_e, tile_e[jnp.maximum(tile_end[-1] - 1, 0)])

    pos = _slots(offset, e6, rank6)[:TOP_K]

    group = PACK_BLOCKS * SUBLANES
    sub0 = (pos // SUBLANES) * group + pos % SUBLANES
    lane = jnp.arange(SC_WINDOW)
    spread = (jnp.arange(group)[:, None]
              == (lane // group) * SUBLANES + lane % SUBLANES).astype(F32)
    idx = jnp.einsum("kws,sd->kwd", sub0.reshape(TOP_K, n_tok // group, group).astype(F32),
                     spread, precision=lax.Precision.HIGHEST).astype(jnp.int32)
    idx = idx + ((lane % group) // SUBLANES * SUBLANES).astype(jnp.int32)
    n_sub = n_tok * PACK_BLOCKS
    idx = idx.reshape(TOP_K, n_sub)

    def windowed(ix):
        windows = ix.shape[1] // (SC_WORKERS * SC_WINDOW)
        return ix.reshape(TOP_K, SC_WORKERS, windows, SC_WINDOW).transpose(1, 2, 0, 3)

    xs = _sc_dispatch(x1p.reshape(n_sub, LANES), windowed(idx), n_rows * PACK_BLOCKS)
    ys = _expert_ffn(xs.reshape(n_rows // SUBLANES, PACK_BLOCKS, SUBLANES, LANES),
                     tile_e, tile_on, exp_w_gate, exp_w_up, exp_w_down)
    ys = ys.reshape(n_rows * PACK_BLOCKS, LANES)
    part_sub = n_sub // MOE_PARTS
    out = None
    for part in range(MOE_PARTS):
        yg = _sc_combine(ys, windowed(idx[:, part * part_sub:(part + 1) * part_sub]), part_sub)
        yg = yg.reshape(TOP_K, part_sub // (PACK_BLOCKS * SUBLANES), PACK_BLOCKS,
                        SUBLANES, LANES)
        out = _moe_out(x1, yg, gates, sh_w_gate, sh_w_up, sh_w_down, ln_g, ln_b, part, out)
    return out


def kernel(x, positions,
           l0_w_in, l0_b_in, l0_conv_w, l0_conv_b, l0_conv_ln_g, l0_conv_ln_b,
           l0_q_norm_g, l0_w_uq, l0_kv_norm_g, l0_w_ukv, l0_w_out, l0_b_out,
           l0_ln1_g, l0_ln1_b,
           l0_router_w, l0_router_bias, l0_exp_w_gate, l0_exp_w_up, l0_exp_w_down,
           l0_sh_w_gate, l0_sh_w_up, l0_sh_w_down, l0_ln2_g, l0_ln2_b,
           l1_w_in, l1_b_in, l1_w_out, l1_b_out, l1_ln1_g, l1_ln1_b,
           l1_router_w, l1_router_bias, l1_exp_w_gate, l1_exp_w_up, l1_exp_w_down,
           l1_sh_w_gate, l1_sh_w_up, l1_sh_w_down, l1_ln2_g, l1_ln2_b):
    bsz, seq, d = x.shape
    n_tok = bsz * seq

    u, q, k, v = _l0_front(x, positions, l0_w_in, l0_b_in, l0_conv_w, l0_conv_b,
                           l0_conv_ln_g, l0_conv_ln_b, l0_q_norm_g, l0_w_uq,
                           l0_kv_norm_g, l0_w_ukv)
    o = _attention(q, k, v, None, packed=False)
    x_flat = x.reshape(n_tok, d)
    routing = _mix_out_router(u.reshape(n_tok, -1), o.reshape(n_tok, -1), x_flat,
                              l0_w_out, l0_b_out, l0_ln1_g, l0_ln1_b,
                              l0_router_w, l0_router_bias)
    x2 = _moe(*routing, l0_exp_w_gate, l0_exp_w_up, l0_exp_w_down,
              l0_sh_w_gate, l0_sh_w_up, l0_sh_w_down, l0_ln2_g, l0_ln2_b)

    q, k, v, cneg = _l1_front(x2.reshape(bsz, seq, d), l1_w_in, l1_b_in)
    o = _attention(q, k, v, cneg[:, :, None, :], packed=True).reshape(n_tok, -1)
    routing = _mix_out_router(o, o, x2, l1_w_out, l1_b_out, l1_ln1_g, l1_ln1_b,
                              l1_router_w, l1_router_bias)
    x3 = _moe(*routing, l1_exp_w_gate, l1_exp_w_up, l1_exp_w_down,
              l1_sh_w_gate, l1_sh_w_up, l1_sh_w_down, l1_ln2_g, l1_ln2_b)
    return x3.reshape(bsz, seq, d)
```

```python
import functools

import jax
import jax.numpy as jnp
from jax import lax
from jax.experimental import pallas as pl
from jax.experimental.pallas import tpu as pltpu
from jax.experimental.pallas import tpu_sc as plsc

D_MODEL = 1024
DEPTH = 2
DN_ALPHA = (2 * DEPTH) ** 0.25
LN_EPS = 1e-5
RMS_EPS = 1e-6

CONV_CH = 512
CONV_WIDTH = 31
MLA_HEADS = 8
QK_NOPE = 64
QK_ROPE = 32
V_DIM = 64
Q_LORA = 256
KV_LORA = 128
ROPE_THETA = 10000.0
FOX_HEADS = 16
FOX_HD = 64
N_EXPERTS = 64
TOP_K = 6
N_GROUPS = 8
TOPK_GROUPS = 4
EXPERT_HID = 256
ROUTED_SCALE = 2.5

LANES = 128
SUBLANES = 8
V7X_VMEM_BYTES = 64 * 1024 * 1024
VMEM_LIMIT_BYTES = V7X_VMEM_BYTES * 7 // 8

SEQ_TILE = 512
TOKEN_TILE = 512
SLOT_TILE = 4096
ATTN_TILE = 1024
ATTN_KEY_TILE = 512
ATTN_UNROLL = 2
GROUP_TILE = 512
FFN_SLOTS = 2
CONV_ROWS = 32
CONV_HALO = 32
PAD_HEAD = 128

PACK_COLS = D_MODEL // 2
PACK_BLOCKS = PACK_COLS // LANES
SC_CORES = 2
SC_SUBCORES = 16
SC_WORKERS = SC_CORES * SC_SUBCORES
SC_WINDOW = 128
MOE_PARTS = 4
BF16 = jnp.bfloat16
F32 = jnp.float32
NEG_INF = float("-inf")
LOG2_E = 1.4426950408889634


def _params(*semantics):
    return pltpu.CompilerParams(dimension_semantics=semantics,
                                vmem_limit_bytes=VMEM_LIMIT_BYTES)


def _resident(shape):
    nd = len(shape)
    return pl.BlockSpec(shape, lambda *_: (0,) * nd)


def _dot(a, b):
    return jnp.dot(a, b, preferred_element_type=F32)


def _dot_nt(a, b, precision=None):
    return lax.dot_general(a, b, (((1,), (1,)), ((), ())),
                           precision=precision, preferred_element_type=F32)


def _layer_norm(x, g, b):
    mu = jnp.mean(x, axis=-1, keepdims=True)
    xc = x - mu
    var = jnp.mean(xc * xc, axis=-1, keepdims=True)
    return xc * lax.rsqrt(var + LN_EPS) * g + b


def _rms_norm(x, g):
    return x * lax.rsqrt(jnp.mean(x * x, axis=-1, keepdims=True) + RMS_EPS) * g


def _ones_upper_half(width):
    lane = lax.broadcasted_iota(jnp.int32, (1, width), 1)
    return jnp.where((lane & (PAD_HEAD - 1)) >= V_DIM, 1.0, 0.0).astype(F32)


def _pad_heads(w, heads):
    d = w.shape[-1] // heads
    w3 = w.reshape(w.shape[:-1] + (heads, d))
    pad = jnp.zeros(w.shape[:-1] + (heads, PAD_HEAD - d), w.dtype)
    return jnp.concatenate([w3, pad], axis=-1).reshape(w.shape[:-1] + (heads * PAD_HEAD,))


def _pack_rows(x):
    lo = lax.bitcast_convert_type(x[:, :PACK_COLS].astype(BF16).astype(F32), jnp.uint32)
    hi = lax.bitcast_convert_type(x[:, PACK_COLS:].astype(BF16).astype(F32), jnp.uint32)
    word = (lo >> 16) | (hi & jnp.uint32(0xFFFF0000))
    return lax.bitcast_convert_type(word, jnp.int32)


def _unpack_rows(words):
    u = lax.bitcast_convert_type(words, jnp.uint32)
    lo = lax.bitcast_convert_type(u << 16, F32)
    hi = lax.bitcast_convert_type(u & jnp.uint32(0xFFFF0000), F32)
    return lo, hi


def _store_packed(ref, words):
    groups = words.shape[0] // SUBLANES
    for cb in range(PACK_BLOCKS):
        ref[:, cb, :, :] = words[:, cb * LANES:(cb + 1) * LANES].reshape(groups, SUBLANES, LANES)


def _load_packed(ref):
    rows = ref.shape[0] * SUBLANES
    return jnp.concatenate(
        [ref[:, cb, :, :].reshape(rows, LANES) for cb in range(PACK_BLOCKS)], axis=1)


L0_A = 0
L0_G = CONV_CH
L0_Q = 2 * CONV_CH
L0_KV = L0_Q + Q_LORA
L0_KR = L0_KV + KV_LORA
L0_KRR = L0_KR + PAD_HEAD
L0_COLS = L0_KRR + PAD_HEAD


def _l0_front_kernel(x_ref, pos_ref, w_in_ref, b_in_ref, conv_w_ref, conv_b_ref,
                     cln_g_ref, cln_b_ref, qn_g_ref, wq_ref, wqr_ref, kvn_g_ref,
                     wk_ref, wv_ref, invf_ref,
                     u_ref, q_ref, k_ref, v_ref, ubuf):
    ts = x_ref.shape[0]

    @pl.when(pl.program_id(1) == 0)
    def _():
        ubuf[0:CONV_HALO, :] = jnp.zeros((CONV_HALO, CONV_CH), F32)

    h = _dot(x_ref[...].astype(BF16), w_in_ref[...]) + b_in_ref[...]

    ubuf[CONV_HALO:CONV_HALO + ts, :] = (
        h[:, L0_A:L0_A + CONV_CH] * jax.nn.sigmoid(h[:, L0_G:L0_G + CONV_CH]))
    first_tap = CONV_HALO - (CONV_WIDTH - 1)

    def conv_chunk(c, carry):
        base = pl.multiple_of(c * CONV_ROWS, CONV_ROWS)
        acc = jnp.broadcast_to(conv_b_ref[...], (CONV_ROWS, CONV_CH))
        for res in range(SUBLANES):
            rows = CONV_ROWS + (SUBLANES if res else 0)
            part = None
            for off in range(res, first_tap + CONV_WIDTH, SUBLANES):
                j = off - first_tap
                if j < 0:
                    continue
                window = ubuf[pl.ds(base + (off - res), rows), :]
                term = window.reshape(rows // SUBLANES, SUBLANES, CONV_CH) * conv_w_ref[j][None]
                part = term if part is None else part + term
            acc = acc + part.reshape(rows, CONV_CH)[res:res + CONV_ROWS, :]
        y = _layer_norm(acc, cln_g_ref[...], cln_b_ref[...])
        u_ref[pl.ds(base, CONV_ROWS), :] = (y * jax.nn.sigmoid(y)).astype(BF16)
        return carry

    lax.fori_loop(0, ts // CONV_ROWS, conv_chunk, 0, unroll=4)
    ubuf[0:CONV_HALO, :] = ubuf[ts:ts + CONV_HALO, :]

    ang = pos_ref[...].astype(F32) * invf_ref[...]
    cos = jnp.cos(ang)
    sin = jnp.sin(ang)
    scale = (QK_NOPE + QK_ROPE) ** -0.5 * LOG2_E

    qn = _rms_norm(h[:, L0_Q:L0_Q + Q_LORA], qn_g_ref[...]).astype(BF16)
    q = _dot(qn, wq_ref[...])
    q_rot = _dot(qn, wqr_ref[...])
    cos_s = cos * scale
    sin_s = sin * scale
    for hh in range(MLA_HEADS):
        blk = slice(hh * PAD_HEAD, (hh + 1) * PAD_HEAD)
        q_ref[:, blk] = (q[:, blk] * cos_s + q_rot[:, blk] * sin_s).astype(BF16)

    kvn = _rms_norm(h[:, L0_KV:L0_KV + KV_LORA], kvn_g_ref[...]).astype(BF16)
    k_nope = _dot(kvn, wk_ref[...])
    k_pe = h[:, L0_KR:L0_KR + PAD_HEAD] * cos + h[:, L0_KRR:L0_KRR + PAD_HEAD] * sin
    for hh in range(MLA_HEADS):
        blk = slice(hh * PAD_HEAD, (hh + 1) * PAD_HEAD)
        k_ref[:, blk] = (k_nope[:, blk] + k_pe).astype(BF16)
    v_ref[...] = (_dot(kvn, wv_ref[...]) + _ones_upper_half(v_ref.shape[1])).astype(BF16)


def _rope_rotate_cols(w):
    half = QK_ROPE // 2
    return jnp.concatenate([-w[..., half:], w[..., :half]], axis=-1)


def _l0_front(x, positions, w_in, b_in, conv_w, conv_b, cln_g, cln_b,
              qn_g, w_uq, kvn_g, w_ukv):
    bsz, seq, _ = x.shape
    pad_lo = jnp.zeros((D_MODEL, QK_NOPE), F32)
    pad_hi = jnp.zeros((D_MODEL, PAD_HEAD - QK_NOPE - QK_ROPE), F32)
    i3 = L0_KR
    w_kr = w_in[:, i3:i3 + QK_ROPE]
    w_in_p = jnp.concatenate(
        [w_in[:, :i3], pad_lo, w_kr, pad_hi, pad_lo, _rope_rotate_cols(w_kr), pad_hi],
        axis=1).astype(BF16)
    b_kr = b_in[i3:i3 + QK_ROPE]
    zlo = jnp.zeros((QK_NOPE,), F32)
    zhi = jnp.zeros((PAD_HEAD - QK_NOPE - QK_ROPE,), F32)
    b_in_p = jnp.concatenate(
        [b_in[:i3], zlo, b_kr, zhi, zlo, _rope_rotate_cols(b_kr), zhi])[None, :]

    dq = QK_NOPE + QK_ROPE
    wq3 = w_uq.reshape(Q_LORA, MLA_HEADS, dq)
    zq = jnp.zeros((Q_LORA, MLA_HEADS, PAD_HEAD - dq), F32)
    wq_p = jnp.concatenate([wq3, zq], axis=-1).reshape(Q_LORA, MLA_HEADS * PAD_HEAD)
    wqr_p = jnp.concatenate(
        [jnp.zeros((Q_LORA, MLA_HEADS, QK_NOPE), F32),
         _rope_rotate_cols(wq3[..., QK_NOPE:]), zq], axis=-1
    ).reshape(Q_LORA, MLA_HEADS * PAD_HEAD)
    wkv3 = w_ukv.reshape(KV_LORA, MLA_HEADS, QK_NOPE + V_DIM)
    wk_p = jnp.concatenate(
        [wkv3[..., :QK_NOPE], jnp.zeros((KV_LORA, MLA_HEADS, PAD_HEAD - QK_NOPE), F32)],
        axis=-1).reshape(KV_LORA, MLA_HEADS * PAD_HEAD)
    wv = _pad_heads(wkv3[..., QK_NOPE:].reshape(KV_LORA, MLA_HEADS * V_DIM), MLA_HEADS)

    inv_freq = 1.0 / (ROPE_THETA ** (jnp.arange(0, QK_ROPE, 2, dtype=F32) / QK_ROPE))
    invf = jnp.concatenate([jnp.zeros((QK_NOPE,), F32), inv_freq, inv_freq,
                            jnp.zeros((PAD_HEAD - QK_NOPE - QK_ROPE,), F32)])[None, :]

    ts = SEQ_TILE
    row = lambda w: pl.BlockSpec((None, ts, w), lambda b, s: (b, s, 0))
    qk_w = MLA_HEADS * PAD_HEAD
    v_w = MLA_HEADS * PAD_HEAD
    return pl.pallas_call(
        _l0_front_kernel,
        grid=(bsz, seq // ts),
        in_specs=[row(D_MODEL), row(1),
                  _resident((D_MODEL, L0_COLS)), _resident((1, L0_COLS)),
                  _resident((CONV_WIDTH, SUBLANES, CONV_CH)), _resident((1, CONV_CH)),
                  _resident((1, CONV_CH)), _resident((1, CONV_CH)),
                  _resident((1, Q_LORA)), _resident((Q_LORA, qk_w)),
                  _resident((Q_LORA, qk_w)), _resident((1, KV_LORA)),
                  _resident((KV_LORA, qk_w)), _resident((KV_LORA, v_w)),
                  _resident((1, PAD_HEAD))],
        out_specs=[row(CONV_CH), row(qk_w), row(qk_w), row(v_w)],
        out_shape=[jax.ShapeDtypeStruct((bsz, seq, CONV_CH), BF16),
                   jax.ShapeDtypeStruct((bsz, seq, qk_w), BF16),
                   jax.ShapeDtypeStruct((bsz, seq, qk_w), BF16),
                   jax.ShapeDtypeStruct((bsz, seq, v_w), BF16)],
        scratch_shapes=[pltpu.VMEM((ts + CONV_HALO, CONV_CH), F32)],
        compiler_params=_params("arbitrary", "arbitrary"),
        name="l0_front",
    )(x, positions[..., None], w_in_p, b_in_p,
      jnp.broadcast_to(conv_w[:, None, :], (CONV_WIDTH, SUBLANES, CONV_CH)), conv_b[None, :],
      cln_g[None, :], cln_b[None, :], qn_g[None, :], wq_p.astype(BF16),
      wqr_p.astype(BF16), kvn_g[None, :], wk_p.astype(BF16), wv.astype(BF16), invf)


def _l1_front_kernel(x_ref, w_ref, b_ref, wv_ref, bv_ref, wf_ref, bf_ref,
                     q_ref, k_ref, v_ref, cneg_ref, carry):
    ts = x_ref.shape[0]

    @pl.when(pl.program_id(1) == 0)
    def _():
        carry[...] = jnp.zeros_like(carry)

    xb = x_ref[...].astype(BF16)
    mix = FOX_HEADS * FOX_HD
    scale = FOX_HD ** -0.5 * LOG2_E
    q_ref[...] = ((_dot(xb, w_ref[:, 0:mix]) + b_ref[:, 0:mix]) * scale).astype(BF16)
    k_ref[...] = (_dot(xb, w_ref[:, mix:2 * mix]) + b_ref[:, mix:2 * mix]).astype(BF16)
    v = _dot(xb, wv_ref[...]) + bv_ref[...]
    head_lanes = lax.broadcasted_iota(jnp.int32, (ts, LANES), 1) < FOX_HD
    for pair in range(FOX_HEADS // 2):
        both = v[:, pair * LANES:(pair + 1) * LANES]
        swapped = pltpu.roll(both, LANES - FOX_HD, axis=1)
        for head, vals in enumerate((both, swapped)):
            blk = 2 * pair + head
            v_ref[:, blk * PAD_HEAD:(blk + 1) * PAD_HEAD] = (
                jnp.where(head_lanes, vals, 1.0).astype(BF16))

    log_f = jax.nn.log_sigmoid(_dot_nt(wf_ref[...], xb) + bf_ref[...])
    r = lax.broadcasted_iota(jnp.int32, (ts, ts), 0)
    c = lax.broadcasted_iota(jnp.int32, (ts, ts), 1)
    upper = (r <= c).astype(F32)
    csum = jnp.dot(log_f, upper, precision=lax.Precision.HIGHEST,
                   preferred_element_type=F32) + carry[...]
    cneg_ref[...] = csum * -LOG2_E
    carry[...] = carry[...] + jnp.sum(log_f, axis=1, keepdims=True)


def _l1_front(x, w_in, b_in):
    bsz, seq, _ = x.shape
    mix = FOX_HEADS * FOX_HD
    ts = SEQ_TILE
    row = lambda w: pl.BlockSpec((None, ts, w), lambda b, s: (b, s, 0))
    w_qk = w_in[:, :2 * mix].astype(BF16)
    b_qk = b_in[None, :2 * mix]
    w_v = w_in[:, 2 * mix:3 * mix].astype(BF16)
    b_v = b_in[None, 2 * mix:3 * mix]
    v_w = FOX_HEADS * PAD_HEAD
    wf_t = w_in[:, 3 * mix:].T.astype(BF16)
    bf_t = b_in[3 * mix:][:, None]
    return pl.pallas_call(
        _l1_front_kernel,
        grid=(bsz, seq // ts),
        in_specs=[row(D_MODEL), _resident((D_MODEL, 2 * mix)), _resident((1, 2 * mix)),
                  _resident((D_MODEL, mix)), _resident((1, mix)),
                  _resident((FOX_HEADS, D_MODEL)), _resident((FOX_HEADS, 1))],
        out_specs=[row(mix), row(mix), row(v_w),
                   pl.BlockSpec((None, FOX_HEADS, ts), lambda b, s: (b, 0, s))],
        out_shape=[jax.ShapeDtypeStruct((bsz, seq, mix), BF16)] * 2
        + [jax.ShapeDtypeStruct((bsz, seq, v_w), BF16),
           jax.ShapeDtypeStruct((bsz, FOX_HEADS, seq), F32)],
        scratch_shapes=[pltpu.VMEM((FOX_HEADS, 1), F32)],
        compiler_params=_params("arbitrary", "arbitrary"),
        name="l1_front",
    )(x, w_qk, b_qk, w_v, b_v, wf_t, bf_t)


def _attn_kernel(*refs, packed, has_bias):
    if has_bias:
        q_ref, k_ref, v_ref, cneg_ref, o_ref = refs
    else:
        q_ref, k_ref, v_ref, o_ref = refs
        cneg_ref = None
    tq = ATTN_TILE
    tk = ATTN_KEY_TILE
    half = LANES // 2

    def query_tile(qi):
        q_rows = pl.ds(pl.multiple_of(qi * tq, tq), tq)
        q2 = q_ref[q_rows, :]
        if packed:
            lane = lax.broadcasted_iota(jnp.int32, q2.shape, 1)
            zero = jnp.zeros_like(q2)
            qs = (jnp.where(lane < half, q2, zero), jnp.where(lane >= half, q2, zero))
        else:
            qs = (q2[:, :PAD_HEAD], q2[:, PAD_HEAD:])

        def k_block(kj, head):
            rows = pl.ds(pl.multiple_of(kj * tk, tk), tk)
            if packed:
                return k_ref[rows, :]
            return k_ref[rows, head * PAD_HEAD:(head + 1) * PAD_HEAD]

        def step(kj, carry, diag):
            rows = pl.ds(pl.multiple_of(kj * tk, tk), tk)
            skip = 0 if diag is None else diag * tk
            out = []
            for head in range(2):
                m_all, acc_all = carry[head]
                m, acc = m_all[skip:], acc_all[skip:]
                s = _dot_nt(qs[head][skip:], k_block(kj, head))
                if has_bias:
                    s = s + cneg_ref[head, :, rows]
                if diag is not None:
                    r = lax.broadcasted_iota(jnp.int32, s.shape, 0)
                    c = lax.broadcasted_iota(jnp.int32, s.shape, 1)
                    s = jnp.where(r >= c, s, NEG_INF)
                m_new = jnp.maximum(m, jnp.max(s, axis=-1, keepdims=True))
                alpha = jnp.exp2(m - m_new)
                p = jnp.exp2(s - m_new).astype(BF16)
                vblk = v_ref[rows, head * PAD_HEAD:(head + 1) * PAD_HEAD]
                acc = alpha * acc + _dot(p, vblk)
                if skip:
                    m_new = jnp.concatenate([m_all[:skip], m_new], axis=0)
                    acc = jnp.concatenate([acc_all[:skip], acc], axis=0)
                out.append((m_new, acc))
            return tuple(out)

        init = tuple((jnp.full((tq, 1), NEG_INF, F32), jnp.zeros((tq, LANES), F32))
                     for _ in range(2))
        per_tile = tq // tk
        n_full = qi * per_tile

        def group(first, c, size):
            for t in range(size):
                c = step(first + t, c, None)
            return c

        def diagonal(c):
            for d in range(per_tile):
                c = step(n_full + d, c, d)
            return c

        trips = n_full // ATTN_UNROLL
        carry = lax.fori_loop(0, jnp.maximum(trips - 1, 0),
                              lambda j, c: group(ATTN_UNROLL * j, c, ATTN_UNROLL), init)
        carry = lax.cond(
            trips >= 1,
            lambda c: diagonal(group(n_full - ATTN_UNROLL, c, ATTN_UNROLL)),
            diagonal, carry)
        (_, acc_a), (_, acc_b) = carry
        out_a = acc_a / pltpu.roll(acc_a, half, axis=1)
        out_b = acc_b / pltpu.roll(acc_b, half, axis=1)
        lane_o = lax.broadcasted_iota(jnp.int32, (tq, LANES), 1)
        o_ref[q_rows, :] = jnp.where(lane_o < half, out_a,
                                     pltpu.roll(out_b, half, axis=1)).astype(o_ref.dtype)

    pl.loop(0, q_ref.shape[0] // tq)(query_tile)


def _attention(q, k, v, cneg, packed):
    bsz, seq, v_w = v.shape
    assert ATTN_TILE % (ATTN_KEY_TILE * ATTN_UNROLL) == 0 and seq % ATTN_TILE == 0
    pairs = v_w // (2 * PAD_HEAD)
    qk_w = q.shape[-1] // pairs
    in_specs = [pl.BlockSpec((None, seq, qk_w), lambda b, p: (b, 0, p)),
                pl.BlockSpec((None, seq, qk_w), lambda b, p: (b, 0, p)),
                pl.BlockSpec((None, seq, 2 * PAD_HEAD), lambda b, p: (b, 0, p))]
    args = [q, k, v]
    if cneg is not None:
        in_specs.append(pl.BlockSpec((None, 2, 1, seq), lambda b, p: (b, p, 0, 0)))
        args.append(cneg)
    return pl.pallas_call(
        functools.partial(_attn_kernel, packed=packed, has_bias=cneg is not None),
        grid=(bsz, pairs),
        in_specs=in_specs,
        out_specs=pl.BlockSpec((None, seq, LANES), lambda b, p: (b, 0, p)),
        out_shape=jax.ShapeDtypeStruct((bsz, seq, pairs * LANES), BF16),
        compiler_params=_params("arbitrary", "arbitrary"),
        name="attn_packed" if packed else "attn_padded",
    )(*args)


def _first_hit(hits, found):
    out = []
    for hcur in hits:
        take = jnp.logical_and(hcur, jnp.logical_not(found))
        found = jnp.logical_or(found, take)
        out.append(take)
    return out, found


def _mix_out_router_kernel(a_ref, b_ref, x_ref, wa_ref, wb_ref, bo_ref, g_ref, be_ref,
                           rw_ref, rwl_ref, rb_ref,
                           x1_ref, x1p_ref, e_ref, gate_ref, rank_ref, cnt_ref, cnt, gate_t):
    tm = x_ref.shape[0]

    @pl.when(pl.program_id(0) == 0)
    def _():
        cnt[...] = jnp.zeros_like(cnt)
        gate_t[...] = jnp.zeros_like(gate_t)

    mix = _dot(a_ref[...], wa_ref[...]) + _dot(b_ref[...], wb_ref[...]) + bo_ref[...]
    x1 = _layer_norm(DN_ALPHA * x_ref[...] + mix, g_ref[...], be_ref[...])
    x1_ref[...] = x1
    _store_packed(x1p_ref, _pack_rows(x1))

    x_hi = x1.astype(BF16)
    x_lo = (x1 - x_hi.astype(F32)).astype(BF16)
    logits = (_dot_nt(rw_ref[...], x_hi) + _dot_nt(rw_ref[...], x_lo)
              + _dot_nt(rwl_ref[...], x_hi))
    aff = jax.nn.sigmoid(logits)
    choice = aff + rb_ref[...]
    per_group = N_EXPERTS // N_GROUPS
    sub = lax.broadcasted_iota(jnp.int32, (per_group, tm), 0)
    groups = [choice[g * per_group:(g + 1) * per_group, :] for g in range(N_GROUPS)]

    gscore = []
    for cg in groups:
        m1 = jnp.max(cg, axis=0, keepdims=True)
        i1 = jnp.min(jnp.where(cg == m1, sub, per_group), axis=0, keepdims=True)
        m2 = jnp.max(jnp.where(sub == i1, NEG_INF, cg), axis=0, keepdims=True)
        gscore.append(m1 + m2)

    gsel = [jnp.zeros((1, tm), jnp.bool_) for _ in range(N_GROUPS)]
    for _ in range(TOPK_GROUPS):
        best = functools.reduce(jnp.maximum, gscore)
        takes, _ = _first_hit([gs == best for gs in gscore], jnp.zeros((1, tm), jnp.bool_))
        gsel = [jnp.logical_or(a, t) for a, t in zip(gsel, takes)]
        gscore = [jnp.where(t, NEG_INF, gs) for gs, t in zip(gscore, takes)]

    masked = [jnp.where(gs, cg, NEG_INF) for gs, cg in zip(gsel, groups)]
    eid = [sub + g * per_group for g in range(N_GROUPS)]
    affs = [aff[g * per_group:(g + 1) * per_group, :] for g in range(N_GROUPS)]
    sel = [jnp.zeros((per_group, tm), jnp.bool_) for _ in range(N_GROUPS)]
    picked_e, picked_w = [], []
    for _ in range(TOP_K):
        best = jnp.max(functools.reduce(jnp.maximum, masked), axis=0, keepdims=True)
        cand = [jnp.where(mg == best, ig, N_EXPERTS) for mg, ig in zip(masked, eid)]
        idx = jnp.min(functools.reduce(jnp.minimum, cand), axis=0, keepdims=True)
        onehot = [ig == idx for ig in eid]
        w = functools.reduce(
            jnp.add, [jnp.sum(jnp.where(oh, ag, 0.0), axis=0, keepdims=True)
                      for oh, ag in zip(onehot, affs)])
        picked_e.append(idx)
        picked_w.append(w)
        sel = [jnp.logical_or(sg, oh) for sg, oh in zip(sel, onehot)]
        masked = [jnp.where(oh, NEG_INF, mg) for mg, oh in zip(masked, onehot)]

    wsum = functools.reduce(jnp.add, picked_w)

    sel_f = jnp.concatenate([sg.astype(F32) for sg in sel], axis=0)
    r = lax.broadcasted_iota(jnp.int32, (tm, tm), 0)
    c = lax.broadcasted_iota(jnp.int32, (tm, tm), 1)
    before = _dot(sel_f.astype(BF16), (r < c).astype(BF16)) + cnt[...]
    eall = lax.broadcasted_iota(jnp.int32, (N_EXPERTS, tm), 0)
    for kk in range(TOP_K):
        rank = jnp.sum(jnp.where(eall == picked_e[kk], before, 0.0), axis=0, keepdims=True)
        e_ref[kk:kk + 1, :] = picked_e[kk]
        gate_t[kk:kk + 1, :] = picked_w[kk] / wsum * ROUTED_SCALE
        rank_ref[kk:kk + 1, :] = rank.astype(jnp.int32)
    pad = SUBLANES - TOP_K
    e_ref[TOP_K:, :] = jnp.zeros((pad, tm), jnp.int32)
    rank_ref[TOP_K:, :] = jnp.zeros((pad, tm), jnp.int32)
    gate_ref[...] = gate_t[...].T
    cnt[...] = cnt[...] + jnp.sum(sel_f, axis=1, keepdims=True)
    cnt_ref[...] = jnp.broadcast_to(cnt[...], cnt_ref.shape)


def _mix_out_router(a, b, x, w_out, b_out, ln_g, ln_b, router_w, router_bias):
    n_tok = x.shape[0]
    tm = TOKEN_TILE
    half = w_out.shape[0] // 2
    row = lambda w: pl.BlockSpec((tm, w), lambda i: (i, 0))
    b_spec = (pl.BlockSpec((tm, half), lambda i: (i, 1)) if b is a
              else pl.BlockSpec((tm, half), lambda i: (i, 0)))
    lane_row = pl.BlockSpec((SUBLANES, tm), lambda i: (0, i))
    packed = pl.BlockSpec((tm // SUBLANES, PACK_BLOCKS, SUBLANES, LANES),
                          lambda i: (i, 0, 0, 0))
    w_bf = w_out.astype(BF16)
    rw_t = router_w.T
    rw_hi = rw_t.astype(BF16)
    rw_lo = (rw_t - rw_hi.astype(F32)).astype(BF16)
    return pl.pallas_call(
        _mix_out_router_kernel,
        grid=(n_tok // tm,),
        in_specs=[row(half), b_spec, row(D_MODEL),
                  pl.BlockSpec((half, D_MODEL), lambda i: (0, 0)),
                  pl.BlockSpec((half, D_MODEL), lambda i: (1, 0)),
                  _resident((1, D_MODEL)), _resident((1, D_MODEL)), _resident((1, D_MODEL)),
                  _resident((N_EXPERTS, D_MODEL)), _resident((N_EXPERTS, D_MODEL)),
                  _resident((N_EXPERTS, 1))],
        out_specs=[row(D_MODEL), packed, lane_row, row(LANES), lane_row,
                   _resident((N_EXPERTS, LANES))],
        out_shape=[jax.ShapeDtypeStruct((n_tok, D_MODEL), F32),
                   jax.ShapeDtypeStruct((n_tok // SUBLANES, PACK_BLOCKS, SUBLANES, LANES),
                                        jnp.int32),
                   jax.ShapeDtypeStruct((SUBLANES, n_tok), jnp.int32),
                   jax.ShapeDtypeStruct((n_tok, LANES), F32),
                   jax.ShapeDtypeStruct((SUBLANES, n_tok), jnp.int32),
                   jax.ShapeDtypeStruct((N_EXPERTS, LANES), F32)],
        scratch_shapes=[pltpu.VMEM((N_EXPERTS, 1), F32), pltpu.VMEM((LANES, tm), F32)],
        compiler_params=_params("arbitrary"),
        name="mix_out_router",
    )(a, b, x, w_bf, w_bf, b_out[None, :], ln_g[None, :], ln_b[None, :],
      rw_hi, rw_lo, router_bias[:, None])


def _expert_ffn_kernel(tile_e_ref, tile_on_ref, xs_ref, *refs):
    w_refs = refs[:3 * FFN_SLOTS]
    ys_ref = refs[3 * FFN_SLOTS]
    wg_bf, wu_bf, wd_bf = refs[3 * FFN_SLOTS + 1:]
    i = pl.program_id(0)
    sub = GROUP_TILE // SUBLANES

    for slot in range(FFN_SLOTS):
        t = FFN_SLOTS * i + slot
        fresh = jnp.logical_or(
            i == 0, tile_e_ref[t] != tile_e_ref[jnp.maximum(t - FFN_SLOTS, 0)])

        @pl.when(fresh)
        def _(slot=slot):
            wg_ref, wu_ref, wd_ref = w_refs[3 * slot:3 * slot + 3]
            wg_bf[slot] = wg_ref[...].astype(BF16)
            wu_bf[slot] = wu_ref[...].astype(BF16)
            wd_bf[slot] = wd_ref[...].astype(BF16)

    on = tile_on_ref[FFN_SLOTS * i] > 0

    @pl.when(on)
    def _():
        for slot in range(FFN_SLOTS):
            rows = pl.ds(slot * sub, sub)
            lo, hi = _unpack_rows(_load_packed(xs_ref.at[rows]))
            lo = lo.astype(BF16)
            hi = hi.astype(BF16)
            gate = (_dot(lo, wg_bf[slot, :PACK_COLS, :]) + _dot(hi, wg_bf[slot, PACK_COLS:, :]))
            up = (_dot(lo, wu_bf[slot, :PACK_COLS, :]) + _dot(hi, wu_bf[slot, PACK_COLS:, :]))
            hid = (jax.nn.silu(gate) * up).astype(BF16)
            _store_packed(ys_ref.at[rows], _pack_rows(_dot(hid, wd_bf[slot])))

    @pl.when(jnp.logical_not(on))
    def _():
        ys_ref[...] = jnp.zeros_like(ys_ref)


def _expert_ffn(xs, tile_e, tile_on, w_gate, w_up, w_down):
    rows_per_step = GROUP_TILE * FFN_SLOTS
    packed = pl.BlockSpec((rows_per_step // SUBLANES, PACK_BLOCKS, SUBLANES, LANES),
                          lambda i, te, on: (i, 0, 0, 0))
    w_specs = []
    for slot in range(FFN_SLOTS):
        pick = lambda i, te, on, slot=slot: (te[FFN_SLOTS * i + slot], 0, 0)
        w_specs += [pl.BlockSpec((None, D_MODEL, EXPERT_HID), pick),
                    pl.BlockSpec((None, D_MODEL, EXPERT_HID), pick),
                    pl.BlockSpec((None, EXPERT_HID, D_MODEL), pick)]
    grid_spec = pltpu.PrefetchScalarGridSpec(
        num_scalar_prefetch=2,
        grid=(xs.shape[0] * SUBLANES // rows_per_step,),
        in_specs=[packed] + w_specs,
        out_specs=packed,
        scratch_shapes=[pltpu.VMEM((FFN_SLOTS, D_MODEL, EXPERT_HID), BF16),
                        pltpu.VMEM((FFN_SLOTS, D_MODEL, EXPERT_HID), BF16),
                        pltpu.VMEM((FFN_SLOTS, EXPERT_HID, D_MODEL), BF16)],
    )
    return pl.pallas_call(
        _expert_ffn_kernel,
        grid_spec=grid_spec,
        out_shape=jax.ShapeDtypeStruct(xs.shape, jnp.int32),
        compiler_params=_params("arbitrary"),
        name="expert_ffn",
    )(tile_e, tile_on, xs, *([w_gate, w_up, w_down] * FFN_SLOTS))


def _moe_out_kernel(x1_ref, yg_ref, gate_ref, sg_ref, su_ref, sd_ref, g_ref, b_ref, *rest):
    o_ref = rest[-1]
    x1 = x1_ref[...]
    xb = x1.astype(BF16)
    hid = jax.nn.silu(_dot(xb, sg_ref[...])) * _dot(xb, su_ref[...])
    shared = _dot(hid.astype(BF16), sd_ref[...])
    gates = gate_ref[...]
    lo_acc = shared[:, :PACK_COLS]
    hi_acc = shared[:, PACK_COLS:]
    for kk in range(TOP_K):
        lo, hi = _unpack_rows(_load_packed(yg_ref.at[kk]))
        w = gates[:, kk:kk + 1]
        lo_acc = lo_acc + w * lo
        hi_acc = hi_acc + w * hi
    moe = jnp.concatenate([lo_acc, hi_acc], axis=1)
    o_ref[...] = _layer_norm(DN_ALPHA * x1 + moe, g_ref[...], b_ref[...])


def _moe_out(x1, yg, gates, sh_gate, sh_up, sh_down, ln_g, ln_b, part, earlier):
    n_tok = x1.shape[0]
    tm = TOKEN_TILE
    steps = n_tok // tm // MOE_PARTS
    first = part * steps
    row = lambda w: pl.BlockSpec((tm, w), lambda i: (i + first, 0))
    in_specs = [row(D_MODEL),
                pl.BlockSpec((TOP_K, tm // SUBLANES, PACK_BLOCKS, SUBLANES, LANES),
                             lambda i: (0, i, 0, 0, 0)),
                row(LANES),
                _resident((D_MODEL, EXPERT_HID)), _resident((D_MODEL, EXPERT_HID)),
                _resident((EXPERT_HID, D_MODEL)), _resident((1, D_MODEL)),
                _resident((1, D_MODEL))]
    args = [x1, yg, gates, sh_gate.astype(BF16), sh_up.astype(BF16), sh_down.astype(BF16),
            ln_g[None, :], ln_b[None, :]]
    aliases = {}
    if earlier is not None:
        in_specs.append(pl.BlockSpec(memory_space=pl.ANY))
        args.append(earlier)
        aliases = {len(args) - 1: 0}
    return pl.pallas_call(
        _moe_out_kernel,
        grid=(steps,),
        in_specs=in_specs,
        out_specs=row(D_MODEL),
        out_shape=jax.ShapeDtypeStruct((n_tok, D_MODEL), F32),
        input_output_aliases=aliases,
        compiler_params=_params("arbitrary"),
        name="moe_out",
    )(*args)


def _slot_kernel(offset_ref, e_ref, rank_ref, pos_ref):
    e = e_ref[...]
    base = jnp.zeros(e.shape, jnp.int32)
    for ex in range(N_EXPERTS):
        base = jnp.where(e == ex, offset_ref[ex], base)
    pos_ref[...] = base + rank_ref[...]


def _slots(offset, e6, rank6):
    n_tok = e6.shape[1]
    tl = SLOT_TILE
    lane_row = pl.BlockSpec((SUBLANES, tl), lambda i, off: (0, i))
    return pl.pallas_call(
        _slot_kernel,
        grid_spec=pltpu.PrefetchScalarGridSpec(
            num_scalar_prefetch=1, grid=(n_tok // tl,),
            in_specs=[lane_row, lane_row], out_specs=lane_row),
        out_shape=jax.ShapeDtypeStruct((SUBLANES, n_tok), jnp.int32),
        compiler_params=_params("arbitrary"),
        name="slots",
    )(offset, e6, rank6)


def _sc_mesh():
    return plsc.VectorSubcoreMesh(core_axis_name="c", subcore_axis_name="s")


def _sc_worker():
    return lax.axis_index("s") * SC_CORES + lax.axis_index("c")


def _sc_dispatch(x_sub, idx, n_out_sub):
    windows = x_sub.shape[0] // (SC_WORKERS * SC_WINDOW)
    assert windows % 2 == 0

    @functools.partial(
        pl.kernel, mesh=_sc_mesh(),
        out_type=jax.ShapeDtypeStruct((n_out_sub, LANES), jnp.int32),
        scratch_types=[pltpu.VMEM((2, SC_WINDOW, LANES), jnp.int32),
                       pltpu.VMEM((2, TOP_K, SC_WINDOW), jnp.int32),
                       pltpu.SemaphoreType.DMA((2,)),
                       pltpu.SemaphoreType.DMA((2,))],
        name="sc_dispatch")
    def run(x_hbm, idx_hbm, out_hbm, rows_v, idx_v, row_sem, idx_sem):
        wid = _sc_worker()

        def loads(c, buf):
            base = (wid * windows + c) * SC_WINDOW
            return (pltpu.make_async_copy(x_hbm.at[pl.ds(base, SC_WINDOW)], rows_v.at[buf],
                                          row_sem.at[buf]),
                    pltpu.make_async_copy(idx_hbm.at[wid, c], idx_v.at[buf], idx_sem.at[buf]))

        def start(c, buf):
            for cp in loads(c, buf):
                cp.start()

        def finish(c, buf):
            for cp in loads(c, buf):
                cp.wait()
            for kk in range(TOP_K):
                pltpu.sync_copy(rows_v.at[buf], out_hbm.at[idx_v.at[buf].at[kk]])

        start(0, 0)

        @pl.loop(0, windows, step=2)
        def _(c):
            start(c + 1, 1)
            finish(c, 0)

            @pl.when(c + 2 < windows)
            def _():
                start(c + 2, 0)

            finish(c + 1, 1)

    return run(x_sub, idx)


def _sc_combine(y_sub, idx, n_tok_sub):
    windows = n_tok_sub // (SC_WORKERS * SC_WINDOW)

    @functools.partial(
        pl.kernel, mesh=_sc_mesh(),
        out_type=jax.ShapeDtypeStruct((TOP_K, n_tok_sub, LANES), jnp.int32),
        scratch_types=[pltpu.VMEM((2, SC_WINDOW, LANES), jnp.int32),
                       pltpu.VMEM((TOP_K, SC_WINDOW), jnp.int32),
                       pltpu.SemaphoreType.DMA((2,)),
                       pltpu.SemaphoreType.DMA((2,))],
        name="sc_combine")
    def run(y_hbm, idx_hbm, out_hbm, rows_v, idx_v, gather_sem, write_sem):
        wid = _sc_worker()

        def gather(kk):
            buf = kk % 2
            return pltpu.async_copy(y_hbm.at[idx_v.at[kk]], rows_v.at[buf], gather_sem.at[buf])

        @pl.loop(0, windows)
        def _(c):
            base = (wid * windows + c) * SC_WINDOW
            pltpu.sync_copy(idx_hbm.at[wid, c], idx_v)
            writes = []
            pending = gather(0)
            for kk in range(TOP_K):
                pending.wait()
                if kk + 1 < TOP_K:
                    if kk >= 1:
                        writes[kk - 1].wait()
                    pending = gather(kk + 1)
                writes.append(pltpu.async_copy(
                    rows_v.at[kk % 2], out_hbm.at[kk, pl.ds(base, SC_WINDOW)],
                    write_sem.at[kk % 2]))
            writes[TOP_K - 2].wait()
            writes[TOP_K - 1].wait()

    return run(y_sub, idx)


def _moe(x1, x1p, e6, gates, rank6, counts, exp_w_gate, exp_w_up, exp_w_down,
         sh_w_gate, sh_w_up, sh_w_down, ln_g, ln_b):
    n_tok = x1.shape[0]
    tg = GROUP_TILE
    n_rows = n_tok * TOP_K + N_EXPERTS * tg
    n_tiles = n_rows // tg

    cnt = counts[:, 0].astype(jnp.int32)
    tiles_per_e = (cnt + tg - 1) // tg
    tile_end = jnp.cumsum(tiles_per_e)
    offset = (tile_end - tiles_per_e) * tg
    tile_ids = jnp.arange(n_tiles, dtype=jnp.int32)
    tile_on = (tile_ids < tile_end[-1]).astype(jnp.int32)
    tile_e = jnp.minimum(
        jnp.sum((tile_ids[:, None] >= tile_end[None, :]).astype(jnp.int32), axis=1),
        N_EXPERTS - 1)
    tile_e = jnp.where(tile_on > 0, tile_e, tile_e[jnp.maximum(tile_end[-1] - 1, 0)])

    pos = _slots(offset, e6, rank6)[:TOP_K]

    group = PACK_BLOCKS * SUBLANES
    sub0 = (pos // SUBLANES) * group + pos % SUBLANES
    lane = jnp.arange(SC_WINDOW)
    spread = (jnp.arange(group)[:, None]
              == (lane // group) * SUBLANES + lane % SUBLANES).astype(F32)
    idx = jnp.einsum("kws,sd->kwd", sub0.reshape(TOP_K, n_tok // group, group).astype(F32),
                     spread, precision=lax.Precision.HIGHEST).astype(jnp.int32)
    idx = idx + ((lane % group) // SUBLANES * SUBLANES).astype(jnp.int32)
    n_sub = n_tok * PACK_BLOCKS
    idx = idx.reshape(TOP_K, n_sub)

    def windowed(ix):
        windows = ix.shape[1] // (SC_WORKERS * SC_WINDOW)
        return ix.reshape(TOP_K, SC_WORKERS, windows, SC_WINDOW).transpose(1, 2, 0, 3)

    xs = _sc_dispatch(x1p.reshape(n_sub, LANES), windowed(idx), n_rows * PACK_BLOCKS)
    ys = _expert_ffn(xs.reshape(n_rows // SUBLANES, PACK_BLOCKS, SUBLANES, LANES),
                     tile_e, tile_on, exp_w_gate, exp_w_up, exp_w_down)
    ys = ys.reshape(n_rows * PACK_BLOCKS, LANES)
    part_sub = n_sub // MOE_PARTS
    out = None
    for part in range(MOE_PARTS):
        yg = _sc_combine(ys, windowed(idx[:, part * part_sub:(part + 1) * part_sub]), part_sub)
        yg = yg.reshape(TOP_K, part_sub // (PACK_BLOCKS * SUBLANES), PACK_BLOCKS,
                        SUBLANES, LANES)
        out = _moe_out(x1, yg, gates, sh_w_gate, sh_w_up, sh_w_down, ln_g, ln_b, part, out)
    return out


def kernel(x, positions,
           l0_w_in, l0_b_in, l0_conv_w, l0_conv_b, l0_conv_ln_g, l0_conv_ln_b,
           l0_q_norm_g, l0_w_uq, l0_kv_norm_g, l0_w_ukv, l0_w_out, l0_b_out,
           l0_ln1_g, l0_ln1_b,
           l0_router_w, l0_router_bias, l0_exp_w_gate, l0_exp_w_up, l0_exp_w_down,
           l0_sh_w_gate, l0_sh_w_up, l0_sh_w_down, l0_ln2_g, l0_ln2_b,
           l1_w_in, l1_b_in, l1_w_out, l1_b_out, l1_ln1_g, l1_ln1_b,
           l1_router_w, l1_router_bias, l1_exp_w_gate, l1_exp_w_up, l1_exp_w_down,
           l1_sh_w_gate, l1_sh_w_up, l1_sh_w_down, l1_ln2_g, l1_ln2_b):
    bsz, seq, d = x.shape
    n_tok = bsz * seq

    u, q, k, v = _l0_front(x, positions, l0_w_in, l0_b_in, l0_conv_w, l0_conv_b,
                           l0_conv_ln_g, l0_conv_ln_b, l0_q_norm_g, l0_w_uq,
                           l0_kv_norm_g, l0_w_ukv)
    o = _attention(q, k, v, None, packed=False)
    x_flat = x.reshape(n_tok, d)
    routing = _mix_out_router(u.reshape(n_tok, -1), o.reshape(n_tok, -1), x_flat,
                              l0_w_out, l0_b_out, l0_ln1_g, l0_ln1_b,
                              l0_router_w, l0_router_bias)
    x2 = _moe(*routing, l0_exp_w_gate, l0_exp_w_up, l0_exp_w_down,
              l0_sh_w_gate, l0_sh_w_up, l0_sh_w_down, l0_ln2_g, l0_ln2_b)

    q, k, v, cneg = _l1_front(x2.reshape(bsz, seq, d), l1_w_in, l1_b_in)
    o = _attention(q, k, v, cneg[:, :, None, :], packed=True).reshape(n_tok, -1)
    routing = _mix_out_router(o, o, x2, l1_w_out, l1_b_out, l1_ln1_g, l1_ln1_b,
                              l1_router_w, l1_router_bias)
    x3 = _moe(*routing, l1_exp_w_gate, l1_exp_w_up, l1_exp_w_down,
              l1_sh_w_gate, l1_sh_w_up, l1_sh_w_down, l1_ln2_g, l1_ln2_b)
    return x3.reshape(bsz, seq, d)
```
